```python
import math
import numpy as np
import jax
import jax.numpy as jnp
from jax import lax

D_MODEL = 2048
BATCH = 32
SEQ = 256
DEPTH = 2
DEC_BATCH = 4
DEC_SEQ = 1024
PAST_LEN = 512

GRID_W = 64
N_BRANCH = 3
BRANCH_W = D_MODEL // 2
GDN_DK = 128
GDN_DV = 128
GDN_HEADS = BRANCH_W // GDN_DV
GDN_CHUNK = 64
SSD_P = 64
SSD_HEADS = BRANCH_W // SSD_P
SSD_GROUPS = 2
SSD_HPG = SSD_HEADS // SSD_GROUPS
SSD_N = 128
SSD_CHUNK = 64
HGRN_DK = 128
HGRN_DV = 128
HGRN_HEADS = BRANCH_W // HGRN_DV
HGRN_CHUNK = 16
CONV_K = 3
N_EXPERTS = 16
EXPERT_FF = D_MODEL // 2
CAPACITY_FACTOR = 2
N_MOD = 6
EPS = 1e-6
GDN_QK_W = GDN_HEADS * GDN_DK
HGRN_QK_W = HGRN_HEADS * HGRN_DK
CONV_SIZES = (GDN_QK_W, GDN_QK_W, BRANCH_W, BRANCH_W, SSD_GROUPS * SSD_N, SSD_GROUPS * SSD_N)
CONV_CH = GDN_QK_W + GDN_QK_W + BRANCH_W + BRANCH_W + 2 * SSD_GROUPS * SSD_N
REST_SIZES = (BRANCH_W, 2 * GDN_HEADS, 2 * GDN_HEADS, BRANCH_W, 2 * SSD_HEADS, HGRN_QK_W, 2 * HGRN_QK_W, BRANCH_W, BRANCH_W, N_BRANCH * D_MODEL)
IN_COLS = CONV_CH + BRANCH_W + 4 * GDN_HEADS + BRANCH_W + 2 * SSD_HEADS + 3 * HGRN_QK_W + 2 * BRANCH_W + N_BRANCH * D_MODEL

kernel_name = 'bidir_hybrid_diffusion_step'


def _rms(x, gain):
    xf = x.astype(jnp.float32)
    y = xf * lax.rsqrt(jnp.mean(xf * xf, axis=-1, keepdims=True) + EPS)
    return (y * gain.astype(jnp.float32)).astype(x.dtype)


def _l2n(x):
    return x * lax.rsqrt(jnp.sum(x * x, axis=-1, keepdims=True) + EPS)


def _split(t, sizes):
    return jnp.split(t, np.cumsum(sizes)[:-1].tolist(), axis=-1)


def _flip(t, d, axis):
    return jnp.flip(t, axis=axis) if d else t


def _masked_exp(diff, mask):
    return jnp.where(mask, jnp.exp(jnp.where(mask, diff, 0.0)), 0.0)


def _dwconv(x, w, b):
    y = lax.conv_general_dilated(x, w[:, None, :].astype(x.dtype), window_strides=(1,),
                                 padding=[(CONV_K // 2, CONV_K // 2)],
                                 dimension_numbers=('NWC', 'WIO', 'NWC'),
                                 feature_group_count=x.shape[-1])
    return y + b.astype(x.dtype)


def _gated_delta_chunked(q, k, v, g, beta, s0):
    nb, nh, L, dk = q.shape
    C = GDN_CHUNK
    n = L // C

    def chunks(t):
        return t.reshape(nb, nh, n, C, *t.shape[3:])

    q, k, v, g, beta = chunks(q), chunks(k), chunks(v), chunks(g), chunks(beta)
    gc = jnp.cumsum(g, axis=-1)
    causal = jnp.tril(jnp.ones((C, C), dtype=bool))
    decay = _masked_exp(gc[..., :, None] - gc[..., None, :], causal)
    kb = k * beta[..., None]
    m_strict = jnp.einsum('bhntd,bhnsd->bhnts', kb, k) * decay * jnp.tril(jnp.ones((C, C), q.dtype), -1)
    a = m_strict + jnp.eye(C, dtype=q.dtype)
    u = jax.lax.linalg.triangular_solve(a, v * beta[..., None], left_side=True, lower=True, unit_diagonal=True)
    w = jax.lax.linalg.triangular_solve(a, kb * jnp.exp(gc)[..., None], left_side=True, lower=True, unit_diagonal=True)
    qk = jnp.einsum('bhntd,bhnsd->bhnts', q, k) * decay

    def step(S, xs):
        q_c, k_c, u_c, w_c, gc_c, qk_c = xs
        v_new = u_c - jnp.einsum('bhcd,bhde->bhce', w_c, S)
        o = (jnp.einsum('bhcd,bhde->bhce', q_c * jnp.exp(gc_c)[..., None], S)
             + jnp.einsum('bhts,bhse->bhte', qk_c, v_new))
        g_last = gc_c[..., -1:]
        S = S * jnp.exp(g_last)[..., None] + jnp.einsum(
            'bhcd,bhce->bhde', k_c * jnp.exp(g_last - gc_c)[..., None], v_new)
        return S, o

    xs = tuple(jnp.moveaxis(t, 2, 0) for t in (q, k, u, w, gc, qk))
    s_fin, o = lax.scan(step, s0, xs)
    return jnp.moveaxis(o, 0, 2).reshape(nb, nh, L, -1), s_fin


def _ssd_chunked(x, dt, a, bm, cm, h0):
    nb, L = x.shape[:2]
    C = SSD_CHUNK
    n = L // C

    def chunks(t):
        return t.reshape(nb, n, C, *t.shape[2:])

    x, dt, bm, cm = chunks(x), chunks(dt), chunks(bm), chunks(cm)
    acs = jnp.cumsum(dt * a, axis=2)
    causal = jnp.tril(jnp.ones((C, C), dtype=bool))[:, :, None, None]
    seg = _masked_exp(acs[:, :, :, None] - acs[:, :, None, :], causal)
    xdt = x * dt[..., None]
    cb = jnp.einsum('bntgk,bnsgk->bntsg', cm, bm)
    y_diag = jnp.einsum('bntsg,bntsgr,bnsgrp->bntgrp', cb, seg, xdt)
    st_local = jnp.einsum('bncgk,bncgr,bncgrp->bngrkp', bm, jnp.exp(acs[:, :, -1:] - acs), xdt)
    chunk_decay = jnp.exp(acs[:, :, -1])

    def step(h, xs):
        st, cd = xs
        return h * cd[..., None, None] + st, h

    h_fin, h_in = lax.scan(step, h0, (jnp.moveaxis(st_local, 1, 0), jnp.moveaxis(chunk_decay, 1, 0)))
    y_off = jnp.einsum('bncgk,nbgrkp,bncgr->bncgrp', cm, h_in, jnp.exp(acs))
    return (y_diag + y_off).reshape(nb, L, *x.shape[3:]), h_fin


def _hgrn2_chunked(q, k, v, log_f, s0):
    nb, nh, L, dk = q.shape
    C = HGRN_CHUNK
    n = L // C
    causal = jnp.tril(jnp.ones((C, C), dtype=bool))[:, :, None]

    def chunks(t):
        return jnp.moveaxis(t.reshape(nb, nh, n, C, t.shape[-1]), 2, 0)

    def step(S, xs):
        q_c, k_c, v_c, lf_c = xs
        G = jnp.cumsum(lf_c, axis=2)
        G_last = G[:, :, -1:]
        dec = _masked_exp(G[:, :, :, None] - G[:, :, None], causal)
        att = jnp.einsum('bhtd,bhsd,bhtsd->bhts', q_c, k_c, dec)
        o = (jnp.einsum('bhtd,bhde->bhte', q_c * jnp.exp(G), S)
             + jnp.einsum('bhts,bhse->bhte', att, v_c))
        S = S * jnp.exp(G_last)[:, :, 0, :, None] + jnp.einsum(
            'bhsd,bhse->bhde', k_c * jnp.exp(G_last - G), v_c)
        return S, o

    s_fin, o = lax.scan(step, s0, (chunks(q), chunks(k), chunks(v), chunks(log_f)))
    return jnp.moveaxis(o, 0, 2).reshape(nb, nh, L, -1), s_fin


def _mixer(h, grid, s_gdn, s_ssd, s_hgrn, lb, w_in, conv_w, conv_b, gdn_a_log, gdn_dt_bias, gdn_norm,
           ssd_a_log, ssd_dt_bias, ssd_d, ssd_norm, hgrn_norm, w_branch, w_out):
    f32 = jnp.float32
    nb, L, _ = h.shape
    proj = h @ w_in
    cx, rest = proj[..., :CONV_CH], proj[..., CONV_CH:]
    if grid:
        rows = L // GRID_W
        cx = _dwconv(cx.reshape(nb * rows, GRID_W, CONV_CH), conv_w, conv_b).reshape(nb, L, CONV_CH)
    else:
        cx = _dwconv(cx, conv_w, conv_b)
    cx = jax.nn.silu(cx).astype(f32)
    rest = rest.astype(f32)
    a_q, a_k, a_v, b_x, b_b, b_c = _split(cx, CONV_SIZES)
    a_g, a_alpha, a_beta, b_z, b_dt, c_q, c_f, c_i, c_g, m_g = _split(rest, REST_SIZES)

    def heads(t, n):
        return t.reshape(nb, L, n, -1).transpose(0, 2, 1, 3)

    q = _l2n(heads(a_q, GDN_HEADS)) * GDN_DK ** -0.5
    k = _l2n(heads(a_k, GDN_HEADS))
    v = heads(a_v, GDN_HEADS)
    alpha = a_alpha.reshape(nb, L, 2, GDN_HEADS).transpose(2, 0, 3, 1)
    beta = jax.nn.sigmoid(a_beta.reshape(nb, L, 2, GDN_HEADS).transpose(2, 0, 3, 1))
    g = -jnp.exp(gdn_a_log.astype(f32))[:, None, :, None] * jax.nn.softplus(
        alpha + gdn_dt_bias.astype(f32)[:, None, :, None])
    outs, s_a = [], []
    for d in range(2):
        o, s = _gated_delta_chunked(_flip(q, d, 2), _flip(k, d, 2), _flip(v, d, 2), _flip(g[d], d, 2),
                                    _flip(beta[d], d, 2), s_gdn[:, d].astype(f32))
        outs.append(_flip(o, d, 2))
        s_a.append(s)
    o_a = _rms((outs[0] + outs[1]).transpose(0, 2, 1, 3), gdn_norm) * jax.nn.silu(
        a_g.reshape(nb, L, GDN_HEADS, GDN_DV))
    o_a = o_a.reshape(nb, L, BRANCH_W)

    xs = b_x.reshape(nb, L, SSD_GROUPS, SSD_HPG, SSD_P)
    bm = b_b.reshape(nb, L, SSD_GROUPS, SSD_N)
    cm = b_c.reshape(nb, L, SSD_GROUPS, SSD_N)
    dt = jax.nn.softplus(b_dt.reshape(nb, L, 2, SSD_GROUPS, SSD_HPG)
                         + ssd_dt_bias.astype(f32).reshape(2, SSD_GROUPS, SSD_HPG))
    a_ssd = -jnp.exp(ssd_a_log.astype(f32)).reshape(2, SSD_GROUPS, SSD_HPG)
    ys, s_b = [], []
    for d in range(2):
        y, s = _ssd_chunked(_flip(xs, d, 1), _flip(dt[:, :, d], d, 1), a_ssd[d], _flip(bm, d, 1),
                            _flip(cm, d, 1),
                            s_ssd[:, d].astype(f32).reshape(nb, SSD_GROUPS, SSD_HPG, SSD_N, SSD_P))
        ys.append(_flip(y, d, 1))
        s_b.append(s.reshape(nb, SSD_HEADS, SSD_N, SSD_P))
    y = ys[0] + ys[1] + ssd_d.astype(f32).reshape(SSD_GROUPS, SSD_HPG, 1) * xs
    y = y.reshape(nb, L, BRANCH_W) * jax.nn.silu(b_z)
    o_b = _rms(y.reshape(nb, L, SSD_GROUPS, BRANCH_W // SSD_GROUPS),
               ssd_norm.reshape(SSD_GROUPS, -1)).reshape(nb, L, BRANCH_W)

    qc = jax.nn.silu(heads(c_q, HGRN_HEADS))
    vc = heads(c_i, HGRN_HEADS)
    zf = c_f.reshape(nb, L, 2, HGRN_HEADS, HGRN_DK).transpose(2, 0, 3, 1, 4)
    lbd = lb.reshape(2, 1, HGRN_HEADS, 1, HGRN_DK)
    log_f = jnp.log(lbd + (1.0 - lbd) * jax.nn.sigmoid(zf))
    k_c = (1.0 - lbd) * jax.nn.sigmoid(-zf)
    outs, s_c = [], []
    for d in range(2):
        o, s = _hgrn2_chunked(_flip(qc, d, 2), _flip(k_c[d], d, 2), _flip(vc, d, 2), _flip(log_f[d], d, 2),
                              s_hgrn[:, d].astype(f32))
        outs.append(_flip(o, d, 2))
        s_c.append(s)
    o_c = _rms((outs[0] + outs[1]).transpose(0, 2, 1, 3), hgrn_norm) * jax.nn.silu(
        c_g.reshape(nb, L, HGRN_HEADS, HGRN_DV))
    o_c = o_c.reshape(nb, L, BRANCH_W)

    branches = jnp.stack([o_a, o_b, o_c], axis=2).astype(h.dtype)
    gates = jax.nn.sigmoid(m_g.reshape(nb, L, N_BRANCH, D_MODEL)).astype(h.dtype)
    merged = jnp.sum(gates * jnp.einsum('blnk,nkd->blnd', branches, w_branch), axis=2)
    out = merged @ w_out
    return out, (jnp.stack(s_a, axis=1), jnp.stack(s_b, axis=1), jnp.stack(s_c, axis=1))


def _ec_moe(h, w_router, w_gu, w_down):
    shape = h.shape
    t = h.reshape(-1, shape[-1])
    cap = CAPACITY_FACTOR * t.shape[0] // N_EXPERTS
    aff = jax.nn.softmax((t @ w_router).astype(jnp.float32), axis=-1)
    gate, idx = lax.top_k(aff.T, cap)
    xe = t[idx]
    gu = jnp.einsum('ecd,edf->ecf', xe, w_gu)
    g, u = jnp.split(gu, 2, axis=-1)
    ye = jnp.einsum('ecf,efd->ecd', jax.nn.silu(g) * u, w_down) * gate[..., None].astype(h.dtype)
    out = jnp.zeros_like(t).at[idx.reshape(-1)].add(ye.reshape(-1, shape[-1]))
    return out.reshape(shape)


def _block(x, cond, grid, s_gdn, s_ssd, s_hgrn, lb, w_mod, b_mod, norm_gain, w_in, conv_w, conv_b,
           gdn_a_log, gdn_dt_bias, gdn_norm, ssd_a_log, ssd_dt_bias, ssd_d, ssd_norm, hgrn_norm,
           w_branch, w_out, w_router, w_gu, w_down):
    mod = (jax.nn.silu(cond) @ w_mod + b_mod).reshape(cond.shape[0], N_MOD, 1, D_MODEL)
    shift1, scale1, gate1, shift2, scale2, gate2 = (mod[:, i] for i in range(N_MOD))
    h = _rms(x, norm_gain[0]) * (1 + scale1) + shift1
    mix, states = _mixer(h, grid, s_gdn, s_ssd, s_hgrn, lb, w_in, conv_w, conv_b, gdn_a_log, gdn_dt_bias,
                         gdn_norm, ssd_a_log, ssd_dt_bias, ssd_d, ssd_norm, hgrn_norm, w_branch, w_out)
    x = x + gate1 * _rms(mix, norm_gain[1])
    h = _rms(x, norm_gain[2]) * (1 + scale2) + shift2
    x = x + gate2 * _rms(_ec_moe(h, w_router, w_gu, w_down), norm_gain[3])
    return x, states


def setup_inputs(seed: int = 0) -> dict:
    key = jax.random.key(seed)
    keys = iter(jax.random.split(key, 40))
    f32 = jnp.float32

    def nrm(shape, scale):
        return jax.random.normal(next(keys), shape, f32) * scale

    def unif(shape, lo, hi):
        return jax.random.uniform(next(keys), shape, f32, lo, hi)

    def dt_bias(shape):
        dt = jnp.exp(unif(shape, math.log(1e-3), math.log(1e-1)))
        return dt + jnp.log(-jnp.expm1(-dt))

    D = D_MODEL
    return {
        'x_prompt': nrm((BATCH, SEQ, D), 1.0),
        'x_sample': nrm((DEC_BATCH, DEC_SEQ, D), 1.0),
        'c': nrm((DEC_BATCH, D), 1.0),
        'state_gdn': nrm((DEC_BATCH, DEPTH, 2, GDN_HEADS, GDN_DK, GDN_DV), 0.1),
        'state_ssd': nrm((DEC_BATCH, DEPTH, 2, SSD_HEADS, SSD_N, SSD_P), 0.1),
        'state_hgrn': nrm((DEC_BATCH, DEPTH, 2, HGRN_HEADS, HGRN_DK, HGRN_DV), 0.1),
        'c_ctx': nrm((D,), 1.0),
        'w_mod': nrm((DEPTH, D, N_MOD * D), 0.5 * D ** -0.5),
        'b_mod': nrm((DEPTH, N_MOD * D), 0.01),
        'norm_gain': 1.0 + nrm((DEPTH, 4, D), 0.05),
        'w_in': nrm((DEPTH, D, IN_COLS), D ** -0.5),
        'conv_w': nrm((DEPTH, CONV_K, CONV_CH), CONV_K ** -0.5),
        'conv_b': nrm((DEPTH, CONV_CH), 0.01),
        'gdn_a_log': jnp.log(unif((DEPTH, 2, GDN_HEADS), 1.0, 16.0)),
        'gdn_dt_bias': dt_bias((DEPTH, 2, GDN_HEADS)),
        'gdn_norm': 1.0 + nrm((DEPTH, GDN_DV), 0.05),
        'ssd_a_log': jnp.log(unif((DEPTH, 2, SSD_HEADS), 1.0, 16.0)),
        'ssd_dt_bias': dt_bias((DEPTH, 2, SSD_HEADS)),
        'ssd_d': 1.0 + nrm((DEPTH, SSD_HEADS), 0.1),
        'ssd_norm': 1.0 + nrm((DEPTH, BRANCH_W), 0.05),
        'hgrn_lb': nrm((DEPTH, 2, HGRN_QK_W), 0.5),
        'hgrn_norm': 1.0 + nrm((DEPTH, HGRN_DV), 0.05),
        'w_branch': nrm((DEPTH, N_BRANCH, BRANCH_W, D), BRANCH_W ** -0.5),
        'w_out': nrm((DEPTH, D, D), D ** -0.5),
        'w_router': nrm((DEPTH, D, N_EXPERTS), D ** -0.5),
        'w_gu': nrm((DEPTH, N_EXPERTS, D, 2 * EXPERT_FF), D ** -0.5),
        'w_down': nrm((DEPTH, N_EXPERTS, EXPERT_FF, D), EXPERT_FF ** -0.5),
    }


def reference(x_prompt, x_sample, c, state_gdn, state_ssd, state_hgrn, c_ctx, w_mod, b_mod, norm_gain,
              w_in, conv_w, conv_b, gdn_a_log, gdn_dt_bias, gdn_norm, ssd_a_log, ssd_dt_bias, ssd_d,
              ssd_norm, hgrn_lb, hgrn_norm, w_branch, w_out, w_router, w_gu, w_down):
    f32 = jnp.float32
    lb_w = jax.nn.softmax(hgrn_lb.astype(f32), axis=0)
    lb = jnp.cumsum(lb_w, axis=0) - lb_w[:1]

    def layer(l, x, cond, grid, s_gdn, s_ssd, s_hgrn):
        return _block(x, cond, grid, s_gdn, s_ssd, s_hgrn, lb[l], w_mod[l], b_mod[l], norm_gain[l], w_in[l],
                      conv_w[l], conv_b[l], gdn_a_log[l], gdn_dt_bias[l], gdn_norm[l], ssd_a_log[l],
                      ssd_dt_bias[l], ssd_d[l], ssd_norm[l], hgrn_norm[l], w_branch[l], w_out[l],
                      w_router[l], w_gu[l], w_down[l])

    nb = x_prompt.shape[0]
    z_gdn = jnp.zeros((nb, 2, GDN_HEADS, GDN_DK, GDN_DV), f32)
    z_ssd = jnp.zeros((nb, 2, SSD_HEADS, SSD_N, SSD_P), f32)
    z_hgrn = jnp.zeros((nb, 2, HGRN_HEADS, HGRN_DK, HGRN_DV), f32)
    y_prompt = x_prompt
    new_gdn, new_ssd, new_hgrn = [], [], []
    for l in range(DEPTH):
        y_prompt, (sg, ss, sh) = layer(l, y_prompt, c_ctx[None, :], False, z_gdn, z_ssd, z_hgrn)
        new_gdn.append(sg)
        new_ssd.append(ss)
        new_hgrn.append(sh)
    new_state_gdn = jnp.stack(new_gdn, axis=1).astype(x_prompt.dtype)
    new_state_ssd = jnp.stack(new_ssd, axis=1).astype(x_prompt.dtype)
    new_state_hgrn = jnp.stack(new_hgrn, axis=1).astype(x_prompt.dtype)

    y_sample = x_sample
    for l in range(DEPTH):
        y_sample, _ = layer(l, y_sample, c, True, state_gdn[:, l], state_ssd[:, l], state_hgrn[:, l])

    return (y_prompt, y_sample, new_state_gdn, new_state_ssd, new_state_hgrn)
```

```python
import functools
import math

import numpy as np
import jax
import jax.numpy as jnp
from jax import lax
from jax.experimental import pallas as pl
from jax.experimental.pallas import tpu as pltpu

D_MODEL = 2048
DEPTH = 2
GRID_W = 64
N_BRANCH = 3
BRANCH_W = D_MODEL // 2
GDN_DK = 128
GDN_DV = 128
GDN_HEADS = BRANCH_W // GDN_DV
GDN_CHUNK = 64
SSD_P = 64
SSD_HEADS = BRANCH_W // SSD_P
SSD_GROUPS = 2
SSD_HPG = SSD_HEADS // SSD_GROUPS
SSD_N = 128
SSD_CHUNK = 64
HGRN_DK = 128
HGRN_DV = 128
HGRN_HEADS = BRANCH_W // HGRN_DV
HGRN_CHUNK = 16
CONV_K = 3
N_EXPERTS = 16
EXPERT_FF = D_MODEL // 2
CAPACITY_FACTOR = 2
N_MOD = 6
EPS = 1e-6
GDN_QK_W = GDN_HEADS * GDN_DK
HGRN_QK_W = HGRN_HEADS * HGRN_DK
CONV_SIZES = (GDN_QK_W, GDN_QK_W, BRANCH_W, BRANCH_W, SSD_GROUPS * SSD_N, SSD_GROUPS * SSD_N)
CONV_CH = GDN_QK_W + GDN_QK_W + BRANCH_W + BRANCH_W + 2 * SSD_GROUPS * SSD_N
REST_SIZES = (BRANCH_W, 2 * GDN_HEADS, 2 * GDN_HEADS, BRANCH_W, 2 * SSD_HEADS, HGRN_QK_W, 2 * HGRN_QK_W,
              BRANCH_W, BRANCH_W, N_BRANCH * D_MODEL)

VMEM_LIMIT = 48 * 1024 * 1024


def _mm_kernel(a_ref, b_ref, o_ref):
    a = a_ref[...].astype(jnp.bfloat16)
    b = b_ref[...].astype(jnp.bfloat16)
    o_ref[...] = jnp.dot(a, b, preferred_element_type=jnp.float32)


def _mm(a, b, tm=512, tn=512):
    m, k = a.shape
    _, n = b.shape
    tm = min(tm, m)
    tn = min(tn, n)
    return pl.pallas_call(
        _mm_kernel,
        grid=(pl.cdiv(m, tm), pl.cdiv(n, tn)),
        in_specs=[pl.BlockSpec((tm, k), lambda i, j: (i, 0)),
                  pl.BlockSpec((k, tn), lambda i, j: (0, j))],
        out_specs=pl.BlockSpec((tm, tn), lambda i, j: (i, j)),
        out_shape=jax.ShapeDtypeStruct((m, n), jnp.float32),
        compiler_params=pltpu.CompilerParams(
            dimension_semantics=("parallel", "arbitrary"), vmem_limit_bytes=VMEM_LIMIT),
        name="mm",
    )(a, b)


def _bmm_kernel(a_ref, b_ref, o_ref):
    a = a_ref[0].astype(jnp.bfloat16)
    b = b_ref[0].astype(jnp.bfloat16)
    o_ref[0] = jnp.dot(a, b, preferred_element_type=jnp.float32)


def _bmm(a, b, tm=512, tn=512):
    e, m, k = a.shape
    n = b.shape[-1]
    tm = min(tm, m)
    tn = min(tn, n)
    return pl.pallas_call(
        _bmm_kernel,
        grid=(e, pl.cdiv(m, tm), pl.cdiv(n, tn)),
        in_specs=[pl.BlockSpec((1, tm, k), lambda g, i, j: (g, i, 0)),
                  pl.BlockSpec((1, k, tn), lambda g, i, j: (g, 0, j))],
        out_specs=pl.BlockSpec((1, tm, tn), lambda g, i, j: (g, i, j)),
        out_shape=jax.ShapeDtypeStruct((e, m, n), jnp.float32),
        compiler_params=pltpu.CompilerParams(
            dimension_semantics=("parallel", "parallel", "arbitrary"), vmem_limit_bytes=VMEM_LIMIT),
        name="bmm",
    )(a, b)


def _rms(x, gain):
    xf = x.astype(jnp.float32)
    y = xf * lax.rsqrt(jnp.mean(xf * xf, axis=-1, keepdims=True) + EPS)
    return (y * gain.astype(jnp.float32)).astype(x.dtype)


def _l2n(x):
    return x * lax.rsqrt(jnp.sum(x * x, axis=-1, keepdims=True) + EPS)


def _split(t, sizes):
    return jnp.split(t, np.cumsum(sizes)[:-1].tolist(), axis=-1)


def _flip(t, d, axis):
    return jnp.flip(t, axis=axis) if d else t


def _masked_exp(diff, mask):
    return jnp.where(mask, jnp.exp(jnp.where(mask, diff, 0.0)), 0.0)


def _dwconv(x, w, b):
    y = lax.conv_general_dilated(x, w[:, None, :].astype(x.dtype), window_strides=(1,),
                                 padding=[(CONV_K // 2, CONV_K // 2)],
                                 dimension_numbers=('NWC', 'WIO', 'NWC'),
                                 feature_group_count=x.shape[-1])
    return y + b.astype(x.dtype)


def _gated_delta_chunked(q, k, v, g, beta, s0):
    nb, nh, L, dk = q.shape
    C = GDN_CHUNK
    n = L // C

    def chunks(t):
        return t.reshape(nb, nh, n, C, *t.shape[3:])

    q, k, v, g, beta = chunks(q), chunks(k), chunks(v), chunks(g), chunks(beta)
    gc = jnp.cumsum(g, axis=-1)
    causal = jnp.tril(jnp.ones((C, C), dtype=bool))
    decay = _masked_exp(gc[..., :, None] - gc[..., None, :], causal)
    kb = k * beta[..., None]
    m_strict = jnp.einsum('bhntd,bhnsd->bhnts', kb, k) * decay * jnp.tril(jnp.ones((C, C), q.dtype), -1)
    a = m_strict + jnp.eye(C, dtype=q.dtype)
    u = jax.lax.linalg.triangular_solve(a, v * beta[..., None], left_side=True, lower=True, unit_diagonal=True)
    w = jax.lax.linalg.triangular_solve(a, kb * jnp.exp(gc)[..., None], left_side=True, lower=True,
                                        unit_diagonal=True)
    qk = jnp.einsum('bhntd,bhnsd->bhnts', q, k) * decay

    def step(S, xs):
        q_c, k_c, u_c, w_c, gc_c, qk_c = xs
        v_new = u_c - jnp.einsum('bhcd,bhde->bhce', w_c, S)
        o = (jnp.einsum('bhcd,bhde->bhce', q_c * jnp.exp(gc_c)[..., None], S)
             + jnp.einsum('bhts,bhse->bhte', qk_c, v_new))
        g_last = gc_c[..., -1:]
        S = S * jnp.exp(g_last)[..., None] + jnp.einsum(
            'bhcd,bhce->bhde', k_c * jnp.exp(g_last - gc_c)[..., None], v_new)
        return S, o

    xs = tuple(jnp.moveaxis(t, 2, 0) for t in (q, k, u, w, gc, qk))
    s_fin, o = lax.scan(step, s0, xs)
    return jnp.moveaxis(o, 0, 2).reshape(nb, nh, L, -1), s_fin


def _ssd_chunked(x, dt, a, bm, cm, h0):
    nb, L = x.shape[:2]
    C = SSD_CHUNK
    n = L // C

    def chunks(t):
        return t.reshape(nb, n, C, *t.shape[2:])

    x, dt, bm, cm = chunks(x), chunks(dt), chunks(bm), chunks(cm)
    acs = jnp.cumsum(dt * a, axis=2)
    causal = jnp.tril(jnp.ones((C, C), dtype=bool))[:, :, None, None]
    seg = _masked_exp(acs[:, :, :, None] - acs[:, :, None, :], causal)
    xdt = x * dt[..., None]
    cb = jnp.einsum('bntgk,bnsgk->bntsg', cm, bm)
    y_diag = jnp.einsum('bntsg,bntsgr,bnsgrp->bntgrp', cb, seg, xdt)
    st_local = jnp.einsum('bncgk,bncgr,bncgrp->bngrkp', bm, jnp.exp(acs[:, :, -1:] - acs), xdt)
    chunk_decay = jnp.exp(acs[:, :, -1])

    def step(h, xs):
        st, cd = xs
        return h * cd[..., None, None] + st, h

    h_fin, h_in = lax.scan(step, h0, (jnp.moveaxis(st_local, 1, 0), jnp.moveaxis(chunk_decay, 1, 0)))
    y_off = jnp.einsum('bncgk,nbgrkp,bncgr->bncgrp', cm, h_in, jnp.exp(acs))
    return (y_diag + y_off).reshape(nb, L, *x.shape[3:]), h_fin


def _hgrn2_chunked(q, k, v, log_f, s0):
    nb, nh, L, dk = q.shape
    C = HGRN_CHUNK
    n = L // C
    causal = jnp.tril(jnp.ones((C, C), dtype=bool))[:, :, None]

    def chunks(t):
        return jnp.moveaxis(t.reshape(nb, nh, n, C, t.shape[-1]), 2, 0)

    def step(S, xs):
        q_c, k_c, v_c, lf_c = xs
        G = jnp.cumsum(lf_c, axis=2)
        G_last = G[:, :, -1:]
        dec = _masked_exp(G[:, :, :, None] - G[:, :, None], causal)
        att = jnp.einsum('bhtd,bhsd,bhtsd->bhts', q_c, k_c, dec)
        o = (jnp.einsum('bhtd,bhde->bhte', q_c * jnp.exp(G), S)
             + jnp.einsum('bhts,bhse->bhte', att, v_c))
        S = S * jnp.exp(G_last)[:, :, 0, :, None] + jnp.einsum(
            'bhsd,bhse->bhde', k_c * jnp.exp(G_last - G), v_c)
        return S, o

    s_fin, o = lax.scan(step, s0, (chunks(q), chunks(k), chunks(v), chunks(log_f)))
    return jnp.moveaxis(o, 0, 2).reshape(nb, nh, L, -1), s_fin


def _mixer(h, grid, s_gdn, s_ssd, s_hgrn, lb, w_in, conv_w, conv_b, gdn_a_log, gdn_dt_bias, gdn_norm,
           ssd_a_log, ssd_dt_bias, ssd_d, ssd_norm, hgrn_norm, w_branch, w_out):
    f32 = jnp.float32
    nb, L, _ = h.shape
    proj = _mm(h.reshape(nb * L, D_MODEL), w_in).reshape(nb, L, -1)
    cx, rest = proj[..., :CONV_CH], proj[..., CONV_CH:]
    if grid:
        rows = L // GRID_W
        cx = _dwconv(cx.reshape(nb * rows, GRID_W, CONV_CH), conv_w, conv_b).reshape(nb, L, CONV_CH)
    else:
        cx = _dwconv(cx, conv_w, conv_b)
    cx = jax.nn.silu(cx).astype(f32)
    rest = rest.astype(f32)
    a_q, a_k, a_v, b_x, b_b, b_c = _split(cx, CONV_SIZES)
    a_g, a_alpha, a_beta, b_z, b_dt, c_q, c_f, c_i, c_g, m_g = _split(rest, REST_SIZES)

    def heads(t, n):
        return t.reshape(nb, L, n, -1).transpose(0, 2, 1, 3)

    q = _l2n(heads(a_q, GDN_HEADS)) * GDN_DK ** -0.5
    k = _l2n(heads(a_k, GDN_HEADS))
    v = heads(a_v, GDN_HEADS)
    alpha = a_alpha.reshape(nb, L, 2, GDN_HEADS).transpose(2, 0, 3, 1)
    beta = jax.nn.sigmoid(a_beta.reshape(nb, L, 2, GDN_HEADS).transpose(2, 0, 3, 1))
    g = -jnp.exp(gdn_a_log.astype(f32))[:, None, :, None] * jax.nn.softplus(
        alpha + gdn_dt_bias.astype(f32)[:, None, :, None])
    outs, s_a = [], []
    for d in range(2):
        o, s = _gated_delta_chunked(_flip(q, d, 2), _flip(k, d, 2), _flip(v, d, 2), _flip(g[d], d, 2),
                                    _flip(beta[d], d, 2), s_gdn[:, d].astype(f32))
        outs.append(_flip(o, d, 2))
        s_a.append(s)
    o_a = _rms((outs[0] + outs[1]).transpose(0, 2, 1, 3), gdn_norm) * jax.nn.silu(
        a_g.reshape(nb, L, GDN_HEADS, GDN_DV))
    o_a = o_a.reshape(nb, L, BRANCH_W)

    xs = b_x.reshape(nb, L, SSD_GROUPS, SSD_HPG, SSD_P)
    bm = b_b.reshape(nb, L, SSD_GROUPS, SSD_N)
    cm = b_c.reshape(nb, L, SSD_GROUPS, SSD_N)
    dt = jax.nn.softplus(b_dt.reshape(nb, L, 2, SSD_GROUPS, SSD_HPG)
                         + ssd_dt_bias.astype(f32).reshape(2, SSD_GROUPS, SSD_HPG))
    a_ssd = -jnp.exp(ssd_a_log.astype(f32)).reshape(2, SSD_GROUPS, SSD_HPG)
    ys, s_b = [], []
    for d in range(2):
        y, s = _ssd_chunked(_flip(xs, d, 1), _flip(dt[:, :, d], d, 1), a_ssd[d], _flip(bm, d, 1),
                            _flip(cm, d, 1),
                            s_ssd[:, d].astype(f32).reshape(nb, SSD_GROUPS, SSD_HPG, SSD_N, SSD_P))
        ys.append(_flip(y, d, 1))
        s_b.append(s.reshape(nb, SSD_HEADS, SSD_N, SSD_P))
    y = ys[0] + ys[1] + ssd_d.astype(f32).reshape(SSD_GROUPS, SSD_HPG, 1) * xs
    y = y.reshape(nb, L, BRANCH_W) * jax.nn.silu(b_z)
    o_b = _rms(y.reshape(nb, L, SSD_GROUPS, BRANCH_W // SSD_GROUPS),
               ssd_norm.reshape(SSD_GROUPS, -1)).reshape(nb, L, BRANCH_W)

    qc = jax.nn.silu(heads(c_q, HGRN_HEADS))
    vc = heads(c_i, HGRN_HEADS)
    zf = c_f.reshape(nb, L, 2, HGRN_HEADS, HGRN_DK).transpose(2, 0, 3, 1, 4)
    lbd = lb.reshape(2, 1, HGRN_HEADS, 1, HGRN_DK)
    log_f = jnp.log(lbd + (1.0 - lbd) * jax.nn.sigmoid(zf))
    k_c = (1.0 - lbd) * jax.nn.sigmoid(-zf)
    outs, s_c = [], []
    for d in range(2):
        o, s = _hgrn2_chunked(_flip(qc, d, 2), _flip(k_c[d], d, 2), _flip(vc, d, 2), _flip(log_f[d], d, 2),
                              s_hgrn[:, d].astype(f32))
        outs.append(_flip(o, d, 2))
        s_c.append(s)
    o_c = _rms((outs[0] + outs[1]).transpose(0, 2, 1, 3), hgrn_norm) * jax.nn.silu(
        c_g.reshape(nb, L, HGRN_HEADS, HGRN_DV))
    o_c = o_c.reshape(nb, L, BRANCH_W)

    branches = jnp.stack([o_a, o_b, o_c], axis=0).reshape(N_BRANCH, nb * L, BRANCH_W)
    gates = jax.nn.sigmoid(m_g.reshape(nb * L, N_BRANCH, D_MODEL))
    bp = _bmm(branches, w_branch)
    merged = jnp.sum(gates * bp.transpose(1, 0, 2), axis=1)
    out = _mm(merged, w_out).reshape(nb, L, D_MODEL)
    return out, (jnp.stack(s_a, axis=1), jnp.stack(s_b, axis=1), jnp.stack(s_c, axis=1))


def _ec_moe(h, w_router, w_gu, w_down):
    shape = h.shape
    t = h.reshape(-1, shape[-1])
    cap = CAPACITY_FACTOR * t.shape[0] // N_EXPERTS
    aff = jax.nn.softmax(jnp.dot(t, w_router, precision=lax.Precision.HIGHEST).astype(jnp.float32), axis=-1)
    gate, idx = lax.top_k(aff.T, cap)
    xe = t[idx]
    gu = _bmm(xe, w_gu)
    g, u = jnp.split(gu, 2, axis=-1)
    ye = _bmm(jax.nn.silu(g) * u, w_down) * gate[..., None].astype(h.dtype)
    out = jnp.zeros_like(t).at[idx.reshape(-1)].add(ye.reshape(-1, shape[-1]))
    return out.reshape(shape)


def _block(x, cond, grid, s_gdn, s_ssd, s_hgrn, lb, w_mod, b_mod, norm_gain, w_in, conv_w, conv_b,
           gdn_a_log, gdn_dt_bias, gdn_norm, ssd_a_log, ssd_dt_bias, ssd_d, ssd_norm, hgrn_norm,
           w_branch, w_out, w_router, w_gu, w_down):
    mod = (jnp.dot(jax.nn.silu(cond), w_mod, precision=lax.Precision.HIGHEST) + b_mod).reshape(
        cond.shape[0], N_MOD, 1, D_MODEL)
    shift1, scale1, gate1, shift2, scale2, gate2 = (mod[:, i] for i in range(N_MOD))
    h = _rms(x, norm_gain[0]) * (1 + scale1) + shift1
    mix, states = _mixer(h, grid, s_gdn, s_ssd, s_hgrn, lb, w_in, conv_w, conv_b, gdn_a_log, gdn_dt_bias,
                         gdn_norm, ssd_a_log, ssd_dt_bias, ssd_d, ssd_norm, hgrn_norm, w_branch, w_out)
    x = x + gate1 * _rms(mix, norm_gain[1])
    h = _rms(x, norm_gain[2]) * (1 + scale2) + shift2
    x = x + gate2 * _rms(_ec_moe(h, w_router, w_gu, w_down), norm_gain[3])
    return x, states


def kernel(x_prompt, x_sample, c, state_gdn, state_ssd, state_hgrn, c_ctx, w_mod, b_mod, norm_gain,
           w_in, conv_w, conv_b, gdn_a_log, gdn_dt_bias, gdn_norm, ssd_a_log, ssd_dt_bias, ssd_d,
           ssd_norm, hgrn_lb, hgrn_norm, w_branch, w_out, w_router, w_gu, w_down):
    f32 = jnp.float32
    lb_w = jax.nn.softmax(hgrn_lb.astype(f32), axis=0)
    lb = jnp.cumsum(lb_w, axis=0) - lb_w[:1]

    def layer(l, x, cond, grid, s_gdn, s_ssd, s_hgrn):
        return _block(x, cond, grid, s_gdn, s_ssd, s_hgrn, lb[l], w_mod[l], b_mod[l], norm_gain[l], w_in[l],
                      conv_w[l], conv_b[l], gdn_a_log[l], gdn_dt_bias[l], gdn_norm[l], ssd_a_log[l],
                      ssd_dt_bias[l], ssd_d[l], ssd_norm[l], hgrn_norm[l], w_branch[l], w_out[l],
                      w_router[l], w_gu[l], w_down[l])

    nb = x_prompt.shape[0]
    z_gdn = jnp.zeros((nb, 2, GDN_HEADS, GDN_DK, GDN_DV), f32)
    z_ssd = jnp.zeros((nb, 2, SSD_HEADS, SSD_N, SSD_P), f32)
    z_hgrn = jnp.zeros((nb, 2, HGRN_HEADS, HGRN_DK, HGRN_DV), f32)
    y_prompt = x_prompt
    new_gdn, new_ssd, new_hgrn = [], [], []
    for l in range(DEPTH):
        y_prompt, (sg, ss, sh) = layer(l, y_prompt, c_ctx[None, :], False, z_gdn, z_ssd, z_hgrn)
        new_gdn.append(sg)
        new_ssd.append(ss)
        new_hgrn.append(sh)
    new_state_gdn = jnp.stack(new_gdn, axis=1).astype(x_prompt.dtype)
    new_state_ssd = jnp.stack(new_ssd, axis=1).astype(x_prompt.dtype)
    new_state_hgrn = jnp.stack(new_hgrn, axis=1).astype(x_prompt.dtype)

    y_sample = x_sample
    for l in range(DEPTH):
        y_sample, _ = layer(l, y_sample, c, True, state_gdn[:, l], state_ssd[:, l], state_hgrn[:, l])

    return (y_prompt, y_sample, new_state_gdn, new_state_ssd, new_state_hgrn)
```

```python
import functools
import math

import numpy as np
import jax
import jax.numpy as jnp
from jax import lax
from jax.experimental import pallas as pl
from jax.experimental.pallas import tpu as pltpu

D_MODEL = 2048
DEPTH = 2
GRID_W = 64
N_BRANCH = 3
BRANCH_W = D_MODEL // 2
GDN_DK = 128
GDN_DV = 128
GDN_HEADS = BRANCH_W // GDN_DV
GDN_CHUNK = 64
SSD_P = 64
SSD_HEADS = BRANCH_W // SSD_P
SSD_GROUPS = 2
SSD_HPG = SSD_HEADS // SSD_GROUPS
SSD_N = 128
SSD_CHUNK = 64
HGRN_DK = 128
HGRN_DV = 128
HGRN_HEADS = BRANCH_W // HGRN_DV
HGRN_CHUNK = 16
CONV_K = 3
N_EXPERTS = 16
EXPERT_FF = D_MODEL // 2
CAPACITY_FACTOR = 2
N_MOD = 6
EPS = 1e-6
GDN_QK_W = GDN_HEADS * GDN_DK
HGRN_QK_W = HGRN_HEADS * HGRN_DK
CONV_SIZES = (GDN_QK_W, GDN_QK_W, BRANCH_W, BRANCH_W, SSD_GROUPS * SSD_N, SSD_GROUPS * SSD_N)
CONV_CH = GDN_QK_W + GDN_QK_W + BRANCH_W + BRANCH_W + 2 * SSD_GROUPS * SSD_N
OFF_A_G = CONV_CH
OFF_ALPHA = OFF_A_G + BRANCH_W
OFF_BETA = OFF_ALPHA + 2 * GDN_HEADS
OFF_B_Z = OFF_BETA + 2 * GDN_HEADS
OFF_B_DT = OFF_B_Z + BRANCH_W
OFF_C_Q = OFF_B_DT + 2 * SSD_HEADS
IN_COLS = OFF_C_Q + 3 * HGRN_QK_W + 2 * BRANCH_W + N_BRANCH * D_MODEL
OFF_M_G = IN_COLS - N_BRANCH * D_MODEL
REST_W = IN_COLS - CONV_CH - 4 * GDN_HEADS - 2 * SSD_HEADS
R_M_G, R_A_G, R_B_Z, R_C_Q, R_C_F, R_C_I, R_C_G = 0, 6144, 7168, 8192, 9216, 11264, 12288
SMALL_W = 128

LANE = 128
VMEM_LIMIT = 48 * 1024 * 1024
ROW_TILE = 1024
SEQ_BLOCK = 256

BF16 = jnp.bfloat16
F32 = jnp.float32


def _cparams(sem):
    return pltpu.CompilerParams(dimension_semantics=sem, vmem_limit_bytes=VMEM_LIMIT)


def _mod_row(i, tm, n_ctx_rows, dec_seq):
    return jnp.maximum((i * tm - n_ctx_rows) // dec_seq + 1, 0)


def _mod_kernel(c_ref, w_ref, b_ref, o_ref):
    c = c_ref[...]
    a = c * jax.nn.sigmoid(c)
    o_ref[0] = jnp.dot(a, w_ref[0], preferred_element_type=F32, precision=lax.Precision.HIGHEST) + b_ref[0]


def _mod_all(cond, w_mod, b_mod, tn=1024):
    nl, d, n = w_mod.shape
    return pl.pallas_call(
        _mod_kernel,
        grid=(nl, n // tn),
        in_specs=[pl.BlockSpec((8, d), lambda l, j: (0, 0)),
                  pl.BlockSpec((1, d, tn), lambda l, j: (l, 0, j)),
                  pl.BlockSpec((1, 1, tn), lambda l, j: (l, 0, j))],
        out_specs=pl.BlockSpec((1, 8, tn), lambda l, j: (l, 0, j)),
        out_shape=jax.ShapeDtypeStruct((nl, 8, n), F32),
        compiler_params=_cparams(("parallel", "arbitrary")),
        name="mod_vectors",
    )(cond, w_mod, b_mod)


def _rms_f32(x, gain):
    return x * lax.rsqrt(jnp.mean(x * x, axis=-1, keepdims=True) + EPS) * gain


def _hmod_kernel(x_ref, g_ref, mod_ref, h_ref, *, i_shift, i_scale):
    x = x_ref[...]
    y = _rms_f32(x, g_ref[...])
    h_ref[...] = (y * (1.0 + mod_ref[0, i_scale:i_scale + 1, :]) + mod_ref[0, i_shift:i_shift + 1, :]).astype(BF16)


def _hmod(x, gain, mod, n_ctx_rows, dec_seq, i_shift, i_scale, tm=512):
    t, d = x.shape
    return pl.pallas_call(
        functools.partial(_hmod_kernel, i_shift=i_shift, i_scale=i_scale),
        grid=(t // tm,),
        in_specs=[pl.BlockSpec((tm, d), lambda i: (i, 0)),
                  pl.BlockSpec((1, d), lambda i: (0, 0)),
                  pl.BlockSpec((1, N_MOD, d), lambda i: (_mod_row(i, tm, n_ctx_rows, dec_seq), 0, 0))],
        out_specs=pl.BlockSpec((tm, d), lambda i: (i, 0)),
        out_shape=jax.ShapeDtypeStruct((t, d), BF16),
        compiler_params=_cparams(("parallel",)),
        name="rms_modulate",
    )(x, gain, mod)


def _mm_plain_kernel(h_ref, w_ref, o_ref):
    o_ref[...] = jnp.dot(h_ref[...], w_ref[...], preferred_element_type=F32).astype(o_ref.dtype)


def _mm_plain(h, w, tn, out_dtype=BF16, tm=ROW_TILE):
    t, k = h.shape
    n = w.shape[1]
    return pl.pallas_call(
        _mm_plain_kernel,
        grid=(t // tm, n // tn),
        in_specs=[pl.BlockSpec((tm, k), lambda i, j: (i, 0)),
                  pl.BlockSpec((k, tn), lambda i, j: (0, j))],
        out_specs=pl.BlockSpec((tm, tn), lambda i, j: (i, j)),
        out_shape=jax.ShapeDtypeStruct((t, n), out_dtype),
        compiler_params=_cparams(("parallel", "arbitrary")),
        name="proj_rest",
    )(h, w)


def _mm_small_kernel(h_ref, w_ref, o_ref):
    o_ref[...] = jnp.dot(h_ref[...].astype(F32), w_ref[...], preferred_element_type=F32,
                         precision=lax.Precision.HIGHEST)


def _mm_small(h, w, tm=256):
    t, k = h.shape
    n = w.shape[1]
    return pl.pallas_call(
        _mm_small_kernel,
        grid=(t // tm,),
        in_specs=[pl.BlockSpec((tm, k), lambda i: (i, 0)),
                  pl.BlockSpec((k, n), lambda i: (0, 0))],
        out_specs=pl.BlockSpec((tm, n), lambda i: (i, 0)),
        out_shape=jax.ShapeDtypeStruct((t, n), F32),
        compiler_params=_cparams(("parallel",)),
        name="proj_small",
    )(h, w)


def _mm_conv_kernel(h_ref, w_ref, cw_ref, cb_ref, o_ref, *, n_ctx_tiles, ctx_period, grid_period):
    acc = jnp.dot(h_ref[...], w_ref[...], preferred_element_type=F32)
    tm = acc.shape[0]
    period = jnp.where(pl.program_id(0) < n_ctx_tiles, ctx_period, grid_period)
    pos = lax.broadcasted_iota(jnp.int32, (tm, 1), 0) & (period - 1)
    prev = jnp.where(pos == 0, 0.0, pltpu.roll(acc, 1, 0))
    nxt = jnp.where(pos == period - 1, 0.0, pltpu.roll(acc, tm - 1, 0))
    y = prev * cw_ref[0:1, :] + acc * cw_ref[1:2, :] + nxt * cw_ref[2:3, :] + cb_ref[...]
    o_ref[...] = (y * jax.nn.sigmoid(y)).astype(o_ref.dtype)


def _mm_conv(h, w, conv_w, conv_b, n_ctx_rows, ctx_period, grid_period, tn=1152, tm=512):
    t, k = h.shape
    n = w.shape[1]
    return pl.pallas_call(
        functools.partial(_mm_conv_kernel, n_ctx_tiles=n_ctx_rows // tm, ctx_period=ctx_period,
                          grid_period=grid_period),
        grid=(t // tm, n // tn),
        in_specs=[pl.BlockSpec((tm, k), lambda i, j: (i, 0)),
                  pl.BlockSpec((k, tn), lambda i, j: (0, j)),
                  pl.BlockSpec((CONV_K, tn), lambda i, j: (0, j)),
                  pl.BlockSpec((1, tn), lambda i, j: (0, j))],
        out_specs=pl.BlockSpec((tm, tn), lambda i, j: (i, j)),
        out_shape=jax.ShapeDtypeStruct((t, n), BF16),
        compiler_params=_cparams(("parallel", "arbitrary")),
        name="proj_conv",
    )(h, w, conv_w, conv_b)


def _merge_kernel(o_ref, g_ref, w_ref, out_ref, acc_ref):
    n = pl.program_id(1)
    p = jnp.dot(o_ref[0], w_ref[0], preferred_element_type=F32)
    contrib = jax.nn.sigmoid(g_ref[...].astype(F32)) * p

    @pl.when(n == 0)
    def _():
        acc_ref[...] = contrib

    @pl.when(n > 0)
    def _():
        acc_ref[...] += contrib

    @pl.when(n == N_BRANCH - 1)
    def _():
        out_ref[...] = acc_ref[...].astype(out_ref.dtype)


def _merge(branches, rest, w_branch, tm=ROW_TILE):
    _, t, kb = branches.shape
    d = w_branch.shape[-1]
    g_blk = R_M_G // d
    return pl.pallas_call(
        _merge_kernel,
        grid=(t // tm, N_BRANCH),
        in_specs=[pl.BlockSpec((1, tm, kb), lambda i, n: (n, i, 0)),
                  pl.BlockSpec((tm, d), lambda i, n: (i, g_blk + n)),
                  pl.BlockSpec((1, kb, d), lambda i, n: (n, 0, 0))],
        out_specs=pl.BlockSpec((tm, d), lambda i, n: (i, 0)),
        out_shape=jax.ShapeDtypeStruct((t, d), BF16),
        scratch_shapes=[pltpu.VMEM((tm, d), F32)],
        compiler_params=_cparams(("parallel", "arbitrary")),
        name="branch_merge",
    )(branches, rest, w_branch)


def _outproj_kernel(m_ref, w_ref, x_ref, g_ref, mod_ref, wr_ref, x1_ref, h2_ref, aff_ref):
    out = jnp.dot(m_ref[...], w_ref[...], preferred_element_type=F32)
    x1 = x_ref[...] + mod_ref[0, 2:3, :] * _rms_f32(out, g_ref[1:2, :])
    x1_ref[...] = x1
    h2 = _rms_f32(x1, g_ref[2:3, :]) * (1.0 + mod_ref[0, 4:5, :]) + mod_ref[0, 3:4, :]
    h2_ref[...] = h2.astype(h2_ref.dtype)
    logits = jnp.dot(h2, wr_ref[...], preferred_element_type=F32, precision=lax.Precision.HIGHEST)
    lane = lax.broadcasted_iota(jnp.int32, logits.shape, 1)
    logits = jnp.where(lane < N_EXPERTS, logits, -jnp.inf)
    e = jnp.exp(logits - jnp.max(logits, axis=-1, keepdims=True))
    aff_ref[...] = e / jnp.sum(e, axis=-1, keepdims=True)


def _outproj(merged, w_out, x, gains, mod, w_router_p, n_ctx_rows, dec_seq, tm=256):
    t, d = x.shape
    return pl.pallas_call(
        _outproj_kernel,
        grid=(t // tm,),
        in_specs=[pl.BlockSpec((tm, d), lambda i: (i, 0)),
                  pl.BlockSpec((d, d), lambda i: (0, 0)),
                  pl.BlockSpec((tm, d), lambda i: (i, 0)),
                  pl.BlockSpec((4, d), lambda i: (0, 0)),
                  pl.BlockSpec((1, N_MOD, d), lambda i: (_mod_row(i, tm, n_ctx_rows, dec_seq), 0, 0)),
                  pl.BlockSpec((d, LANE), lambda i: (0, 0))],
        out_specs=[pl.BlockSpec((tm, d), lambda i: (i, 0)),
                   pl.BlockSpec((tm, d), lambda i: (i, 0)),
                   pl.BlockSpec((tm, LANE), lambda i: (i, 0))],
        out_shape=[jax.ShapeDtypeStruct((t, d), F32),
                   jax.ShapeDtypeStruct((t, d), F32),
                   jax.ShapeDtypeStruct((t, LANE), F32)],
        compiler_params=_cparams(("parallel",)),
        name="out_proj",
    )(merged, w_out, x, gains, mod, w_router_p)


def _bmm_kernel(a_ref, b_ref, o_ref):
    a = a_ref[0].astype(BF16)
    b = b_ref[0].astype(BF16)
    o_ref[0] = jnp.dot(a, b, preferred_element_type=F32)


def _bmm(a, b, tm=512, tn=512):
    e, m, k = a.shape
    n = b.shape[-1]
    tm = min(tm, m)
    tn = min(tn, n)
    return pl.pallas_call(
        _bmm_kernel,
        grid=(e, pl.cdiv(m, tm), pl.cdiv(n, tn)),
        in_specs=[pl.BlockSpec((1, tm, k), lambda g, i, j: (g, i, 0)),
                  pl.BlockSpec((1, k, tn), lambda g, i, j: (g, 0, j))],
        out_specs=pl.BlockSpec((1, tm, tn), lambda g, i, j: (g, i, j)),
        out_shape=jax.ShapeDtypeStruct((e, m, n), F32),
        compiler_params=_cparams(("parallel", "parallel", "arbitrary")),
        name="bmm",
    )(a, b)


def _rms(x, gain):
    xf = x.astype(jnp.float32)
    y = xf * lax.rsqrt(jnp.mean(xf * xf, axis=-1, keepdims=True) + EPS)
    return (y * gain.astype(jnp.float32)).astype(x.dtype)


def _l2n(x):
    return x * lax.rsqrt(jnp.sum(x * x, axis=-1, keepdims=True) + EPS)


def _flip(t, d, axis):
    return jnp.flip(t, axis=axis) if d else t


def _masked_exp(diff, mask):
    return jnp.where(mask, jnp.exp(jnp.where(mask, diff, 0.0)), 0.0)


def _gated_delta_chunked(q, k, v, g, beta, s0):
    nb, nh, L, dk = q.shape
    C = GDN_CHUNK
    n = L // C

    def chunks(t):
        return t.reshape(nb, nh, n, C, *t.shape[3:])

    q, k, v, g, beta = chunks(q), chunks(k), chunks(v), chunks(g), chunks(beta)
    gc = jnp.cumsum(g, axis=-1)
    causal = jnp.tril(jnp.ones((C, C), dtype=bool))
    decay = _masked_exp(gc[..., :, None] - gc[..., None, :], causal)
    kb = k * beta[..., None]
    m_strict = jnp.einsum('bhntd,bhnsd->bhnts', kb, k) * decay * jnp.tril(jnp.ones((C, C), q.dtype), -1)
    a = m_strict + jnp.eye(C, dtype=q.dtype)
    u = jax.lax.linalg.triangular_solve(a, v * beta[..., None], left_side=True, lower=True, unit_diagonal=True)
    w = jax.lax.linalg.triangular_solve(a, kb * jnp.exp(gc)[..., None], left_side=True, lower=True,
                                        unit_diagonal=True)
    qk = jnp.einsum('bhntd,bhnsd->bhnts', q, k) * decay

    def step(S, xs):
        q_c, k_c, u_c, w_c, gc_c, qk_c = xs
        v_new = u_c - jnp.einsum('bhcd,bhde->bhce', w_c, S)
        o = (jnp.einsum('bhcd,bhde->bhce', q_c * jnp.exp(gc_c)[..., None], S)
             + jnp.einsum('bhts,bhse->bhte', qk_c, v_new))
        g_last = gc_c[..., -1:]
        S = S * jnp.exp(g_last)[..., None] + jnp.einsum(
            'bhcd,bhce->bhde', k_c * jnp.exp(g_last - gc_c)[..., None], v_new)
        return S, o

    xs = tuple(jnp.moveaxis(t, 2, 0) for t in (q, k, u, w, gc, qk))
    s_fin, o = lax.scan(step, s0, xs)
    return jnp.moveaxis(o, 0, 2).reshape(nb, nh, L, -1), s_fin


def _ssd_chunked(x, dt, a, bm, cm, h0):
    nb, L = x.shape[:2]
    C = SSD_CHUNK
    n = L // C

    def chunks(t):
        return t.reshape(nb, n, C, *t.shape[2:])

    x, dt, bm, cm = chunks(x), chunks(dt), chunks(bm), chunks(cm)
    acs = jnp.cumsum(dt * a, axis=2)
    causal = jnp.tril(jnp.ones((C, C), dtype=bool))[:, :, None, None]
    seg = _masked_exp(acs[:, :, :, None] - acs[:, :, None, :], causal)
    xdt = x * dt[..., None]
    cb = jnp.einsum('bntgk,bnsgk->bntsg', cm, bm)
    y_diag = jnp.einsum('bntsg,bntsgr,bnsgrp->bntgrp', cb, seg, xdt)
    st_local = jnp.einsum('bncgk,bncgr,bncgrp->bngrkp', bm, jnp.exp(acs[:, :, -1:] - acs), xdt)
    chunk_decay = jnp.exp(acs[:, :, -1])

    def step(h, xs):
        st, cd = xs
        return h * cd[..., None, None] + st, h

    h_fin, h_in = lax.scan(step, h0, (jnp.moveaxis(st_local, 1, 0), jnp.moveaxis(chunk_decay, 1, 0)))
    y_off = jnp.einsum('bncgk,nbgrkp,bncgr->bncgrp', cm, h_in, jnp.exp(acs))
    return (y_diag + y_off).reshape(nb, L, *x.shape[3:]), h_fin


def _hgrn2_chunked(q, k, v, log_f, s0):
    nb, nh, L, dk = q.shape
    C = HGRN_CHUNK
    n = L // C
    causal = jnp.tril(jnp.ones((C, C), dtype=bool))[:, :, None]

    def chunks(t):
        return jnp.moveaxis(t.reshape(nb, nh, n, C, t.shape[-1]), 2, 0)

    def step(S, xs):
        q_c, k_c, v_c, lf_c = xs
        G = jnp.cumsum(lf_c, axis=2)
        G_last = G[:, :, -1:]
        dec = _masked_exp(G[:, :, :, None] - G[:, :, None], causal)
        att = jnp.einsum('bhtd,bhsd,bhtsd->bhts', q_c, k_c, dec)
        o = (jnp.einsum('bhtd,bhde->bhte', q_c * jnp.exp(G), S)
             + jnp.einsum('bhts,bhse->bhte', att, v_c))
        S = S * jnp.exp(G_last)[:, :, 0, :, None] + jnp.einsum(
            'bhsd,bhse->bhde', k_c * jnp.exp(G_last - G), v_c)
        return S, o

    s_fin, o = lax.scan(step, s0, (chunks(q), chunks(k), chunks(v), chunks(log_f)))
    return jnp.moveaxis(o, 0, 2).reshape(nb, nh, L, -1), s_fin


def _mixers_jnp(cx, rest, small, nb, L, s_gdn, s_ssd, s_hgrn, lb, gdn_a_log, gdn_dt_bias, gdn_norm,
                ssd_a_log, ssd_dt_bias, ssd_d, ssd_norm, hgrn_norm):
    f32 = jnp.float32
    cx = cx.astype(f32).reshape(nb, L, -1)
    rest = rest.astype(f32).reshape(nb, L, -1)
    small = small.reshape(nb, L, -1)
    a_q, a_k, a_v, b_x, b_b, b_c = jnp.split(cx, np.cumsum(CONV_SIZES)[:-1].tolist(), axis=-1)
    a_g = rest[..., R_A_G:R_A_G + 1024]
    b_z = rest[..., R_B_Z:R_B_Z + 1024]
    c_q = rest[..., R_C_Q:R_C_Q + 1024]
    c_f = rest[..., R_C_F:R_C_F + 2048]
    c_i = rest[..., R_C_I:R_C_I + 1024]
    c_g = rest[..., R_C_G:R_C_G + 1024]
    a_alpha = small[..., 0:16]
    a_beta = small[..., 16:32]
    b_dt = small[..., 32:64]

    def heads(t, n):
        return t.reshape(nb, L, n, -1).transpose(0, 2, 1, 3)

    q = _l2n(heads(a_q, GDN_HEADS)) * GDN_DK ** -0.5
    k = _l2n(heads(a_k, GDN_HEADS))
    v = heads(a_v, GDN_HEADS)
    alpha = a_alpha.reshape(nb, L, 2, GDN_HEADS).transpose(2, 0, 3, 1)
    beta = jax.nn.sigmoid(a_beta.reshape(nb, L, 2, GDN_HEADS).transpose(2, 0, 3, 1))
    g = -jnp.exp(gdn_a_log.astype(f32))[:, None, :, None] * jax.nn.softplus(
        alpha + gdn_dt_bias.astype(f32)[:, None, :, None])
    outs, s_a = [], []
    for d in range(2):
        o, s = _gated_delta_chunked(_flip(q, d, 2), _flip(k, d, 2), _flip(v, d, 2), _flip(g[d], d, 2),
                                    _flip(beta[d], d, 2), s_gdn[:, d].astype(f32))
        outs.append(_flip(o, d, 2))
        s_a.append(s)
    o_a = _rms((outs[0] + outs[1]).transpose(0, 2, 1, 3), gdn_norm) * jax.nn.silu(
        a_g.reshape(nb, L, GDN_HEADS, GDN_DV))
    o_a = o_a.reshape(nb * L, BRANCH_W)

    xs = b_x.reshape(nb, L, SSD_GROUPS, SSD_HPG, SSD_P)
    bm = b_b.reshape(nb, L, SSD_GROUPS, SSD_N)
    cm = b_c.reshape(nb, L, SSD_GROUPS, SSD_N)
    dt = jax.nn.softplus(b_dt.reshape(nb, L, 2, SSD_GROUPS, SSD_HPG)
                         + ssd_dt_bias.astype(f32).reshape(2, SSD_GROUPS, SSD_HPG))
    a_ssd = -jnp.exp(ssd_a_log.astype(f32)).reshape(2, SSD_GROUPS, SSD_HPG)
    ys, s_b = [], []
    for d in range(2):
        y, s = _ssd_chunked(_flip(xs, d, 1), _flip(dt[:, :, d], d, 1), a_ssd[d], _flip(bm, d, 1),
                            _flip(cm, d, 1),
                            s_ssd[:, d].astype(f32).reshape(nb, SSD_GROUPS, SSD_HPG, SSD_N, SSD_P))
        ys.append(_flip(y, d, 1))
        s_b.append(s.reshape(nb, SSD_HEADS, SSD_N, SSD_P))
    y = ys[0] + ys[1] + ssd_d.astype(f32).reshape(SSD_GROUPS, SSD_HPG, 1) * xs
    y = y.reshape(nb, L, BRANCH_W) * jax.nn.silu(b_z)
    o_b = _rms(y.reshape(nb, L, SSD_GROUPS, BRANCH_W // SSD_GROUPS),
               ssd_norm.reshape(SSD_GROUPS, -1)).reshape(nb * L, BRANCH_W)

    qc = jax.nn.silu(heads(c_q, HGRN_HEADS))
    vc = heads(c_i, HGRN_HEADS)
    zf = c_f.reshape(nb, L, 2, HGRN_HEADS, HGRN_DK).transpose(2, 0, 3, 1, 4)
    lbd = lb.reshape(2, 1, HGRN_HEADS, 1, HGRN_DK)
    log_f = jnp.log(lbd + (1.0 - lbd) * jax.nn.sigmoid(zf))
    k_c = (1.0 - lbd) * jax.nn.sigmoid(-zf)
    outs, s_c = [], []
    for d in range(2):
        o, s = _hgrn2_chunked(_flip(qc, d, 2), _flip(k_c[d], d, 2), _flip(vc, d, 2), _flip(log_f[d], d, 2),
                              s_hgrn[:, d].astype(f32))
        outs.append(_flip(o, d, 2))
        s_c.append(s)
    o_c = _rms((outs[0] + outs[1]).transpose(0, 2, 1, 3), hgrn_norm) * jax.nn.silu(
        c_g.reshape(nb, L, HGRN_HEADS, HGRN_DV))
    o_c = o_c.reshape(nb * L, BRANCH_W)
    return (o_a, o_b, o_c), (jnp.stack(s_a, axis=1), jnp.stack(s_b, axis=1), jnp.stack(s_c, axis=1))


def _ec_moe(t, aff, w_gu, w_down):
    cap = CAPACITY_FACTOR * t.shape[0] // N_EXPERTS
    gate, idx = lax.top_k(aff.T, cap)
    xe = t[idx]
    gu = _bmm(xe, w_gu)
    g, u = jnp.split(gu, 2, axis=-1)
    ye = _bmm(jax.nn.silu(g) * u, w_down) * gate[..., None]
    return jnp.zeros_like(t).at[idx.reshape(-1)].add(ye.reshape(-1, t.shape[-1]))


def kernel(x_prompt, x_sample, c, state_gdn, state_ssd, state_hgrn, c_ctx, w_mod, b_mod, norm_gain,
           w_in, conv_w, conv_b, gdn_a_log, gdn_dt_bias, gdn_norm, ssd_a_log, ssd_dt_bias, ssd_d,
           ssd_norm, hgrn_lb, hgrn_norm, w_branch, w_out, w_router, w_gu, w_down):
    nb_c, seq_c, d = x_prompt.shape
    nb_s, seq_s, _ = x_sample.shape
    n_ctx = nb_c * seq_c
    n_smp = nb_s * seq_s

    lb_w = jax.nn.softmax(hgrn_lb.astype(F32), axis=0)
    lb = jnp.cumsum(lb_w, axis=0) - lb_w[:1]

    cond = jnp.concatenate([c_ctx[None, :], c, jnp.zeros((8 - 1 - nb_s, d), F32)], axis=0)
    mod_all = _mod_all(cond, w_mod, b_mod[:, None, :]).reshape(DEPTH, 8, N_MOD, d)

    x = jnp.concatenate([x_prompt.reshape(n_ctx, d), x_sample.reshape(n_smp, d)], axis=0)
    z_gdn = jnp.zeros((nb_c, 2, GDN_HEADS, GDN_DK, GDN_DV), F32)
    z_ssd = jnp.zeros((nb_c, 2, SSD_HEADS, SSD_N, SSD_P), F32)
    z_hgrn = jnp.zeros((nb_c, 2, HGRN_HEADS, HGRN_DK, HGRN_DV), F32)
    new_gdn, new_ssd, new_hgrn = [], [], []

    for l in range(DEPTH):
        mod = mod_all[l]
        wl = w_in[l]
        w_conv = wl[:, :CONV_CH].astype(BF16)
        w_rest = jnp.concatenate([wl[:, OFF_M_G:], wl[:, OFF_A_G:OFF_ALPHA], wl[:, OFF_B_Z:OFF_B_DT],
                                  wl[:, OFF_C_Q:OFF_M_G]], axis=1).astype(BF16)
        w_small = jnp.concatenate([wl[:, OFF_ALPHA:OFF_B_Z], wl[:, OFF_B_DT:OFF_C_Q],
                                   jnp.zeros((d, SMALL_W - 4 * GDN_HEADS - 2 * SSD_HEADS), F32)], axis=1)

        h = _hmod(x, norm_gain[l, 0:1], mod, n_ctx, seq_s, i_shift=0, i_scale=1)
        cx = _mm_conv(h, w_conv, conv_w[l], conv_b[l][None, :], n_ctx, seq_c, GRID_W)
        rest = _mm_plain(h, w_rest, tn=1024)
        small = _mm_small(h, w_small)

        mix_args = (lb[l], gdn_a_log[l], gdn_dt_bias[l], gdn_norm[l], ssd_a_log[l], ssd_dt_bias[l], ssd_d[l],
                    ssd_norm[l], hgrn_norm[l])
        (oa_c, ob_c, oc_c), (sg, ss, sh) = _mixers_jnp(
            cx[:n_ctx], rest[:n_ctx], small[:n_ctx], nb_c, seq_c, z_gdn, z_ssd, z_hgrn, *mix_args)
        (oa_s, ob_s, oc_s), _ = _mixers_jnp(
            cx[n_ctx:], rest[n_ctx:], small[n_ctx:], nb_s, seq_s, state_gdn[:, l], state_ssd[:, l],
            state_hgrn[:, l], *mix_args)
        new_gdn.append(sg)
        new_ssd.append(ss)
        new_hgrn.append(sh)
        branches = jnp.stack([jnp.concatenate([oa_c, oa_s]), jnp.concatenate([ob_c, ob_s]),
                              jnp.concatenate([oc_c, oc_s])]).astype(BF16)

        merged = _merge(branches, rest, w_branch[l].astype(BF16))
        w_router_p = jnp.concatenate([w_router[l], jnp.zeros((d, LANE - N_EXPERTS), F32)], axis=1)
        x1, h2, aff = _outproj(merged, w_out[l].astype(BF16), x, norm_gain[l], mod, w_router_p, n_ctx, seq_s)

        moe_c = _ec_moe(h2[:n_ctx], aff[:n_ctx, :N_EXPERTS], w_gu[l], w_down[l])
        moe_s = _ec_moe(h2[n_ctx:], aff[n_ctx:, :N_EXPERTS], w_gu[l], w_down[l])
        moe = jnp.concatenate([moe_c, moe_s])
        gate2 = jnp.concatenate([jnp.broadcast_to(mod[0, 5], (n_ctx, d)),
                                 jnp.repeat(mod[1:1 + nb_s, 5], seq_s, axis=0)])
        x = x1 + gate2 * _rms(moe, norm_gain[l, 3])

    y_prompt = x[:n_ctx].reshape(nb_c, seq_c, d)
    y_sample = x[n_ctx:].reshape(nb_s, seq_s, d)
    return (y_prompt, y_sample, jnp.stack(new_gdn, axis=1), jnp.stack(new_ssd, axis=1),
            jnp.stack(new_hgrn, axis=1))
```

```python
import functools
import math

import numpy as np
import jax
import jax.numpy as jnp
from jax import lax
from jax.experimental import pallas as pl
from jax.experimental.pallas import tpu as pltpu

D_MODEL = 2048
DEPTH = 2
GRID_W = 64
N_BRANCH = 3
BRANCH_W = D_MODEL // 2
GDN_DK = 128
GDN_DV = 128
GDN_HEADS = BRANCH_W // GDN_DV
GDN_CHUNK = 64
SSD_P = 64
SSD_HEADS = BRANCH_W // SSD_P
SSD_GROUPS = 2
SSD_HPG = SSD_HEADS // SSD_GROUPS
SSD_N = 128
SSD_CHUNK = 64
HGRN_DK = 128
HGRN_DV = 128
HGRN_HEADS = BRANCH_W // HGRN_DV
HGRN_CHUNK = 16
CONV_K = 3
N_EXPERTS = 16
EXPERT_FF = D_MODEL // 2
CAPACITY_FACTOR = 2
N_MOD = 6
EPS = 1e-6
GDN_QK_W = GDN_HEADS * GDN_DK
HGRN_QK_W = HGRN_HEADS * HGRN_DK
CONV_SIZES = (GDN_QK_W, GDN_QK_W, BRANCH_W, BRANCH_W, SSD_GROUPS * SSD_N, SSD_GROUPS * SSD_N)
CONV_CH = GDN_QK_W + GDN_QK_W + BRANCH_W + BRANCH_W + 2 * SSD_GROUPS * SSD_N
OFF_A_G = CONV_CH
OFF_ALPHA = OFF_A_G + BRANCH_W
OFF_BETA = OFF_ALPHA + 2 * GDN_HEADS
OFF_B_Z = OFF_BETA + 2 * GDN_HEADS
OFF_B_DT = OFF_B_Z + BRANCH_W
OFF_C_Q = OFF_B_DT + 2 * SSD_HEADS
IN_COLS = OFF_C_Q + 3 * HGRN_QK_W + 2 * BRANCH_W + N_BRANCH * D_MODEL
OFF_M_G = IN_COLS - N_BRANCH * D_MODEL
REST_W = IN_COLS - CONV_CH - 4 * GDN_HEADS - 2 * SSD_HEADS
R_M_G, R_A_G, R_B_Z, R_C_Q, R_C_F, R_C_I, R_C_G = 0, 6144, 7168, 8192, 9216, 11264, 12288
SMALL_W = 128

LANE = 128
VMEM_LIMIT = 48 * 1024 * 1024
ROW_TILE = 1024
SEQ_BLOCK = 256

BF16 = jnp.bfloat16
F32 = jnp.float32


def _cparams(sem):
    return pltpu.CompilerParams(dimension_semantics=sem, vmem_limit_bytes=VMEM_LIMIT)


def _mod_row(i, tm, n_ctx_rows, dec_seq):
    return jnp.maximum((i * tm - n_ctx_rows) // dec_seq + 1, 0)


def _mod_kernel(c_ref, w_ref, b_ref, o_ref):
    c = c_ref[...]
    a = c * jax.nn.sigmoid(c)
    o_ref[0] = jnp.dot(a, w_ref[0], preferred_element_type=F32, precision=lax.Precision.HIGHEST) + b_ref[0]


def _mod_all(cond, w_mod, b_mod, tn=1024):
    nl, d, n = w_mod.shape
    return pl.pallas_call(
        _mod_kernel,
        grid=(nl, n // tn),
        in_specs=[pl.BlockSpec((8, d), lambda l, j: (0, 0)),
                  pl.BlockSpec((1, d, tn), lambda l, j: (l, 0, j)),
                  pl.BlockSpec((1, 1, tn), lambda l, j: (l, 0, j))],
        out_specs=pl.BlockSpec((1, 8, tn), lambda l, j: (l, 0, j)),
        out_shape=jax.ShapeDtypeStruct((nl, 8, n), F32),
        compiler_params=_cparams(("parallel", "arbitrary")),
        name="mod_vectors",
    )(cond, w_mod, b_mod)


def _rms_f32(x, gain):
    return x * lax.rsqrt(jnp.mean(x * x, axis=-1, keepdims=True) + EPS) * gain


def _hmod_kernel(x_ref, g_ref, mod_ref, h_ref, *, i_shift, i_scale):
    x = x_ref[...]
    y = _rms_f32(x, g_ref[...])
    h_ref[...] = (y * (1.0 + mod_ref[0, i_scale:i_scale + 1, :]) + mod_ref[0, i_shift:i_shift + 1, :]).astype(BF16)


def _hmod(x, gain, mod, n_ctx_rows, dec_seq, i_shift, i_scale, tm=512):
    t, d = x.shape
    return pl.pallas_call(
        functools.partial(_hmod_kernel, i_shift=i_shift, i_scale=i_scale),
        grid=(t // tm,),
        in_specs=[pl.BlockSpec((tm, d), lambda i: (i, 0)),
                  pl.BlockSpec((1, d), lambda i: (0, 0)),
                  pl.BlockSpec((1, N_MOD, d), lambda i: (_mod_row(i, tm, n_ctx_rows, dec_seq), 0, 0))],
        out_specs=pl.BlockSpec((tm, d), lambda i: (i, 0)),
        out_shape=jax.ShapeDtypeStruct((t, d), BF16),
        compiler_params=_cparams(("parallel",)),
        name="rms_modulate",
    )(x, gain, mod)


def _mm_plain_kernel(h_ref, w_ref, o_ref):
    o_ref[...] = jnp.dot(h_ref[...], w_ref[...], preferred_element_type=F32).astype(o_ref.dtype)


def _mm_plain(h, w, tn, out_dtype=BF16, tm=ROW_TILE):
    t, k = h.shape
    n = w.shape[1]
    return pl.pallas_call(
        _mm_plain_kernel,
        grid=(t // tm, n // tn),
        in_specs=[pl.BlockSpec((tm, k), lambda i, j: (i, 0)),
                  pl.BlockSpec((k, tn), lambda i, j: (0, j))],
        out_specs=pl.BlockSpec((tm, tn), lambda i, j: (i, j)),
        out_shape=jax.ShapeDtypeStruct((t, n), out_dtype),
        compiler_params=_cparams(("parallel", "arbitrary")),
        name="proj_rest",
    )(h, w)


def _mm_small_kernel(h_ref, w_ref, o_ref):
    o_ref[...] = jnp.dot(h_ref[...].astype(F32), w_ref[...], preferred_element_type=F32,
                         precision=lax.Precision.HIGHEST)


def _mm_small(h, w, tm=256):
    t, k = h.shape
    n = w.shape[1]
    return pl.pallas_call(
        _mm_small_kernel,
        grid=(t // tm,),
        in_specs=[pl.BlockSpec((tm, k), lambda i: (i, 0)),
                  pl.BlockSpec((k, n), lambda i: (0, 0))],
        out_specs=pl.BlockSpec((tm, n), lambda i: (i, 0)),
        out_shape=jax.ShapeDtypeStruct((t, n), F32),
        compiler_params=_cparams(("parallel",)),
        name="proj_small",
    )(h, w)


def _mm_conv_kernel(h_ref, w_ref, cw_ref, cb_ref, o_ref, *, n_ctx_tiles, ctx_period, grid_period):
    acc = jnp.dot(h_ref[...], w_ref[...], preferred_element_type=F32)
    tm = acc.shape[0]
    period = jnp.where(pl.program_id(0) < n_ctx_tiles, ctx_period, grid_period)
    pos = lax.broadcasted_iota(jnp.int32, (tm, 1), 0) & (period - 1)
    prev = jnp.where(pos == 0, 0.0, pltpu.roll(acc, 1, 0))
    nxt = jnp.where(pos == period - 1, 0.0, pltpu.roll(acc, tm - 1, 0))
    y = prev * cw_ref[0:1, :] + acc * cw_ref[1:2, :] + nxt * cw_ref[2:3, :] + cb_ref[...]
    o_ref[...] = (y * jax.nn.sigmoid(y)).astype(o_ref.dtype)


def _mm_conv(h, w, conv_w, conv_b, n_ctx_rows, ctx_period, grid_period, tn=1152, tm=512):
    t, k = h.shape
    n = w.shape[1]
    return pl.pallas_call(
        functools.partial(_mm_conv_kernel, n_ctx_tiles=n_ctx_rows // tm, ctx_period=ctx_period,
                          grid_period=grid_period),
        grid=(t // tm, n // tn),
        in_specs=[pl.BlockSpec((tm, k), lambda i, j: (i, 0)),
                  pl.BlockSpec((k, tn), lambda i, j: (0, j)),
                  pl.BlockSpec((CONV_K, tn), lambda i, j: (0, j)),
                  pl.BlockSpec((1, tn), lambda i, j: (0, j))],
        out_specs=pl.BlockSpec((tm, tn), lambda i, j: (i, j)),
        out_shape=jax.ShapeDtypeStruct((t, n), BF16),
        compiler_params=_cparams(("parallel", "arbitrary")),
        name="proj_conv",
    )(h, w, conv_w, conv_b)


def _merge_kernel(o_ref, g_ref, w_ref, out_ref, acc_ref):
    n = pl.program_id(1)
    p = jnp.dot(o_ref[0], w_ref[0], preferred_element_type=F32)
    contrib = jax.nn.sigmoid(g_ref[...].astype(F32)) * p

    @pl.when(n == 0)
    def _():
        acc_ref[...] = contrib

    @pl.when(n > 0)
    def _():
        acc_ref[...] += contrib

    @pl.when(n == N_BRANCH - 1)
    def _():
        out_ref[...] = acc_ref[...].astype(out_ref.dtype)


def _merge(branches, rest, w_branch, tm=ROW_TILE):
    _, t, kb = branches.shape
    d = w_branch.shape[-1]
    g_blk = R_M_G // d
    return pl.pallas_call(
        _merge_kernel,
        grid=(t // tm, N_BRANCH),
        in_specs=[pl.BlockSpec((1, tm, kb), lambda i, n: (n, i, 0)),
                  pl.BlockSpec((tm, d), lambda i, n: (i, g_blk + n)),
                  pl.BlockSpec((1, kb, d), lambda i, n: (n, 0, 0))],
        out_specs=pl.BlockSpec((tm, d), lambda i, n: (i, 0)),
        out_shape=jax.ShapeDtypeStruct((t, d), BF16),
        scratch_shapes=[pltpu.VMEM((tm, d), F32)],
        compiler_params=_cparams(("parallel", "arbitrary")),
        name="branch_merge",
    )(branches, rest, w_branch)


def _outproj_kernel(m_ref, w_ref, x_ref, g_ref, mod_ref, wr_ref, x1_ref, h2_ref, aff_ref):
    out = jnp.dot(m_ref[...], w_ref[...], preferred_element_type=F32)
    x1 = x_ref[...] + mod_ref[0, 2:3, :] * _rms_f32(out, g_ref[1:2, :])
    x1_ref[...] = x1
    h2 = _rms_f32(x1, g_ref[2:3, :]) * (1.0 + mod_ref[0, 4:5, :]) + mod_ref[0, 3:4, :]
    h2_ref[...] = h2.astype(h2_ref.dtype)
    logits = jnp.dot(h2, wr_ref[...], preferred_element_type=F32, precision=lax.Precision.HIGHEST)
    lane = lax.broadcasted_iota(jnp.int32, logits.shape, 1)
    logits = jnp.where(lane < N_EXPERTS, logits, -jnp.inf)
    e = jnp.exp(logits - jnp.max(logits, axis=-1, keepdims=True))
    aff_ref[...] = e / jnp.sum(e, axis=-1, keepdims=True)


def _outproj(merged, w_out, x, gains, mod, w_router_p, n_ctx_rows, dec_seq, tm=256):
    t, d = x.shape
    return pl.pallas_call(
        _outproj_kernel,
        grid=(t // tm,),
        in_specs=[pl.BlockSpec((tm, d), lambda i: (i, 0)),
                  pl.BlockSpec((d, d), lambda i: (0, 0)),
                  pl.BlockSpec((tm, d), lambda i: (i, 0)),
                  pl.BlockSpec((4, d), lambda i: (0, 0)),
                  pl.BlockSpec((1, N_MOD, d), lambda i: (_mod_row(i, tm, n_ctx_rows, dec_seq), 0, 0)),
                  pl.BlockSpec((d, LANE), lambda i: (0, 0))],
        out_specs=[pl.BlockSpec((tm, d), lambda i: (i, 0)),
                   pl.BlockSpec((tm, d), lambda i: (i, 0)),
                   pl.BlockSpec((tm, LANE), lambda i: (i, 0))],
        out_shape=[jax.ShapeDtypeStruct((t, d), F32),
                   jax.ShapeDtypeStruct((t, d), F32),
                   jax.ShapeDtypeStruct((t, LANE), F32)],
        compiler_params=_cparams(("parallel",)),
        name="out_proj",
    )(merged, w_out, x, gains, mod, w_router_p)


def _bmm_kernel(a_ref, b_ref, o_ref):
    a = a_ref[0].astype(BF16)
    b = b_ref[0].astype(BF16)
    o_ref[0] = jnp.dot(a, b, preferred_element_type=F32)


def _bmm(a, b, tm=512, tn=512):
    e, m, k = a.shape
    n = b.shape[-1]
    tm = min(tm, m)
    tn = min(tn, n)
    return pl.pallas_call(
        _bmm_kernel,
        grid=(e, pl.cdiv(m, tm), pl.cdiv(n, tn)),
        in_specs=[pl.BlockSpec((1, tm, k), lambda g, i, j: (g, i, 0)),
                  pl.BlockSpec((1, k, tn), lambda g, i, j: (g, 0, j))],
        out_specs=pl.BlockSpec((1, tm, tn), lambda g, i, j: (g, i, j)),
        out_shape=jax.ShapeDtypeStruct((e, m, n), F32),
        compiler_params=_cparams(("parallel", "parallel", "arbitrary")),
        name="bmm",
    )(a, b)


CHUNK = 64
CHUNK_SHIFT = 6


def _dot(a, b):
    return jnp.dot(a.astype(BF16), b.astype(BF16), preferred_element_type=F32)


def _dot_nt(a, b):
    return lax.dot_general(a.astype(BF16), b.astype(BF16), (((1,), (1,)), ((), ())),
                           preferred_element_type=F32)


def _dot_tn(a, b):
    return lax.dot_general(a.astype(BF16), b.astype(BF16), (((0,), (0,)), ((), ())),
                           preferred_element_type=F32)


def _split3(x):
    x1 = x.astype(BF16)
    r1 = x - x1.astype(F32)
    x2 = r1.astype(BF16)
    x3 = (r1 - x2.astype(F32)).astype(BF16)
    return x1, x2, x3


def _dot3_left(mask_bf16, x):
    x1, x2, x3 = _split3(x)
    f = lambda p: jnp.dot(mask_bf16, p, preferred_element_type=F32)
    return f(x3) + f(x2) + f(x1)


def _dot3_right(x, mask_bf16):
    x1, x2, x3 = _split3(x)
    f = lambda p: jnp.dot(p, mask_bf16, preferred_element_type=F32)
    return f(x3) + f(x2) + f(x1)


def _softplus(x):
    return jnp.maximum(x, 0.0) + jnp.log(1.0 + jnp.exp(-jnp.abs(x)))


def _silu(x):
    return x * jax.nn.sigmoid(x)


def _chunk_tri(n, upper):
    r = lax.broadcasted_iota(jnp.int32, (n, n), 0)
    c = lax.broadcasted_iota(jnp.int32, (n, n), 1)
    same = (r >> CHUNK_SHIFT) == (c >> CHUNK_SHIFT)
    tri = (c >= r) if upper else (c <= r)
    return jnp.where(same & tri, 1.0, 0.0).astype(BF16)


def _unit_tri_inverse(m):
    n = m.shape[0]
    r = lax.broadcasted_iota(jnp.int32, (n, n), 0)
    c = lax.broadcasted_iota(jnp.int32, (n, n), 1)
    eye = jnp.where(r == c, 1.0, 0.0)
    same = lambda s: (r >> s) == (c >> s)
    m8 = jnp.where(same(3), m, 0.0)
    m2 = _dot(m8, m8)
    m4 = _dot(m2, m2)
    p = eye - m8
    p = p + _dot(p, m2)
    x = p + _dot(p, m4)
    s = 3
    while (1 << s) < n:
        off = jnp.where(same(s + 1) & jnp.logical_not(same(s)), m, 0.0)
        x = x - _dot(_dot(x, off), x)
        s += 1
    return x


def _gdn_kernel(*refs, zero_init):
    if zero_init:
        (q_ref, k_ref, v_ref, ag_ref, sc_ref, sr_ref, pc_ref, pr_ref, norm_ref,
         o_ref, sfin_ref, qn_ref, kn_ref, gcf_ref, gcb_ref, grf_ref, grb_ref, of_ref, ob_ref) = refs
        s0_ref = None
    else:
        (q_ref, k_ref, v_ref, ag_ref, sc_ref, sr_ref, pc_ref, pr_ref, norm_ref, s0_ref,
         o_ref, sfin_ref, qn_ref, kn_ref, gcf_ref, gcb_ref, grf_ref, grb_ref, of_ref, ob_ref) = refs
    L = q_ref.shape[0]
    n_chunks = L // CHUNK

    qf = q_ref[...].astype(F32)
    qn_ref[...] = qf * lax.rsqrt(jnp.sum(qf * qf, axis=-1, keepdims=True) + EPS) * (GDN_DK ** -0.5)
    kf = k_ref[...].astype(F32)
    kn_ref[...] = kf * lax.rsqrt(jnp.sum(kf * kf, axis=-1, keepdims=True) + EPS)

    gcf_ref[...] = jnp.zeros_like(gcf_ref)
    gcf_ref[:, 0:4] = -jnp.exp(pc_ref[0, 0:1, 0:4]) * _softplus(sc_ref[0] + pc_ref[0, 1:2, 0:4])
    blk = min(L, 256)
    lo_blk = _chunk_tri(blk, upper=False)
    up_blk = _chunk_tri(blk, upper=True)
    for b in range(L // blk):
        rows = slice(b * blk, (b + 1) * blk)
        g_col = gcf_ref[rows, :]
        gcb_ref[rows, :] = _dot3_left(up_blk, g_col)
        gcf_ref[rows, :] = _dot3_left(lo_blk, g_col)
    g_row = -jnp.exp(pr_ref[0, 0]) * _softplus(sr_ref[0] + pr_ref[0, 1])
    g_row = g_row.reshape(n_chunks * 8, CHUNK)
    up_c = _chunk_tri(CHUNK, upper=True)
    lo_c = _chunk_tri(CHUNK, upper=False)
    grf_ref[...] = _dot3_right(g_row, up_c).reshape(n_chunks, 8, CHUNK)
    grb_ref[...] = _dot3_right(g_row, lo_c).reshape(n_chunks, 8, CHUNK)

    ti = lax.broadcasted_iota(jnp.int32, (CHUNK, CHUNK), 0)
    si = lax.broadcasted_iota(jnp.int32, (CHUNK, CHUNK), 1)

    def step(d, c, s_state):
        rows = pl.ds(pl.multiple_of(c * CHUNK, CHUNK), CHUNK)
        q = qn_ref[rows, :]
        k = kn_ref[rows, :]
        v = v_ref[rows, :].astype(F32)
        gc = (gcb_ref if d else gcf_ref)[rows, d:d + 1]
        beta = jax.nn.sigmoid(sc_ref[0, rows, 2 + d:3 + d])
        gcr = (grb_ref if d else grf_ref)[c][d:d + 1, :]
        incl = (si >= ti) if d else (si <= ti)
        strict = (si > ti) if d else (si < ti)
        decay = jnp.where(incl, jnp.exp(jnp.where(incl, gc - gcr, 0.0)), 0.0)
        kb = k * beta
        m = jnp.where(strict, _dot_nt(kb, k) * decay, 0.0)
        qk = _dot_nt(q, k) * decay
        x = _unit_tri_inverse(m)
        egc = jnp.exp(gc)
        uw = _dot(x, jnp.concatenate([v * beta, kb * egc], axis=1))
        u = uw[:, :GDN_DV]
        w = uw[:, GDN_DV:]
        s_b = s_state.astype(BF16)
        v_new = u - _dot(w, s_b)
        o = _dot(q * egc, s_b) + _dot(qk, v_new)
        g_last = gc[0:1] if d else gc[CHUNK - 1:CHUNK]
        (ob_ref if d else of_ref)[rows, :] = o
        return s_state * jnp.exp(g_last) + _dot_tn(k * jnp.exp(g_last - gc), v_new)

    def body(i, carry):
        return step(0, i, carry[0]), step(1, n_chunks - 1 - i, carry[1])

    if zero_init:
        init = (jnp.zeros((GDN_DK, GDN_DV), F32), jnp.zeros((GDN_DK, GDN_DV), F32))
    else:
        init = (s0_ref[0, 0, 0], s0_ref[0, 1, 0])
    s_f, s_b = lax.fori_loop(0, n_chunks, body, init)
    sfin_ref[0, 0, 0] = s_f
    sfin_ref[0, 1, 0] = s_b

    o = of_ref[...] + ob_ref[...]
    y = _rms_f32(o, norm_ref[...])
    o_ref[...] = (y * _silu(ag_ref[...].astype(F32))).astype(o_ref.dtype)


def _gdn(cx, rest, sc, sr, pc, pr, norm, s0, *, n_seq, seq_len, row0):
    L = seq_len
    blk0 = row0 // L
    nh = GDN_HEADS
    zero_init = s0 is None
    k_off = GDN_QK_W // LANE
    v_off = 2 * GDN_QK_W // LANE
    ag_off = R_A_G // LANE
    in_specs = [pl.BlockSpec((L, LANE), lambda s, h: (blk0 + s, h)),
                pl.BlockSpec((L, LANE), lambda s, h: (blk0 + s, k_off + h)),
                pl.BlockSpec((L, LANE), lambda s, h: (blk0 + s, v_off + h)),
                pl.BlockSpec((L, LANE), lambda s, h: (blk0 + s, ag_off + h)),
                pl.BlockSpec((1, L, 4), lambda s, h: (h, blk0 + s, 0)),
                pl.BlockSpec((1, L // CHUNK, 8, CHUNK), lambda s, h: (h, blk0 + s, 0, 0)),
                pl.BlockSpec((1, 8, LANE), lambda s, h: (h, 0, 0)),
                pl.BlockSpec((1, 2, 8, CHUNK), lambda s, h: (h, 0, 0, 0)),
                pl.BlockSpec((1, GDN_DV), lambda s, h: (0, 0))]
    args = [cx, cx, cx, rest, sc, sr, pc, pr, norm]
    state_spec = pl.BlockSpec((1, 2, 1, GDN_DK, GDN_DV), lambda s, h: (s, 0, h, 0, 0))
    if not zero_init:
        in_specs.append(state_spec)
        args.append(s0)
    return pl.pallas_call(
        functools.partial(_gdn_kernel, zero_init=zero_init),
        grid=(n_seq, nh),
        in_specs=in_specs,
        out_specs=[pl.BlockSpec((L, GDN_DV), lambda s, h: (s, h)), state_spec],
        out_shape=[jax.ShapeDtypeStruct((n_seq * L, BRANCH_W), BF16),
                   jax.ShapeDtypeStruct((n_seq, 2, nh, GDN_DK, GDN_DV), F32)],
        scratch_shapes=[pltpu.VMEM((L, GDN_DK), F32), pltpu.VMEM((L, GDN_DK), F32),
                        pltpu.VMEM((L, LANE), F32), pltpu.VMEM((L, LANE), F32),
                        pltpu.VMEM((L // CHUNK, 8, CHUNK), F32), pltpu.VMEM((L // CHUNK, 8, CHUNK), F32),
                        pltpu.VMEM((L, GDN_DV), F32), pltpu.VMEM((L, GDN_DV), F32)],
        compiler_params=_cparams(("parallel", "arbitrary")),
        name="gdn_scan",
    )(*args)


def _gdn_side_inputs(small, gdn_a_log, gdn_dt_bias):
    t = small.shape[0]
    nh = GDN_HEADS
    ab = small[:, :4 * nh].reshape(t, 2, 2, nh)
    sc = ab.transpose(3, 0, 1, 2).reshape(nh, t, 4)
    rows = ab.transpose(3, 1, 2, 0).reshape(nh, 4, t // CHUNK, CHUNK).transpose(0, 2, 1, 3)
    sr = jnp.concatenate([rows, jnp.zeros_like(rows)], axis=2)
    pc = jnp.zeros((nh, 8, LANE), F32)
    pc = pc.at[:, 0, 0:2].set(gdn_a_log.T).at[:, 1, 0:2].set(gdn_dt_bias.T)
    pr = jnp.zeros((nh, 2, 8, CHUNK), F32)
    pr = pr.at[:, 0, 0:2, :].set(jnp.broadcast_to(gdn_a_log.T[:, :, None], (nh, 2, CHUNK)))
    pr = pr.at[:, 1, 0:2, :].set(jnp.broadcast_to(gdn_dt_bias.T[:, :, None], (nh, 2, CHUNK)))
    return sc, sr, pc, pr


def _rms(x, gain):
    xf = x.astype(jnp.float32)
    y = xf * lax.rsqrt(jnp.mean(xf * xf, axis=-1, keepdims=True) + EPS)
    return (y * gain.astype(jnp.float32)).astype(x.dtype)


def _l2n(x):
    return x * lax.rsqrt(jnp.sum(x * x, axis=-1, keepdims=True) + EPS)


def _flip(t, d, axis):
    return jnp.flip(t, axis=axis) if d else t


def _masked_exp(diff, mask):
    return jnp.where(mask, jnp.exp(jnp.where(mask, diff, 0.0)), 0.0)


def _gated_delta_chunked(q, k, v, g, beta, s0):
    nb, nh, L, dk = q.shape
    C = GDN_CHUNK
    n = L // C

    def chunks(t):
        return t.reshape(nb, nh, n, C, *t.shape[3:])

    q, k, v, g, beta = chunks(q), chunks(k), chunks(v), chunks(g), chunks(beta)
    gc = jnp.cumsum(g, axis=-1)
    causal = jnp.tril(jnp.ones((C, C), dtype=bool))
    decay = _masked_exp(gc[..., :, None] - gc[..., None, :], causal)
    kb = k * beta[..., None]
    m_strict = jnp.einsum('bhntd,bhnsd->bhnts', kb, k) * decay * jnp.tril(jnp.ones((C, C), q.dtype), -1)
    a = m_strict + jnp.eye(C, dtype=q.dtype)
    u = jax.lax.linalg.triangular_solve(a, v * beta[..., None], left_side=True, lower=True, unit_diagonal=True)
    w = jax.lax.linalg.triangular_solve(a, kb * jnp.exp(gc)[..., None], left_side=True, lower=True,
                                        unit_diagonal=True)
    qk = jnp.einsum('bhntd,bhnsd->bhnts', q, k) * decay

    def step(S, xs):
        q_c, k_c, u_c, w_c, gc_c, qk_c = xs
        v_new = u_c - jnp.einsum('bhcd,bhde->bhce', w_c, S)
        o = (jnp.einsum('bhcd,bhde->bhce', q_c * jnp.exp(gc_c)[..., None], S)
             + jnp.einsum('bhts,bhse->bhte', qk_c, v_new))
        g_last = gc_c[..., -1:]
        S = S * jnp.exp(g_last)[..., None] + jnp.einsum(
            'bhcd,bhce->bhde', k_c * jnp.exp(g_last - gc_c)[..., None], v_new)
        return S, o

    xs = tuple(jnp.moveaxis(t, 2, 0) for t in (q, k, u, w, gc, qk))
    s_fin, o = lax.scan(step, s0, xs)
    return jnp.moveaxis(o, 0, 2).reshape(nb, nh, L, -1), s_fin


def _ssd_chunked(x, dt, a, bm, cm, h0):
    nb, L = x.shape[:2]
    C = SSD_CHUNK
    n = L // C

    def chunks(t):
        return t.reshape(nb, n, C, *t.shape[2:])

    x, dt, bm, cm = chunks(x), chunks(dt), chunks(bm), chunks(cm)
    acs = jnp.cumsum(dt * a, axis=2)
    causal = jnp.tril(jnp.ones((C, C), dtype=bool))[:, :, None, None]
    seg = _masked_exp(acs[:, :, :, None] - acs[:, :, None, :], causal)
    xdt = x * dt[..., None]
    cb = jnp.einsum('bntgk,bnsgk->bntsg', cm, bm)
    y_diag = jnp.einsum('bntsg,bntsgr,bnsgrp->bntgrp', cb, seg, xdt)
    st_local = jnp.einsum('bncgk,bncgr,bncgrp->bngrkp', bm, jnp.exp(acs[:, :, -1:] - acs), xdt)
    chunk_decay = jnp.exp(acs[:, :, -1])

    def step(h, xs):
        st, cd = xs
        return h * cd[..., None, None] + st, h

    h_fin, h_in = lax.scan(step, h0, (jnp.moveaxis(st_local, 1, 0), jnp.moveaxis(chunk_decay, 1, 0)))
    y_off = jnp.einsum('bncgk,nbgrkp,bncgr->bncgrp', cm, h_in, jnp.exp(acs))
    return (y_diag + y_off).reshape(nb, L, *x.shape[3:]), h_fin


def _hgrn2_chunked(q, k, v, log_f, s0):
    nb, nh, L, dk = q.shape
    C = HGRN_CHUNK
    n = L // C
    causal = jnp.tril(jnp.ones((C, C), dtype=bool))[:, :, None]

    def chunks(t):
        return jnp.moveaxis(t.reshape(nb, nh, n, C, t.shape[-1]), 2, 0)

    def step(S, xs):
        q_c, k_c, v_c, lf_c = xs
        G = jnp.cumsum(lf_c, axis=2)
        G_last = G[:, :, -1:]
        dec = _masked_exp(G[:, :, :, None] - G[:, :, None], causal)
        att = jnp.einsum('bhtd,bhsd,bhtsd->bhts', q_c, k_c, dec)
        o = (jnp.einsum('bhtd,bhde->bhte', q_c * jnp.exp(G), S)
             + jnp.einsum('bhts,bhse->bhte', att, v_c))
        S = S * jnp.exp(G_last)[:, :, 0, :, None] + jnp.einsum(
            'bhsd,bhse->bhde', k_c * jnp.exp(G_last - G), v_c)
        return S, o

    s_fin, o = lax.scan(step, s0, (chunks(q), chunks(k), chunks(v), chunks(log_f)))
    return jnp.moveaxis(o, 0, 2).reshape(nb, nh, L, -1), s_fin


def _mixers_jnp(cx, rest, small, nb, L, s_gdn, s_ssd, s_hgrn, lb, gdn_a_log, gdn_dt_bias, gdn_norm,
                ssd_a_log, ssd_dt_bias, ssd_d, ssd_norm, hgrn_norm, do="abc"):
    f32 = jnp.float32
    cx = cx.astype(f32).reshape(nb, L, -1)
    rest = rest.astype(f32).reshape(nb, L, -1)
    small = small.reshape(nb, L, -1)
    a_q, a_k, a_v, b_x, b_b, b_c = jnp.split(cx, np.cumsum(CONV_SIZES)[:-1].tolist(), axis=-1)
    a_g = rest[..., R_A_G:R_A_G + 1024]
    b_z = rest[..., R_B_Z:R_B_Z + 1024]
    c_q = rest[..., R_C_Q:R_C_Q + 1024]
    c_f = rest[..., R_C_F:R_C_F + 2048]
    c_i = rest[..., R_C_I:R_C_I + 1024]
    c_g = rest[..., R_C_G:R_C_G + 1024]
    a_alpha = small[..., 0:16]
    a_beta = small[..., 16:32]
    b_dt = small[..., 32:64]

    def heads(t, n):
        return t.reshape(nb, L, n, -1).transpose(0, 2, 1, 3)

    q = _l2n(heads(a_q, GDN_HEADS)) * GDN_DK ** -0.5
    k = _l2n(heads(a_k, GDN_HEADS))
    v = heads(a_v, GDN_HEADS)
    alpha = a_alpha.reshape(nb, L, 2, GDN_HEADS).transpose(2, 0, 3, 1)
    beta = jax.nn.sigmoid(a_beta.reshape(nb, L, 2, GDN_HEADS).transpose(2, 0, 3, 1))
    g = -jnp.exp(gdn_a_log.astype(f32))[:, None, :, None] * jax.nn.softplus(
        alpha + gdn_dt_bias.astype(f32)[:, None, :, None])
    outs, s_a = [], []
    for d in range(2 if "a" in do else 0):
        o, s = _gated_delta_chunked(_flip(q, d, 2), _flip(k, d, 2), _flip(v, d, 2), _flip(g[d], d, 2),
                                    _flip(beta[d], d, 2), s_gdn[:, d].astype(f32))
        outs.append(_flip(o, d, 2))
        s_a.append(s)
    if "a" in do:
        o_a = _rms((outs[0] + outs[1]).transpose(0, 2, 1, 3), gdn_norm) * jax.nn.silu(
            a_g.reshape(nb, L, GDN_HEADS, GDN_DV))
        o_a = o_a.reshape(nb * L, BRANCH_W)
    else:
        o_a, s_a = None, [s_gdn[:, 0], s_gdn[:, 1]]

    xs = b_x.reshape(nb, L, SSD_GROUPS, SSD_HPG, SSD_P)
    bm = b_b.reshape(nb, L, SSD_GROUPS, SSD_N)
    cm = b_c.reshape(nb, L, SSD_GROUPS, SSD_N)
    dt = jax.nn.softplus(b_dt.reshape(nb, L, 2, SSD_GROUPS, SSD_HPG)
                         + ssd_dt_bias.astype(f32).reshape(2, SSD_GROUPS, SSD_HPG))
    a_ssd = -jnp.exp(ssd_a_log.astype(f32)).reshape(2, SSD_GROUPS, SSD_HPG)
    ys, s_b = [], []
    for d in range(2):
        y, s = _ssd_chunked(_flip(xs, d, 1), _flip(dt[:, :, d], d, 1), a_ssd[d], _flip(bm, d, 1),
                            _flip(cm, d, 1),
                            s_ssd[:, d].astype(f32).reshape(nb, SSD_GROUPS, SSD_HPG, SSD_N, SSD_P))
        ys.append(_flip(y, d, 1))
        s_b.append(s.reshape(nb, SSD_HEADS, SSD_N, SSD_P))
    y = ys[0] + ys[1] + ssd_d.astype(f32).reshape(SSD_GROUPS, SSD_HPG, 1) * xs
    y = y.reshape(nb, L, BRANCH_W) * jax.nn.silu(b_z)
    o_b = _rms(y.reshape(nb, L, SSD_GROUPS, BRANCH_W // SSD_GROUPS),
               ssd_norm.reshape(SSD_GROUPS, -1)).reshape(nb * L, BRANCH_W)

    qc = jax.nn.silu(heads(c_q, HGRN_HEADS))
    vc = heads(c_i, HGRN_HEADS)
    zf = c_f.reshape(nb, L, 2, HGRN_HEADS, HGRN_DK).transpose(2, 0, 3, 1, 4)
    lbd = lb.reshape(2, 1, HGRN_HEADS, 1, HGRN_DK)
    log_f = jnp.log(lbd + (1.0 - lbd) * jax.nn.sigmoid(zf))
    k_c = (1.0 - lbd) * jax.nn.sigmoid(-zf)
    outs, s_c = [], []
    for d in range(2):
        o, s = _hgrn2_chunked(_flip(qc, d, 2), _flip(k_c[d], d, 2), _flip(vc, d, 2), _flip(log_f[d], d, 2),
                              s_hgrn[:, d].astype(f32))
        outs.append(_flip(o, d, 2))
        s_c.append(s)
    o_c = _rms((outs[0] + outs[1]).transpose(0, 2, 1, 3), hgrn_norm) * jax.nn.silu(
        c_g.reshape(nb, L, HGRN_HEADS, HGRN_DV))
    o_c = o_c.reshape(nb * L, BRANCH_W)
    return (o_a, o_b, o_c), (jnp.stack(s_a, axis=1), jnp.stack(s_b, axis=1), jnp.stack(s_c, axis=1))


def _ec_moe(t, aff, w_gu, w_down):
    cap = CAPACITY_FACTOR * t.shape[0] // N_EXPERTS
    gate, idx = lax.top_k(aff.T, cap)
    xe = t[idx]
    gu = _bmm(xe, w_gu)
    g, u = jnp.split(gu, 2, axis=-1)
    ye = _bmm(jax.nn.silu(g) * u, w_down) * gate[..., None]
    return jnp.zeros_like(t).at[idx.reshape(-1)].add(ye.reshape(-1, t.shape[-1]))


def kernel(x_prompt, x_sample, c, state_gdn, state_ssd, state_hgrn, c_ctx, w_mod, b_mod, norm_gain,
           w_in, conv_w, conv_b, gdn_a_log, gdn_dt_bias, gdn_norm, ssd_a_log, ssd_dt_bias, ssd_d,
           ssd_norm, hgrn_lb, hgrn_norm, w_branch, w_out, w_router, w_gu, w_down):
    nb_c, seq_c, d = x_prompt.shape
    nb_s, seq_s, _ = x_sample.shape
    n_ctx = nb_c * seq_c
    n_smp = nb_s * seq_s

    lb_w = jax.nn.softmax(hgrn_lb.astype(F32), axis=0)
    lb = jnp.cumsum(lb_w, axis=0) - lb_w[:1]

    cond = jnp.concatenate([c_ctx[None, :], c, jnp.zeros((8 - 1 - nb_s, d), F32)], axis=0)
    mod_all = _mod_all(cond, w_mod, b_mod[:, None, :]).reshape(DEPTH, 8, N_MOD, d)

    x = jnp.concatenate([x_prompt.reshape(n_ctx, d), x_sample.reshape(n_smp, d)], axis=0)
    z_gdn = jnp.zeros((nb_c, 2, GDN_HEADS, GDN_DK, GDN_DV), F32)
    z_ssd = jnp.zeros((nb_c, 2, SSD_HEADS, SSD_N, SSD_P), F32)
    z_hgrn = jnp.zeros((nb_c, 2, HGRN_HEADS, HGRN_DK, HGRN_DV), F32)
    new_gdn, new_ssd, new_hgrn = [], [], []

    for l in range(DEPTH):
        mod = mod_all[l]
        wl = w_in[l]
        w_conv = wl[:, :CONV_CH].astype(BF16)
        w_rest = jnp.concatenate([wl[:, OFF_M_G:], wl[:, OFF_A_G:OFF_ALPHA], wl[:, OFF_B_Z:OFF_B_DT],
                                  wl[:, OFF_C_Q:OFF_M_G]], axis=1).astype(BF16)
        w_small = jnp.concatenate([wl[:, OFF_ALPHA:OFF_B_Z], wl[:, OFF_B_DT:OFF_C_Q],
                                   jnp.zeros((d, SMALL_W - 4 * GDN_HEADS - 2 * SSD_HEADS), F32)], axis=1)

        h = _hmod(x, norm_gain[l, 0:1], mod, n_ctx, seq_s, i_shift=0, i_scale=1)
        cx = _mm_conv(h, w_conv, conv_w[l], conv_b[l][None, :], n_ctx, seq_c, GRID_W)
        rest = _mm_plain(h, w_rest, tn=1024)
        small = _mm_small(h, w_small)

        mix_args = (lb[l], gdn_a_log[l], gdn_dt_bias[l], gdn_norm[l], ssd_a_log[l], ssd_dt_bias[l], ssd_d[l],
                    ssd_norm[l], hgrn_norm[l])
        gdn_side = _gdn_side_inputs(small, gdn_a_log[l], gdn_dt_bias[l])
        oa_c, sg = _gdn(cx, rest, *gdn_side, gdn_norm[l][None, :], None, n_seq=nb_c, seq_len=seq_c, row0=0)
        oa_s, _ = _gdn(cx, rest, *gdn_side, gdn_norm[l][None, :], state_gdn[:, l], n_seq=nb_s, seq_len=seq_s,
                       row0=n_ctx)
        (_, ob_c, oc_c), (_, ss, sh) = _mixers_jnp(
            cx[:n_ctx], rest[:n_ctx], small[:n_ctx], nb_c, seq_c, z_gdn, z_ssd, z_hgrn, *mix_args, do="bc")
        (_, ob_s, oc_s), _ = _mixers_jnp(
            cx[n_ctx:], rest[n_ctx:], small[n_ctx:], nb_s, seq_s, state_gdn[:, l], state_ssd[:, l],
            state_hgrn[:, l], *mix_args, do="bc")
        new_gdn.append(sg)
        new_ssd.append(ss)
        new_hgrn.append(sh)
        branches = jnp.stack([jnp.concatenate([oa_c, oa_s]), jnp.concatenate([ob_c, ob_s]).astype(BF16),
                              jnp.concatenate([oc_c, oc_s]).astype(BF16)])

        merged = _merge(branches, rest, w_branch[l].astype(BF16))
        w_router_p = jnp.concatenate([w_router[l], jnp.zeros((d, LANE - N_EXPERTS), F32)], axis=1)
        x1, h2, aff = _outproj(merged, w_out[l].astype(BF16), x, norm_gain[l], mod, w_router_p, n_ctx, seq_s)

        moe_c = _ec_moe(h2[:n_ctx], aff[:n_ctx, :N_EXPERTS], w_gu[l], w_down[l])
        moe_s = _ec_moe(h2[n_ctx:], aff[n_ctx:, :N_EXPERTS], w_gu[l], w_down[l])
        moe = jnp.concatenate([moe_c, moe_s])
        gate2 = jnp.concatenate([jnp.broadcast_to(mod[0, 5], (n_ctx, d)),
                                 jnp.repeat(mod[1:1 + nb_s, 5], seq_s, axis=0)])
        x = x1 + gate2 * _rms(moe, norm_gain[l, 3])

    y_prompt = x[:n_ctx].reshape(nb_c, seq_c, d)
    y_sample = x[n_ctx:].reshape(nb_s, seq_s, d)
    return (y_prompt, y_sample, jnp.stack(new_gdn, axis=1), jnp.stack(new_ssd, axis=1),
            jnp.stack(new_hgrn, axis=1))
```

```python
import functools
import math

import numpy as np
import jax
import jax.numpy as jnp
from jax import lax
from jax.experimental import pallas as pl
from jax.experimental.pallas import tpu as pltpu

D_MODEL = 2048
DEPTH = 2
GRID_W = 64
N_BRANCH = 3
BRANCH_W = D_MODEL // 2
GDN_DK = 128
GDN_DV = 128
GDN_HEADS = BRANCH_W // GDN_DV
GDN_CHUNK = 64
SSD_P = 64
SSD_HEADS = BRANCH_W // SSD_P
SSD_GROUPS = 2
SSD_HPG = SSD_HEADS // SSD_GROUPS
SSD_N = 128
SSD_CHUNK = 64
HGRN_DK = 128
HGRN_DV = 128
HGRN_HEADS = BRANCH_W // HGRN_DV
HGRN_CHUNK = 16
CONV_K = 3
N_EXPERTS = 16
EXPERT_FF = D_MODEL // 2
CAPACITY_FACTOR = 2
N_MOD = 6
EPS = 1e-6
GDN_QK_W = GDN_HEADS * GDN_DK
HGRN_QK_W = HGRN_HEADS * HGRN_DK
CONV_SIZES = (GDN_QK_W, GDN_QK_W, BRANCH_W, BRANCH_W, SSD_GROUPS * SSD_N, SSD_GROUPS * SSD_N)
CONV_CH = GDN_QK_W + GDN_QK_W + BRANCH_W + BRANCH_W + 2 * SSD_GROUPS * SSD_N
OFF_A_G = CONV_CH
OFF_ALPHA = OFF_A_G + BRANCH_W
OFF_BETA = OFF_ALPHA + 2 * GDN_HEADS
OFF_B_Z = OFF_BETA + 2 * GDN_HEADS
OFF_B_DT = OFF_B_Z + BRANCH_W
OFF_C_Q = OFF_B_DT + 2 * SSD_HEADS
IN_COLS = OFF_C_Q + 3 * HGRN_QK_W + 2 * BRANCH_W + N_BRANCH * D_MODEL
OFF_M_G = IN_COLS - N_BRANCH * D_MODEL
REST_W = IN_COLS - CONV_CH - 4 * GDN_HEADS - 2 * SSD_HEADS
R_M_G, R_A_G, R_B_Z, R_C_Q, R_C_F, R_C_I, R_C_G = 0, 6144, 7168, 8192, 9216, 11264, 12288
SMALL_W = 128

LANE = 128
VMEM_LIMIT = 48 * 1024 * 1024
ROW_TILE = 1024
SEQ_BLOCK = 256

BF16 = jnp.bfloat16
F32 = jnp.float32


def _cparams(sem):
    return pltpu.CompilerParams(dimension_semantics=sem, vmem_limit_bytes=VMEM_LIMIT)


def _mod_row(i, tm, n_ctx_rows, dec_seq):
    return jnp.maximum((i * tm - n_ctx_rows) // dec_seq + 1, 0)


def _mod_kernel(c_ref, w_ref, b_ref, o_ref):
    c = c_ref[...]
    a = c * jax.nn.sigmoid(c)
    o_ref[0] = jnp.dot(a, w_ref[0], preferred_element_type=F32, precision=lax.Precision.HIGHEST) + b_ref[0]


def _mod_all(cond, w_mod, b_mod, tn=1024):
    nl, d, n = w_mod.shape
    return pl.pallas_call(
        _mod_kernel,
        grid=(nl, n // tn),
        in_specs=[pl.BlockSpec((8, d), lambda l, j: (0, 0)),
                  pl.BlockSpec((1, d, tn), lambda l, j: (l, 0, j)),
                  pl.BlockSpec((1, 1, tn), lambda l, j: (l, 0, j))],
        out_specs=pl.BlockSpec((1, 8, tn), lambda l, j: (l, 0, j)),
        out_shape=jax.ShapeDtypeStruct((nl, 8, n), F32),
        compiler_params=_cparams(("parallel", "arbitrary")),
        name="mod_vectors",
    )(cond, w_mod, b_mod)


def _rms_f32(x, gain):
    return x * lax.rsqrt(jnp.mean(x * x, axis=-1, keepdims=True) + EPS) * gain


def _hmod_kernel(x_ref, g_ref, mod_ref, h_ref, *, i_shift, i_scale):
    x = x_ref[...]
    y = _rms_f32(x, g_ref[...])
    h_ref[...] = (y * (1.0 + mod_ref[0, i_scale:i_scale + 1, :]) + mod_ref[0, i_shift:i_shift + 1, :]).astype(BF16)


def _hmod(x, gain, mod, n_ctx_rows, dec_seq, i_shift, i_scale, tm=512):
    t, d = x.shape
    return pl.pallas_call(
        functools.partial(_hmod_kernel, i_shift=i_shift, i_scale=i_scale),
        grid=(t // tm,),
        in_specs=[pl.BlockSpec((tm, d), lambda i: (i, 0)),
                  pl.BlockSpec((1, d), lambda i: (0, 0)),
                  pl.BlockSpec((1, N_MOD, d), lambda i: (_mod_row(i, tm, n_ctx_rows, dec_seq), 0, 0))],
        out_specs=pl.BlockSpec((tm, d), lambda i: (i, 0)),
        out_shape=jax.ShapeDtypeStruct((t, d), BF16),
        compiler_params=_cparams(("parallel",)),
        name="rms_modulate",
    )(x, gain, mod)


def _mm_plain_kernel(h_ref, w_ref, o_ref):
    o_ref[...] = jnp.dot(h_ref[...], w_ref[...], preferred_element_type=F32).astype(o_ref.dtype)


def _mm_plain(h, w, tn, out_dtype=BF16, tm=ROW_TILE):
    t, k = h.shape
    n = w.shape[1]
    return pl.pallas_call(
        _mm_plain_kernel,
        grid=(t // tm, n // tn),
        in_specs=[pl.BlockSpec((tm, k), lambda i, j: (i, 0)),
                  pl.BlockSpec((k, tn), lambda i, j: (0, j))],
        out_specs=pl.BlockSpec((tm, tn), lambda i, j: (i, j)),
        out_shape=jax.ShapeDtypeStruct((t, n), out_dtype),
        compiler_params=_cparams(("parallel", "arbitrary")),
        name="proj_rest",
    )(h, w)


def _mm_small_kernel(h_ref, w_ref, o_ref):
    o_ref[...] = jnp.dot(h_ref[...].astype(F32), w_ref[...], preferred_element_type=F32,
                         precision=lax.Precision.HIGHEST)


def _mm_small(h, w, tm=256):
    t, k = h.shape
    n = w.shape[1]
    return pl.pallas_call(
        _mm_small_kernel,
        grid=(t // tm,),
        in_specs=[pl.BlockSpec((tm, k), lambda i: (i, 0)),
                  pl.BlockSpec((k, n), lambda i: (0, 0))],
        out_specs=pl.BlockSpec((tm, n), lambda i: (i, 0)),
        out_shape=jax.ShapeDtypeStruct((t, n), F32),
        compiler_params=_cparams(("parallel",)),
        name="proj_small",
    )(h, w)


def _mm_conv_kernel(h_ref, w_ref, cw_ref, cb_ref, o_ref, *, n_ctx_tiles, ctx_period, grid_period):
    acc = jnp.dot(h_ref[...], w_ref[...], preferred_element_type=F32)
    tm = acc.shape[0]
    period = jnp.where(pl.program_id(0) < n_ctx_tiles, ctx_period, grid_period)
    pos = lax.broadcasted_iota(jnp.int32, (tm, 1), 0) & (period - 1)
    prev = jnp.where(pos == 0, 0.0, pltpu.roll(acc, 1, 0))
    nxt = jnp.where(pos == period - 1, 0.0, pltpu.roll(acc, tm - 1, 0))
    y = prev * cw_ref[0:1, :] + acc * cw_ref[1:2, :] + nxt * cw_ref[2:3, :] + cb_ref[...]
    o_ref[...] = (y * jax.nn.sigmoid(y)).astype(o_ref.dtype)


def _mm_conv(h, w, conv_w, conv_b, n_ctx_rows, ctx_period, grid_period, tn=1152, tm=512):
    t, k = h.shape
    n = w.shape[1]
    return pl.pallas_call(
        functools.partial(_mm_conv_kernel, n_ctx_tiles=n_ctx_rows // tm, ctx_period=ctx_period,
                          grid_period=grid_period),
        grid=(t // tm, n // tn),
        in_specs=[pl.BlockSpec((tm, k), lambda i, j: (i, 0)),
                  pl.BlockSpec((k, tn), lambda i, j: (0, j)),
                  pl.BlockSpec((CONV_K, tn), lambda i, j: (0, j)),
                  pl.BlockSpec((1, tn), lambda i, j: (0, j))],
        out_specs=pl.BlockSpec((tm, tn), lambda i, j: (i, j)),
        out_shape=jax.ShapeDtypeStruct((t, n), BF16),
        compiler_params=_cparams(("parallel", "arbitrary")),
        name="proj_conv",
    )(h, w, conv_w, conv_b)


def _merge_kernel(o_ref, g_ref, w_ref, out_ref, acc_ref):
    n = pl.program_id(1)
    p = jnp.dot(o_ref[0], w_ref[0], preferred_element_type=F32)
    contrib = jax.nn.sigmoid(g_ref[...].astype(F32)) * p

    @pl.when(n == 0)
    def _():
        acc_ref[...] = contrib

    @pl.when(n > 0)
    def _():
        acc_ref[...] += contrib

    @pl.when(n == N_BRANCH - 1)
    def _():
        out_ref[...] = acc_ref[...].astype(out_ref.dtype)


def _merge(branches, rest, w_branch, tm=ROW_TILE):
    _, t, kb = branches.shape
    d = w_branch.shape[-1]
    g_blk = R_M_G // d
    return pl.pallas_call(
        _merge_kernel,
        grid=(t // tm, N_BRANCH),
        in_specs=[pl.BlockSpec((1, tm, kb), lambda i, n: (n, i, 0)),
                  pl.BlockSpec((tm, d), lambda i, n: (i, g_blk + n)),
                  pl.BlockSpec((1, kb, d), lambda i, n: (n, 0, 0))],
        out_specs=pl.BlockSpec((tm, d), lambda i, n: (i, 0)),
        out_shape=jax.ShapeDtypeStruct((t, d), BF16),
        scratch_shapes=[pltpu.VMEM((tm, d), F32)],
        compiler_params=_cparams(("parallel", "arbitrary")),
        name="branch_merge",
    )(branches, rest, w_branch)


def _outproj_kernel(m_ref, w_ref, x_ref, g_ref, mod_ref, wr_ref, x1_ref, h2_ref, aff_ref):
    out = jnp.dot(m_ref[...], w_ref[...], preferred_element_type=F32)
    x1 = x_ref[...] + mod_ref[0, 2:3, :] * _rms_f32(out, g_ref[1:2, :])
    x1_ref[...] = x1
    h2 = _rms_f32(x1, g_ref[2:3, :]) * (1.0 + mod_ref[0, 4:5, :]) + mod_ref[0, 3:4, :]
    h2_ref[...] = h2.astype(h2_ref.dtype)
    logits = jnp.dot(h2, wr_ref[...], preferred_element_type=F32, precision=lax.Precision.HIGHEST)
    lane = lax.broadcasted_iota(jnp.int32, logits.shape, 1)
    logits = jnp.where(lane < N_EXPERTS, logits, -jnp.inf)
    e = jnp.exp(logits - jnp.max(logits, axis=-1, keepdims=True))
    aff_ref[...] = e / jnp.sum(e, axis=-1, keepdims=True)


def _outproj(merged, w_out, x, gains, mod, w_router_p, n_ctx_rows, dec_seq, tm=256):
    t, d = x.shape
    return pl.pallas_call(
        _outproj_kernel,
        grid=(t // tm,),
        in_specs=[pl.BlockSpec((tm, d), lambda i: (i, 0)),
                  pl.BlockSpec((d, d), lambda i: (0, 0)),
                  pl.BlockSpec((tm, d), lambda i: (i, 0)),
                  pl.BlockSpec((4, d), lambda i: (0, 0)),
                  pl.BlockSpec((1, N_MOD, d), lambda i: (_mod_row(i, tm, n_ctx_rows, dec_seq), 0, 0)),
                  pl.BlockSpec((d, LANE), lambda i: (0, 0))],
        out_specs=[pl.BlockSpec((tm, d), lambda i: (i, 0)),
                   pl.BlockSpec((tm, d), lambda i: (i, 0)),
                   pl.BlockSpec((tm, LANE), lambda i: (i, 0))],
        out_shape=[jax.ShapeDtypeStruct((t, d), F32),
                   jax.ShapeDtypeStruct((t, d), F32),
                   jax.ShapeDtypeStruct((t, LANE), F32)],
        compiler_params=_cparams(("parallel",)),
        name="out_proj",
    )(merged, w_out, x, gains, mod, w_router_p)


def _bmm_kernel(a_ref, b_ref, o_ref):
    a = a_ref[0].astype(BF16)
    b = b_ref[0].astype(BF16)
    o_ref[0] = jnp.dot(a, b, preferred_element_type=F32)


def _bmm(a, b, tm=512, tn=512):
    e, m, k = a.shape
    n = b.shape[-1]
    tm = min(tm, m)
    tn = min(tn, n)
    return pl.pallas_call(
        _bmm_kernel,
        grid=(e, pl.cdiv(m, tm), pl.cdiv(n, tn)),
        in_specs=[pl.BlockSpec((1, tm, k), lambda g, i, j: (g, i, 0)),
                  pl.BlockSpec((1, k, tn), lambda g, i, j: (g, 0, j))],
        out_specs=pl.BlockSpec((1, tm, tn), lambda g, i, j: (g, i, j)),
        out_shape=jax.ShapeDtypeStruct((e, m, n), F32),
        compiler_params=_cparams(("parallel", "parallel", "arbitrary")),
        name="bmm",
    )(a, b)


CHUNK = 64
CHUNK_SHIFT = 6


def _dot(a, b):
    return jnp.dot(a.astype(BF16), b.astype(BF16), preferred_element_type=F32)


def _dot_nt(a, b):
    return lax.dot_general(a.astype(BF16), b.astype(BF16), (((1,), (1,)), ((), ())),
                           preferred_element_type=F32)


def _dot_tn(a, b):
    return lax.dot_general(a.astype(BF16), b.astype(BF16), (((0,), (0,)), ((), ())),
                           preferred_element_type=F32)


def _split3(x):
    x1 = x.astype(BF16)
    r1 = x - x1.astype(F32)
    x2 = r1.astype(BF16)
    x3 = (r1 - x2.astype(F32)).astype(BF16)
    return x1, x2, x3


def _dot3_left(mask_bf16, x):
    x1, x2, x3 = _split3(x)
    f = lambda p: jnp.dot(mask_bf16, p, preferred_element_type=F32)
    return f(x3) + f(x2) + f(x1)


def _dot3_right(x, mask_bf16):
    x1, x2, x3 = _split3(x)
    f = lambda p: jnp.dot(p, mask_bf16, preferred_element_type=F32)
    return f(x3) + f(x2) + f(x1)


def _softplus(x):
    return jnp.maximum(x, 0.0) + jnp.log(1.0 + jnp.exp(-jnp.abs(x)))


def _silu(x):
    return x * jax.nn.sigmoid(x)


def _chunk_tri(n, upper):
    r = lax.broadcasted_iota(jnp.int32, (n, n), 0)
    c = lax.broadcasted_iota(jnp.int32, (n, n), 1)
    same = (r >> CHUNK_SHIFT) == (c >> CHUNK_SHIFT)
    tri = (c >= r) if upper else (c <= r)
    return jnp.where(same & tri, 1.0, 0.0).astype(BF16)


def _unit_tri_inverse(ms):
    n = ms[0].shape[0]
    r = lax.broadcasted_iota(jnp.int32, (n, n), 0)
    c = lax.broadcasted_iota(jnp.int32, (n, n), 1)
    eye = jnp.where(r == c, 1.0, 0.0)
    same = lambda s: (r >> s) == (c >> s)
    m8 = [jnp.where(same(3), m, 0.0) for m in ms]
    m2 = [_dot(a, a) for a in m8]
    p = [eye - a for a in m8]
    m4 = [_dot(a, a) for a in m2]
    p = [a + _dot(a, b) for a, b in zip(p, m2)]
    xs = [a + _dot(a, b) for a, b in zip(p, m4)]
    s = 3
    while (1 << s) < n:
        band = same(s + 1) & jnp.logical_not(same(s))
        t = [_dot(x, jnp.where(band, m, 0.0)) for x, m in zip(xs, ms)]
        xs = [x - _dot(a, x) for x, a in zip(xs, t)]
        s += 1
    return xs


def _gdn_kernel(*refs, zero_init):
    if zero_init:
        (q_ref, k_ref, v_ref, ag_ref, sc_ref, sr_ref, pc_ref, pr_ref, norm_ref,
         o_ref, sfin_ref, qn_ref, kn_ref, gcf_ref, gcb_ref, grf_ref, grb_ref, of_ref, ob_ref,
         uwf_ref, uwb_ref, qkf_ref, qkb_ref) = refs
        s0_ref = None
    else:
        (q_ref, k_ref, v_ref, ag_ref, sc_ref, sr_ref, pc_ref, pr_ref, norm_ref, s0_ref,
         o_ref, sfin_ref, qn_ref, kn_ref, gcf_ref, gcb_ref, grf_ref, grb_ref, of_ref, ob_ref,
         uwf_ref, uwb_ref, qkf_ref, qkb_ref) = refs
    L = q_ref.shape[0]
    n_chunks = L // CHUNK

    qf = q_ref[...].astype(F32)
    qn_ref[...] = qf * lax.rsqrt(jnp.sum(qf * qf, axis=-1, keepdims=True) + EPS) * (GDN_DK ** -0.5)
    kf = k_ref[...].astype(F32)
    kn_ref[...] = kf * lax.rsqrt(jnp.sum(kf * kf, axis=-1, keepdims=True) + EPS)

    gcf_ref[...] = jnp.zeros_like(gcf_ref)
    gcf_ref[:, 0:4] = -jnp.exp(pc_ref[0, 0:1, 0:4]) * _softplus(sc_ref[0] + pc_ref[0, 1:2, 0:4])
    blk = min(L, 256)
    lo_blk = _chunk_tri(blk, upper=False)
    up_blk = _chunk_tri(blk, upper=True)
    for b in range(L // blk):
        rows = slice(b * blk, (b + 1) * blk)
        g_col = gcf_ref[rows, :]
        gcb_ref[rows, :] = _dot3_left(up_blk, g_col)
        gcf_ref[rows, :] = _dot3_left(lo_blk, g_col)
    g_row = -jnp.exp(pr_ref[0, 0]) * _softplus(sr_ref[0] + pr_ref[0, 1])
    g_row = g_row.reshape(n_chunks * 8, CHUNK)
    up_c = _chunk_tri(CHUNK, upper=True)
    lo_c = _chunk_tri(CHUNK, upper=False)
    grf_ref[...] = _dot3_right(g_row, up_c).reshape(n_chunks, 8, CHUNK)
    grb_ref[...] = _dot3_right(g_row, lo_c).reshape(n_chunks, 8, CHUNK)

    ti = lax.broadcasted_iota(jnp.int32, (CHUNK, CHUNK), 0)
    si = lax.broadcasted_iota(jnp.int32, (CHUNK, CHUNK), 1)

    def chunk_rows(c):
        return pl.ds(c * CHUNK if isinstance(c, int) else pl.multiple_of(c * CHUNK, CHUNK), CHUNK)

    group = min(n_chunks, 4)

    def prep_group(gi):
        items = [(d, gi * group + j) for j in range(group) for d in range(2)]
        ms, rhs = [], []
        for d, c in items:
            rows = chunk_rows(c)
            q = qn_ref[rows, :]
            k = kn_ref[rows, :]
            v = v_ref[rows, :].astype(F32)
            gc = (gcb_ref if d else gcf_ref)[rows, d:d + 1]
            beta = jax.nn.sigmoid(sc_ref[0, rows, 2 + d:3 + d])
            gcr = (grb_ref if d else grf_ref)[c][d:d + 1, :]
            incl = (si >= ti) if d else (si <= ti)
            strict = (si > ti) if d else (si < ti)
            decay = jnp.where(incl, jnp.exp(jnp.where(incl, gc - gcr, 0.0)), 0.0)
            kb = k * beta
            ms.append(jnp.where(strict, _dot_nt(kb, k) * decay, 0.0))
            (qkb_ref if d else qkf_ref)[rows, :] = _dot_nt(q, k) * decay
            rhs.append(jnp.concatenate([v * beta, kb * jnp.exp(gc)], axis=1).astype(BF16))
        xs = _unit_tri_inverse(ms)
        for (d, c), x, r in zip(items, xs, rhs):
            (uwb_ref if d else uwf_ref)[chunk_rows(c), :] = _dot(x, r)

    if n_chunks == group:
        prep_group(0)
    else:
        def prep_body(gi, carry):
            prep_group(gi)
            return carry
        lax.fori_loop(0, n_chunks // group, prep_body, 0)

    def body(i, carry):
        cs = (i, n_chunks - 1 - i)
        rows = [chunk_rows(c) for c in cs]
        gcs = [(gcb_ref if d else gcf_ref)[rows[d], d:d + 1] for d in range(2)]
        s_b = [carry[d].astype(BF16) for d in range(2)]
        ws = [_dot((uwb_ref if d else uwf_ref)[rows[d], GDN_DV:], s_b[d]) for d in range(2)]
        v_new = [(uwb_ref if d else uwf_ref)[rows[d], :GDN_DV] - ws[d] for d in range(2)]
        g_last = [gcs[0][CHUNK - 1:CHUNK], gcs[1][0:1]]
        upd = [_dot_tn(kn_ref[rows[d], :] * jnp.exp(g_last[d] - gcs[d]), v_new[d]) for d in range(2)]
        new = tuple(carry[d] * jnp.exp(g_last[d]) + upd[d] for d in range(2))
        for d in range(2):
            o = (_dot(qn_ref[rows[d], :] * jnp.exp(gcs[d]), s_b[d])
                 + _dot((qkb_ref if d else qkf_ref)[rows[d], :], v_new[d]))
            (ob_ref if d else of_ref)[rows[d], :] = o
        return new

    if zero_init:
        init = (jnp.zeros((GDN_DK, GDN_DV), F32), jnp.zeros((GDN_DK, GDN_DV), F32))
    else:
        init = (s0_ref[0, 0, 0], s0_ref[0, 1, 0])
    s_f, s_b = lax.fori_loop(0, n_chunks, body, init)
    sfin_ref[0, 0, 0] = s_f
    sfin_ref[0, 1, 0] = s_b

    o = of_ref[...] + ob_ref[...]
    y = _rms_f32(o, norm_ref[...])
    o_ref[...] = (y * _silu(ag_ref[...].astype(F32))).astype(o_ref.dtype)


def _gdn(cx, rest, sc, sr, pc, pr, norm, s0, *, n_seq, seq_len, row0):
    L = seq_len
    blk0 = row0 // L
    nh = GDN_HEADS
    zero_init = s0 is None
    k_off = GDN_QK_W // LANE
    v_off = 2 * GDN_QK_W // LANE
    ag_off = R_A_G // LANE
    in_specs = [pl.BlockSpec((L, LANE), lambda s, h: (blk0 + s, h)),
                pl.BlockSpec((L, LANE), lambda s, h: (blk0 + s, k_off + h)),
                pl.BlockSpec((L, LANE), lambda s, h: (blk0 + s, v_off + h)),
                pl.BlockSpec((L, LANE), lambda s, h: (blk0 + s, ag_off + h)),
                pl.BlockSpec((1, L, 4), lambda s, h: (h, blk0 + s, 0)),
                pl.BlockSpec((1, L // CHUNK, 8, CHUNK), lambda s, h: (h, blk0 + s, 0, 0)),
                pl.BlockSpec((1, 8, LANE), lambda s, h: (h, 0, 0)),
                pl.BlockSpec((1, 2, 8, CHUNK), lambda s, h: (h, 0, 0, 0)),
                pl.BlockSpec((1, GDN_DV), lambda s, h: (0, 0))]
    args = [cx, cx, cx, rest, sc, sr, pc, pr, norm]
    state_spec = pl.BlockSpec((1, 2, 1, GDN_DK, GDN_DV), lambda s, h: (s, 0, h, 0, 0))
    if not zero_init:
        in_specs.append(state_spec)
        args.append(s0)
    return pl.pallas_call(
        functools.partial(_gdn_kernel, zero_init=zero_init),
        grid=(n_seq, nh),
        in_specs=in_specs,
        out_specs=[pl.BlockSpec((L, GDN_DV), lambda s, h: (s, h)), state_spec],
        out_shape=[jax.ShapeDtypeStruct((n_seq * L, BRANCH_W), BF16),
                   jax.ShapeDtypeStruct((n_seq, 2, nh, GDN_DK, GDN_DV), F32)],
        scratch_shapes=[pltpu.VMEM((L, GDN_DK), F32), pltpu.VMEM((L, GDN_DK), F32),
                        pltpu.VMEM((L, LANE), F32), pltpu.VMEM((L, LANE), F32),
                        pltpu.VMEM((L // CHUNK, 8, CHUNK), F32), pltpu.VMEM((L // CHUNK, 8, CHUNK), F32),
                        pltpu.VMEM((L, GDN_DV), F32), pltpu.VMEM((L, GDN_DV), F32),
                        pltpu.VMEM((L, 2 * GDN_DV), F32), pltpu.VMEM((L, 2 * GDN_DV), F32),
                        pltpu.VMEM((L, CHUNK), F32), pltpu.VMEM((L, CHUNK), F32)],
        compiler_params=_cparams(("parallel", "arbitrary")),
        name="gdn_scan",
    )(*args)


def _gdn_side_inputs(small, gdn_a_log, gdn_dt_bias):
    t = small.shape[0]
    nh = GDN_HEADS
    ab = small[:, :4 * nh].reshape(t, 2, 2, nh)
    sc = ab.transpose(3, 0, 1, 2).reshape(nh, t, 4)
    rows = ab.transpose(3, 1, 2, 0).reshape(nh, 4, t // CHUNK, CHUNK).transpose(0, 2, 1, 3)
    sr = jnp.concatenate([rows, jnp.zeros_like(rows)], axis=2)
    pc = jnp.zeros((nh, 8, LANE), F32)
    pc = pc.at[:, 0, 0:2].set(gdn_a_log.T).at[:, 1, 0:2].set(gdn_dt_bias.T)
    pr = jnp.zeros((nh, 2, 8, CHUNK), F32)
    pr = pr.at[:, 0, 0:2, :].set(jnp.broadcast_to(gdn_a_log.T[:, :, None], (nh, 2, CHUNK)))
    pr = pr.at[:, 1, 0:2, :].set(jnp.broadcast_to(gdn_dt_bias.T[:, :, None], (nh, 2, CHUNK)))
    return sc, sr, pc, pr


GROUP_W = SSD_HPG * SSD_P


def _ssd_kernel(*refs, zero_init):
    if zero_init:
        (x_ref, b_ref, c_ref, z_ref, dc_ref, dr_ref, pc_ref, pr_ref, dskip_ref, norm_ref,
         o_ref, hfin_ref, dtc_ref, acf_ref, acb_ref, dtr_ref, arf_ref, arb_ref, yf_ref, yb_ref,
         hf_ref, hb_ref) = refs
        h0_ref = None
    else:
        (x_ref, b_ref, c_ref, z_ref, dc_ref, dr_ref, pc_ref, pr_ref, dskip_ref, norm_ref, h0_ref,
         o_ref, hfin_ref, dtc_ref, acf_ref, acb_ref, dtr_ref, arf_ref, arb_ref, yf_ref, yb_ref,
         hf_ref, hb_ref) = refs
    L = x_ref.shape[0]
    n_chunks = L // CHUNK
    nh = SSD_HPG

    dtc_ref[...] = jnp.zeros_like(dtc_ref)
    dtc_ref[:, 0:2 * nh] = _softplus(dc_ref[0] + pc_ref[0, 1:2, 0:2 * nh])
    acf_ref[...] = jnp.zeros_like(acf_ref)
    acf_ref[:, 0:2 * nh] = -jnp.exp(pc_ref[0, 0:1, 0:2 * nh]) * dtc_ref[:, 0:2 * nh]
    blk = min(L, 256)
    lo_blk = _chunk_tri(blk, upper=False)
    up_blk = _chunk_tri(blk, upper=True)
    for b in range(L // blk):
        rows = slice(b * blk, (b + 1) * blk)
        da = acf_ref[rows, :]
        acb_ref[rows, :] = _dot3_left(up_blk, da)
        acf_ref[rows, :] = _dot3_left(lo_blk, da)
    dt_row = _softplus(dr_ref[0] + pr_ref[0, 1])
    dtr_ref[...] = dt_row
    da_row = (-jnp.exp(pr_ref[0, 0]) * dt_row).reshape(n_chunks * 2 * nh, CHUNK)
    arf_ref[...] = _dot3_right(da_row, _chunk_tri(CHUNK, upper=True)).reshape(n_chunks, 2 * nh, CHUNK)
    arb_ref[...] = _dot3_right(da_row, _chunk_tri(CHUNK, upper=False)).reshape(n_chunks, 2 * nh, CHUNK)

    if zero_init:
        hf_ref[...] = jnp.zeros_like(hf_ref)
        hb_ref[...] = jnp.zeros_like(hb_ref)
    else:
        hf_ref[...] = h0_ref[0, 0, 0]
        hb_ref[...] = h0_ref[0, 1, 0]

    ti = lax.broadcasted_iota(jnp.int32, (CHUNK, CHUNK), 0)
    si = lax.broadcasted_iota(jnp.int32, (CHUNK, CHUNK), 1)
    ej = lax.broadcasted_iota(jnp.int32, (LANE, GROUP_W), 0)
    ec = lax.broadcasted_iota(jnp.int32, (LANE, GROUP_W), 1)

    def step(d, c):
        rows = pl.ds(pl.multiple_of(c * CHUNK, CHUNK), CHUNK)
        h_ref = hb_ref if d else hf_ref
        x = x_ref[rows, :]
        bm = b_ref[rows, :]
        cm = c_ref[rows, :]
        acs = (acb_ref if d else acf_ref)[rows, :]
        acs_r = (arb_ref if d else arf_ref)[c][d * nh:(d + 1) * nh, :]
        dt_r = dtr_ref[c][d * nh:(d + 1) * nh, :]
        incl = (si >= ti) if d else (si <= ti)
        cb = _dot_nt(cm, bm)
        ys = []
        for h in range(nh):
            diff = acs[:, d * nh + h:d * nh + h + 1] - acs_r[h:h + 1, :]
            seg = jnp.where(incl, jnp.exp(jnp.where(incl, diff, 0.0)), 0.0)
            ys.append(_dot(cb * seg * dt_r[h:h + 1, :], x[:, h * SSD_P:(h + 1) * SSD_P]))
        y_diag = jnp.concatenate(ys, axis=1)
        spread = jnp.where(ej == d * nh + (ec >> 6), 1.0, 0.0).astype(BF16)
        last = acs[0:1] if d else acs[CHUNK - 1:CHUNK]
        p_full = _dot3_right(dtc_ref[rows, :] * jnp.exp(last - acs), spread)
        e_full = _dot3_right(jnp.exp(acs), spread)
        cd_full = _dot3_right(jnp.broadcast_to(jnp.exp(last), (8, LANE)), spread)[0:1]
        h_in = h_ref[...]
        y = y_diag + _dot(cm, h_in) * e_full
        (yb_ref if d else yf_ref)[rows, :] = y
        h_ref[...] = h_in * cd_full + _dot_tn(bm, x.astype(F32) * p_full)

    def body(i, carry):
        step(0, i)
        step(1, n_chunks - 1 - i)
        return carry

    lax.fori_loop(0, n_chunks, body, 0)
    hfin_ref[0, 0, 0] = hf_ref[...]
    hfin_ref[0, 1, 0] = hb_ref[...]

    y = yf_ref[...] + yb_ref[...] + dskip_ref[0] * x_ref[...].astype(F32)
    y = y * _silu(z_ref[...].astype(F32))
    o_ref[...] = _rms_f32(y, norm_ref[...]).astype(o_ref.dtype)


def _ssd(cx, rest, dc, dr, pc, pr, dskip, norm, h0, *, n_seq, seq_len, row0):
    L = seq_len
    blk0 = row0 // L
    ng = SSD_GROUPS
    zero_init = h0 is None
    x_off = (2 * GDN_QK_W + BRANCH_W) // GROUP_W
    b_off = (2 * GDN_QK_W + 2 * BRANCH_W) // SSD_N
    c_off = b_off + ng
    z_off = R_B_Z // GROUP_W
    in_specs = [pl.BlockSpec((L, GROUP_W), lambda s, g: (blk0 + s, x_off + g)),
                pl.BlockSpec((L, SSD_N), lambda s, g: (blk0 + s, b_off + g)),
                pl.BlockSpec((L, SSD_N), lambda s, g: (blk0 + s, c_off + g)),
                pl.BlockSpec((L, GROUP_W), lambda s, g: (blk0 + s, z_off + g)),
                pl.BlockSpec((1, L, 2 * SSD_HPG), lambda s, g: (g, blk0 + s, 0)),
                pl.BlockSpec((1, L // CHUNK, 2 * SSD_HPG, CHUNK), lambda s, g: (g, blk0 + s, 0, 0)),
                pl.BlockSpec((1, 8, LANE), lambda s, g: (g, 0, 0)),
                pl.BlockSpec((1, 2, 2 * SSD_HPG, CHUNK), lambda s, g: (g, 0, 0, 0)),
                pl.BlockSpec((1, 1, GROUP_W), lambda s, g: (g, 0, 0)),
                pl.BlockSpec((1, GROUP_W), lambda s, g: (0, g))]
    args = [cx, cx, cx, rest, dc, dr, pc, pr, dskip, norm]
    state_spec = pl.BlockSpec((1, 2, 1, SSD_N, GROUP_W), lambda s, g: (s, 0, g, 0, 0))
    if not zero_init:
        in_specs.append(state_spec)
        args.append(h0)
    return pl.pallas_call(
        functools.partial(_ssd_kernel, zero_init=zero_init),
        grid=(n_seq, ng),
        in_specs=in_specs,
        out_specs=[pl.BlockSpec((L, GROUP_W), lambda s, g: (s, g)), state_spec],
        out_shape=[jax.ShapeDtypeStruct((n_seq * L, BRANCH_W), BF16),
                   jax.ShapeDtypeStruct((n_seq, 2, ng, SSD_N, GROUP_W), F32)],
        scratch_shapes=[pltpu.VMEM((L, LANE), F32), pltpu.VMEM((L, LANE), F32), pltpu.VMEM((L, LANE), F32),
                        pltpu.VMEM((L // CHUNK, 2 * SSD_HPG, CHUNK), F32),
                        pltpu.VMEM((L // CHUNK, 2 * SSD_HPG, CHUNK), F32),
                        pltpu.VMEM((L // CHUNK, 2 * SSD_HPG, CHUNK), F32),
                        pltpu.VMEM((L, GROUP_W), F32), pltpu.VMEM((L, GROUP_W), F32),
                        pltpu.VMEM((SSD_N, GROUP_W), F32), pltpu.VMEM((SSD_N, GROUP_W), F32)],
        compiler_params=_cparams(("parallel", "arbitrary")),
        name="ssd_scan",
    )(*args)


def _ssd_side_inputs(small, ssd_a_log, ssd_dt_bias, ssd_d):
    t = small.shape[0]
    ng, nh = SSD_GROUPS, SSD_HPG
    off = 4 * GDN_HEADS
    dt = small[:, off:off + 2 * SSD_HEADS].reshape(t, 2, ng, nh)
    dc = dt.transpose(2, 0, 1, 3).reshape(ng, t, 2 * nh)
    dr = dt.transpose(2, 1, 3, 0).reshape(ng, 2 * nh, t // CHUNK, CHUNK).transpose(0, 2, 1, 3)
    a = ssd_a_log.reshape(2, ng, nh).transpose(1, 0, 2).reshape(ng, 2 * nh)
    bias = ssd_dt_bias.reshape(2, ng, nh).transpose(1, 0, 2).reshape(ng, 2 * nh)
    pc = jnp.zeros((ng, 8, LANE), F32).at[:, 0, 0:2 * nh].set(a).at[:, 1, 0:2 * nh].set(bias)
    pr = jnp.stack([jnp.broadcast_to(a[:, :, None], (ng, 2 * nh, CHUNK)),
                    jnp.broadcast_to(bias[:, :, None], (ng, 2 * nh, CHUNK))], axis=1)
    dskip = jnp.repeat(ssd_d.reshape(ng, nh), SSD_P, axis=1)[:, None, :]
    return dc, dr, pc, pr, dskip


def _ssd_state_to_kernel(s):
    n = s.shape[0]
    return s.reshape(n, 2, SSD_GROUPS, SSD_HPG, SSD_N, SSD_P).transpose(0, 1, 2, 4, 3, 5).reshape(
        n, 2, SSD_GROUPS, SSD_N, GROUP_W)


def _ssd_state_from_kernel(s):
    n = s.shape[0]
    return s.reshape(n, 2, SSD_GROUPS, SSD_N, SSD_HPG, SSD_P).transpose(0, 1, 2, 4, 3, 5).reshape(
        n, 2, SSD_HEADS, SSD_N, SSD_P)


SUB = 4
SUB_SHIFT = 2
N_SUB = CHUNK // SUB


def _hgrn_kernel(*refs, zero_init):
    if zero_init:
        (q_ref, zf_ref, zb_ref, v_ref, cg_ref, lb_ref, norm_ref,
         o_ref, sfin_ref, qs_ref, kf_ref, kb_ref, gf_ref, gb_ref, of_ref, ob_ref) = refs
        s0_ref = None
    else:
        (q_ref, zf_ref, zb_ref, v_ref, cg_ref, lb_ref, norm_ref, s0_ref,
         o_ref, sfin_ref, qs_ref, kf_ref, kb_ref, gf_ref, gb_ref, of_ref, ob_ref) = refs
    L = q_ref.shape[0]
    n_chunks = L // CHUNK

    qs_ref[...] = _silu(q_ref[...].astype(F32))
    blk = min(L, 256)
    lo_blk = _chunk_tri(blk, upper=False)
    up_blk = _chunk_tri(blk, upper=True)
    for d, (z_ref, k_ref, g_ref, tri) in enumerate(((zf_ref, kf_ref, gf_ref, lo_blk),
                                                    (zb_ref, kb_ref, gb_ref, up_blk))):
        lbd = lb_ref[d:d + 1, :]
        z = z_ref[...].astype(F32)
        k_ref[...] = (1.0 - lbd) * jax.nn.sigmoid(-z)
        log_f = jnp.log(lbd + (1.0 - lbd) * jax.nn.sigmoid(z))
        for b in range(L // blk):
            rows = slice(b * blk, (b + 1) * blk)
            g_ref[rows, :] = _dot3_left(tri, log_f[rows])

    ti = lax.broadcasted_iota(jnp.int32, (CHUNK, CHUNK), 0)
    si = lax.broadcasted_iota(jnp.int32, (CHUNK, CHUNK), 1)
    tcol = lax.broadcasted_iota(jnp.int32, (CHUNK, 1), 0)

    def step(d, c, st):
        rows = pl.ds(pl.multiple_of(c * CHUNK, CHUNK), CHUNK)
        q = qs_ref[rows, :]
        k = (kb_ref if d else kf_ref)[rows, :]
        g = (gb_ref if d else gf_ref)[rows, :]
        v = v_ref[rows, :].astype(F32)
        if d:
            sel = jnp.where(si == ((ti >> SUB_SHIFT) << SUB_SHIFT) + SUB, 1.0, 0.0).astype(BF16)
        else:
            sel = jnp.where(si == ((ti >> SUB_SHIFT) << SUB_SHIFT) - 1, 1.0, 0.0).astype(BF16)
        g_ref_rows = _dot3_left(sel, g)
        q_in = q * jnp.exp(g - g_ref_rows)
        k_parts, q_parts = [], []
        for j in range(1, N_SUB):
            if d:
                ref_row = g[CHUNK - SUB * j:CHUNK - SUB * j + 1, :]
                use = tcol >= CHUNK - SUB * j
                mine = (tcol >> SUB_SHIFT) == N_SUB - 1 - j
            else:
                ref_row = g[SUB * j - 1:SUB * j, :]
                use = tcol < SUB * j
                mine = (tcol >> SUB_SHIFT) == j
            k_parts.append(jnp.where(use, k * jnp.exp(jnp.where(use, ref_row - g, 0.0)), 0.0).astype(BF16))
            q_parts.append(jnp.where(mine, q_in, 0.0).astype(BF16))
        att = _dot_nt(jnp.concatenate(q_parts, axis=1), jnp.concatenate(k_parts, axis=1))
        o = _dot(att, v) + _dot_nt(q * jnp.exp(g), st)
        pos = tcol & (SUB - 1)
        for lag in range(SUB):
            if lag == 0:
                a = jnp.sum(q * k, axis=-1, keepdims=True)
                o = o + a * v
                continue
            shift = (CHUNK - lag) if d else lag
            valid = (pos + lag <= SUB - 1) if d else (pos >= lag)
            k_l = pltpu.roll(k, shift, 0)
            g_l = pltpu.roll(g, shift, 0)
            v_l = pltpu.roll(v, shift, 0)
            dec = jnp.exp(jnp.where(valid, g - g_l, 0.0))
            a = jnp.where(valid, jnp.sum(q * k_l * dec, axis=-1, keepdims=True), 0.0)
            o = o + a * v_l
        (ob_ref if d else of_ref)[rows, :] = o
        g_last = g[0:1] if d else g[CHUNK - 1:CHUNK]
        return st * jnp.exp(g_last) + _dot_tn(v, k * jnp.exp(g_last - g))

    def body(i, carry):
        return step(0, i, carry[0]), step(1, n_chunks - 1 - i, carry[1])

    if zero_init:
        init = (jnp.zeros((HGRN_DV, HGRN_DK), F32), jnp.zeros((HGRN_DV, HGRN_DK), F32))
    else:
        init = (s0_ref[0, 0, 0], s0_ref[0, 1, 0])
    s_f, s_b = lax.fori_loop(0, n_chunks, body, init)
    sfin_ref[0, 0, 0] = s_f
    sfin_ref[0, 1, 0] = s_b

    o = of_ref[...] + ob_ref[...]
    y = _rms_f32(o, norm_ref[...])
    o_ref[...] = (y * _silu(cg_ref[...].astype(F32))).astype(o_ref.dtype)


def _hgrn(rest, lb, norm, s0, *, n_seq, seq_len, row0):
    L = seq_len
    blk0 = row0 // L
    nh = HGRN_HEADS
    zero_init = s0 is None
    q_off, zf_off, v_off, g_off = R_C_Q // LANE, R_C_F // LANE, R_C_I // LANE, R_C_G // LANE
    zb_off = zf_off + nh
    in_specs = [pl.BlockSpec((L, LANE), lambda s, h: (blk0 + s, q_off + h)),
                pl.BlockSpec((L, LANE), lambda s, h: (blk0 + s, zf_off + h)),
                pl.BlockSpec((L, LANE), lambda s, h: (blk0 + s, zb_off + h)),
                pl.BlockSpec((L, LANE), lambda s, h: (blk0 + s, v_off + h)),
                pl.BlockSpec((L, LANE), lambda s, h: (blk0 + s, g_off + h)),
                pl.BlockSpec((2, HGRN_DK), lambda s, h: (0, h)),
                pl.BlockSpec((1, HGRN_DV), lambda s, h: (0, 0))]
    args = [rest, rest, rest, rest, rest, lb, norm]
    state_spec = pl.BlockSpec((1, 2, 1, HGRN_DV, HGRN_DK), lambda s, h: (s, 0, h, 0, 0))
    if not zero_init:
        in_specs.append(state_spec)
        args.append(s0)
    return pl.pallas_call(
        functools.partial(_hgrn_kernel, zero_init=zero_init),
        grid=(n_seq, nh),
        in_specs=in_specs,
        out_specs=[pl.BlockSpec((L, HGRN_DV), lambda s, h: (s, h)), state_spec],
        out_shape=[jax.ShapeDtypeStruct((n_seq * L, BRANCH_W), BF16),
                   jax.ShapeDtypeStruct((n_seq, 2, nh, HGRN_DV, HGRN_DK), F32)],
        scratch_shapes=[pltpu.VMEM((L, HGRN_DK), F32)] * 5 + [pltpu.VMEM((L, HGRN_DV), F32)] * 2,
        compiler_params=_cparams(("parallel", "arbitrary")),
        name="hgrn_scan",
    )(*args)


def _rms(x, gain):
    xf = x.astype(jnp.float32)
    y = xf * lax.rsqrt(jnp.mean(xf * xf, axis=-1, keepdims=True) + EPS)
    return (y * gain.astype(jnp.float32)).astype(x.dtype)


def _l2n(x):
    return x * lax.rsqrt(jnp.sum(x * x, axis=-1, keepdims=True) + EPS)


def _flip(t, d, axis):
    return jnp.flip(t, axis=axis) if d else t


def _masked_exp(diff, mask):
    return jnp.where(mask, jnp.exp(jnp.where(mask, diff, 0.0)), 0.0)


def _gated_delta_chunked(q, k, v, g, beta, s0):
    nb, nh, L, dk = q.shape
    C = GDN_CHUNK
    n = L // C

    def chunks(t):
        return t.reshape(nb, nh, n, C, *t.shape[3:])

    q, k, v, g, beta = chunks(q), chunks(k), chunks(v), chunks(g), chunks(beta)
    gc = jnp.cumsum(g, axis=-1)
    causal = jnp.tril(jnp.ones((C, C), dtype=bool))
    decay = _masked_exp(gc[..., :, None] - gc[..., None, :], causal)
    kb = k * beta[..., None]
    m_strict = jnp.einsum('bhntd,bhnsd->bhnts', kb, k) * decay * jnp.tril(jnp.ones((C, C), q.dtype), -1)
    a = m_strict + jnp.eye(C, dtype=q.dtype)
    u = jax.lax.linalg.triangular_solve(a, v * beta[..., None], left_side=True, lower=True, unit_diagonal=True)
    w = jax.lax.linalg.triangular_solve(a, kb * jnp.exp(gc)[..., None], left_side=True, lower=True,
                                        unit_diagonal=True)
    qk = jnp.einsum('bhntd,bhnsd->bhnts', q, k) * decay

    def step(S, xs):
        q_c, k_c, u_c, w_c, gc_c, qk_c = xs
        v_new = u_c - jnp.einsum('bhcd,bhde->bhce', w_c, S)
        o = (jnp.einsum('bhcd,bhde->bhce', q_c * jnp.exp(gc_c)[..., None], S)
             + jnp.einsum('bhts,bhse->bhte', qk_c, v_new))
        g_last = gc_c[..., -1:]
        S = S * jnp.exp(g_last)[..., None] + jnp.einsum(
            'bhcd,bhce->bhde', k_c * jnp.exp(g_last - gc_c)[..., None], v_new)
        return S, o

    xs = tuple(jnp.moveaxis(t, 2, 0) for t in (q, k, u, w, gc, qk))
    s_fin, o = lax.scan(step, s0, xs)
    return jnp.moveaxis(o, 0, 2).reshape(nb, nh, L, -1), s_fin


def _ssd_chunked(x, dt, a, bm, cm, h0):
    nb, L = x.shape[:2]
    C = SSD_CHUNK
    n = L // C

    def chunks(t):
        return t.reshape(nb, n, C, *t.shape[2:])

    x, dt, bm, cm = chunks(x), chunks(dt), chunks(bm), chunks(cm)
    acs = jnp.cumsum(dt * a, axis=2)
    causal = jnp.tril(jnp.ones((C, C), dtype=bool))[:, :, None, None]
    seg = _masked_exp(acs[:, :, :, None] - acs[:, :, None, :], causal)
    xdt = x * dt[..., None]
    cb = jnp.einsum('bntgk,bnsgk->bntsg', cm, bm)
    y_diag = jnp.einsum('bntsg,bntsgr,bnsgrp->bntgrp', cb, seg, xdt)
    st_local = jnp.einsum('bncgk,bncgr,bncgrp->bngrkp', bm, jnp.exp(acs[:, :, -1:] - acs), xdt)
    chunk_decay = jnp.exp(acs[:, :, -1])

    def step(h, xs):
        st, cd = xs
        return h * cd[..., None, None] + st, h

    h_fin, h_in = lax.scan(step, h0, (jnp.moveaxis(st_local, 1, 0), jnp.moveaxis(chunk_decay, 1, 0)))
    y_off = jnp.einsum('bncgk,nbgrkp,bncgr->bncgrp', cm, h_in, jnp.exp(acs))
    return (y_diag + y_off).reshape(nb, L, *x.shape[3:]), h_fin


def _hgrn2_chunked(q, k, v, log_f, s0):
    nb, nh, L, dk = q.shape
    C = HGRN_CHUNK
    n = L // C
    causal = jnp.tril(jnp.ones((C, C), dtype=bool))[:, :, None]

    def chunks(t):
        return jnp.moveaxis(t.reshape(nb, nh, n, C, t.shape[-1]), 2, 0)

    def step(S, xs):
        q_c, k_c, v_c, lf_c = xs
        G = jnp.cumsum(lf_c, axis=2)
        G_last = G[:, :, -1:]
        dec = _masked_exp(G[:, :, :, None] - G[:, :, None], causal)
        att = jnp.einsum('bhtd,bhsd,bhtsd->bhts', q_c, k_c, dec)
        o = (jnp.einsum('bhtd,bhde->bhte', q_c * jnp.exp(G), S)
             + jnp.einsum('bhts,bhse->bhte', att, v_c))
        S = S * jnp.exp(G_last)[:, :, 0, :, None] + jnp.einsum(
            'bhsd,bhse->bhde', k_c * jnp.exp(G_last - G), v_c)
        return S, o

    s_fin, o = lax.scan(step, s0, (chunks(q), chunks(k), chunks(v), chunks(log_f)))
    return jnp.moveaxis(o, 0, 2).reshape(nb, nh, L, -1), s_fin


def _mixers_jnp(cx, rest, small, nb, L, s_gdn, s_ssd, s_hgrn, lb, gdn_a_log, gdn_dt_bias, gdn_norm,
                ssd_a_log, ssd_dt_bias, ssd_d, ssd_norm, hgrn_norm, do="abc"):
    f32 = jnp.float32
    cx = cx.astype(f32).reshape(nb, L, -1)
    rest = rest.astype(f32).reshape(nb, L, -1)
    small = small.reshape(nb, L, -1)
    a_q, a_k, a_v, b_x, b_b, b_c = jnp.split(cx, np.cumsum(CONV_SIZES)[:-1].tolist(), axis=-1)
    a_g = rest[..., R_A_G:R_A_G + 1024]
    b_z = rest[..., R_B_Z:R_B_Z + 1024]
    c_q = rest[..., R_C_Q:R_C_Q + 1024]
    c_f = rest[..., R_C_F:R_C_F + 2048]
    c_i = rest[..., R_C_I:R_C_I + 1024]
    c_g = rest[..., R_C_G:R_C_G + 1024]
    a_alpha = small[..., 0:16]
    a_beta = small[..., 16:32]
    b_dt = small[..., 32:64]

    def heads(t, n):
        return t.reshape(nb, L, n, -1).transpose(0, 2, 1, 3)

    q = _l2n(heads(a_q, GDN_HEADS)) * GDN_DK ** -0.5
    k = _l2n(heads(a_k, GDN_HEADS))
    v = heads(a_v, GDN_HEADS)
    alpha = a_alpha.reshape(nb, L, 2, GDN_HEADS).transpose(2, 0, 3, 1)
    beta = jax.nn.sigmoid(a_beta.reshape(nb, L, 2, GDN_HEADS).transpose(2, 0, 3, 1))
    g = -jnp.exp(gdn_a_log.astype(f32))[:, None, :, None] * jax.nn.softplus(
        alpha + gdn_dt_bias.astype(f32)[:, None, :, None])
    outs, s_a = [], []
    for d in range(2 if "a" in do else 0):
        o, s = _gated_delta_chunked(_flip(q, d, 2), _flip(k, d, 2), _flip(v, d, 2), _flip(g[d], d, 2),
                                    _flip(beta[d], d, 2), s_gdn[:, d].astype(f32))
        outs.append(_flip(o, d, 2))
        s_a.append(s)
    if "a" in do:
        o_a = _rms((outs[0] + outs[1]).transpose(0, 2, 1, 3), gdn_norm) * jax.nn.silu(
            a_g.reshape(nb, L, GDN_HEADS, GDN_DV))
        o_a = o_a.reshape(nb * L, BRANCH_W)
    else:
        o_a, s_a = None, [s_gdn[:, 0], s_gdn[:, 1]]

    xs = b_x.reshape(nb, L, SSD_GROUPS, SSD_HPG, SSD_P)
    bm = b_b.reshape(nb, L, SSD_GROUPS, SSD_N)
    cm = b_c.reshape(nb, L, SSD_GROUPS, SSD_N)
    dt = jax.nn.softplus(b_dt.reshape(nb, L, 2, SSD_GROUPS, SSD_HPG)
                         + ssd_dt_bias.astype(f32).reshape(2, SSD_GROUPS, SSD_HPG))
    a_ssd = -jnp.exp(ssd_a_log.astype(f32)).reshape(2, SSD_GROUPS, SSD_HPG)
    ys, s_b = [], []
    for d in range(2):
        y, s = _ssd_chunked(_flip(xs, d, 1), _flip(dt[:, :, d], d, 1), a_ssd[d], _flip(bm, d, 1),
                            _flip(cm, d, 1),
                            s_ssd[:, d].astype(f32).reshape(nb, SSD_GROUPS, SSD_HPG, SSD_N, SSD_P))
        ys.append(_flip(y, d, 1))
        s_b.append(s.reshape(nb, SSD_HEADS, SSD_N, SSD_P))
    y = ys[0] + ys[1] + ssd_d.astype(f32).reshape(SSD_GROUPS, SSD_HPG, 1) * xs
    y = y.reshape(nb, L, BRANCH_W) * jax.nn.silu(b_z)
    o_b = _rms(y.reshape(nb, L, SSD_GROUPS, BRANCH_W // SSD_GROUPS),
               ssd_norm.reshape(SSD_GROUPS, -1)).reshape(nb * L, BRANCH_W)

    qc = jax.nn.silu(heads(c_q, HGRN_HEADS))
    vc = heads(c_i, HGRN_HEADS)
    zf = c_f.reshape(nb, L, 2, HGRN_HEADS, HGRN_DK).transpose(2, 0, 3, 1, 4)
    lbd = lb.reshape(2, 1, HGRN_HEADS, 1, HGRN_DK)
    log_f = jnp.log(lbd + (1.0 - lbd) * jax.nn.sigmoid(zf))
    k_c = (1.0 - lbd) * jax.nn.sigmoid(-zf)
    outs, s_c = [], []
    for d in range(2):
        o, s = _hgrn2_chunked(_flip(qc, d, 2), _flip(k_c[d], d, 2), _flip(vc, d, 2), _flip(log_f[d], d, 2),
                              s_hgrn[:, d].astype(f32))
        outs.append(_flip(o, d, 2))
        s_c.append(s)
    o_c = _rms((outs[0] + outs[1]).transpose(0, 2, 1, 3), hgrn_norm) * jax.nn.silu(
        c_g.reshape(nb, L, HGRN_HEADS, HGRN_DV))
    o_c = o_c.reshape(nb * L, BRANCH_W)
    return (o_a, o_b, o_c), (jnp.stack(s_a, axis=1), jnp.stack(s_b, axis=1), jnp.stack(s_c, axis=1))


def _ec_moe(t, aff, w_gu, w_down):
    cap = CAPACITY_FACTOR * t.shape[0] // N_EXPERTS
    gate, idx = lax.top_k(aff.T, cap)
    xe = t[idx]
    gu = _bmm(xe, w_gu)
    g, u = jnp.split(gu, 2, axis=-1)
    ye = _bmm(jax.nn.silu(g) * u, w_down) * gate[..., None]
    return jnp.zeros_like(t).at[idx.reshape(-1)].add(ye.reshape(-1, t.shape[-1]))


def kernel(x_prompt, x_sample, c, state_gdn, state_ssd, state_hgrn, c_ctx, w_mod, b_mod, norm_gain,
           w_in, conv_w, conv_b, gdn_a_log, gdn_dt_bias, gdn_norm, ssd_a_log, ssd_dt_bias, ssd_d,
           ssd_norm, hgrn_lb, hgrn_norm, w_branch, w_out, w_router, w_gu, w_down):
    nb_c, seq_c, d = x_prompt.shape
    nb_s, seq_s, _ = x_sample.shape
    n_ctx = nb_c * seq_c
    n_smp = nb_s * seq_s

    lb_w = jax.nn.softmax(hgrn_lb.astype(F32), axis=0)
    lb = jnp.cumsum(lb_w, axis=0) - lb_w[:1]

    cond = jnp.concatenate([c_ctx[None, :], c, jnp.zeros((8 - 1 - nb_s, d), F32)], axis=0)
    mod_all = _mod_all(cond, w_mod, b_mod[:, None, :]).reshape(DEPTH, 8, N_MOD, d)

    x = jnp.concatenate([x_prompt.reshape(n_ctx, d), x_sample.reshape(n_smp, d)], axis=0)
    z_gdn = jnp.zeros((nb_c, 2, GDN_HEADS, GDN_DK, GDN_DV), F32)
    z_ssd = jnp.zeros((nb_c, 2, SSD_HEADS, SSD_N, SSD_P), F32)
    z_hgrn = jnp.zeros((nb_c, 2, HGRN_HEADS, HGRN_DK, HGRN_DV), F32)
    new_gdn, new_ssd, new_hgrn = [], [], []

    for l in range(DEPTH):
        mod = mod_all[l]
        wl = w_in[l]
        w_conv = wl[:, :CONV_CH].astype(BF16)
        w_rest = jnp.concatenate([wl[:, OFF_M_G:], wl[:, OFF_A_G:OFF_ALPHA], wl[:, OFF_B_Z:OFF_B_DT],
                                  wl[:, OFF_C_Q:OFF_M_G]], axis=1).astype(BF16)
        w_small = jnp.concatenate([wl[:, OFF_ALPHA:OFF_B_Z], wl[:, OFF_B_DT:OFF_C_Q],
                                   jnp.zeros((d, SMALL_W - 4 * GDN_HEADS - 2 * SSD_HEADS), F32)], axis=1)

        h = _hmod(x, norm_gain[l, 0:1], mod, n_ctx, seq_s, i_shift=0, i_scale=1)
        cx = _mm_conv(h, w_conv, conv_w[l], conv_b[l][None, :], n_ctx, seq_c, GRID_W)
        rest = _mm_plain(h, w_rest, tn=1024)
        small = _mm_small(h, w_small)

        mix_args = (lb[l], gdn_a_log[l], gdn_dt_bias[l], gdn_norm[l], ssd_a_log[l], ssd_dt_bias[l], ssd_d[l],
                    ssd_norm[l], hgrn_norm[l])
        gdn_side = _gdn_side_inputs(small, gdn_a_log[l], gdn_dt_bias[l])
        oa_c, sg = _gdn(cx, rest, *gdn_side, gdn_norm[l][None, :], None, n_seq=nb_c, seq_len=seq_c, row0=0)
        oa_s, _ = _gdn(cx, rest, *gdn_side, gdn_norm[l][None, :], state_gdn[:, l], n_seq=nb_s, seq_len=seq_s,
                       row0=n_ctx)
        ssd_side = _ssd_side_inputs(small, ssd_a_log[l], ssd_dt_bias[l], ssd_d[l])
        ob_c, ss = _ssd(cx, rest, *ssd_side, ssd_norm[l][None, :], None, n_seq=nb_c, seq_len=seq_c, row0=0)
        ob_s, _ = _ssd(cx, rest, *ssd_side, ssd_norm[l][None, :], _ssd_state_to_kernel(state_ssd[:, l]),
                       n_seq=nb_s, seq_len=seq_s, row0=n_ctx)
        ss = _ssd_state_from_kernel(ss)
        oc_c, sh = _hgrn(rest, lb[l], hgrn_norm[l][None, :], None, n_seq=nb_c, seq_len=seq_c, row0=0)
        oc_s, _ = _hgrn(rest, lb[l], hgrn_norm[l][None, :], jnp.swapaxes(state_hgrn[:, l], -1, -2),
                        n_seq=nb_s, seq_len=seq_s, row0=n_ctx)
        sh = jnp.swapaxes(sh, -1, -2)
        new_gdn.append(sg)
        new_ssd.append(ss)
        new_hgrn.append(sh)
        branches = jnp.stack([jnp.concatenate([oa_c, oa_s]), jnp.concatenate([ob_c, ob_s]),
                              jnp.concatenate([oc_c, oc_s])])

        merged = _merge(branches, rest, w_branch[l].astype(BF16))
        w_router_p = jnp.concatenate([w_router[l], jnp.zeros((d, LANE - N_EXPERTS), F32)], axis=1)
        x1, h2, aff = _outproj(merged, w_out[l].astype(BF16), x, norm_gain[l], mod, w_router_p, n_ctx, seq_s)

        moe_c = _ec_moe(h2[:n_ctx], aff[:n_ctx, :N_EXPERTS], w_gu[l], w_down[l])
        moe_s = _ec_moe(h2[n_ctx:], aff[n_ctx:, :N_EXPERTS], w_gu[l], w_down[l])
        moe = jnp.concatenate([moe_c, moe_s])
        gate2 = jnp.concatenate([jnp.broadcast_to(mod[0, 5], (n_ctx, d)),
                                 jnp.repeat(mod[1:1 + nb_s, 5], seq_s, axis=0)])
        x = x1 + gate2 * _rms(moe, norm_gain[l, 3])

    y_prompt = x[:n_ctx].reshape(nb_c, seq_c, d)
    y_sample = x[n_ctx:].reshape(nb_s, seq_s, d)
    return (y_prompt, y_sample, jnp.stack(new_gdn, axis=1), jnp.stack(new_ssd, axis=1),
            jnp.stack(new_hgrn, axis=1))
```

```python
import functools
import math

import numpy as np
import jax
import jax.numpy as jnp
from jax import lax
from jax.experimental import pallas as pl
from jax.experimental.pallas import tpu as pltpu

D_MODEL = 2048
DEPTH = 2
GRID_W = 64
N_BRANCH = 3
BRANCH_W = D_MODEL // 2
GDN_DK = 128
GDN_DV = 128
GDN_HEADS = BRANCH_W // GDN_DV
GDN_CHUNK = 64
SSD_P = 64
SSD_HEADS = BRANCH_W // SSD_P
SSD_GROUPS = 2
SSD_HPG = SSD_HEADS // SSD_GROUPS
SSD_N = 128
SSD_CHUNK = 64
HGRN_DK = 128
HGRN_DV = 128
HGRN_HEADS = BRANCH_W // HGRN_DV
HGRN_CHUNK = 16
CONV_K = 3
N_EXPERTS = 16
EXPERT_FF = D_MODEL // 2
CAPACITY_FACTOR = 2
N_MOD = 6
EPS = 1e-6
GDN_QK_W = GDN_HEADS * GDN_DK
HGRN_QK_W = HGRN_HEADS * HGRN_DK
CONV_SIZES = (GDN_QK_W, GDN_QK_W, BRANCH_W, BRANCH_W, SSD_GROUPS * SSD_N, SSD_GROUPS * SSD_N)
CONV_CH = GDN_QK_W + GDN_QK_W + BRANCH_W + BRANCH_W + 2 * SSD_GROUPS * SSD_N
OFF_A_G = CONV_CH
OFF_ALPHA = OFF_A_G + BRANCH_W
OFF_BETA = OFF_ALPHA + 2 * GDN_HEADS
OFF_B_Z = OFF_BETA + 2 * GDN_HEADS
OFF_B_DT = OFF_B_Z + BRANCH_W
OFF_C_Q = OFF_B_DT + 2 * SSD_HEADS
IN_COLS = OFF_C_Q + 3 * HGRN_QK_W + 2 * BRANCH_W + N_BRANCH * D_MODEL
OFF_M_G = IN_COLS - N_BRANCH * D_MODEL
REST_W = IN_COLS - CONV_CH - 4 * GDN_HEADS - 2 * SSD_HEADS
R_M_G, R_A_G, R_B_Z, R_C_Q, R_C_F, R_C_I, R_C_G = 0, 6144, 7168, 8192, 9216, 11264, 12288
SMALL_W = 128

LANE = 128
VMEM_LIMIT = 48 * 1024 * 1024
ROW_TILE = 1024
SEQ_BLOCK = 256

BF16 = jnp.bfloat16
F32 = jnp.float32


def _cparams(sem, vmem_limit=VMEM_LIMIT):
    return pltpu.CompilerParams(dimension_semantics=sem, vmem_limit_bytes=vmem_limit)


def _mod_row(i, tm, n_ctx_rows, dec_seq):
    return jnp.maximum((i * tm - n_ctx_rows) // dec_seq + 1, 0)


def _mod_kernel(c_ref, w_ref, b_ref, o_ref):
    c = c_ref[...]
    a = c * jax.nn.sigmoid(c)
    o_ref[0] = jnp.dot(a, w_ref[0], preferred_element_type=F32, precision=lax.Precision.HIGHEST) + b_ref[0]


def _mod_all(cond, w_mod, b_mod, tn=1024):
    nl, d, n = w_mod.shape
    return pl.pallas_call(
        _mod_kernel,
        grid=(nl, n // tn),
        in_specs=[pl.BlockSpec((8, d), lambda l, j: (0, 0)),
                  pl.BlockSpec((1, d, tn), lambda l, j: (l, 0, j)),
                  pl.BlockSpec((1, 1, tn), lambda l, j: (l, 0, j))],
        out_specs=pl.BlockSpec((1, 8, tn), lambda l, j: (l, 0, j)),
        out_shape=jax.ShapeDtypeStruct((nl, 8, n), F32),
        compiler_params=_cparams(("parallel", "arbitrary")),
        name="mod_vectors",
    )(cond, w_mod, b_mod)


def _rms_f32(x, gain):
    return x * lax.rsqrt(jnp.mean(x * x, axis=-1, keepdims=True) + EPS) * gain


def _hmod_kernel(x_ref, g_ref, mod_ref, h_ref, *, i_shift, i_scale):
    x = x_ref[...]
    y = _rms_f32(x, g_ref[...])
    h_ref[...] = (y * (1.0 + mod_ref[0, i_scale:i_scale + 1, :]) + mod_ref[0, i_shift:i_shift + 1, :]).astype(BF16)


def _hmod(x, gain, mod, n_ctx_rows, dec_seq, i_shift, i_scale, tm=512):
    t, d = x.shape
    return pl.pallas_call(
        functools.partial(_hmod_kernel, i_shift=i_shift, i_scale=i_scale),
        grid=(t // tm,),
        in_specs=[pl.BlockSpec((tm, d), lambda i: (i, 0)),
                  pl.BlockSpec((1, d), lambda i: (0, 0)),
                  pl.BlockSpec((1, N_MOD, d), lambda i: (_mod_row(i, tm, n_ctx_rows, dec_seq), 0, 0))],
        out_specs=pl.BlockSpec((tm, d), lambda i: (i, 0)),
        out_shape=jax.ShapeDtypeStruct((t, d), BF16),
        compiler_params=_cparams(("parallel",)),
        name="rms_modulate",
    )(x, gain, mod)


def _mm_plain_kernel(h_ref, w_ref, o_ref):
    o_ref[...] = jnp.dot(h_ref[...], w_ref[...], preferred_element_type=F32).astype(o_ref.dtype)


def _mm_plain(h, w, tn, out_dtype=BF16, tm=ROW_TILE):
    t, k = h.shape
    n = w.shape[1]
    return pl.pallas_call(
        _mm_plain_kernel,
        grid=(t // tm, n // tn),
        in_specs=[pl.BlockSpec((tm, k), lambda i, j: (i, 0)),
                  pl.BlockSpec((k, tn), lambda i, j: (0, j))],
        out_specs=pl.BlockSpec((tm, tn), lambda i, j: (i, j)),
        out_shape=jax.ShapeDtypeStruct((t, n), out_dtype),
        compiler_params=_cparams(("parallel", "arbitrary")),
        name="proj_rest",
    )(h, w)


def _mm_small_kernel(h_ref, w_ref, o_ref):
    o_ref[...] = jnp.dot(h_ref[...].astype(F32), w_ref[...], preferred_element_type=F32,
                         precision=lax.Precision.HIGHEST)


def _mm_small(h, w, tm=256):
    t, k = h.shape
    n = w.shape[1]
    return pl.pallas_call(
        _mm_small_kernel,
        grid=(t // tm,),
        in_specs=[pl.BlockSpec((tm, k), lambda i: (i, 0)),
                  pl.BlockSpec((k, n), lambda i: (0, 0))],
        out_specs=pl.BlockSpec((tm, n), lambda i: (i, 0)),
        out_shape=jax.ShapeDtypeStruct((t, n), F32),
        compiler_params=_cparams(("parallel",)),
        name="proj_small",
    )(h, w)


def _mm_conv_kernel(h_ref, w_ref, cw_ref, cb_ref, o_ref, *, n_ctx_tiles, ctx_period, grid_period):
    acc = jnp.dot(h_ref[...], w_ref[...], preferred_element_type=F32)
    tm = acc.shape[0]
    period = jnp.where(pl.program_id(0) < n_ctx_tiles, ctx_period, grid_period)
    pos = lax.broadcasted_iota(jnp.int32, (tm, 1), 0) & (period - 1)
    prev = jnp.where(pos == 0, 0.0, pltpu.roll(acc, 1, 0))
    nxt = jnp.where(pos == period - 1, 0.0, pltpu.roll(acc, tm - 1, 0))
    y = prev * cw_ref[0:1, :] + acc * cw_ref[1:2, :] + nxt * cw_ref[2:3, :] + cb_ref[...]
    o_ref[...] = (y * jax.nn.sigmoid(y)).astype(o_ref.dtype)


def _mm_conv(h, w, conv_w, conv_b, n_ctx_rows, ctx_period, grid_period, tn=1152, tm=512):
    t, k = h.shape
    n = w.shape[1]
    return pl.pallas_call(
        functools.partial(_mm_conv_kernel, n_ctx_tiles=n_ctx_rows // tm, ctx_period=ctx_period,
                          grid_period=grid_period),
        grid=(t // tm, n // tn),
        in_specs=[pl.BlockSpec((tm, k), lambda i, j: (i, 0)),
                  pl.BlockSpec((k, tn), lambda i, j: (0, j)),
                  pl.BlockSpec((CONV_K, tn), lambda i, j: (0, j)),
                  pl.BlockSpec((1, tn), lambda i, j: (0, j))],
        out_specs=pl.BlockSpec((tm, tn), lambda i, j: (i, j)),
        out_shape=jax.ShapeDtypeStruct((t, n), BF16),
        compiler_params=_cparams(("parallel", "arbitrary")),
        name="proj_conv",
    )(h, w, conv_w, conv_b)


def _merge_kernel(o_ref, g_ref, w_ref, out_ref, acc_ref):
    n = pl.program_id(1)
    p = jnp.dot(o_ref[0], w_ref[0], preferred_element_type=F32)
    contrib = jax.nn.sigmoid(g_ref[...].astype(F32)) * p

    @pl.when(n == 0)
    def _():
        acc_ref[...] = contrib

    @pl.when(n > 0)
    def _():
        acc_ref[...] += contrib

    @pl.when(n == N_BRANCH - 1)
    def _():
        out_ref[...] = acc_ref[...].astype(out_ref.dtype)


def _merge(branches, rest, w_branch, tm=ROW_TILE):
    _, t, kb = branches.shape
    d = w_branch.shape[-1]
    g_blk = R_M_G // d
    return pl.pallas_call(
        _merge_kernel,
        grid=(t // tm, N_BRANCH),
        in_specs=[pl.BlockSpec((1, tm, kb), lambda i, n: (n, i, 0)),
                  pl.BlockSpec((tm, d), lambda i, n: (i, g_blk + n)),
                  pl.BlockSpec((1, kb, d), lambda i, n: (n, 0, 0))],
        out_specs=pl.BlockSpec((tm, d), lambda i, n: (i, 0)),
        out_shape=jax.ShapeDtypeStruct((t, d), BF16),
        scratch_shapes=[pltpu.VMEM((tm, d), F32)],
        compiler_params=_cparams(("parallel", "arbitrary")),
        name="branch_merge",
    )(branches, rest, w_branch)


def _outproj_kernel(m_ref, w_ref, x_ref, g_ref, mod_ref, wr_ref, x1_ref, h2_ref, aff_ref):
    out = jnp.dot(m_ref[...], w_ref[...], preferred_element_type=F32)
    x1 = x_ref[...] + mod_ref[0, 2:3, :] * _rms_f32(out, g_ref[1:2, :])
    x1_ref[...] = x1
    h2 = _rms_f32(x1, g_ref[2:3, :]) * (1.0 + mod_ref[0, 4:5, :]) + mod_ref[0, 3:4, :]
    h2_ref[...] = h2.astype(h2_ref.dtype)
    logits = jnp.dot(h2, wr_ref[...], preferred_element_type=F32, precision=lax.Precision.HIGHEST)
    lane = lax.broadcasted_iota(jnp.int32, logits.shape, 1)
    logits = jnp.where(lane < N_EXPERTS, logits, -jnp.inf)
    e = jnp.exp(logits - jnp.max(logits, axis=-1, keepdims=True))
    aff_ref[...] = e / jnp.sum(e, axis=-1, keepdims=True)


def _outproj(merged, w_out, x, gains, mod, w_router_p, n_ctx_rows, dec_seq, tm=256):
    t, d = x.shape
    return pl.pallas_call(
        _outproj_kernel,
        grid=(t // tm,),
        in_specs=[pl.BlockSpec((tm, d), lambda i: (i, 0)),
                  pl.BlockSpec((d, d), lambda i: (0, 0)),
                  pl.BlockSpec((tm, d), lambda i: (i, 0)),
                  pl.BlockSpec((4, d), lambda i: (0, 0)),
                  pl.BlockSpec((1, N_MOD, d), lambda i: (_mod_row(i, tm, n_ctx_rows, dec_seq), 0, 0)),
                  pl.BlockSpec((d, LANE), lambda i: (0, 0))],
        out_specs=[pl.BlockSpec((tm, d), lambda i: (i, 0)),
                   pl.BlockSpec((tm, d), lambda i: (i, 0)),
                   pl.BlockSpec((tm, LANE), lambda i: (i, 0))],
        out_shape=[jax.ShapeDtypeStruct((t, d), F32),
                   jax.ShapeDtypeStruct((t, d), BF16),
                   jax.ShapeDtypeStruct((t, LANE), F32)],
        compiler_params=_cparams(("parallel",)),
        name="out_proj",
    )(merged, w_out, x, gains, mod, w_router_p)


def _bmm_kernel(a_ref, b_ref, o_ref):
    a = a_ref[0].astype(BF16)
    b = b_ref[0].astype(BF16)
    o_ref[0] = jnp.dot(a, b, preferred_element_type=F32)


def _bmm(a, b, tm=512, tn=512):
    e, m, k = a.shape
    n = b.shape[-1]
    tm = min(tm, m)
    tn = min(tn, n)
    return pl.pallas_call(
        _bmm_kernel,
        grid=(e, pl.cdiv(m, tm), pl.cdiv(n, tn)),
        in_specs=[pl.BlockSpec((1, tm, k), lambda g, i, j: (g, i, 0)),
                  pl.BlockSpec((1, k, tn), lambda g, i, j: (g, 0, j))],
        out_specs=pl.BlockSpec((1, tm, tn), lambda g, i, j: (g, i, j)),
        out_shape=jax.ShapeDtypeStruct((e, m, n), F32),
        compiler_params=_cparams(("parallel", "parallel", "arbitrary")),
        name="bmm",
    )(a, b)


CHUNK = 64
CHUNK_SHIFT = 6


def _dot(a, b):
    return jnp.dot(a.astype(BF16), b.astype(BF16), preferred_element_type=F32)


def _dot_nt(a, b):
    return lax.dot_general(a.astype(BF16), b.astype(BF16), (((1,), (1,)), ((), ())),
                           preferred_element_type=F32)


def _dot_tn(a, b):
    return lax.dot_general(a.astype(BF16), b.astype(BF16), (((0,), (0,)), ((), ())),
                           preferred_element_type=F32)


def _interleave(*gens):
    results = [None] * len(gens)
    live = list(range(len(gens)))
    while live:
        for i in list(live):
            try:
                next(gens[i])
            except StopIteration as stop:
                results[i] = stop.value
                live.remove(i)
    return results


def _split3(x):
    x1 = x.astype(BF16)
    r1 = x - x1.astype(F32)
    x2 = r1.astype(BF16)
    x3 = (r1 - x2.astype(F32)).astype(BF16)
    return x1, x2, x3


def _dot3_left(mask_bf16, x):
    x1, x2, x3 = _split3(x)
    f = lambda p: jnp.dot(mask_bf16, p, preferred_element_type=F32)
    return f(x3) + f(x2) + f(x1)


def _dot3_right(x, mask_bf16):
    x1, x2, x3 = _split3(x)
    f = lambda p: jnp.dot(p, mask_bf16, preferred_element_type=F32)
    return f(x3) + f(x2) + f(x1)


def _softplus(x):
    return jnp.maximum(x, 0.0) + jnp.log(1.0 + jnp.exp(-jnp.abs(x)))


def _silu(x):
    return x * jax.nn.sigmoid(x)


def _chunk_tri(n, upper):
    r = lax.broadcasted_iota(jnp.int32, (n, n), 0)
    c = lax.broadcasted_iota(jnp.int32, (n, n), 1)
    same = (r >> CHUNK_SHIFT) == (c >> CHUNK_SHIFT)
    tri = (c >= r) if upper else (c <= r)
    return jnp.where(same & tri, 1.0, 0.0).astype(BF16)


def _unit_tri_inverse(ms):
    n = ms[0].shape[0]
    r = lax.broadcasted_iota(jnp.int32, (n, n), 0)
    c = lax.broadcasted_iota(jnp.int32, (n, n), 1)
    eye = jnp.where(r == c, 1.0, 0.0)
    same = lambda s: (r >> s) == (c >> s)
    m8 = [jnp.where(same(3), m, 0.0) for m in ms]
    m2 = [_dot(a, a) for a in m8]
    p = [eye - a for a in m8]
    m4 = [_dot(a, a) for a in m2]
    p = [a + _dot(a, b) for a, b in zip(p, m2)]
    xs = [a + _dot(a, b) for a, b in zip(p, m4)]
    s = 3
    while (1 << s) < n:
        band = same(s + 1) & jnp.logical_not(same(s))
        t = [_dot(x, jnp.where(band, m, 0.0)) for x, m in zip(xs, ms)]
        xs = [x - _dot(a, x) for x, a in zip(xs, t)]
        s += 1
    return xs


def _gdn_kernel(*refs, zero_init):
    if zero_init:
        (q_ref, k_ref, v_ref, ag_ref, sc_ref, sr_ref, pc_ref, pr_ref, norm_ref,
         o_ref, sfin_ref, qn_ref, kn_ref, gcf_ref, gcb_ref, grf_ref, grb_ref, of_ref, ob_ref,
         uwf_ref, uwb_ref, qkf_ref, qkb_ref) = refs
        s0_ref = None
    else:
        (q_ref, k_ref, v_ref, ag_ref, sc_ref, sr_ref, pc_ref, pr_ref, norm_ref, s0_ref,
         o_ref, sfin_ref, qn_ref, kn_ref, gcf_ref, gcb_ref, grf_ref, grb_ref, of_ref, ob_ref,
         uwf_ref, uwb_ref, qkf_ref, qkb_ref) = refs
    L = q_ref.shape[0]
    n_chunks = L // CHUNK

    qf = q_ref[...].astype(F32)
    qn_ref[...] = qf * lax.rsqrt(jnp.sum(qf * qf, axis=-1, keepdims=True) + EPS) * (GDN_DK ** -0.5)
    kf = k_ref[...].astype(F32)
    kn_ref[...] = kf * lax.rsqrt(jnp.sum(kf * kf, axis=-1, keepdims=True) + EPS)

    gcf_ref[...] = jnp.zeros_like(gcf_ref)
    gcf_ref[:, 0:4] = -jnp.exp(pc_ref[0, 0:1, 0:4]) * _softplus(sc_ref[0] + pc_ref[0, 1:2, 0:4])
    blk = min(L, 256)
    lo_blk = _chunk_tri(blk, upper=False)
    up_blk = _chunk_tri(blk, upper=True)
    for b in range(L // blk):
        rows = slice(b * blk, (b + 1) * blk)
        g_col = gcf_ref[rows, :]
        gcb_ref[rows, :] = _dot3_left(up_blk, g_col)
        gcf_ref[rows, :] = _dot3_left(lo_blk, g_col)
    g_row = -jnp.exp(pr_ref[0, 0]) * _softplus(sr_ref[0] + pr_ref[0, 1])
    g_row = g_row.reshape(n_chunks * 8, CHUNK)
    up_c = _chunk_tri(CHUNK, upper=True)
    lo_c = _chunk_tri(CHUNK, upper=False)
    grf_ref[...] = _dot3_right(g_row, up_c).reshape(n_chunks, 8, CHUNK)
    grb_ref[...] = _dot3_right(g_row, lo_c).reshape(n_chunks, 8, CHUNK)

    ti = lax.broadcasted_iota(jnp.int32, (CHUNK, CHUNK), 0)
    si = lax.broadcasted_iota(jnp.int32, (CHUNK, CHUNK), 1)

    def chunk_rows(c):
        return pl.ds(c * CHUNK if isinstance(c, int) else pl.multiple_of(c * CHUNK, CHUNK), CHUNK)

    group = min(n_chunks, 4)

    def prep_group(gi):
        items = [(d, gi * group + j) for j in range(group) for d in range(2)]
        ms, rhs = [], []
        for d, c in items:
            rows = chunk_rows(c)
            q = qn_ref[rows, :]
            k = kn_ref[rows, :]
            v = v_ref[rows, :].astype(F32)
            gc = (gcb_ref if d else gcf_ref)[rows, d:d + 1]
            beta = jax.nn.sigmoid(sc_ref[0, rows, 2 + d:3 + d])
            gcr = (grb_ref if d else grf_ref)[c][d:d + 1, :]
            incl = (si >= ti) if d else (si <= ti)
            strict = (si > ti) if d else (si < ti)
            decay = jnp.where(incl, jnp.exp(jnp.where(incl, gc - gcr, 0.0)), 0.0)
            kb = k * beta
            ms.append(jnp.where(strict, _dot_nt(kb, k) * decay, 0.0))
            (qkb_ref if d else qkf_ref)[rows, :] = _dot_nt(q, k) * decay
            rhs.append(jnp.concatenate([v * beta, kb * jnp.exp(gc)], axis=1).astype(BF16))
        xs = _unit_tri_inverse(ms)
        for (d, c), x, r in zip(items, xs, rhs):
            (uwb_ref if d else uwf_ref)[chunk_rows(c), :] = _dot(x, r)

    if n_chunks == group:
        prep_group(0)
    else:
        def prep_body(gi, carry):
            prep_group(gi)
            return carry
        lax.fori_loop(0, n_chunks // group, prep_body, 0)

    def body(i, carry):
        cs = (i, n_chunks - 1 - i)
        rows = [chunk_rows(c) for c in cs]
        gcs = [(gcb_ref if d else gcf_ref)[rows[d], d:d + 1] for d in range(2)]
        s_b = [carry[d].astype(BF16) for d in range(2)]
        ws = [_dot((uwb_ref if d else uwf_ref)[rows[d], GDN_DV:], s_b[d]) for d in range(2)]
        v_new = [(uwb_ref if d else uwf_ref)[rows[d], :GDN_DV] - ws[d] for d in range(2)]
        g_last = [gcs[0][CHUNK - 1:CHUNK], gcs[1][0:1]]
        upd = [_dot_tn(kn_ref[rows[d], :] * jnp.exp(g_last[d] - gcs[d]), v_new[d]) for d in range(2)]
        new = tuple(carry[d] * jnp.exp(g_last[d]) + upd[d] for d in range(2))
        for d in range(2):
            o = (_dot(qn_ref[rows[d], :] * jnp.exp(gcs[d]), s_b[d])
                 + _dot((qkb_ref if d else qkf_ref)[rows[d], :], v_new[d]))
            (ob_ref if d else of_ref)[rows[d], :] = o
        return new

    if zero_init:
        init = (jnp.zeros((GDN_DK, GDN_DV), F32), jnp.zeros((GDN_DK, GDN_DV), F32))
    else:
        init = (s0_ref[0, 0, 0], s0_ref[0, 1, 0])
    s_f, s_b = lax.fori_loop(0, n_chunks, body, init)
    sfin_ref[0, 0, 0] = s_f
    sfin_ref[0, 1, 0] = s_b

    o = of_ref[...] + ob_ref[...]
    y = _rms_f32(o, norm_ref[...])
    o_ref[...] = (y * _silu(ag_ref[...].astype(F32))).astype(o_ref.dtype)


def _gdn(cx, rest, sc, sr, pc, pr, norm, s0, *, n_seq, seq_len, row0):
    L = seq_len
    blk0 = row0 // L
    nh = GDN_HEADS
    zero_init = s0 is None
    k_off = GDN_QK_W // LANE
    v_off = 2 * GDN_QK_W // LANE
    ag_off = R_A_G // LANE
    in_specs = [pl.BlockSpec((L, LANE), lambda s, h: (blk0 + s, h)),
                pl.BlockSpec((L, LANE), lambda s, h: (blk0 + s, k_off + h)),
                pl.BlockSpec((L, LANE), lambda s, h: (blk0 + s, v_off + h)),
                pl.BlockSpec((L, LANE), lambda s, h: (blk0 + s, ag_off + h)),
                pl.BlockSpec((1, L, 4), lambda s, h: (h, blk0 + s, 0)),
                pl.BlockSpec((1, L // CHUNK, 8, CHUNK), lambda s, h: (h, blk0 + s, 0, 0)),
                pl.BlockSpec((1, 8, LANE), lambda s, h: (h, 0, 0)),
                pl.BlockSpec((1, 2, 8, CHUNK), lambda s, h: (h, 0, 0, 0)),
                pl.BlockSpec((1, GDN_DV), lambda s, h: (0, 0))]
    args = [cx, cx, cx, rest, sc, sr, pc, pr, norm]
    state_spec = pl.BlockSpec((1, 2, 1, GDN_DK, GDN_DV), lambda s, h: (s, 0, h, 0, 0))
    if not zero_init:
        in_specs.append(state_spec)
        args.append(s0)
    return pl.pallas_call(
        functools.partial(_gdn_kernel, zero_init=zero_init),
        grid=(n_seq, nh),
        in_specs=in_specs,
        out_specs=[pl.BlockSpec((L, GDN_DV), lambda s, h: (s, h)), state_spec],
        out_shape=[jax.ShapeDtypeStruct((n_seq * L, BRANCH_W), BF16),
                   jax.ShapeDtypeStruct((n_seq, 2, nh, GDN_DK, GDN_DV), F32)],
        scratch_shapes=[pltpu.VMEM((L, GDN_DK), F32), pltpu.VMEM((L, GDN_DK), F32),
                        pltpu.VMEM((L, LANE), F32), pltpu.VMEM((L, LANE), F32),
                        pltpu.VMEM((L // CHUNK, 8, CHUNK), F32), pltpu.VMEM((L // CHUNK, 8, CHUNK), F32),
                        pltpu.VMEM((L, GDN_DV), F32), pltpu.VMEM((L, GDN_DV), F32),
                        pltpu.VMEM((L, 2 * GDN_DV), F32), pltpu.VMEM((L, 2 * GDN_DV), F32),
                        pltpu.VMEM((L, CHUNK), F32), pltpu.VMEM((L, CHUNK), F32)],
        compiler_params=_cparams(("parallel", "arbitrary")),
        name="gdn_scan",
    )(*args)


def _gdn_side_inputs(small, gdn_a_log, gdn_dt_bias):
    t = small.shape[0]
    nh = GDN_HEADS
    ab = small[:, :4 * nh].reshape(t, 2, 2, nh)
    sc = ab.transpose(3, 0, 1, 2).reshape(nh, t, 4)
    rows = ab.transpose(3, 1, 2, 0).reshape(nh, 4, t // CHUNK, CHUNK).transpose(0, 2, 1, 3)
    sr = jnp.concatenate([rows, jnp.zeros_like(rows)], axis=2)
    pc = jnp.zeros((nh, 8, LANE), F32)
    pc = pc.at[:, 0, 0:2].set(gdn_a_log.T).at[:, 1, 0:2].set(gdn_dt_bias.T)
    pr = jnp.zeros((nh, 2, 8, CHUNK), F32)
    pr = pr.at[:, 0, 0:2, :].set(jnp.broadcast_to(gdn_a_log.T[:, :, None], (nh, 2, CHUNK)))
    pr = pr.at[:, 1, 0:2, :].set(jnp.broadcast_to(gdn_dt_bias.T[:, :, None], (nh, 2, CHUNK)))
    return sc, sr, pc, pr


GROUP_W = SSD_HPG * SSD_P


def _ssd_kernel(*refs, zero_init):
    if zero_init:
        (x_ref, b_ref, c_ref, z_ref, dc_ref, dr_ref, pc_ref, pr_ref, dskip_ref, norm_ref,
         o_ref, hfin_ref, dtc_ref, acf_ref, acb_ref, dtr_ref, arf_ref, arb_ref, yf_ref, yb_ref,
         hf_ref, hb_ref) = refs
        h0_ref = None
    else:
        (x_ref, b_ref, c_ref, z_ref, dc_ref, dr_ref, pc_ref, pr_ref, dskip_ref, norm_ref, h0_ref,
         o_ref, hfin_ref, dtc_ref, acf_ref, acb_ref, dtr_ref, arf_ref, arb_ref, yf_ref, yb_ref,
         hf_ref, hb_ref) = refs
    L = x_ref.shape[0]
    n_chunks = L // CHUNK
    nh = SSD_HPG

    dtc_ref[...] = jnp.zeros_like(dtc_ref)
    dtc_ref[:, 0:2 * nh] = _softplus(dc_ref[0] + pc_ref[0, 1:2, 0:2 * nh])
    acf_ref[...] = jnp.zeros_like(acf_ref)
    acf_ref[:, 0:2 * nh] = -jnp.exp(pc_ref[0, 0:1, 0:2 * nh]) * dtc_ref[:, 0:2 * nh]
    blk = min(L, 256)
    lo_blk = _chunk_tri(blk, upper=False)
    up_blk = _chunk_tri(blk, upper=True)
    for b in range(L // blk):
        rows = slice(b * blk, (b + 1) * blk)
        da = acf_ref[rows, :]
        acb_ref[rows, :] = _dot3_left(up_blk, da)
        acf_ref[rows, :] = _dot3_left(lo_blk, da)
    dt_row = _softplus(dr_ref[0] + pr_ref[0, 1])
    dtr_ref[...] = dt_row
    da_row = (-jnp.exp(pr_ref[0, 0]) * dt_row).reshape(n_chunks * 2 * nh, CHUNK)
    arf_ref[...] = _dot3_right(da_row, _chunk_tri(CHUNK, upper=True)).reshape(n_chunks, 2 * nh, CHUNK)
    arb_ref[...] = _dot3_right(da_row, _chunk_tri(CHUNK, upper=False)).reshape(n_chunks, 2 * nh, CHUNK)

    if zero_init:
        hf_ref[...] = jnp.zeros_like(hf_ref)
        hb_ref[...] = jnp.zeros_like(hb_ref)
    else:
        hf_ref[...] = h0_ref[0, 0, 0]
        hb_ref[...] = h0_ref[0, 1, 0]

    ti = lax.broadcasted_iota(jnp.int32, (CHUNK, CHUNK), 0)
    si = lax.broadcasted_iota(jnp.int32, (CHUNK, CHUNK), 1)
    ej = lax.broadcasted_iota(jnp.int32, (LANE, GROUP_W), 0)
    ec = lax.broadcasted_iota(jnp.int32, (LANE, GROUP_W), 1)

    def step(d, c):
        rows = pl.ds(pl.multiple_of(c * CHUNK, CHUNK), CHUNK)
        h_ref = hb_ref if d else hf_ref
        x = x_ref[rows, :]
        bm = b_ref[rows, :]
        cm = c_ref[rows, :]
        acs = (acb_ref if d else acf_ref)[rows, :]
        acs_r = (arb_ref if d else arf_ref)[c][d * nh:(d + 1) * nh, :]
        dt_r = dtr_ref[c][d * nh:(d + 1) * nh, :]
        incl = (si >= ti) if d else (si <= ti)
        cb = _dot_nt(cm, bm)
        spread = jnp.where(ej == d * nh + (ec >> 6), 1.0, 0.0).astype(BF16)
        last = acs[0:1] if d else acs[CHUNK - 1:CHUNK]
        p_full = _dot3_right(dtc_ref[rows, :] * jnp.exp(last - acs), spread)
        e_full = _dot3_right(jnp.exp(acs), spread)
        cd_full = _dot3_right(jnp.broadcast_to(jnp.exp(last), (8, LANE)), spread)[0:1]
        h_in = h_ref[...]
        y_off = _dot(cm, h_in)
        yield
        ys = []
        for h in range(nh):
            diff = acs[:, d * nh + h:d * nh + h + 1] - acs_r[h:h + 1, :]
            seg = jnp.where(incl, jnp.exp(jnp.where(incl, diff, 0.0)), 0.0)
            ys.append(_dot(cb * seg * dt_r[h:h + 1, :], x[:, h * SSD_P:(h + 1) * SSD_P]))
        st = _dot_tn(bm, x.astype(F32) * p_full)
        yield
        (yb_ref if d else yf_ref)[rows, :] = jnp.concatenate(ys, axis=1) + y_off * e_full
        h_ref[...] = h_in * cd_full + st

    def body(i, carry):
        _interleave(step(0, i), step(1, n_chunks - 1 - i))
        return carry

    lax.fori_loop(0, n_chunks, body, 0)
    hfin_ref[0, 0, 0] = hf_ref[...]
    hfin_ref[0, 1, 0] = hb_ref[...]

    y = yf_ref[...] + yb_ref[...] + dskip_ref[0] * x_ref[...].astype(F32)
    y = y * _silu(z_ref[...].astype(F32))
    o_ref[...] = _rms_f32(y, norm_ref[...]).astype(o_ref.dtype)


def _ssd(cx, rest, dc, dr, pc, pr, dskip, norm, h0, *, n_seq, seq_len, row0):
    L = seq_len
    blk0 = row0 // L
    ng = SSD_GROUPS
    zero_init = h0 is None
    x_off = (2 * GDN_QK_W + BRANCH_W) // GROUP_W
    b_off = (2 * GDN_QK_W + 2 * BRANCH_W) // SSD_N
    c_off = b_off + ng
    z_off = R_B_Z // GROUP_W
    in_specs = [pl.BlockSpec((L, GROUP_W), lambda s, g: (blk0 + s, x_off + g)),
                pl.BlockSpec((L, SSD_N), lambda s, g: (blk0 + s, b_off + g)),
                pl.BlockSpec((L, SSD_N), lambda s, g: (blk0 + s, c_off + g)),
                pl.BlockSpec((L, GROUP_W), lambda s, g: (blk0 + s, z_off + g)),
                pl.BlockSpec((1, L, 2 * SSD_HPG), lambda s, g: (g, blk0 + s, 0)),
                pl.BlockSpec((1, L // CHUNK, 2 * SSD_HPG, CHUNK), lambda s, g: (g, blk0 + s, 0, 0)),
                pl.BlockSpec((1, 8, LANE), lambda s, g: (g, 0, 0)),
                pl.BlockSpec((1, 2, 2 * SSD_HPG, CHUNK), lambda s, g: (g, 0, 0, 0)),
                pl.BlockSpec((1, 1, GROUP_W), lambda s, g: (g, 0, 0)),
                pl.BlockSpec((1, GROUP_W), lambda s, g: (0, g))]
    args = [cx, cx, cx, rest, dc, dr, pc, pr, dskip, norm]
    state_spec = pl.BlockSpec((1, 2, 1, SSD_N, GROUP_W), lambda s, g: (s, 0, g, 0, 0))
    if not zero_init:
        in_specs.append(state_spec)
        args.append(h0)
    return pl.pallas_call(
        functools.partial(_ssd_kernel, zero_init=zero_init),
        grid=(n_seq, ng),
        in_specs=in_specs,
        out_specs=[pl.BlockSpec((L, GROUP_W), lambda s, g: (s, g)), state_spec],
        out_shape=[jax.ShapeDtypeStruct((n_seq * L, BRANCH_W), BF16),
                   jax.ShapeDtypeStruct((n_seq, 2, ng, SSD_N, GROUP_W), F32)],
        scratch_shapes=[pltpu.VMEM((L, LANE), F32), pltpu.VMEM((L, LANE), F32), pltpu.VMEM((L, LANE), F32),
                        pltpu.VMEM((L // CHUNK, 2 * SSD_HPG, CHUNK), F32),
                        pltpu.VMEM((L // CHUNK, 2 * SSD_HPG, CHUNK), F32),
                        pltpu.VMEM((L // CHUNK, 2 * SSD_HPG, CHUNK), F32),
                        pltpu.VMEM((L, GROUP_W), F32), pltpu.VMEM((L, GROUP_W), F32),
                        pltpu.VMEM((SSD_N, GROUP_W), F32), pltpu.VMEM((SSD_N, GROUP_W), F32)],
        compiler_params=_cparams(("parallel", "arbitrary")),
        name="ssd_scan",
    )(*args)


def _ssd_side_inputs(small, ssd_a_log, ssd_dt_bias, ssd_d):
    t = small.shape[0]
    ng, nh = SSD_GROUPS, SSD_HPG
    off = 4 * GDN_HEADS
    dt = small[:, off:off + 2 * SSD_HEADS].reshape(t, 2, ng, nh)
    dc = dt.transpose(2, 0, 1, 3).reshape(ng, t, 2 * nh)
    dr = dt.transpose(2, 1, 3, 0).reshape(ng, 2 * nh, t // CHUNK, CHUNK).transpose(0, 2, 1, 3)
    a = ssd_a_log.reshape(2, ng, nh).transpose(1, 0, 2).reshape(ng, 2 * nh)
    bias = ssd_dt_bias.reshape(2, ng, nh).transpose(1, 0, 2).reshape(ng, 2 * nh)
    pc = jnp.zeros((ng, 8, LANE), F32).at[:, 0, 0:2 * nh].set(a).at[:, 1, 0:2 * nh].set(bias)
    pr = jnp.stack([jnp.broadcast_to(a[:, :, None], (ng, 2 * nh, CHUNK)),
                    jnp.broadcast_to(bias[:, :, None], (ng, 2 * nh, CHUNK))], axis=1)
    dskip = jnp.repeat(ssd_d.reshape(ng, nh), SSD_P, axis=1)[:, None, :]
    return dc, dr, pc, pr, dskip


def _ssd_state_to_kernel(s):
    n = s.shape[0]
    return s.reshape(n, 2, SSD_GROUPS, SSD_HPG, SSD_N, SSD_P).transpose(0, 1, 2, 4, 3, 5).reshape(
        n, 2, SSD_GROUPS, SSD_N, GROUP_W)


def _ssd_state_from_kernel(s):
    n = s.shape[0]
    return s.reshape(n, 2, SSD_GROUPS, SSD_N, SSD_HPG, SSD_P).transpose(0, 1, 2, 4, 3, 5).reshape(
        n, 2, SSD_HEADS, SSD_N, SSD_P)


SUB = 4
SUB_SHIFT = 2
N_SUB = CHUNK // SUB


def _hgrn_kernel(*refs, zero_init):
    if zero_init:
        (q_ref, zf_ref, zb_ref, v_ref, cg_ref, lb_ref, norm_ref,
         o_ref, sfin_ref, qs_ref, kf_ref, kb_ref, gf_ref, gb_ref, of_ref, ob_ref) = refs
        s0_ref = None
    else:
        (q_ref, zf_ref, zb_ref, v_ref, cg_ref, lb_ref, norm_ref, s0_ref,
         o_ref, sfin_ref, qs_ref, kf_ref, kb_ref, gf_ref, gb_ref, of_ref, ob_ref) = refs
    L = q_ref.shape[0]
    n_chunks = L // CHUNK

    qs_ref[...] = _silu(q_ref[...].astype(F32))
    blk = min(L, 256)
    lo_blk = _chunk_tri(blk, upper=False)
    up_blk = _chunk_tri(blk, upper=True)
    for d, (z_ref, k_ref, g_ref, tri) in enumerate(((zf_ref, kf_ref, gf_ref, lo_blk),
                                                    (zb_ref, kb_ref, gb_ref, up_blk))):
        lbd = lb_ref[d:d + 1, :]
        z = z_ref[...].astype(F32)
        k_ref[...] = (1.0 - lbd) * jax.nn.sigmoid(-z)
        log_f = jnp.log(lbd + (1.0 - lbd) * jax.nn.sigmoid(z))
        for b in range(L // blk):
            rows = slice(b * blk, (b + 1) * blk)
            g_ref[rows, :] = _dot3_left(tri, log_f[rows])

    ti = lax.broadcasted_iota(jnp.int32, (CHUNK, CHUNK), 0)
    si = lax.broadcasted_iota(jnp.int32, (CHUNK, CHUNK), 1)
    tcol = lax.broadcasted_iota(jnp.int32, (CHUNK, 1), 0)

    def step(d, c, st_in, st_out):
        st = st_in()
        rows = pl.ds(pl.multiple_of(c * CHUNK, CHUNK), CHUNK)
        q = qs_ref[rows, :]
        k = (kb_ref if d else kf_ref)[rows, :]
        g = (gb_ref if d else gf_ref)[rows, :]
        v = v_ref[rows, :].astype(F32)
        if d:
            sel = jnp.where(si == ((ti >> SUB_SHIFT) << SUB_SHIFT) + SUB, 1.0, 0.0).astype(BF16)
        else:
            sel = jnp.where(si == ((ti >> SUB_SHIFT) << SUB_SHIFT) - 1, 1.0, 0.0).astype(BF16)
        g_ref_rows = _dot3_left(sel, g)
        g_last = g[0:1] if d else g[CHUNK - 1:CHUNK]
        st_out(st * jnp.exp(g_last) + _dot_tn(v, k * jnp.exp(g_last - g)))
        yield
        q_in = q * jnp.exp(g - g_ref_rows)
        k_parts, q_parts = [], []
        def padded(part, lo):
            hi = lo + part.shape[0]
            pieces = ([jnp.zeros((lo, HGRN_DK), F32)] if lo else []) + [part] + (
                [jnp.zeros((CHUNK - hi, HGRN_DK), F32)] if hi < CHUNK else [])
            return jnp.concatenate(pieces, axis=0).astype(BF16)
        for j in range(1, N_SUB):
            if d:
                edge = CHUNK - SUB * j
                lo, hi = (edge // 8) * 8, CHUNK
                use = tcol[lo:hi] >= edge
                mine_blk = N_SUB - 1 - j
            else:
                edge = SUB * j
                lo, hi = 0, -(-edge // 8) * 8
                use = tcol[lo:hi] < edge
                mine_blk = j
            ref_row = g[edge:edge + 1, :] if d else g[edge - 1:edge, :]
            part = jnp.where(use, k[lo:hi] * jnp.exp(jnp.where(use, ref_row - g[lo:hi], 0.0)), 0.0)
            k_parts.append(padded(part, lo))
            qlo = (mine_blk * SUB // 8) * 8
            qpart = jnp.where((tcol[qlo:qlo + 8] >> SUB_SHIFT) == mine_blk, q_in[qlo:qlo + 8], 0.0)
            q_parts.append(padded(qpart, qlo))
        yield
        att = _dot_nt(jnp.concatenate(q_parts, axis=1), jnp.concatenate(k_parts, axis=1))
        o_state = _dot_nt(q * jnp.exp(g), st)
        yield
        o = _dot(att, v) + o_state
        pos = tcol & (SUB - 1)
        for lag in range(SUB):
            if lag == 0:
                a = jnp.sum(q * k, axis=-1, keepdims=True)
                o = o + a * v
                continue
            shift = (CHUNK - lag) if d else lag
            valid = (pos + lag <= SUB - 1) if d else (pos >= lag)
            k_l = pltpu.roll(k, shift, 0)
            g_l = pltpu.roll(g, shift, 0)
            v_l = pltpu.roll(v, shift, 0)
            dec = jnp.exp(jnp.where(valid, g - g_l, 0.0))
            a = jnp.where(valid, jnp.sum(q * k_l * dec, axis=-1, keepdims=True), 0.0)
            o = o + a * v_l
        yield
        (ob_ref if d else of_ref)[rows, :] = o

    def body(i, carry):
        box = {"f0": carry[0], "b0": carry[1]}
        put = lambda key: (lambda val: box.__setitem__(key, val))
        get = lambda key: (lambda: box[key])
        _interleave(step(0, 2 * i, get("f0"), put("f1")),
                    step(1, n_chunks - 1 - 2 * i, get("b0"), put("b1")),
                    step(0, 2 * i + 1, get("f1"), put("f2")),
                    step(1, n_chunks - 2 - 2 * i, get("b1"), put("b2")))
        return box["f2"], box["b2"]

    if zero_init:
        init = (jnp.zeros((HGRN_DV, HGRN_DK), F32), jnp.zeros((HGRN_DV, HGRN_DK), F32))
    else:
        init = (s0_ref[0, 0, 0], s0_ref[0, 1, 0])
    s_f, s_b = lax.fori_loop(0, n_chunks // 2, body, init)
    sfin_ref[0, 0, 0] = s_f
    sfin_ref[0, 1, 0] = s_b

    o = of_ref[...] + ob_ref[...]
    y = _rms_f32(o, norm_ref[...])
    o_ref[...] = (y * _silu(cg_ref[...].astype(F32))).astype(o_ref.dtype)


def _hgrn(rest, lb, norm, s0, *, n_seq, seq_len, row0):
    L = seq_len
    blk0 = row0 // L
    nh = HGRN_HEADS
    zero_init = s0 is None
    q_off, zf_off, v_off, g_off = R_C_Q // LANE, R_C_F // LANE, R_C_I // LANE, R_C_G // LANE
    zb_off = zf_off + nh
    in_specs = [pl.BlockSpec((L, LANE), lambda s, h: (blk0 + s, q_off + h)),
                pl.BlockSpec((L, LANE), lambda s, h: (blk0 + s, zf_off + h)),
                pl.BlockSpec((L, LANE), lambda s, h: (blk0 + s, zb_off + h)),
                pl.BlockSpec((L, LANE), lambda s, h: (blk0 + s, v_off + h)),
                pl.BlockSpec((L, LANE), lambda s, h: (blk0 + s, g_off + h)),
                pl.BlockSpec((2, HGRN_DK), lambda s, h: (0, h)),
                pl.BlockSpec((1, HGRN_DV), lambda s, h: (0, 0))]
    args = [rest, rest, rest, rest, rest, lb, norm]
    state_spec = pl.BlockSpec((1, 2, 1, HGRN_DV, HGRN_DK), lambda s, h: (s, 0, h, 0, 0))
    if not zero_init:
        in_specs.append(state_spec)
        args.append(s0)
    return pl.pallas_call(
        functools.partial(_hgrn_kernel, zero_init=zero_init),
        grid=(n_seq, nh),
        in_specs=in_specs,
        out_specs=[pl.BlockSpec((L, HGRN_DV), lambda s, h: (s, h)), state_spec],
        out_shape=[jax.ShapeDtypeStruct((n_seq * L, BRANCH_W), BF16),
                   jax.ShapeDtypeStruct((n_seq, 2, nh, HGRN_DV, HGRN_DK), F32)],
        scratch_shapes=[pltpu.VMEM((L, HGRN_DK), F32)] * 5 + [pltpu.VMEM((L, HGRN_DV), F32)] * 2,
        compiler_params=_cparams(("parallel", "arbitrary")),
        name="hgrn_scan",
    )(*args)


def _rms(x, gain):
    xf = x.astype(jnp.float32)
    y = xf * lax.rsqrt(jnp.mean(xf * xf, axis=-1, keepdims=True) + EPS)
    return (y * gain.astype(jnp.float32)).astype(x.dtype)


def _l2n(x):
    return x * lax.rsqrt(jnp.sum(x * x, axis=-1, keepdims=True) + EPS)


def _flip(t, d, axis):
    return jnp.flip(t, axis=axis) if d else t


def _masked_exp(diff, mask):
    return jnp.where(mask, jnp.exp(jnp.where(mask, diff, 0.0)), 0.0)


def _gated_delta_chunked(q, k, v, g, beta, s0):
    nb, nh, L, dk = q.shape
    C = GDN_CHUNK
    n = L // C

    def chunks(t):
        return t.reshape(nb, nh, n, C, *t.shape[3:])

    q, k, v, g, beta = chunks(q), chunks(k), chunks(v), chunks(g), chunks(beta)
    gc = jnp.cumsum(g, axis=-1)
    causal = jnp.tril(jnp.ones((C, C), dtype=bool))
    decay = _masked_exp(gc[..., :, None] - gc[..., None, :], causal)
    kb = k * beta[..., None]
    m_strict = jnp.einsum('bhntd,bhnsd->bhnts', kb, k) * decay * jnp.tril(jnp.ones((C, C), q.dtype), -1)
    a = m_strict + jnp.eye(C, dtype=q.dtype)
    u = jax.lax.linalg.triangular_solve(a, v * beta[..., None], left_side=True, lower=True, unit_diagonal=True)
    w = jax.lax.linalg.triangular_solve(a, kb * jnp.exp(gc)[..., None], left_side=True, lower=True,
                                        unit_diagonal=True)
    qk = jnp.einsum('bhntd,bhnsd->bhnts', q, k) * decay

    def step(S, xs):
        q_c, k_c, u_c, w_c, gc_c, qk_c = xs
        v_new = u_c - jnp.einsum('bhcd,bhde->bhce', w_c, S)
        o = (jnp.einsum('bhcd,bhde->bhce', q_c * jnp.exp(gc_c)[..., None], S)
             + jnp.einsum('bhts,bhse->bhte', qk_c, v_new))
        g_last = gc_c[..., -1:]
        S = S * jnp.exp(g_last)[..., None] + jnp.einsum(
            'bhcd,bhce->bhde', k_c * jnp.exp(g_last - gc_c)[..., None], v_new)
        return S, o

    xs = tuple(jnp.moveaxis(t, 2, 0) for t in (q, k, u, w, gc, qk))
    s_fin, o = lax.scan(step, s0, xs)
    return jnp.moveaxis(o, 0, 2).reshape(nb, nh, L, -1), s_fin


def _ssd_chunked(x, dt, a, bm, cm, h0):
    nb, L = x.shape[:2]
    C = SSD_CHUNK
    n = L // C

    def chunks(t):
        return t.reshape(nb, n, C, *t.shape[2:])

    x, dt, bm, cm = chunks(x), chunks(dt), chunks(bm), chunks(cm)
    acs = jnp.cumsum(dt * a, axis=2)
    causal = jnp.tril(jnp.ones((C, C), dtype=bool))[:, :, None, None]
    seg = _masked_exp(acs[:, :, :, None] - acs[:, :, None, :], causal)
    xdt = x * dt[..., None]
    cb = jnp.einsum('bntgk,bnsgk->bntsg', cm, bm)
    y_diag = jnp.einsum('bntsg,bntsgr,bnsgrp->bntgrp', cb, seg, xdt)
    st_local = jnp.einsum('bncgk,bncgr,bncgrp->bngrkp', bm, jnp.exp(acs[:, :, -1:] - acs), xdt)
    chunk_decay = jnp.exp(acs[:, :, -1])

    def step(h, xs):
        st, cd = xs
        return h * cd[..., None, None] + st, h

    h_fin, h_in = lax.scan(step, h0, (jnp.moveaxis(st_local, 1, 0), jnp.moveaxis(chunk_decay, 1, 0)))
    y_off = jnp.einsum('bncgk,nbgrkp,bncgr->bncgrp', cm, h_in, jnp.exp(acs))
    return (y_diag + y_off).reshape(nb, L, *x.shape[3:]), h_fin


def _hgrn2_chunked(q, k, v, log_f, s0):
    nb, nh, L, dk = q.shape
    C = HGRN_CHUNK
    n = L // C
    causal = jnp.tril(jnp.ones((C, C), dtype=bool))[:, :, None]

    def chunks(t):
        return jnp.moveaxis(t.reshape(nb, nh, n, C, t.shape[-1]), 2, 0)

    def step(S, xs):
        q_c, k_c, v_c, lf_c = xs
        G = jnp.cumsum(lf_c, axis=2)
        G_last = G[:, :, -1:]
        dec = _masked_exp(G[:, :, :, None] - G[:, :, None], causal)
        att = jnp.einsum('bhtd,bhsd,bhtsd->bhts', q_c, k_c, dec)
        o = (jnp.einsum('bhtd,bhde->bhte', q_c * jnp.exp(G), S)
             + jnp.einsum('bhts,bhse->bhte', att, v_c))
        S = S * jnp.exp(G_last)[:, :, 0, :, None] + jnp.einsum(
            'bhsd,bhse->bhde', k_c * jnp.exp(G_last - G), v_c)
        return S, o

    s_fin, o = lax.scan(step, s0, (chunks(q), chunks(k), chunks(v), chunks(log_f)))
    return jnp.moveaxis(o, 0, 2).reshape(nb, nh, L, -1), s_fin


def _mixers_jnp(cx, rest, small, nb, L, s_gdn, s_ssd, s_hgrn, lb, gdn_a_log, gdn_dt_bias, gdn_norm,
                ssd_a_log, ssd_dt_bias, ssd_d, ssd_norm, hgrn_norm, do="abc"):
    f32 = jnp.float32
    cx = cx.astype(f32).reshape(nb, L, -1)
    rest = rest.astype(f32).reshape(nb, L, -1)
    small = small.reshape(nb, L, -1)
    a_q, a_k, a_v, b_x, b_b, b_c = jnp.split(cx, np.cumsum(CONV_SIZES)[:-1].tolist(), axis=-1)
    a_g = rest[..., R_A_G:R_A_G + 1024]
    b_z = rest[..., R_B_Z:R_B_Z + 1024]
    c_q = rest[..., R_C_Q:R_C_Q + 1024]
    c_f = rest[..., R_C_F:R_C_F + 2048]
    c_i = rest[..., R_C_I:R_C_I + 1024]
    c_g = rest[..., R_C_G:R_C_G + 1024]
    a_alpha = small[..., 0:16]
    a_beta = small[..., 16:32]
    b_dt = small[..., 32:64]

    def heads(t, n):
        return t.reshape(nb, L, n, -1).transpose(0, 2, 1, 3)

    q = _l2n(heads(a_q, GDN_HEADS)) * GDN_DK ** -0.5
    k = _l2n(heads(a_k, GDN_HEADS))
    v = heads(a_v, GDN_HEADS)
    alpha = a_alpha.reshape(nb, L, 2, GDN_HEADS).transpose(2, 0, 3, 1)
    beta = jax.nn.sigmoid(a_beta.reshape(nb, L, 2, GDN_HEADS).transpose(2, 0, 3, 1))
    g = -jnp.exp(gdn_a_log.astype(f32))[:, None, :, None] * jax.nn.softplus(
        alpha + gdn_dt_bias.astype(f32)[:, None, :, None])
    outs, s_a = [], []
    for d in range(2 if "a" in do else 0):
        o, s = _gated_delta_chunked(_flip(q, d, 2), _flip(k, d, 2), _flip(v, d, 2), _flip(g[d], d, 2),
                                    _flip(beta[d], d, 2), s_gdn[:, d].astype(f32))
        outs.append(_flip(o, d, 2))
        s_a.append(s)
    if "a" in do:
        o_a = _rms((outs[0] + outs[1]).transpose(0, 2, 1, 3), gdn_norm) * jax.nn.silu(
            a_g.reshape(nb, L, GDN_HEADS, GDN_DV))
        o_a = o_a.reshape(nb * L, BRANCH_W)
    else:
        o_a, s_a = None, [s_gdn[:, 0], s_gdn[:, 1]]

    xs = b_x.reshape(nb, L, SSD_GROUPS, SSD_HPG, SSD_P)
    bm = b_b.reshape(nb, L, SSD_GROUPS, SSD_N)
    cm = b_c.reshape(nb, L, SSD_GROUPS, SSD_N)
    dt = jax.nn.softplus(b_dt.reshape(nb, L, 2, SSD_GROUPS, SSD_HPG)
                         + ssd_dt_bias.astype(f32).reshape(2, SSD_GROUPS, SSD_HPG))
    a_ssd = -jnp.exp(ssd_a_log.astype(f32)).reshape(2, SSD_GROUPS, SSD_HPG)
    ys, s_b = [], []
    for d in range(2):
        y, s = _ssd_chunked(_flip(xs, d, 1), _flip(dt[:, :, d], d, 1), a_ssd[d], _flip(bm, d, 1),
                            _flip(cm, d, 1),
                            s_ssd[:, d].astype(f32).reshape(nb, SSD_GROUPS, SSD_HPG, SSD_N, SSD_P))
        ys.append(_flip(y, d, 1))
        s_b.append(s.reshape(nb, SSD_HEADS, SSD_N, SSD_P))
    y = ys[0] + ys[1] + ssd_d.astype(f32).reshape(SSD_GROUPS, SSD_HPG, 1) * xs
    y = y.reshape(nb, L, BRANCH_W) * jax.nn.silu(b_z)
    o_b = _rms(y.reshape(nb, L, SSD_GROUPS, BRANCH_W // SSD_GROUPS),
               ssd_norm.reshape(SSD_GROUPS, -1)).reshape(nb * L, BRANCH_W)

    qc = jax.nn.silu(heads(c_q, HGRN_HEADS))
    vc = heads(c_i, HGRN_HEADS)
    zf = c_f.reshape(nb, L, 2, HGRN_HEADS, HGRN_DK).transpose(2, 0, 3, 1, 4)
    lbd = lb.reshape(2, 1, HGRN_HEADS, 1, HGRN_DK)
    log_f = jnp.log(lbd + (1.0 - lbd) * jax.nn.sigmoid(zf))
    k_c = (1.0 - lbd) * jax.nn.sigmoid(-zf)
    outs, s_c = [], []
    for d in range(2):
        o, s = _hgrn2_chunked(_flip(qc, d, 2), _flip(k_c[d], d, 2), _flip(vc, d, 2), _flip(log_f[d], d, 2),
                              s_hgrn[:, d].astype(f32))
        outs.append(_flip(o, d, 2))
        s_c.append(s)
    o_c = _rms((outs[0] + outs[1]).transpose(0, 2, 1, 3), hgrn_norm) * jax.nn.silu(
        c_g.reshape(nb, L, HGRN_HEADS, HGRN_DV))
    o_c = o_c.reshape(nb * L, BRANCH_W)
    return (o_a, o_b, o_c), (jnp.stack(s_a, axis=1), jnp.stack(s_b, axis=1), jnp.stack(s_c, axis=1))


def _route_kernel(aff_ref, sel_ref, *, cap):
    bits = pltpu.bitcast(aff_ref[...], jnp.int32)
    n_rows = bits.shape[0]
    row = lax.broadcasted_iota(jnp.int32, bits.shape, 0)
    count = lambda m: jnp.sum(m.astype(jnp.int32), axis=0, keepdims=True)

    def thr_body(i, thr):
        cand = thr | (1 << (30 - i))
        return jnp.where(count(bits >= cand) >= cap, cand, thr)

    thr = lax.fori_loop(0, 31, thr_body, jnp.zeros((1, bits.shape[1]), jnp.int32))
    above = bits > thr
    tie = bits == thr
    need = cap - count(above)
    n_bits = max(1, (n_rows - 1).bit_length())

    def tie_body(i, lim):
        cand = lim | (1 << (n_bits - 1 - i))
        return jnp.where(count(tie & (row < cand)) < need, cand, lim)

    lim = lax.fori_loop(0, n_bits, tie_body, jnp.zeros_like(thr))
    sel = above | (tie & (row <= lim) & (need > 0))
    sel_ref[...] = jnp.where(sel, 1.0, 0.0)


def _route(aff, row0, n_rows):
    cap = CAPACITY_FACTOR * n_rows // N_EXPERTS
    return pl.pallas_call(
        functools.partial(_route_kernel, cap=cap),
        grid=(1,),
        in_specs=[pl.BlockSpec((n_rows, LANE), lambda i: (row0 // n_rows, 0))],
        out_specs=pl.BlockSpec((n_rows, LANE), lambda i: (0, 0)),
        out_shape=jax.ShapeDtypeStruct((n_rows, LANE), F32),
        compiler_params=_cparams(("arbitrary",)),
        name="route_topk",
    )(aff)


FF_TILE = 256
FFN_VMEM_LIMIT = 56 * 1024 * 1024


def _ffn_kernel(x_ref, wg_ref, wu_ref, wd_ref, gate_ref, o_ref):
    f = pl.program_id(1)
    x = x_ref[0]
    g = jnp.dot(x, wg_ref[0].astype(BF16), preferred_element_type=F32)
    u = jnp.dot(x, wu_ref[0].astype(BF16), preferred_element_type=F32)
    y = jnp.dot((_silu(g) * u).astype(BF16), wd_ref[0].astype(BF16), preferred_element_type=F32)

    @pl.when(f == 0)
    def _():
        o_ref[0] = y

    @pl.when(f > 0)
    def _():
        o_ref[0] += y

    @pl.when(f == pl.num_programs(1) - 1)
    def _():
        o_ref[0] = o_ref[0] * gate_ref[0]


def _expert_ffn(xe, w_gu, w_down, gate):
    e, cap, d = xe.shape
    ff = w_down.shape[1]
    nf = ff // FF_TILE
    return pl.pallas_call(
        _ffn_kernel,
        grid=(e, nf),
        in_specs=[pl.BlockSpec((1, cap, d), lambda g, f: (g, 0, 0)),
                  pl.BlockSpec((1, d, FF_TILE), lambda g, f: (g, 0, f)),
                  pl.BlockSpec((1, d, FF_TILE), lambda g, f: (g, 0, nf + f)),
                  pl.BlockSpec((1, FF_TILE, d), lambda g, f: (g, f, 0)),
                  pl.BlockSpec((1, cap, 1), lambda g, f: (g, 0, 0))],
        out_specs=pl.BlockSpec((1, cap, d), lambda g, f: (g, 0, 0)),
        out_shape=jax.ShapeDtypeStruct((e, cap, d), F32),
        compiler_params=_cparams(("parallel", "arbitrary"), FFN_VMEM_LIMIT),
        name="expert_ffn",
    )(xe, w_gu, w_gu, w_down, gate)


def _ec_moe(h2, aff, sel, w_gu, w_down):
    t = h2.shape[0]
    cap = CAPACITY_FACTOR * t // N_EXPERTS
    tok = jnp.arange(t, dtype=jnp.int32)[:, None]
    keys = jnp.where(sel[:, :N_EXPERTS] > 0.5, tok, tok + t).T
    idx = jnp.sort(keys, axis=1)[:, :cap]
    gate = jnp.take_along_axis(aff[:, :N_EXPERTS].T, idx, axis=1)[..., None]
    ye = _expert_ffn(h2[idx], w_gu, w_down, gate)
    return jnp.zeros((t, h2.shape[1]), F32).at[idx.reshape(-1)].add(ye.reshape(-1, h2.shape[1]))


def _resid_kernel(x_ref, m_ref, g_ref, mod_ref, gn_ref, modn_ref, xo_ref, h_ref):
    x = x_ref[...] + mod_ref[0, 5:6, :] * _rms_f32(m_ref[...], g_ref[...])
    xo_ref[...] = x
    h_ref[...] = (_rms_f32(x, gn_ref[...]) * (1.0 + modn_ref[0, 1:2, :]) + modn_ref[0, 0:1, :]).astype(BF16)


def _moe_residual(x1, moe, gain3, mod, gain_next, mod_next, n_ctx_rows, dec_seq, tm=512):
    t, d = x1.shape
    row_spec = pl.BlockSpec((tm, d), lambda i: (i, 0))
    vec_spec = pl.BlockSpec((1, d), lambda i: (0, 0))
    mod_spec = pl.BlockSpec((1, N_MOD, d), lambda i: (_mod_row(i, tm, n_ctx_rows, dec_seq), 0, 0))
    return pl.pallas_call(
        _resid_kernel,
        grid=(t // tm,),
        in_specs=[row_spec, row_spec, vec_spec, mod_spec, vec_spec, mod_spec],
        out_specs=[row_spec, row_spec],
        out_shape=[jax.ShapeDtypeStruct((t, d), F32), jax.ShapeDtypeStruct((t, d), BF16)],
        compiler_params=_cparams(("parallel",)),
        name="moe_residual",
    )(x1, moe, gain3, mod, gain_next, mod_next)


def kernel(x_prompt, x_sample, c, state_gdn, state_ssd, state_hgrn, c_ctx, w_mod, b_mod, norm_gain,
           w_in, conv_w, conv_b, gdn_a_log, gdn_dt_bias, gdn_norm, ssd_a_log, ssd_dt_bias, ssd_d,
           ssd_norm, hgrn_lb, hgrn_norm, w_branch, w_out, w_router, w_gu, w_down):
    nb_c, seq_c, d = x_prompt.shape
    nb_s, seq_s, _ = x_sample.shape
    n_ctx = nb_c * seq_c
    n_smp = nb_s * seq_s

    lb_w = jax.nn.softmax(hgrn_lb.astype(F32), axis=0)
    lb = jnp.cumsum(lb_w, axis=0) - lb_w[:1]

    cond = jnp.concatenate([c_ctx[None, :], c, jnp.zeros((8 - 1 - nb_s, d), F32)], axis=0)
    mod_all = _mod_all(cond, w_mod, b_mod[:, None, :]).reshape(DEPTH, 8, N_MOD, d)

    x = jnp.concatenate([x_prompt.reshape(n_ctx, d), x_sample.reshape(n_smp, d)], axis=0)
    z_gdn = jnp.zeros((nb_c, 2, GDN_HEADS, GDN_DK, GDN_DV), F32)
    z_ssd = jnp.zeros((nb_c, 2, SSD_HEADS, SSD_N, SSD_P), F32)
    z_hgrn = jnp.zeros((nb_c, 2, HGRN_HEADS, HGRN_DK, HGRN_DV), F32)
    new_gdn, new_ssd, new_hgrn = [], [], []

    for l in range(DEPTH):
        mod = mod_all[l]
        wl = w_in[l]
        w_conv = wl[:, :CONV_CH].astype(BF16)
        w_rest = jnp.concatenate([wl[:, OFF_M_G:], wl[:, OFF_A_G:OFF_ALPHA], wl[:, OFF_B_Z:OFF_B_DT],
                                  wl[:, OFF_C_Q:OFF_M_G]], axis=1).astype(BF16)
        w_small = jnp.concatenate([wl[:, OFF_ALPHA:OFF_B_Z], wl[:, OFF_B_DT:OFF_C_Q],
                                   jnp.zeros((d, SMALL_W - 4 * GDN_HEADS - 2 * SSD_HEADS), F32)], axis=1)

        if l == 0:
            h = _hmod(x, norm_gain[l, 0:1], mod, n_ctx, seq_s, i_shift=0, i_scale=1)
        cx = _mm_conv(h, w_conv, conv_w[l], conv_b[l][None, :], n_ctx, seq_c, GRID_W)
        rest = _mm_plain(h, w_rest, tn=1024)
        small = _mm_small(h, w_small)

        mix_args = (lb[l], gdn_a_log[l], gdn_dt_bias[l], gdn_norm[l], ssd_a_log[l], ssd_dt_bias[l], ssd_d[l],
                    ssd_norm[l], hgrn_norm[l])
        gdn_side = _gdn_side_inputs(small, gdn_a_log[l], gdn_dt_bias[l])
        oa_c, sg = _gdn(cx, rest, *gdn_side, gdn_norm[l][None, :], None, n_seq=nb_c, seq_len=seq_c, row0=0)
        oa_s, _ = _gdn(cx, rest, *gdn_side, gdn_norm[l][None, :], state_gdn[:, l], n_seq=nb_s, seq_len=seq_s,
                       row0=n_ctx)
        ssd_side = _ssd_side_inputs(small, ssd_a_log[l], ssd_dt_bias[l], ssd_d[l])
        ob_c, ss = _ssd(cx, rest, *ssd_side, ssd_norm[l][None, :], None, n_seq=nb_c, seq_len=seq_c, row0=0)
        ob_s, _ = _ssd(cx, rest, *ssd_side, ssd_norm[l][None, :], _ssd_state_to_kernel(state_ssd[:, l]),
                       n_seq=nb_s, seq_len=seq_s, row0=n_ctx)
        ss = _ssd_state_from_kernel(ss)
        oc_c, sh = _hgrn(rest, lb[l], hgrn_norm[l][None, :], None, n_seq=nb_c, seq_len=seq_c, row0=0)
        oc_s, _ = _hgrn(rest, lb[l], hgrn_norm[l][None, :], jnp.swapaxes(state_hgrn[:, l], -1, -2),
                        n_seq=nb_s, seq_len=seq_s, row0=n_ctx)
        sh = jnp.swapaxes(sh, -1, -2)
        new_gdn.append(sg)
        new_ssd.append(ss)
        new_hgrn.append(sh)
        branches = jnp.stack([jnp.concatenate([oa_c, oa_s]), jnp.concatenate([ob_c, ob_s]),
                              jnp.concatenate([oc_c, oc_s])])

        merged = _merge(branches, rest, w_branch[l].astype(BF16))
        w_router_p = jnp.concatenate([w_router[l], jnp.zeros((d, LANE - N_EXPERTS), F32)], axis=1)
        x1, h2, aff = _outproj(merged, w_out[l].astype(BF16), x, norm_gain[l], mod, w_router_p, n_ctx, seq_s)

        moe_c = _ec_moe(h2[:n_ctx], aff[:n_ctx], _route(aff, 0, n_ctx), w_gu[l], w_down[l])
        moe_s = _ec_moe(h2[n_ctx:], aff[n_ctx:], _route(aff, n_ctx, n_smp), w_gu[l], w_down[l])
        moe = jnp.concatenate([moe_c, moe_s])
        l_next = min(l + 1, DEPTH - 1)
        x, h = _moe_residual(x1, moe, norm_gain[l, 3:4], mod, norm_gain[l_next, 0:1], mod_all[l_next],
                             n_ctx, seq_s)

    y_prompt = x[:n_ctx].reshape(nb_c, seq_c, d)
    y_sample = x[n_ctx:].reshape(nb_s, seq_s, d)
    return (y_prompt, y_sample, jnp.stack(new_gdn, axis=1), jnp.stack(new_ssd, axis=1),
            jnp.stack(new_hgrn, axis=1))
```

```python
import functools

import numpy as np
import jax
import jax.numpy as jnp
from jax import lax
from jax.experimental import pallas as pl
from jax.experimental.pallas import tpu as pltpu

D_MODEL = 2048
DEPTH = 2
GRID_W = 64
N_BRANCH = 3
BRANCH_W = D_MODEL // 2
GDN_DK = 128
GDN_DV = 128
GDN_HEADS = BRANCH_W // GDN_DV
GDN_CHUNK = 64
SSD_P = 64
SSD_HEADS = BRANCH_W // SSD_P
SSD_GROUPS = 2
SSD_HPG = SSD_HEADS // SSD_GROUPS
SSD_N = 128
SSD_CHUNK = 64
HGRN_DK = 128
HGRN_DV = 128
HGRN_HEADS = BRANCH_W // HGRN_DV
HGRN_CHUNK = 16
CONV_K = 3
N_EXPERTS = 16
EXPERT_FF = D_MODEL // 2
CAPACITY_FACTOR = 2
N_MOD = 6
EPS = 1e-6
GDN_QK_W = GDN_HEADS * GDN_DK
HGRN_QK_W = HGRN_HEADS * HGRN_DK
CONV_SIZES = (GDN_QK_W, GDN_QK_W, BRANCH_W, BRANCH_W, SSD_GROUPS * SSD_N, SSD_GROUPS * SSD_N)
CONV_CH = GDN_QK_W + GDN_QK_W + BRANCH_W + BRANCH_W + 2 * SSD_GROUPS * SSD_N
OFF_A_G = CONV_CH
OFF_ALPHA = OFF_A_G + BRANCH_W
OFF_BETA = OFF_ALPHA + 2 * GDN_HEADS
OFF_B_Z = OFF_BETA + 2 * GDN_HEADS
OFF_B_DT = OFF_B_Z + BRANCH_W
OFF_C_Q = OFF_B_DT + 2 * SSD_HEADS
IN_COLS = OFF_C_Q + 3 * HGRN_QK_W + 2 * BRANCH_W + N_BRANCH * D_MODEL
OFF_M_G = IN_COLS - N_BRANCH * D_MODEL
REST_W = IN_COLS - CONV_CH - 4 * GDN_HEADS - 2 * SSD_HEADS
R_M_G, R_A_G, R_B_Z, R_C_Q, R_C_F, R_C_I, R_C_G = 0, 6144, 7168, 8192, 9216, 11264, 12288
SMALL_W = 128

LANE = 128
VMEM_LIMIT = 48 * 1024 * 1024
BIG_VMEM_LIMIT = 56 * 1024 * 1024
ROW_TILE = 1024

BF16 = jnp.bfloat16
F32 = jnp.float32


def _cparams(sem, vmem_limit=VMEM_LIMIT):
    return pltpu.CompilerParams(dimension_semantics=sem, vmem_limit_bytes=vmem_limit)


def _mod_row(i, tm, n_ctx_rows, dec_seq):
    return jnp.maximum((i * tm - n_ctx_rows) // dec_seq + 1, 0)


def _mod_kernel(c_ref, w_ref, b_ref, o_ref):
    c = c_ref[...]
    a = c * jax.nn.sigmoid(c)
    o_ref[0] = jnp.dot(a, w_ref[0], preferred_element_type=F32, precision=lax.Precision.HIGHEST) + b_ref[0]


def _mod_all(cond, w_mod, b_mod, tn=1024):
    nl, d, n = w_mod.shape
    return pl.pallas_call(
        _mod_kernel,
        grid=(nl, n // tn),
        in_specs=[pl.BlockSpec((8, d), lambda l, j: (0, 0)),
                  pl.BlockSpec((1, d, tn), lambda l, j: (l, 0, j)),
                  pl.BlockSpec((1, 1, tn), lambda l, j: (l, 0, j))],
        out_specs=pl.BlockSpec((1, 8, tn), lambda l, j: (l, 0, j)),
        out_shape=jax.ShapeDtypeStruct((nl, 8, n), F32),
        compiler_params=_cparams(("parallel", "arbitrary")),
        name="mod_vectors",
    )(cond, w_mod, b_mod)


def _rms_f32(x, gain):
    return x * lax.rsqrt(jnp.mean(x * x, axis=-1, keepdims=True) + EPS) * gain


def _hmod_kernel(x_ref, g_ref, mod_ref, h_ref, *, i_shift, i_scale):
    x = x_ref[...]
    y = _rms_f32(x, g_ref[...])
    h_ref[...] = (y * (1.0 + mod_ref[0, i_scale:i_scale + 1, :]) + mod_ref[0, i_shift:i_shift + 1, :]).astype(BF16)


def _hmod(x, gain, mod, n_ctx_rows, dec_seq, i_shift, i_scale, tm=512):
    t, d = x.shape
    return pl.pallas_call(
        functools.partial(_hmod_kernel, i_shift=i_shift, i_scale=i_scale),
        grid=(t // tm,),
        in_specs=[pl.BlockSpec((tm, d), lambda i: (i, 0)),
                  pl.BlockSpec((1, d), lambda i: (0, 0)),
                  pl.BlockSpec((1, N_MOD, d), lambda i: (_mod_row(i, tm, n_ctx_rows, dec_seq), 0, 0))],
        out_specs=pl.BlockSpec((tm, d), lambda i: (i, 0)),
        out_shape=jax.ShapeDtypeStruct((t, d), BF16),
        compiler_params=_cparams(("parallel",)),
        name="rms_modulate",
    )(x, gain, mod)


def _mm_plain_kernel(h_ref, w_ref, o_ref):
    o_ref[...] = jnp.dot(h_ref[...], w_ref[...], preferred_element_type=F32).astype(o_ref.dtype)


def _mm_plain(h, w, tn, out_dtype=BF16, tm=ROW_TILE):
    t, k = h.shape
    n = w.shape[1]
    return pl.pallas_call(
        _mm_plain_kernel,
        grid=(t // tm, n // tn),
        in_specs=[pl.BlockSpec((tm, k), lambda i, j: (i, 0)),
                  pl.BlockSpec((k, tn), lambda i, j: (0, j))],
        out_specs=pl.BlockSpec((tm, tn), lambda i, j: (i, j)),
        out_shape=jax.ShapeDtypeStruct((t, n), out_dtype),
        compiler_params=_cparams(("parallel", "arbitrary")),
        name="proj_rest",
    )(h, w)


def _mm_small_kernel(h_ref, w_ref, o_ref):
    o_ref[...] = jnp.dot(h_ref[...].astype(F32), w_ref[...], preferred_element_type=F32,
                         precision=lax.Precision.HIGHEST)


def _mm_small(h, w, tm=256):
    t, k = h.shape
    n = w.shape[1]
    return pl.pallas_call(
        _mm_small_kernel,
        grid=(t // tm,),
        in_specs=[pl.BlockSpec((tm, k), lambda i: (i, 0)),
                  pl.BlockSpec((k, n), lambda i: (0, 0))],
        out_specs=pl.BlockSpec((tm, n), lambda i: (i, 0)),
        out_shape=jax.ShapeDtypeStruct((t, n), F32),
        compiler_params=_cparams(("parallel",)),
        name="proj_small",
    )(h, w)


def _mm_conv_kernel(h_ref, w_ref, cw_ref, cb_ref, o_ref, wb_ref, *, n_ctx_tiles, ctx_period, grid_period):
    i = pl.program_id(1)

    @pl.when(i == 0)
    def _():
        wb_ref[...] = w_ref[0].astype(BF16)

    acc = jnp.dot(h_ref[...], wb_ref[...], preferred_element_type=F32)
    tm = acc.shape[0]
    period = jnp.where(i < n_ctx_tiles, ctx_period, grid_period)
    pos = lax.broadcasted_iota(jnp.int32, (tm, 1), 0) & (period - 1)
    prev = jnp.where(pos == 0, 0.0, pltpu.roll(acc, 1, 0))
    nxt = jnp.where(pos == period - 1, 0.0, pltpu.roll(acc, tm - 1, 0))
    y = prev * cw_ref[0:1, :] + acc * cw_ref[1:2, :] + nxt * cw_ref[2:3, :] + cb_ref[...]
    o_ref[...] = (y * jax.nn.sigmoid(y)).astype(o_ref.dtype)


def _mm_conv(h, w_in, layer, conv_w, conv_b, n_ctx_rows, ctx_period, grid_period, tn=1152, tm=512):
    t, k = h.shape
    n = CONV_CH
    return pl.pallas_call(
        functools.partial(_mm_conv_kernel, n_ctx_tiles=n_ctx_rows // tm, ctx_period=ctx_period,
                          grid_period=grid_period),
        grid=(n // tn, t // tm),
        in_specs=[pl.BlockSpec((tm, k), lambda j, i: (i, 0)),
                  pl.BlockSpec((1, k, tn), lambda j, i: (layer, 0, j)),
                  pl.BlockSpec((CONV_K, tn), lambda j, i: (0, j)),
                  pl.BlockSpec((1, tn), lambda j, i: (0, j))],
        out_specs=pl.BlockSpec((tm, tn), lambda j, i: (i, j)),
        out_shape=jax.ShapeDtypeStruct((t, n), BF16),
        scratch_shapes=[pltpu.VMEM((k, tn), BF16)],
        compiler_params=_cparams(("arbitrary", "arbitrary")),
        name="proj_conv",
    )(h, w_in, conv_w, conv_b)


def _merge_kernel(o_ref, g_ref, w_ref, out_ref, acc_ref):
    n = pl.program_id(1)
    p = jnp.dot(o_ref[0], w_ref[0], preferred_element_type=F32)
    contrib = jax.nn.sigmoid(g_ref[...].astype(F32)) * p

    @pl.when(n == 0)
    def _():
        acc_ref[...] = contrib

    @pl.when(n > 0)
    def _():
        acc_ref[...] += contrib

    @pl.when(n == N_BRANCH - 1)
    def _():
        out_ref[...] = acc_ref[...].astype(out_ref.dtype)


def _merge(branches, rest, w_branch, tm=ROW_TILE):
    _, t, kb = branches.shape
    d = w_branch.shape[-1]
    g_blk = R_M_G // d
    return pl.pallas_call(
        _merge_kernel,
        grid=(t // tm, N_BRANCH),
        in_specs=[pl.BlockSpec((1, tm, kb), lambda i, n: (n, i, 0)),
                  pl.BlockSpec((tm, d), lambda i, n: (i, g_blk + n)),
                  pl.BlockSpec((1, kb, d), lambda i, n: (n, 0, 0))],
        out_specs=pl.BlockSpec((tm, d), lambda i, n: (i, 0)),
        out_shape=jax.ShapeDtypeStruct((t, d), BF16),
        scratch_shapes=[pltpu.VMEM((tm, d), F32)],
        compiler_params=_cparams(("parallel", "arbitrary")),
        name="branch_merge",
    )(branches, rest, w_branch)


def _outproj_kernel(m_ref, w_ref, x_ref, g_ref, mod_ref, wr_ref, x1_ref, h2_ref, aff_ref):
    out = jnp.dot(m_ref[...], w_ref[...], preferred_element_type=F32)
    x1 = x_ref[...] + mod_ref[0, 2:3, :] * _rms_f32(out, g_ref[1:2, :])
    x1_ref[...] = x1
    h2 = _rms_f32(x1, g_ref[2:3, :]) * (1.0 + mod_ref[0, 4:5, :]) + mod_ref[0, 3:4, :]
    h2_ref[...] = h2.astype(h2_ref.dtype)
    logits = jnp.dot(h2, wr_ref[...], preferred_element_type=F32, precision=lax.Precision.HIGHEST)
    lane = lax.broadcasted_iota(jnp.int32, logits.shape, 1)
    logits = jnp.where(lane < N_EXPERTS, logits, -jnp.inf)
    e = jnp.exp(logits - jnp.max(logits, axis=-1, keepdims=True))
    aff_ref[...] = e / jnp.sum(e, axis=-1, keepdims=True)


def _outproj(merged, w_out, x, gains, mod, w_router_p, n_ctx_rows, dec_seq, tm=512):
    t, d = x.shape
    return pl.pallas_call(
        _outproj_kernel,
        grid=(t // tm,),
        in_specs=[pl.BlockSpec((tm, d), lambda i: (i, 0)),
                  pl.BlockSpec((d, d), lambda i: (0, 0)),
                  pl.BlockSpec((tm, d), lambda i: (i, 0)),
                  pl.BlockSpec((4, d), lambda i: (0, 0)),
                  pl.BlockSpec((1, N_MOD, d), lambda i: (_mod_row(i, tm, n_ctx_rows, dec_seq), 0, 0)),
                  pl.BlockSpec((d, LANE), lambda i: (0, 0))],
        out_specs=[pl.BlockSpec((tm, d), lambda i: (i, 0)),
                   pl.BlockSpec((tm, d), lambda i: (i, 0)),
                   pl.BlockSpec((tm, LANE), lambda i: (i, 0))],
        out_shape=[jax.ShapeDtypeStruct((t, d), F32),
                   jax.ShapeDtypeStruct((t, d), BF16),
                   jax.ShapeDtypeStruct((t, LANE), F32)],
        compiler_params=_cparams(("parallel",), BIG_VMEM_LIMIT),
        name="out_proj",
    )(merged, w_out, x, gains, mod, w_router_p)


CHUNK = 64
CHUNK_SHIFT = 6


def _dot(a, b):
    return jnp.dot(a.astype(BF16), b.astype(BF16), preferred_element_type=F32)


def _dot_nt(a, b):
    return lax.dot_general(a.astype(BF16), b.astype(BF16), (((1,), (1,)), ((), ())),
                           preferred_element_type=F32)


def _dot_tn(a, b):
    return lax.dot_general(a.astype(BF16), b.astype(BF16), (((0,), (0,)), ((), ())),
                           preferred_element_type=F32)


def _interleave(*gens):
    results = [None] * len(gens)
    live = list(range(len(gens)))
    while live:
        for i in list(live):
            try:
                next(gens[i])
            except StopIteration as stop:
                results[i] = stop.value
                live.remove(i)
    return results


def _split3(x):
    x1 = x.astype(BF16)
    r1 = x - x1.astype(F32)
    x2 = r1.astype(BF16)
    x3 = (r1 - x2.astype(F32)).astype(BF16)
    return x1, x2, x3


def _dot3_left(mask_bf16, x):
    x1, x2, x3 = _split3(x)
    f = lambda p: jnp.dot(mask_bf16, p, preferred_element_type=F32)
    return f(x3) + f(x2) + f(x1)


def _dot3_right(x, mask_bf16):
    x1, x2, x3 = _split3(x)
    f = lambda p: jnp.dot(p, mask_bf16, preferred_element_type=F32)
    return f(x3) + f(x2) + f(x1)


def _softplus(x):
    return jnp.maximum(x, 0.0) + jnp.log(1.0 + jnp.exp(-jnp.abs(x)))


def _silu(x):
    return x * jax.nn.sigmoid(x)


def _chunk_tri(n, upper):
    r = lax.broadcasted_iota(jnp.int32, (n, n), 0)
    c = lax.broadcasted_iota(jnp.int32, (n, n), 1)
    same = (r >> CHUNK_SHIFT) == (c >> CHUNK_SHIFT)
    tri = (c >= r) if upper else (c <= r)
    return jnp.where(same & tri, 1.0, 0.0).astype(BF16)


def _unit_tri_inverse(ms):
    n = ms[0].shape[0]
    r = lax.broadcasted_iota(jnp.int32, (n, n), 0)
    c = lax.broadcasted_iota(jnp.int32, (n, n), 1)
    eye = jnp.where(r == c, 1.0, 0.0)
    same = lambda s: (r >> s) == (c >> s)
    m8 = [jnp.where(same(3), m, 0.0) for m in ms]
    m2 = [_dot(a, a) for a in m8]
    p = [eye - a for a in m8]
    m4 = [_dot(a, a) for a in m2]
    p = [a + _dot(a, b) for a, b in zip(p, m2)]
    xs = [a + _dot(a, b) for a, b in zip(p, m4)]
    s = 3
    while (1 << s) < n:
        band = same(s + 1) & jnp.logical_not(same(s))
        t = [_dot(x, jnp.where(band, m, 0.0)) for x, m in zip(xs, ms)]
        xs = [x - _dot(a, x) for x, a in zip(xs, t)]
        s += 1
    return xs


def _gdn_kernel(*refs, zero_init):
    if zero_init:
        (q_ref, k_ref, v_ref, ag_ref, sc_ref, sr_ref, pc_ref, pr_ref, norm_ref,
         o_ref, sfin_ref, qn_ref, kn_ref, gcf_ref, gcb_ref, grf_ref, grb_ref, of_ref, ob_ref,
         uwf_ref, uwb_ref, qkf_ref, qkb_ref) = refs
        s0_ref = None
    else:
        (q_ref, k_ref, v_ref, ag_ref, sc_ref, sr_ref, pc_ref, pr_ref, norm_ref, s0_ref,
         o_ref, sfin_ref, qn_ref, kn_ref, gcf_ref, gcb_ref, grf_ref, grb_ref, of_ref, ob_ref,
         uwf_ref, uwb_ref, qkf_ref, qkb_ref) = refs
    L = q_ref.shape[0]
    n_chunks = L // CHUNK
    heads = range(q_ref.shape[1] // GDN_DK)
    hcols = lambda hb: slice(hb * GDN_DK, (hb + 1) * GDN_DK)

    blk = min(L, 256)
    lo_blk = _chunk_tri(blk, upper=False)
    up_blk = _chunk_tri(blk, upper=True)
    up_c = _chunk_tri(CHUNK, upper=True)
    lo_c = _chunk_tri(CHUNK, upper=False)
    for hb in heads:
        qf = q_ref[:, hcols(hb)].astype(F32)
        qn_ref[:, hcols(hb)] = qf * lax.rsqrt(jnp.sum(qf * qf, axis=-1, keepdims=True) + EPS) * (GDN_DK ** -0.5)
        kf = k_ref[:, hcols(hb)].astype(F32)
        kn_ref[:, hcols(hb)] = kf * lax.rsqrt(jnp.sum(kf * kf, axis=-1, keepdims=True) + EPS)
        gcf_ref[hb] = jnp.zeros(gcf_ref.shape[1:], F32)
        gcf_ref[hb, :, 0:4] = -jnp.exp(pc_ref[hb, 0:1, 0:4]) * _softplus(sc_ref[hb] + pc_ref[hb, 1:2, 0:4])
        for b in range(L // blk):
            rows = slice(b * blk, (b + 1) * blk)
            g_col = gcf_ref[hb, rows, :]
            gcb_ref[hb, rows, :] = _dot3_left(up_blk, g_col)
            gcf_ref[hb, rows, :] = _dot3_left(lo_blk, g_col)
        g_row = -jnp.exp(pr_ref[hb, 0]) * _softplus(sr_ref[hb] + pr_ref[hb, 1])
        g_row = g_row.reshape(n_chunks * 8, CHUNK)
        grf_ref[hb] = _dot3_right(g_row, up_c).reshape(n_chunks, 8, CHUNK)
        grb_ref[hb] = _dot3_right(g_row, lo_c).reshape(n_chunks, 8, CHUNK)

    ti = lax.broadcasted_iota(jnp.int32, (CHUNK, CHUNK), 0)
    si = lax.broadcasted_iota(jnp.int32, (CHUNK, CHUNK), 1)

    def chunk_rows(c):
        return pl.ds(c * CHUNK if isinstance(c, int) else pl.multiple_of(c * CHUNK, CHUNK), CHUNK)

    group = min(n_chunks, 4)

    def prep_group(gi):
        items = [(hb, d, gi * group + j) for j in range(group) for hb in heads for d in range(2)]
        ms, rhs = [], []
        for hb, d, c in items:
            rows = chunk_rows(c)
            q = qn_ref[rows, hcols(hb)]
            k = kn_ref[rows, hcols(hb)]
            v = v_ref[rows, hcols(hb)].astype(F32)
            gc = (gcb_ref if d else gcf_ref)[hb, rows, d:d + 1]
            beta = jax.nn.sigmoid(sc_ref[hb, rows, 2 + d:3 + d])
            gcr = (grb_ref if d else grf_ref)[hb, c][d:d + 1, :]
            incl = (si >= ti) if d else (si <= ti)
            strict = (si > ti) if d else (si < ti)
            decay = jnp.where(incl, jnp.exp(jnp.where(incl, gc - gcr, 0.0)), 0.0)
            kb = k * beta
            ms.append(jnp.where(strict, _dot_nt(kb, k) * decay, 0.0))
            (qkb_ref if d else qkf_ref)[hb, rows, :] = _dot_nt(q, k) * decay
            rhs.append(jnp.concatenate([v * beta, kb * jnp.exp(gc)], axis=1).astype(BF16))
        xs = _unit_tri_inverse(ms)
        for (hb, d, c), x, r in zip(items, xs, rhs):
            (uwb_ref if d else uwf_ref)[hb, chunk_rows(c), :] = _dot(x, r)

    if n_chunks == group:
        prep_group(0)
    else:
        def prep_body(gi, carry):
            prep_group(gi)
            return carry
        lax.fori_loop(0, n_chunks // group, prep_body, 0)

    chains = [(hb, d) for hb in heads for d in range(2)]

    def body(i, carry):
        rows = [chunk_rows(n_chunks - 1 - i if d else i) for _, d in chains]
        gcs = [(gcb_ref if d else gcf_ref)[hb, r, d:d + 1] for (hb, d), r in zip(chains, rows)]
        s_b = [s.astype(BF16) for s in carry]
        uw = [(uwb_ref if d else uwf_ref)[hb, r, :] for (hb, d), r in zip(chains, rows)]
        ws = [_dot(a[:, GDN_DV:], s) for a, s in zip(uw, s_b)]
        v_new = [a[:, :GDN_DV] - w for a, w in zip(uw, ws)]
        g_last = [g[0:1] if d else g[CHUNK - 1:CHUNK] for (_, d), g in zip(chains, gcs)]
        upd = [_dot_tn(kn_ref[r, hcols(hb)] * jnp.exp(gl - g), vn)
               for (hb, _), r, gl, g, vn in zip(chains, rows, g_last, gcs, v_new)]
        new = tuple(s * jnp.exp(gl) + u for s, gl, u in zip(carry, g_last, upd))
        for (hb, d), r, g, s, vn in zip(chains, rows, gcs, s_b, v_new):
            o = _dot(qn_ref[r, hcols(hb)] * jnp.exp(g), s) + _dot((qkb_ref if d else qkf_ref)[hb, r, :], vn)
            (ob_ref if d else of_ref)[r, hcols(hb)] = o
        return new

    if zero_init:
        init = tuple(jnp.zeros((GDN_DK, GDN_DV), F32) for _ in chains)
    else:
        init = tuple(s0_ref[0, d, hb] for hb, d in chains)
    final = lax.fori_loop(0, n_chunks, body, init)
    for (hb, d), s in zip(chains, final):
        sfin_ref[0, d, hb] = s

    for hb in heads:
        o = of_ref[:, hcols(hb)] + ob_ref[:, hcols(hb)]
        y = _rms_f32(o, norm_ref[...])
        o_ref[:, hcols(hb)] = (y * _silu(ag_ref[:, hcols(hb)].astype(F32))).astype(o_ref.dtype)


GDN_HB = 2


def _gdn(cx, rest, sc, sr, pc, pr, norm, s0, *, n_seq, seq_len, row0):
    L = seq_len
    blk0 = row0 // L
    nh = GDN_HEADS
    hb = GDN_HB
    w = hb * GDN_DK
    zero_init = s0 is None
    k_off = GDN_QK_W // w
    v_off = 2 * GDN_QK_W // w
    ag_off = R_A_G // w
    in_specs = [pl.BlockSpec((L, w), lambda s, h: (blk0 + s, h)),
                pl.BlockSpec((L, w), lambda s, h: (blk0 + s, k_off + h)),
                pl.BlockSpec((L, w), lambda s, h: (blk0 + s, v_off + h)),
                pl.BlockSpec((L, w), lambda s, h: (blk0 + s, ag_off + h)),
                pl.BlockSpec((hb, L, 4), lambda s, h: (h, blk0 + s, 0)),
                pl.BlockSpec((hb, L // CHUNK, 8, CHUNK), lambda s, h: (h, blk0 + s, 0, 0)),
                pl.BlockSpec((hb, 8, LANE), lambda s, h: (h, 0, 0)),
                pl.BlockSpec((hb, 2, 8, CHUNK), lambda s, h: (h, 0, 0, 0)),
                pl.BlockSpec((1, GDN_DV), lambda s, h: (0, 0))]
    args = [cx, cx, cx, rest, sc, sr, pc, pr, norm]
    state_spec = pl.BlockSpec((1, 2, hb, GDN_DK, GDN_DV), lambda s, h: (s, 0, h, 0, 0))
    if not zero_init:
        in_specs.append(state_spec)
        args.append(s0)
    return pl.pallas_call(
        functools.partial(_gdn_kernel, zero_init=zero_init),
        grid=(n_seq, nh // hb),
        in_specs=in_specs,
        out_specs=[pl.BlockSpec((L, w), lambda s, h: (s, h)), state_spec],
        out_shape=[jax.ShapeDtypeStruct((n_seq * L, BRANCH_W), BF16),
                   jax.ShapeDtypeStruct((n_seq, 2, nh, GDN_DK, GDN_DV), F32)],
        scratch_shapes=[pltpu.VMEM((L, w), F32), pltpu.VMEM((L, w), F32),
                        pltpu.VMEM((hb, L, LANE), F32), pltpu.VMEM((hb, L, LANE), F32),
                        pltpu.VMEM((hb, L // CHUNK, 8, CHUNK), F32), pltpu.VMEM((hb, L // CHUNK, 8, CHUNK), F32),
                        pltpu.VMEM((L, w), F32), pltpu.VMEM((L, w), F32),
                        pltpu.VMEM((hb, L, 2 * GDN_DV), F32), pltpu.VMEM((hb, L, 2 * GDN_DV), F32),
                        pltpu.VMEM((hb, L, CHUNK), F32), pltpu.VMEM((hb, L, CHUNK), F32)],
        compiler_params=_cparams(("parallel", "arbitrary")),
        name="gdn_scan",
    )(*args)


def _gdn_side_inputs(small, gdn_a_log, gdn_dt_bias):
    t = small.shape[0]
    nh = GDN_HEADS
    ab = small[:, :4 * nh].reshape(t, 2, 2, nh)
    sc = ab.transpose(3, 0, 1, 2).reshape(nh, t, 4)
    rows = ab.transpose(3, 1, 2, 0).reshape(nh, 4, t // CHUNK, CHUNK).transpose(0, 2, 1, 3)
    sr = jnp.concatenate([rows, jnp.zeros_like(rows)], axis=2)
    pc = jnp.zeros((nh, 8, LANE), F32)
    pc = pc.at[:, 0, 0:2].set(gdn_a_log.T).at[:, 1, 0:2].set(gdn_dt_bias.T)
    pr = jnp.zeros((nh, 2, 8, CHUNK), F32)
    pr = pr.at[:, 0, 0:2, :].set(jnp.broadcast_to(gdn_a_log.T[:, :, None], (nh, 2, CHUNK)))
    pr = pr.at[:, 1, 0:2, :].set(jnp.broadcast_to(gdn_dt_bias.T[:, :, None], (nh, 2, CHUNK)))
    return sc, sr, pc, pr


GROUP_W = SSD_HPG * SSD_P


def _ssd_kernel(*refs, zero_init):
    if zero_init:
        (x_ref, b_ref, c_ref, z_ref, dc_ref, dr_ref, pc_ref, pr_ref, dskip_ref, norm_ref,
         o_ref, hfin_ref, dtc_ref, acf_ref, acb_ref, dtr_ref, arf_ref, arb_ref, yf_ref, yb_ref,
         hf_ref, hb_ref) = refs
        h0_ref = None
    else:
        (x_ref, b_ref, c_ref, z_ref, dc_ref, dr_ref, pc_ref, pr_ref, dskip_ref, norm_ref, h0_ref,
         o_ref, hfin_ref, dtc_ref, acf_ref, acb_ref, dtr_ref, arf_ref, arb_ref, yf_ref, yb_ref,
         hf_ref, hb_ref) = refs
    L = x_ref.shape[0]
    n_chunks = L // CHUNK
    nh = SSD_HPG

    dtc_ref[...] = jnp.zeros_like(dtc_ref)
    dtc_ref[:, 0:2 * nh] = _softplus(dc_ref[0] + pc_ref[0, 1:2, 0:2 * nh])
    acf_ref[...] = jnp.zeros_like(acf_ref)
    acf_ref[:, 0:2 * nh] = -jnp.exp(pc_ref[0, 0:1, 0:2 * nh]) * dtc_ref[:, 0:2 * nh]
    blk = min(L, 256)
    lo_blk = _chunk_tri(blk, upper=False)
    up_blk = _chunk_tri(blk, upper=True)
    for b in range(L // blk):
        rows = slice(b * blk, (b + 1) * blk)
        da = acf_ref[rows, :]
        acb_ref[rows, :] = _dot3_left(up_blk, da)
        acf_ref[rows, :] = _dot3_left(lo_blk, da)
    dt_row = _softplus(dr_ref[0] + pr_ref[0, 1])
    dtr_ref[...] = dt_row
    da_row = (-jnp.exp(pr_ref[0, 0]) * dt_row).reshape(n_chunks * 2 * nh, CHUNK)
    arf_ref[...] = _dot3_right(da_row, _chunk_tri(CHUNK, upper=True)).reshape(n_chunks, 2 * nh, CHUNK)
    arb_ref[...] = _dot3_right(da_row, _chunk_tri(CHUNK, upper=False)).reshape(n_chunks, 2 * nh, CHUNK)

    if zero_init:
        hf_ref[...] = jnp.zeros_like(hf_ref)
        hb_ref[...] = jnp.zeros_like(hb_ref)
    else:
        hf_ref[...] = h0_ref[0, 0, 0]
        hb_ref[...] = h0_ref[0, 1, 0]

    ti = lax.broadcasted_iota(jnp.int32, (CHUNK, CHUNK), 0)
    si = lax.broadcasted_iota(jnp.int32, (CHUNK, CHUNK), 1)
    ej = lax.broadcasted_iota(jnp.int32, (LANE, GROUP_W), 0)
    ec = lax.broadcasted_iota(jnp.int32, (LANE, GROUP_W), 1)

    def step(d, c):
        rows = pl.ds(pl.multiple_of(c * CHUNK, CHUNK), CHUNK)
        h_ref = hb_ref if d else hf_ref
        x = x_ref[rows, :]
        bm = b_ref[rows, :]
        cm = c_ref[rows, :]
        acs = (acb_ref if d else acf_ref)[rows, :]
        acs_r = (arb_ref if d else arf_ref)[c][d * nh:(d + 1) * nh, :]
        dt_r = dtr_ref[c][d * nh:(d + 1) * nh, :]
        incl = (si >= ti) if d else (si <= ti)
        cb = _dot_nt(cm, bm)
        spread = jnp.where(ej == d * nh + (ec >> 6), 1.0, 0.0).astype(BF16)
        last = acs[0:1] if d else acs[CHUNK - 1:CHUNK]
        p_full = _dot3_right(dtc_ref[rows, :] * jnp.exp(last - acs), spread)
        e_full = _dot3_right(jnp.exp(acs), spread)
        cd_full = _dot3_right(jnp.broadcast_to(jnp.exp(last), (8, LANE)), spread)[0:1]
        h_in = h_ref[...]
        y_off = _dot(cm, h_in)
        yield
        ys = []
        for h in range(nh):
            diff = acs[:, d * nh + h:d * nh + h + 1] - acs_r[h:h + 1, :]
            seg = jnp.where(incl, jnp.exp(jnp.where(incl, diff, 0.0)), 0.0)
            ys.append(_dot(cb * seg * dt_r[h:h + 1, :], x[:, h * SSD_P:(h + 1) * SSD_P]))
        st = _dot_tn(bm, x.astype(F32) * p_full)
        yield
        (yb_ref if d else yf_ref)[rows, :] = jnp.concatenate(ys, axis=1) + y_off * e_full
        h_ref[...] = h_in * cd_full + st

    def body(i, carry):
        _interleave(step(0, i), step(1, n_chunks - 1 - i))
        return carry

    lax.fori_loop(0, n_chunks, body, 0)
    hfin_ref[0, 0, 0] = hf_ref[...]
    hfin_ref[0, 1, 0] = hb_ref[...]

    y = yf_ref[...] + yb_ref[...] + dskip_ref[0] * x_ref[...].astype(F32)
    y = y * _silu(z_ref[...].astype(F32))
    o_ref[...] = _rms_f32(y, norm_ref[...]).astype(o_ref.dtype)


def _ssd(cx, rest, dc, dr, pc, pr, dskip, norm, h0, *, n_seq, seq_len, row0):
    L = seq_len
    blk0 = row0 // L
    ng = SSD_GROUPS
    zero_init = h0 is None
    x_off = (2 * GDN_QK_W + BRANCH_W) // GROUP_W
    b_off = (2 * GDN_QK_W + 2 * BRANCH_W) // SSD_N
    c_off = b_off + ng
    z_off = R_B_Z // GROUP_W
    in_specs = [pl.BlockSpec((L, GROUP_W), lambda s, g: (blk0 + s, x_off + g)),
                pl.BlockSpec((L, SSD_N), lambda s, g: (blk0 + s, b_off + g)),
                pl.BlockSpec((L, SSD_N), lambda s, g: (blk0 + s, c_off + g)),
                pl.BlockSpec((L, GROUP_W), lambda s, g: (blk0 + s, z_off + g)),
                pl.BlockSpec((1, L, 2 * SSD_HPG), lambda s, g: (g, blk0 + s, 0)),
                pl.BlockSpec((1, L // CHUNK, 2 * SSD_HPG, CHUNK), lambda s, g: (g, blk0 + s, 0, 0)),
                pl.BlockSpec((1, 8, LANE), lambda s, g: (g, 0, 0)),
                pl.BlockSpec((1, 2, 2 * SSD_HPG, CHUNK), lambda s, g: (g, 0, 0, 0)),
                pl.BlockSpec((1, 1, GROUP_W), lambda s, g: (g, 0, 0)),
                pl.BlockSpec((1, GROUP_W), lambda s, g: (0, g))]
    args = [cx, cx, cx, rest, dc, dr, pc, pr, dskip, norm]
    state_spec = pl.BlockSpec((1, 2, 1, SSD_N, GROUP_W), lambda s, g: (s, 0, g, 0, 0))
    if not zero_init:
        in_specs.append(state_spec)
        args.append(h0)
    return pl.pallas_call(
        functools.partial(_ssd_kernel, zero_init=zero_init),
        grid=(n_seq, ng),
        in_specs=in_specs,
        out_specs=[pl.BlockSpec((L, GROUP_W), lambda s, g: (s, g)), state_spec],
        out_shape=[jax.ShapeDtypeStruct((n_seq * L, BRANCH_W), BF16),
                   jax.ShapeDtypeStruct((n_seq, 2, ng, SSD_N, GROUP_W), F32)],
        scratch_shapes=[pltpu.VMEM((L, LANE), F32), pltpu.VMEM((L, LANE), F32), pltpu.VMEM((L, LANE), F32),
                        pltpu.VMEM((L // CHUNK, 2 * SSD_HPG, CHUNK), F32),
                        pltpu.VMEM((L // CHUNK, 2 * SSD_HPG, CHUNK), F32),
                        pltpu.VMEM((L // CHUNK, 2 * SSD_HPG, CHUNK), F32),
                        pltpu.VMEM((L, GROUP_W), F32), pltpu.VMEM((L, GROUP_W), F32),
                        pltpu.VMEM((SSD_N, GROUP_W), F32), pltpu.VMEM((SSD_N, GROUP_W), F32)],
        compiler_params=_cparams(("parallel", "arbitrary")),
        name="ssd_scan",
    )(*args)


def _ssd_side_inputs(small, ssd_a_log, ssd_dt_bias, ssd_d):
    t = small.shape[0]
    ng, nh = SSD_GROUPS, SSD_HPG
    off = 4 * GDN_HEADS
    dt = small[:, off:off + 2 * SSD_HEADS].reshape(t, 2, ng, nh)
    dc = dt.transpose(2, 0, 1, 3).reshape(ng, t, 2 * nh)
    dr = dt.transpose(2, 1, 3, 0).reshape(ng, 2 * nh, t // CHUNK, CHUNK).transpose(0, 2, 1, 3)
    a = ssd_a_log.reshape(2, ng, nh).transpose(1, 0, 2).reshape(ng, 2 * nh)
    bias = ssd_dt_bias.reshape(2, ng, nh).transpose(1, 0, 2).reshape(ng, 2 * nh)
    pc = jnp.zeros((ng, 8, LANE), F32).at[:, 0, 0:2 * nh].set(a).at[:, 1, 0:2 * nh].set(bias)
    pr = jnp.stack([jnp.broadcast_to(a[:, :, None], (ng, 2 * nh, CHUNK)),
                    jnp.broadcast_to(bias[:, :, None], (ng, 2 * nh, CHUNK))], axis=1)
    dskip = jnp.repeat(ssd_d.reshape(ng, nh), SSD_P, axis=1)[:, None, :]
    return dc, dr, pc, pr, dskip


def _ssd_state_to_kernel(s):
    n = s.shape[0]
    return s.reshape(n, 2, SSD_GROUPS, SSD_HPG, SSD_N, SSD_P).transpose(0, 1, 2, 4, 3, 5).reshape(
        n, 2, SSD_GROUPS, SSD_N, GROUP_W)


def _ssd_state_from_kernel(s):
    n = s.shape[0]
    return s.reshape(n, 2, SSD_GROUPS, SSD_N, SSD_HPG, SSD_P).transpose(0, 1, 2, 4, 3, 5).reshape(
        n, 2, SSD_HEADS, SSD_N, SSD_P)


SUB = 4
SUB_SHIFT = 2
N_SUB = CHUNK // SUB


def _hgrn_kernel(*refs, zero_init):
    if zero_init:
        (q_ref, zf_ref, zb_ref, v_ref, cg_ref, lb_ref, norm_ref,
         o_ref, sfin_ref, qs_ref, kf_ref, kb_ref, gf_ref, gb_ref, of_ref, ob_ref) = refs
        s0_ref = None
    else:
        (q_ref, zf_ref, zb_ref, v_ref, cg_ref, lb_ref, norm_ref, s0_ref,
         o_ref, sfin_ref, qs_ref, kf_ref, kb_ref, gf_ref, gb_ref, of_ref, ob_ref) = refs
    L = q_ref.shape[0]
    n_chunks = L // CHUNK

    qs_ref[...] = _silu(q_ref[...].astype(F32))
    blk = min(L, 256)
    lo_blk = _chunk_tri(blk, upper=False)
    up_blk = _chunk_tri(blk, upper=True)
    for d, (z_ref, k_ref, g_ref, tri) in enumerate(((zf_ref, kf_ref, gf_ref, lo_blk),
                                                    (zb_ref, kb_ref, gb_ref, up_blk))):
        lbd = lb_ref[d:d + 1, :]
        z = z_ref[...].astype(F32)
        k_ref[...] = (1.0 - lbd) * jax.nn.sigmoid(-z)
        log_f = jnp.log(lbd + (1.0 - lbd) * jax.nn.sigmoid(z))
        for b in range(L // blk):
            rows = slice(b * blk, (b + 1) * blk)
            g_ref[rows, :] = _dot3_left(tri, log_f[rows])

    ti = lax.broadcasted_iota(jnp.int32, (CHUNK, CHUNK), 0)
    si = lax.broadcasted_iota(jnp.int32, (CHUNK, CHUNK), 1)
    tcol = lax.broadcasted_iota(jnp.int32, (CHUNK, 1), 0)

    def padded(part, lo):
        hi = lo + part.shape[0]
        pieces = ([jnp.zeros((lo, HGRN_DK), F32)] if lo else []) + [part] + (
            [jnp.zeros((CHUNK - hi, HGRN_DK), F32)] if hi < CHUNK else [])
        return jnp.concatenate(pieces, axis=0).astype(BF16)

    def step(d, c, st_in, st_out):
        st = st_in()
        rows = pl.ds(pl.multiple_of(c * CHUNK, CHUNK), CHUNK)
        q = qs_ref[rows, :]
        k = (kb_ref if d else kf_ref)[rows, :]
        g = (gb_ref if d else gf_ref)[rows, :]
        v = v_ref[rows, :].astype(F32)
        if d:
            sel = jnp.where(si == ((ti >> SUB_SHIFT) << SUB_SHIFT) + SUB, 1.0, 0.0).astype(BF16)
        else:
            sel = jnp.where(si == ((ti >> SUB_SHIFT) << SUB_SHIFT) - 1, 1.0, 0.0).astype(BF16)
        g_ref_rows = _dot3_left(sel, g)
        g_last = g[0:1] if d else g[CHUNK - 1:CHUNK]
        st_out(st * jnp.exp(g_last) + _dot_tn(v, k * jnp.exp(g_last - g)))
        yield
        q_in = q * jnp.exp(g - g_ref_rows)
        k_parts, q_parts = [], []
        for j in range(1, N_SUB):
            if d:
                edge = CHUNK - SUB * j
                lo, hi = (edge // 8) * 8, CHUNK
                use = tcol[lo:hi] >= edge
                mine_blk = N_SUB - 1 - j
            else:
                edge = SUB * j
                lo, hi = 0, -(-edge // 8) * 8
                use = tcol[lo:hi] < edge
                mine_blk = j
            ref_row = g[edge:edge + 1, :] if d else g[edge - 1:edge, :]
            part = jnp.where(use, k[lo:hi] * jnp.exp(jnp.where(use, ref_row - g[lo:hi], 0.0)), 0.0)
            k_parts.append(padded(part, lo))
            qlo = (mine_blk * SUB // 8) * 8
            qpart = jnp.where((tcol[qlo:qlo + 8] >> SUB_SHIFT) == mine_blk, q_in[qlo:qlo + 8], 0.0)
            q_parts.append(padded(qpart, qlo))
        yield
        att = _dot_nt(jnp.concatenate(q_parts, axis=1), jnp.concatenate(k_parts, axis=1))
        o_state = _dot_nt(q * jnp.exp(g), st)
        yield
        o = _dot(att, v) + o_state
        pos = tcol & (SUB - 1)
        for lag in range(SUB):
            if lag == 0:
                a = jnp.sum(q * k, axis=-1, keepdims=True)
                o = o + a * v
                continue
            shift = (CHUNK - lag) if d else lag
            valid = (pos + lag <= SUB - 1) if d else (pos >= lag)
            k_l = pltpu.roll(k, shift, 0)
            g_l = pltpu.roll(g, shift, 0)
            v_l = pltpu.roll(v, shift, 0)
            dec = jnp.exp(jnp.where(valid, g - g_l, 0.0))
            a = jnp.where(valid, jnp.sum(q * k_l * dec, axis=-1, keepdims=True), 0.0)
            o = o + a * v_l
        yield
        (ob_ref if d else of_ref)[rows, :] = o

    def body(i, carry):
        box = {"f0": carry[0], "b0": carry[1]}
        put = lambda key: (lambda val: box.__setitem__(key, val))
        get = lambda key: (lambda: box[key])
        _interleave(step(0, 2 * i, get("f0"), put("f1")),
                    step(1, n_chunks - 1 - 2 * i, get("b0"), put("b1")),
                    step(0, 2 * i + 1, get("f1"), put("f2")),
                    step(1, n_chunks - 2 - 2 * i, get("b1"), put("b2")))
        return box["f2"], box["b2"]

    if zero_init:
        init = (jnp.zeros((HGRN_DV, HGRN_DK), F32), jnp.zeros((HGRN_DV, HGRN_DK), F32))
    else:
        init = (s0_ref[0, 0, 0], s0_ref[0, 1, 0])
    s_f, s_b = lax.fori_loop(0, n_chunks // 2, body, init)
    sfin_ref[0, 0, 0] = s_f
    sfin_ref[0, 1, 0] = s_b

    o = of_ref[...] + ob_ref[...]
    y = _rms_f32(o, norm_ref[...])
    o_ref[...] = (y * _silu(cg_ref[...].astype(F32))).astype(o_ref.dtype)


def _hgrn(rest, lb, norm, s0, *, n_seq, seq_len, row0):
    L = seq_len
    blk0 = row0 // L
    nh = HGRN_HEADS
    zero_init = s0 is None
    q_off, zf_off, v_off, g_off = R_C_Q // LANE, R_C_F // LANE, R_C_I // LANE, R_C_G // LANE
    zb_off = zf_off + nh
    in_specs = [pl.BlockSpec((L, LANE), lambda s, h: (blk0 + s, q_off + h)),
                pl.BlockSpec((L, LANE), lambda s, h: (blk0 + s, zf_off + h)),
                pl.BlockSpec((L, LANE), lambda s, h: (blk0 + s, zb_off + h)),
                pl.BlockSpec((L, LANE), lambda s, h: (blk0 + s, v_off + h)),
                pl.BlockSpec((L, LANE), lambda s, h: (blk0 + s, g_off + h)),
                pl.BlockSpec((2, HGRN_DK), lambda s, h: (0, h)),
                pl.BlockSpec((1, HGRN_DV), lambda s, h: (0, 0))]
    args = [rest, rest, rest, rest, rest, lb, norm]
    state_spec = pl.BlockSpec((1, 2, 1, HGRN_DV, HGRN_DK), lambda s, h: (s, 0, h, 0, 0))
    if not zero_init:
        in_specs.append(state_spec)
        args.append(s0)
    return pl.pallas_call(
        functools.partial(_hgrn_kernel, zero_init=zero_init),
        grid=(n_seq, nh),
        in_specs=in_specs,
        out_specs=[pl.BlockSpec((L, HGRN_DV), lambda s, h: (s, h)), state_spec],
        out_shape=[jax.ShapeDtypeStruct((n_seq * L, BRANCH_W), BF16),
                   jax.ShapeDtypeStruct((n_seq, 2, nh, HGRN_DV, HGRN_DK), F32)],
        scratch_shapes=[pltpu.VMEM((L, HGRN_DK), F32)] * 5 + [pltpu.VMEM((L, HGRN_DV), F32)] * 2,
        compiler_params=_cparams(("parallel", "arbitrary")),
        name="hgrn_scan",
    )(*args)


def _route_kernel(aff_ref, sel_ref, *, cap):
    bits = pltpu.bitcast(aff_ref[...], jnp.int32)
    n_rows = bits.shape[0]
    row = lax.broadcasted_iota(jnp.int32, bits.shape, 0)
    count = lambda m: jnp.sum(m.astype(jnp.int32), axis=0, keepdims=True)

    def thr_body(i, thr):
        cand = thr | (1 << (30 - i))
        return jnp.where(count(bits >= cand) >= cap, cand, thr)

    thr = lax.fori_loop(0, 31, thr_body, jnp.zeros((1, bits.shape[1]), jnp.int32))
    above = bits > thr
    tie = bits == thr
    need = cap - count(above)
    n_bits = max(1, (n_rows - 1).bit_length())

    def tie_body(i, lim):
        cand = lim | (1 << (n_bits - 1 - i))
        return jnp.where(count(tie & (row < cand)) < need, cand, lim)

    lim = lax.fori_loop(0, n_bits, tie_body, jnp.zeros_like(thr))
    sel = above | (tie & (row <= lim) & (need > 0))
    sel_ref[...] = jnp.where(sel, 1.0, 0.0)


def _route(aff, row0, n_rows):
    cap = CAPACITY_FACTOR * n_rows // N_EXPERTS
    return pl.pallas_call(
        functools.partial(_route_kernel, cap=cap),
        grid=(1,),
        in_specs=[pl.BlockSpec((n_rows, LANE), lambda i: (row0 // n_rows, 0))],
        out_specs=pl.BlockSpec((n_rows, LANE), lambda i: (0, 0)),
        out_shape=jax.ShapeDtypeStruct((n_rows, LANE), F32),
        compiler_params=_cparams(("arbitrary",)),
        name="route_topk",
    )(aff)


FF_TILE = 256


def _ffn_kernel(x_ref, wg_ref, wu_ref, wd_ref, gate_ref, o_ref):
    f = pl.program_id(1)
    x = x_ref[0]
    g = jnp.dot(x, wg_ref[0].astype(BF16), preferred_element_type=F32)
    u = jnp.dot(x, wu_ref[0].astype(BF16), preferred_element_type=F32)
    y = jnp.dot((_silu(g) * u).astype(BF16), wd_ref[0].astype(BF16), preferred_element_type=F32)

    @pl.when(f == 0)
    def _():
        o_ref[0] = y

    @pl.when(f > 0)
    def _():
        o_ref[0] += y

    @pl.when(f == pl.num_programs(1) - 1)
    def _():
        o_ref[0] = o_ref[0] * gate_ref[0]


def _expert_ffn(xe, w_gu, w_down, gate):
    e, cap, d = xe.shape
    ff = w_down.shape[1]
    nf = ff // FF_TILE
    return pl.pallas_call(
        _ffn_kernel,
        grid=(e, nf),
        in_specs=[pl.BlockSpec((1, cap, d), lambda g, f: (g, 0, 0)),
                  pl.BlockSpec((1, d, FF_TILE), lambda g, f: (g, 0, f)),
                  pl.BlockSpec((1, d, FF_TILE), lambda g, f: (g, 0, nf + f)),
                  pl.BlockSpec((1, FF_TILE, d), lambda g, f: (g, f, 0)),
                  pl.BlockSpec((1, cap, 1), lambda g, f: (g, 0, 0))],
        out_specs=pl.BlockSpec((1, cap, d), lambda g, f: (g, 0, 0)),
        out_shape=jax.ShapeDtypeStruct((e, cap, d), F32),
        compiler_params=_cparams(("parallel", "arbitrary"), BIG_VMEM_LIMIT),
        name="expert_ffn",
    )(xe, w_gu, w_gu, w_down, gate)


def _ec_moe(h2, aff, sel, w_gu, w_down):
    t = h2.shape[0]
    cap = CAPACITY_FACTOR * t // N_EXPERTS
    tok = jnp.arange(t, dtype=jnp.int32)[:, None]
    keys = jnp.where(sel[:, :N_EXPERTS] > 0.5, tok, tok + t).T
    idx = jnp.sort(keys, axis=1)[:, :cap]
    gate = jnp.take_along_axis(aff[:, :N_EXPERTS].T, idx, axis=1)[..., None]
    words = lax.bitcast_convert_type(h2.reshape(t, h2.shape[1] // 2, 2), jnp.uint32)
    xe = lax.bitcast_convert_type(words[idx], BF16).reshape(N_EXPERTS, cap, h2.shape[1])
    ye = _expert_ffn(xe, w_gu, w_down, gate)
    return jnp.zeros((t, h2.shape[1]), F32).at[idx.reshape(-1)].add(ye.reshape(-1, h2.shape[1]))


def _resid_kernel(x_ref, m_ref, g_ref, mod_ref, gn_ref, modn_ref, xo_ref, h_ref):
    x = x_ref[...] + mod_ref[0, 5:6, :] * _rms_f32(m_ref[...], g_ref[...])
    xo_ref[...] = x
    h_ref[...] = (_rms_f32(x, gn_ref[...]) * (1.0 + modn_ref[0, 1:2, :]) + modn_ref[0, 0:1, :]).astype(BF16)


def _moe_residual(x1, moe, gain3, mod, gain_next, mod_next, n_ctx_rows, dec_seq, tm=512):
    t, d = x1.shape
    row_spec = pl.BlockSpec((tm, d), lambda i: (i, 0))
    vec_spec = pl.BlockSpec((1, d), lambda i: (0, 0))
    mod_spec = pl.BlockSpec((1, N_MOD, d), lambda i: (_mod_row(i, tm, n_ctx_rows, dec_seq), 0, 0))
    return pl.pallas_call(
        _resid_kernel,
        grid=(t // tm,),
        in_specs=[row_spec, row_spec, vec_spec, mod_spec, vec_spec, mod_spec],
        out_specs=[row_spec, row_spec],
        out_shape=[jax.ShapeDtypeStruct((t, d), F32), jax.ShapeDtypeStruct((t, d), BF16)],
        compiler_params=_cparams(("parallel",)),
        name="moe_residual",
    )(x1, moe, gain3, mod, gain_next, mod_next)


def kernel(x_prompt, x_sample, c, state_gdn, state_ssd, state_hgrn, c_ctx, w_mod, b_mod, norm_gain,
           w_in, conv_w, conv_b, gdn_a_log, gdn_dt_bias, gdn_norm, ssd_a_log, ssd_dt_bias, ssd_d,
           ssd_norm, hgrn_lb, hgrn_norm, w_branch, w_out, w_router, w_gu, w_down):
    nb_c, seq_c, d = x_prompt.shape
    nb_s, seq_s, _ = x_sample.shape
    n_ctx = nb_c * seq_c
    n_smp = nb_s * seq_s

    lb_w = jax.nn.softmax(hgrn_lb.astype(F32), axis=0)
    lb = jnp.cumsum(lb_w, axis=0) - lb_w[:1]

    cond = jnp.concatenate([c_ctx[None, :], c, jnp.zeros((8 - 1 - nb_s, d), F32)], axis=0)
    mod_all = _mod_all(cond, w_mod, b_mod[:, None, :]).reshape(DEPTH, 8, N_MOD, d)

    x = jnp.concatenate([x_prompt.reshape(n_ctx, d), x_sample.reshape(n_smp, d)], axis=0)
    new_gdn, new_ssd, new_hgrn = [], [], []

    for l in range(DEPTH):
        mod = mod_all[l]
        wl = w_in[l]
        w_rest = jnp.concatenate([wl[:, OFF_M_G:], wl[:, OFF_A_G:OFF_ALPHA], wl[:, OFF_B_Z:OFF_B_DT],
                                  wl[:, OFF_C_Q:OFF_M_G]], axis=1).astype(BF16)
        w_small = jnp.concatenate([wl[:, OFF_ALPHA:OFF_B_Z], wl[:, OFF_B_DT:OFF_C_Q],
                                   jnp.zeros((d, SMALL_W - 4 * GDN_HEADS - 2 * SSD_HEADS), F32)], axis=1)

        if l == 0:
            h = _hmod(x, norm_gain[l, 0:1], mod, n_ctx, seq_s, i_shift=0, i_scale=1)
        cx = _mm_conv(h, w_in, l, conv_w[l], conv_b[l][None, :], n_ctx, seq_c, GRID_W)
        rest = _mm_plain(h, w_rest, tn=1024)
        small = _mm_small(h, w_small)

        gdn_side = _gdn_side_inputs(small, gdn_a_log[l], gdn_dt_bias[l])
        oa_c, sg = _gdn(cx, rest, *gdn_side, gdn_norm[l][None, :], None, n_seq=nb_c, seq_len=seq_c, row0=0)
        oa_s, _ = _gdn(cx, rest, *gdn_side, gdn_norm[l][None, :], state_gdn[:, l], n_seq=nb_s, seq_len=seq_s,
                       row0=n_ctx)
        ssd_side = _ssd_side_inputs(small, ssd_a_log[l], ssd_dt_bias[l], ssd_d[l])
        ob_c, ss = _ssd(cx, rest, *ssd_side, ssd_norm[l][None, :], None, n_seq=nb_c, seq_len=seq_c, row0=0)
        ob_s, _ = _ssd(cx, rest, *ssd_side, ssd_norm[l][None, :], _ssd_state_to_kernel(state_ssd[:, l]),
                       n_seq=nb_s, seq_len=seq_s, row0=n_ctx)
        ss = _ssd_state_from_kernel(ss)
        oc_c, sh = _hgrn(rest, lb[l], hgrn_norm[l][None, :], None, n_seq=nb_c, seq_len=seq_c, row0=0)
        oc_s, _ = _hgrn(rest, lb[l], hgrn_norm[l][None, :], jnp.swapaxes(state_hgrn[:, l], -1, -2),
                        n_seq=nb_s, seq_len=seq_s, row0=n_ctx)
        sh = jnp.swapaxes(sh, -1, -2)
        new_gdn.append(sg)
        new_ssd.append(ss)
        new_hgrn.append(sh)
        branches = jnp.stack([jnp.concatenate([oa_c, oa_s]), jnp.concatenate([ob_c, ob_s]),
                              jnp.concatenate([oc_c, oc_s])])

        merged = _merge(branches, rest, w_branch[l].astype(BF16))
        w_router_p = jnp.concatenate([w_router[l], jnp.zeros((d, LANE - N_EXPERTS), F32)], axis=1)
        x1, h2, aff = _outproj(merged, w_out[l].astype(BF16), x, norm_gain[l], mod, w_router_p, n_ctx, seq_s)

        moe_c = _ec_moe(h2[:n_ctx], aff[:n_ctx], _route(aff, 0, n_ctx), w_gu[l], w_down[l])
        moe_s = _ec_moe(h2[n_ctx:], aff[n_ctx:], _route(aff, n_ctx, n_smp), w_gu[l], w_down[l])
        moe = jnp.concatenate([moe_c, moe_s])
        l_next = min(l + 1, DEPTH - 1)
        x, h = _moe_residual(x1, moe, norm_gain[l, 3:4], mod, norm_gain[l_next, 0:1], mod_all[l_next],
                             n_ctx, seq_s)

    y_prompt = x[:n_ctx].reshape(nb_c, seq_c, d)
    y_sample = x[n_ctx:].reshape(nb_s, seq_s, d)
    return (y_prompt, y_sample, jnp.stack(new_gdn, axis=1), jnp.stack(new_ssd, axis=1),
            jnp.stack(new_hgrn, axis=1))
```

```python
import functools

import numpy as np
import jax
import jax.numpy as jnp
from jax import lax
from jax.experimental import pallas as pl
from jax.experimental.pallas import tpu as pltpu

D_MODEL = 2048
DEPTH = 2
GRID_W = 64
N_BRANCH = 3
BRANCH_W = D_MODEL // 2
GDN_DK = 128
GDN_DV = 128
GDN_HEADS = BRANCH_W // GDN_DV
GDN_CHUNK = 64
SSD_P = 64
SSD_HEADS = BRANCH_W // SSD_P
SSD_GROUPS = 2
SSD_HPG = SSD_HEADS // SSD_GROUPS
SSD_N = 128
SSD_CHUNK = 64
HGRN_DK = 128
HGRN_DV = 128
HGRN_HEADS = BRANCH_W // HGRN_DV
HGRN_CHUNK = 16
CONV_K = 3
N_EXPERTS = 16
EXPERT_FF = D_MODEL // 2
CAPACITY_FACTOR = 2
N_MOD = 6
EPS = 1e-6
GDN_QK_W = GDN_HEADS * GDN_DK
HGRN_QK_W = HGRN_HEADS * HGRN_DK
CONV_SIZES = (GDN_QK_W, GDN_QK_W, BRANCH_W, BRANCH_W, SSD_GROUPS * SSD_N, SSD_GROUPS * SSD_N)
CONV_CH = GDN_QK_W + GDN_QK_W + BRANCH_W + BRANCH_W + 2 * SSD_GROUPS * SSD_N
OFF_A_G = CONV_CH
OFF_ALPHA = OFF_A_G + BRANCH_W
OFF_BETA = OFF_ALPHA + 2 * GDN_HEADS
OFF_B_Z = OFF_BETA + 2 * GDN_HEADS
OFF_B_DT = OFF_B_Z + BRANCH_W
OFF_C_Q = OFF_B_DT + 2 * SSD_HEADS
IN_COLS = OFF_C_Q + 3 * HGRN_QK_W + 2 * BRANCH_W + N_BRANCH * D_MODEL
OFF_M_G = IN_COLS - N_BRANCH * D_MODEL
REST_W = IN_COLS - CONV_CH - 4 * GDN_HEADS - 2 * SSD_HEADS
R_M_G, R_A_G, R_B_Z, R_C_Q, R_C_F, R_C_I, R_C_G = 0, 6144, 7168, 8192, 9216, 11264, 12288
SMALL_W = 128

LANE = 128
VMEM_LIMIT = 48 * 1024 * 1024
BIG_VMEM_LIMIT = 56 * 1024 * 1024
ROW_TILE = 1024

BF16 = jnp.bfloat16
F32 = jnp.float32


def _cparams(sem, vmem_limit=VMEM_LIMIT):
    return pltpu.CompilerParams(dimension_semantics=sem, vmem_limit_bytes=vmem_limit)


def _mod_row(i, tm, n_ctx_rows, dec_seq):
    return jnp.maximum((i * tm - n_ctx_rows) // dec_seq + 1, 0)


def _mod_kernel(c_ref, w_ref, b_ref, o_ref):
    c = c_ref[...]
    a = c * jax.nn.sigmoid(c)
    o_ref[0] = jnp.dot(a, w_ref[0], preferred_element_type=F32, precision=lax.Precision.HIGHEST) + b_ref[0]


def _mod_all(cond, w_mod, b_mod, tn=1024):
    nl, d, n = w_mod.shape
    return pl.pallas_call(
        _mod_kernel,
        grid=(nl, n // tn),
        in_specs=[pl.BlockSpec((8, d), lambda l, j: (0, 0)),
                  pl.BlockSpec((1, d, tn), lambda l, j: (l, 0, j)),
                  pl.BlockSpec((1, 1, tn), lambda l, j: (l, 0, j))],
        out_specs=pl.BlockSpec((1, 8, tn), lambda l, j: (l, 0, j)),
        out_shape=jax.ShapeDtypeStruct((nl, 8, n), F32),
        compiler_params=_cparams(("parallel", "arbitrary")),
        name="mod_vectors",
    )(cond, w_mod, b_mod)


def _rms_f32(x, gain):
    return x * lax.rsqrt(jnp.mean(x * x, axis=-1, keepdims=True) + EPS) * gain


def _hmod_kernel(x_ref, g_ref, mod_ref, h_ref, *, i_shift, i_scale):
    x = x_ref[...]
    y = _rms_f32(x, g_ref[...])
    h_ref[...] = (y * (1.0 + mod_ref[0, i_scale:i_scale + 1, :]) + mod_ref[0, i_shift:i_shift + 1, :]).astype(BF16)


def _hmod(x, gain, mod, n_ctx_rows, dec_seq, i_shift, i_scale, tm=512):
    t, d = x.shape
    return pl.pallas_call(
        functools.partial(_hmod_kernel, i_shift=i_shift, i_scale=i_scale),
        grid=(t // tm,),
        in_specs=[pl.BlockSpec((tm, d), lambda i: (i, 0)),
                  pl.BlockSpec((1, d), lambda i: (0, 0)),
                  pl.BlockSpec((1, N_MOD, d), lambda i: (_mod_row(i, tm, n_ctx_rows, dec_seq), 0, 0))],
        out_specs=pl.BlockSpec((tm, d), lambda i: (i, 0)),
        out_shape=jax.ShapeDtypeStruct((t, d), BF16),
        compiler_params=_cparams(("parallel",)),
        name="rms_modulate",
    )(x, gain, mod)


def _mm_plain_kernel(h_ref, w_ref, o_ref):
    o_ref[...] = jnp.dot(h_ref[...], w_ref[...], preferred_element_type=F32).astype(o_ref.dtype)


def _mm_plain(h, w, tn, out_dtype=BF16, tm=ROW_TILE):
    t, k = h.shape
    n = w.shape[1]
    return pl.pallas_call(
        _mm_plain_kernel,
        grid=(t // tm, n // tn),
        in_specs=[pl.BlockSpec((tm, k), lambda i, j: (i, 0)),
                  pl.BlockSpec((k, tn), lambda i, j: (0, j))],
        out_specs=pl.BlockSpec((tm, tn), lambda i, j: (i, j)),
        out_shape=jax.ShapeDtypeStruct((t, n), out_dtype),
        compiler_params=_cparams(("parallel", "arbitrary")),
        name="proj_rest",
    )(h, w)


def _mm_small_kernel(h_ref, w_ref, o_ref):
    o_ref[...] = jnp.dot(h_ref[...].astype(F32), w_ref[...], preferred_element_type=F32,
                         precision=lax.Precision.HIGHEST)


def _mm_small(h, w, tm=256):
    t, k = h.shape
    n = w.shape[1]
    return pl.pallas_call(
        _mm_small_kernel,
        grid=(t // tm,),
        in_specs=[pl.BlockSpec((tm, k), lambda i: (i, 0)),
                  pl.BlockSpec((k, n), lambda i: (0, 0))],
        out_specs=pl.BlockSpec((tm, n), lambda i: (i, 0)),
        out_shape=jax.ShapeDtypeStruct((t, n), F32),
        compiler_params=_cparams(("parallel",)),
        name="proj_small",
    )(h, w)


def _mm_conv_kernel(h_ref, w_ref, cw_ref, cb_ref, o_ref, wb_ref, *, n_ctx_tiles, ctx_period, grid_period):
    i = pl.program_id(1)

    @pl.when(i == 0)
    def _():
        wb_ref[...] = w_ref[...].astype(BF16)

    acc = jnp.dot(h_ref[...], wb_ref[...], preferred_element_type=F32)
    tm = acc.shape[0]
    period = jnp.where(i < n_ctx_tiles, ctx_period, grid_period)
    pos = lax.broadcasted_iota(jnp.int32, (tm, 1), 0) & (period - 1)
    prev = jnp.where(pos == 0, 0.0, pltpu.roll(acc, 1, 0))
    nxt = jnp.where(pos == period - 1, 0.0, pltpu.roll(acc, tm - 1, 0))
    y = prev * cw_ref[0:1, :] + acc * cw_ref[1:2, :] + nxt * cw_ref[2:3, :] + cb_ref[...]
    o_ref[...] = (y * jax.nn.sigmoid(y)).astype(o_ref.dtype)


def _mm_conv(h, w_in, layer, conv_w, conv_b, n_ctx_rows, ctx_period, grid_period, tn=1152, tm=512):
    t, k = h.shape
    n = CONV_CH
    return pl.pallas_call(
        functools.partial(_mm_conv_kernel, n_ctx_tiles=n_ctx_rows // tm, ctx_period=ctx_period,
                          grid_period=grid_period),
        grid=(n // tn, t // tm),
        in_specs=[pl.BlockSpec((tm, k), lambda j, i: (i, 0)),
                  pl.BlockSpec((k, tn), lambda j, i: (layer, j)),
                  pl.BlockSpec((CONV_K, tn), lambda j, i: (0, j)),
                  pl.BlockSpec((1, tn), lambda j, i: (0, j))],
        out_specs=pl.BlockSpec((tm, tn), lambda j, i: (i, j)),
        out_shape=jax.ShapeDtypeStruct((t, n), BF16),
        scratch_shapes=[pltpu.VMEM((k, tn), BF16)],
        compiler_params=_cparams(("arbitrary", "arbitrary")),
        name="proj_conv",
    )(h, w_in.reshape(-1, w_in.shape[-1]), conv_w, conv_b)


def _merge_kernel(o_ref, g_ref, w_ref, out_ref, acc_ref):
    n = pl.program_id(1)
    p = jnp.dot(o_ref[0], w_ref[0], preferred_element_type=F32)
    contrib = jax.nn.sigmoid(g_ref[...].astype(F32)) * p

    @pl.when(n == 0)
    def _():
        acc_ref[...] = contrib

    @pl.when(n > 0)
    def _():
        acc_ref[...] += contrib

    @pl.when(n == N_BRANCH - 1)
    def _():
        out_ref[...] = acc_ref[...].astype(out_ref.dtype)


def _merge(branches, rest, w_branch, tm=ROW_TILE):
    _, t, kb = branches.shape
    d = w_branch.shape[-1]
    g_blk = R_M_G // d
    return pl.pallas_call(
        _merge_kernel,
        grid=(t // tm, N_BRANCH),
        in_specs=[pl.BlockSpec((1, tm, kb), lambda i, n: (n, i, 0)),
                  pl.BlockSpec((tm, d), lambda i, n: (i, g_blk + n)),
                  pl.BlockSpec((1, kb, d), lambda i, n: (n, 0, 0))],
        out_specs=pl.BlockSpec((tm, d), lambda i, n: (i, 0)),
        out_shape=jax.ShapeDtypeStruct((t, d), BF16),
        scratch_shapes=[pltpu.VMEM((tm, d), F32)],
        compiler_params=_cparams(("parallel", "arbitrary")),
        name="branch_merge",
    )(branches, rest, w_branch)


def _outproj_kernel(m_ref, w_ref, x_ref, g_ref, mod_ref, wr_ref, x1_ref, h2_ref, aff_ref):
    out = jnp.dot(m_ref[...], w_ref[...], preferred_element_type=F32)
    x1 = x_ref[...] + mod_ref[0, 2:3, :] * _rms_f32(out, g_ref[1:2, :])
    x1_ref[...] = x1
    h2 = _rms_f32(x1, g_ref[2:3, :]) * (1.0 + mod_ref[0, 4:5, :]) + mod_ref[0, 3:4, :]
    half = h2.shape[1] // 2
    hi = pltpu.bitcast(h2[:, :half].astype(BF16).astype(F32), jnp.int32)
    lo = pltpu.bitcast(h2[:, half:].astype(BF16).astype(F32), jnp.int32)
    h2_ref[...] = hi | lax.shift_right_logical(lo, 16)
    logits = jnp.dot(h2, wr_ref[...], preferred_element_type=F32, precision=lax.Precision.HIGHEST)
    lane = lax.broadcasted_iota(jnp.int32, logits.shape, 1)
    logits = jnp.where(lane < N_EXPERTS, logits, -jnp.inf)
    e = jnp.exp(logits - jnp.max(logits, axis=-1, keepdims=True))
    aff_ref[...] = e / jnp.sum(e, axis=-1, keepdims=True)


def _outproj(merged, w_out, x, gains, mod, w_router_p, n_ctx_rows, dec_seq, tm=512):
    t, d = x.shape
    return pl.pallas_call(
        _outproj_kernel,
        grid=(t // tm,),
        in_specs=[pl.BlockSpec((tm, d), lambda i: (i, 0)),
                  pl.BlockSpec((d, d), lambda i: (0, 0)),
                  pl.BlockSpec((tm, d), lambda i: (i, 0)),
                  pl.BlockSpec((4, d), lambda i: (0, 0)),
                  pl.BlockSpec((1, N_MOD, d), lambda i: (_mod_row(i, tm, n_ctx_rows, dec_seq), 0, 0)),
                  pl.BlockSpec((d, LANE), lambda i: (0, 0))],
        out_specs=[pl.BlockSpec((tm, d), lambda i: (i, 0)),
                   pl.BlockSpec((tm, d // 2), lambda i: (i, 0)),
                   pl.BlockSpec((tm, LANE), lambda i: (i, 0))],
        out_shape=[jax.ShapeDtypeStruct((t, d), F32),
                   jax.ShapeDtypeStruct((t, d // 2), jnp.int32),
                   jax.ShapeDtypeStruct((t, LANE), F32)],
        compiler_params=_cparams(("parallel",), BIG_VMEM_LIMIT),
        name="out_proj",
    )(merged, w_out, x, gains, mod, w_router_p)


CHUNK = 64
CHUNK_SHIFT = 6


def _dot(a, b):
    return jnp.dot(a.astype(BF16), b.astype(BF16), preferred_element_type=F32)


def _dot_nt(a, b):
    return lax.dot_general(a.astype(BF16), b.astype(BF16), (((1,), (1,)), ((), ())),
                           preferred_element_type=F32)


def _dot_tn(a, b):
    return lax.dot_general(a.astype(BF16), b.astype(BF16), (((0,), (0,)), ((), ())),
                           preferred_element_type=F32)


def _interleave(*gens):
    results = [None] * len(gens)
    live = list(range(len(gens)))
    while live:
        for i in list(live):
            try:
                next(gens[i])
            except StopIteration as stop:
                results[i] = stop.value
                live.remove(i)
    return results


def _split3(x):
    x1 = x.astype(BF16)
    r1 = x - x1.astype(F32)
    x2 = r1.astype(BF16)
    x3 = (r1 - x2.astype(F32)).astype(BF16)
    return x1, x2, x3


def _dot3_left(mask_bf16, x):
    x1, x2, x3 = _split3(x)
    f = lambda p: jnp.dot(mask_bf16, p, preferred_element_type=F32)
    return f(x3) + f(x2) + f(x1)


def _dot3_right(x, mask_bf16):
    x1, x2, x3 = _split3(x)
    f = lambda p: jnp.dot(p, mask_bf16, preferred_element_type=F32)
    return f(x3) + f(x2) + f(x1)


def _softplus(x):
    return jnp.maximum(x, 0.0) + jnp.log(1.0 + jnp.exp(-jnp.abs(x)))


def _silu(x):
    return x * jax.nn.sigmoid(x)


def _chunk_tri(n, upper):
    r = lax.broadcasted_iota(jnp.int32, (n, n), 0)
    c = lax.broadcasted_iota(jnp.int32, (n, n), 1)
    same = (r >> CHUNK_SHIFT) == (c >> CHUNK_SHIFT)
    tri = (c >= r) if upper else (c <= r)
    return jnp.where(same & tri, 1.0, 0.0).astype(BF16)


def _unit_tri_inverse(ms):
    n = ms[0].shape[0]
    r = lax.broadcasted_iota(jnp.int32, (n, n), 0)
    c = lax.broadcasted_iota(jnp.int32, (n, n), 1)
    eye = jnp.where(r == c, 1.0, 0.0)
    same = lambda s: (r >> s) == (c >> s)
    m8 = [jnp.where(same(3), m, 0.0) for m in ms]
    m2 = [_dot(a, a) for a in m8]
    p = [eye - a for a in m8]
    m4 = [_dot(a, a) for a in m2]
    p = [a + _dot(a, b) for a, b in zip(p, m2)]
    xs = [a + _dot(a, b) for a, b in zip(p, m4)]
    s = 3
    while (1 << s) < n:
        band = same(s + 1) & jnp.logical_not(same(s))
        t = [_dot(x, jnp.where(band, m, 0.0)) for x, m in zip(xs, ms)]
        xs = [x - _dot(a, x) for x, a in zip(xs, t)]
        s += 1
    return xs


def _gdn_kernel(*refs, zero_init):
    if zero_init:
        (q_ref, k_ref, v_ref, ag_ref, sc_ref, sr_ref, pc_ref, pr_ref, norm_ref,
         o_ref, sfin_ref, qn_ref, kn_ref, gcf_ref, gcb_ref, grf_ref, grb_ref, of_ref, ob_ref,
         uwf_ref, uwb_ref, qkf_ref, qkb_ref) = refs
        s0_ref = None
    else:
        (q_ref, k_ref, v_ref, ag_ref, sc_ref, sr_ref, pc_ref, pr_ref, norm_ref, s0_ref,
         o_ref, sfin_ref, qn_ref, kn_ref, gcf_ref, gcb_ref, grf_ref, grb_ref, of_ref, ob_ref,
         uwf_ref, uwb_ref, qkf_ref, qkb_ref) = refs
    L = q_ref.shape[0]
    n_chunks = L // CHUNK
    heads = range(q_ref.shape[1] // GDN_DK)
    hcols = lambda hb: slice(hb * GDN_DK, (hb + 1) * GDN_DK)

    blk = min(L, 256)
    lo_blk = _chunk_tri(blk, upper=False)
    up_blk = _chunk_tri(blk, upper=True)
    up_c = _chunk_tri(CHUNK, upper=True)
    lo_c = _chunk_tri(CHUNK, upper=False)
    for hb in heads:
        qf = q_ref[:, hcols(hb)].astype(F32)
        qn_ref[:, hcols(hb)] = qf * lax.rsqrt(jnp.sum(qf * qf, axis=-1, keepdims=True) + EPS) * (GDN_DK ** -0.5)
        kf = k_ref[:, hcols(hb)].astype(F32)
        kn_ref[:, hcols(hb)] = kf * lax.rsqrt(jnp.sum(kf * kf, axis=-1, keepdims=True) + EPS)
        gcf_ref[hb] = jnp.zeros(gcf_ref.shape[1:], F32)
        gcf_ref[hb, :, 0:4] = -jnp.exp(pc_ref[hb, 0:1, 0:4]) * _softplus(sc_ref[hb] + pc_ref[hb, 1:2, 0:4])
        for b in range(L // blk):
            rows = slice(b * blk, (b + 1) * blk)
            g_col = gcf_ref[hb, rows, :]
            gcb_ref[hb, rows, :] = _dot3_left(up_blk, g_col)
            gcf_ref[hb, rows, :] = _dot3_left(lo_blk, g_col)
        g_row = -jnp.exp(pr_ref[hb, 0]) * _softplus(sr_ref[hb] + pr_ref[hb, 1])
        g_row = g_row.reshape(n_chunks * 8, CHUNK)
        grf_ref[hb] = _dot3_right(g_row, up_c).reshape(n_chunks, 8, CHUNK)
        grb_ref[hb] = _dot3_right(g_row, lo_c).reshape(n_chunks, 8, CHUNK)

    ti = lax.broadcasted_iota(jnp.int32, (CHUNK, CHUNK), 0)
    si = lax.broadcasted_iota(jnp.int32, (CHUNK, CHUNK), 1)

    def chunk_rows(c):
        return pl.ds(c * CHUNK if isinstance(c, int) else pl.multiple_of(c * CHUNK, CHUNK), CHUNK)

    group = min(n_chunks, 4)

    def prep_group(gi):
        items = [(hb, d, gi * group + j) for j in range(group) for hb in heads for d in range(2)]
        ms, rhs = [], []
        for hb, d, c in items:
            rows = chunk_rows(c)
            q = qn_ref[rows, hcols(hb)]
            k = kn_ref[rows, hcols(hb)]
            v = v_ref[rows, hcols(hb)].astype(F32)
            gc = (gcb_ref if d else gcf_ref)[hb, rows, d:d + 1]
            beta = jax.nn.sigmoid(sc_ref[hb, rows, 2 + d:3 + d])
            gcr = (grb_ref if d else grf_ref)[hb, c][d:d + 1, :]
            incl = (si >= ti) if d else (si <= ti)
            strict = (si > ti) if d else (si < ti)
            decay = jnp.where(incl, jnp.exp(jnp.where(incl, gc - gcr, 0.0)), 0.0)
            kb = k * beta
            ms.append(jnp.where(strict, _dot_nt(kb, k) * decay, 0.0))
            (qkb_ref if d else qkf_ref)[hb, rows, :] = _dot_nt(q, k) * decay
            rhs.append(jnp.concatenate([v * beta, kb * jnp.exp(gc)], axis=1).astype(BF16))
        xs = _unit_tri_inverse(ms)
        for (hb, d, c), x, r in zip(items, xs, rhs):
            (uwb_ref if d else uwf_ref)[hb, chunk_rows(c), :] = _dot(x, r)

    if n_chunks == group:
        prep_group(0)
    else:
        def prep_body(gi, carry):
            prep_group(gi)
            return carry
        lax.fori_loop(0, n_chunks // group, prep_body, 0)

    chains = [(hb, d) for hb in heads for d in range(2)]

    def body(i, carry):
        rows = [chunk_rows(n_chunks - 1 - i if d else i) for _, d in chains]
        gcs = [(gcb_ref if d else gcf_ref)[hb, r, d:d + 1] for (hb, d), r in zip(chains, rows)]
        s_b = [s.astype(BF16) for s in carry]
        uw = [(uwb_ref if d else uwf_ref)[hb, r, :] for (hb, d), r in zip(chains, rows)]
        ws = [_dot(a[:, GDN_DV:], s) for a, s in zip(uw, s_b)]
        v_new = [a[:, :GDN_DV] - w for a, w in zip(uw, ws)]
        g_last = [g[0:1] if d else g[CHUNK - 1:CHUNK] for (_, d), g in zip(chains, gcs)]
        upd = [_dot_tn(kn_ref[r, hcols(hb)] * jnp.exp(gl - g), vn)
               for (hb, _), r, gl, g, vn in zip(chains, rows, g_last, gcs, v_new)]
        new = tuple(s * jnp.exp(gl) + u for s, gl, u in zip(carry, g_last, upd))
        for (hb, d), r, g, s, vn in zip(chains, rows, gcs, s_b, v_new):
            o = _dot(qn_ref[r, hcols(hb)] * jnp.exp(g), s) + _dot((qkb_ref if d else qkf_ref)[hb, r, :], vn)
            (ob_ref if d else of_ref)[r, hcols(hb)] = o
        return new

    if zero_init:
        init = tuple(jnp.zeros((GDN_DK, GDN_DV), F32) for _ in chains)
    else:
        init = tuple(s0_ref[0, d, hb] for hb, d in chains)
    final = lax.fori_loop(0, n_chunks, body, init)
    for (hb, d), s in zip(chains, final):
        sfin_ref[0, d, hb] = s

    for hb in heads:
        o = of_ref[:, hcols(hb)] + ob_ref[:, hcols(hb)]
        y = _rms_f32(o, norm_ref[...])
        o_ref[:, hcols(hb)] = (y * _silu(ag_ref[:, hcols(hb)].astype(F32))).astype(o_ref.dtype)


GDN_HB = 2


def _gdn(cx, rest, sc, sr, pc, pr, norm, s0, *, n_seq, seq_len, row0):
    L = seq_len
    blk0 = row0 // L
    nh = GDN_HEADS
    hb = GDN_HB
    w = hb * GDN_DK
    zero_init = s0 is None
    k_off = GDN_QK_W // w
    v_off = 2 * GDN_QK_W // w
    ag_off = R_A_G // w
    in_specs = [pl.BlockSpec((L, w), lambda s, h: (blk0 + s, h)),
                pl.BlockSpec((L, w), lambda s, h: (blk0 + s, k_off + h)),
                pl.BlockSpec((L, w), lambda s, h: (blk0 + s, v_off + h)),
                pl.BlockSpec((L, w), lambda s, h: (blk0 + s, ag_off + h)),
                pl.BlockSpec((hb, L, 4), lambda s, h: (h, blk0 + s, 0)),
                pl.BlockSpec((hb, L // CHUNK, 8, CHUNK), lambda s, h: (h, blk0 + s, 0, 0)),
                pl.BlockSpec((hb, 8, LANE), lambda s, h: (h, 0, 0)),
                pl.BlockSpec((hb, 2, 8, CHUNK), lambda s, h: (h, 0, 0, 0)),
                pl.BlockSpec((1, GDN_DV), lambda s, h: (0, 0))]
    args = [cx, cx, cx, rest, sc, sr, pc, pr, norm]
    state_spec = pl.BlockSpec((1, 2, hb, GDN_DK, GDN_DV), lambda s, h: (s, 0, h, 0, 0))
    if not zero_init:
        in_specs.append(state_spec)
        args.append(s0)
    return pl.pallas_call(
        functools.partial(_gdn_kernel, zero_init=zero_init),
        grid=(n_seq, nh // hb),
        in_specs=in_specs,
        out_specs=[pl.BlockSpec((L, w), lambda s, h: (s, h)), state_spec],
        out_shape=[jax.ShapeDtypeStruct((n_seq * L, BRANCH_W), BF16),
                   jax.ShapeDtypeStruct((n_seq, 2, nh, GDN_DK, GDN_DV), F32)],
        scratch_shapes=[pltpu.VMEM((L, w), F32), pltpu.VMEM((L, w), F32),
                        pltpu.VMEM((hb, L, LANE), F32), pltpu.VMEM((hb, L, LANE), F32),
                        pltpu.VMEM((hb, L // CHUNK, 8, CHUNK), F32), pltpu.VMEM((hb, L // CHUNK, 8, CHUNK), F32),
                        pltpu.VMEM((L, w), F32), pltpu.VMEM((L, w), F32),
                        pltpu.VMEM((hb, L, 2 * GDN_DV), F32), pltpu.VMEM((hb, L, 2 * GDN_DV), F32),
                        pltpu.VMEM((hb, L, CHUNK), F32), pltpu.VMEM((hb, L, CHUNK), F32)],
        compiler_params=_cparams(("parallel", "arbitrary")),
        name="gdn_scan",
    )(*args)


def _gdn_side_inputs(small, gdn_a_log, gdn_dt_bias):
    t = small.shape[0]
    nh = GDN_HEADS
    ab = small[:, :4 * nh].reshape(t, 2, 2, nh)
    sc = ab.transpose(3, 0, 1, 2).reshape(nh, t, 4)
    rows = ab.transpose(3, 1, 2, 0).reshape(nh, 4, t // CHUNK, CHUNK).transpose(0, 2, 1, 3)
    sr = jnp.concatenate([rows, jnp.zeros_like(rows)], axis=2)
    pc = jnp.zeros((nh, 8, LANE), F32)
    pc = pc.at[:, 0, 0:2].set(gdn_a_log.T).at[:, 1, 0:2].set(gdn_dt_bias.T)
    pr = jnp.zeros((nh, 2, 8, CHUNK), F32)
    pr = pr.at[:, 0, 0:2, :].set(jnp.broadcast_to(gdn_a_log.T[:, :, None], (nh, 2, CHUNK)))
    pr = pr.at[:, 1, 0:2, :].set(jnp.broadcast_to(gdn_dt_bias.T[:, :, None], (nh, 2, CHUNK)))
    return sc, sr, pc, pr


GROUP_W = SSD_HPG * SSD_P


def _ssd_kernel(*refs, zero_init):
    if zero_init:
        (x_ref, b_ref, c_ref, z_ref, dc_ref, dr_ref, pc_ref, pr_ref, dskip_ref, norm_ref,
         o_ref, hfin_ref, dtc_ref, acf_ref, acb_ref, dtr_ref, arf_ref, arb_ref, yf_ref, yb_ref,
         hf_ref, hb_ref) = refs
        h0_ref = None
    else:
        (x_ref, b_ref, c_ref, z_ref, dc_ref, dr_ref, pc_ref, pr_ref, dskip_ref, norm_ref, h0_ref,
         o_ref, hfin_ref, dtc_ref, acf_ref, acb_ref, dtr_ref, arf_ref, arb_ref, yf_ref, yb_ref,
         hf_ref, hb_ref) = refs
    L = x_ref.shape[0]
    n_chunks = L // CHUNK
    nh = SSD_HPG

    dtc_ref[...] = jnp.zeros_like(dtc_ref)
    dtc_ref[:, 0:2 * nh] = _softplus(dc_ref[0] + pc_ref[0, 1:2, 0:2 * nh])
    acf_ref[...] = jnp.zeros_like(acf_ref)
    acf_ref[:, 0:2 * nh] = -jnp.exp(pc_ref[0, 0:1, 0:2 * nh]) * dtc_ref[:, 0:2 * nh]
    blk = min(L, 256)
    lo_blk = _chunk_tri(blk, upper=False)
    up_blk = _chunk_tri(blk, upper=True)
    for b in range(L // blk):
        rows = slice(b * blk, (b + 1) * blk)
        da = acf_ref[rows, :]
        acb_ref[rows, :] = _dot3_left(up_blk, da)
        acf_ref[rows, :] = _dot3_left(lo_blk, da)
    dt_row = _softplus(dr_ref[0] + pr_ref[0, 1])
    dtr_ref[...] = dt_row
    da_row = (-jnp.exp(pr_ref[0, 0]) * dt_row).reshape(n_chunks * 2 * nh, CHUNK)
    arf_ref[...] = _dot3_right(da_row, _chunk_tri(CHUNK, upper=True)).reshape(n_chunks, 2 * nh, CHUNK)
    arb_ref[...] = _dot3_right(da_row, _chunk_tri(CHUNK, upper=False)).reshape(n_chunks, 2 * nh, CHUNK)

    if zero_init:
        hf_ref[...] = jnp.zeros_like(hf_ref)
        hb_ref[...] = jnp.zeros_like(hb_ref)
    else:
        hf_ref[...] = h0_ref[0, 0, 0]
        hb_ref[...] = h0_ref[0, 1, 0]

    ti = lax.broadcasted_iota(jnp.int32, (CHUNK, CHUNK), 0)
    si = lax.broadcasted_iota(jnp.int32, (CHUNK, CHUNK), 1)
    ej = lax.broadcasted_iota(jnp.int32, (LANE, GROUP_W), 0)
    ec = lax.broadcasted_iota(jnp.int32, (LANE, GROUP_W), 1)

    def step(d, c):
        rows = pl.ds(pl.multiple_of(c * CHUNK, CHUNK), CHUNK)
        h_ref = hb_ref if d else hf_ref
        x = x_ref[rows, :]
        bm = b_ref[rows, :]
        cm = c_ref[rows, :]
        acs = (acb_ref if d else acf_ref)[rows, :]
        acs_r = (arb_ref if d else arf_ref)[c][d * nh:(d + 1) * nh, :]
        dt_r = dtr_ref[c][d * nh:(d + 1) * nh, :]
        incl = (si >= ti) if d else (si <= ti)
        cb = _dot_nt(cm, bm)
        spread = jnp.where(ej == d * nh + (ec >> 6), 1.0, 0.0).astype(BF16)
        last = acs[0:1] if d else acs[CHUNK - 1:CHUNK]
        p_full = _dot3_right(dtc_ref[rows, :] * jnp.exp(last - acs), spread)
        e_full = _dot3_right(jnp.exp(acs), spread)
        cd_full = _dot3_right(jnp.broadcast_to(jnp.exp(last), (8, LANE)), spread)[0:1]
        h_in = h_ref[...]
        y_off = _dot(cm, h_in)
        h_ref[...] = h_in * cd_full + _dot_tn(bm, x.astype(F32) * p_full)
        yield
        ys = []
        for h in range(nh):
            diff = acs[:, d * nh + h:d * nh + h + 1] - acs_r[h:h + 1, :]
            seg = jnp.where(incl, jnp.exp(jnp.where(incl, diff, 0.0)), 0.0)
            ys.append(_dot(cb * seg * dt_r[h:h + 1, :], x[:, h * SSD_P:(h + 1) * SSD_P]))
        yield
        (yb_ref if d else yf_ref)[rows, :] = jnp.concatenate(ys, axis=1) + y_off * e_full

    def body(i, carry):
        _interleave(step(0, 2 * i), step(1, n_chunks - 1 - 2 * i),
                    step(0, 2 * i + 1), step(1, n_chunks - 2 - 2 * i))
        return carry

    lax.fori_loop(0, n_chunks // 2, body, 0)
    hfin_ref[0, 0, 0] = hf_ref[...]
    hfin_ref[0, 1, 0] = hb_ref[...]

    y = yf_ref[...] + yb_ref[...] + dskip_ref[0] * x_ref[...].astype(F32)
    y = y * _silu(z_ref[...].astype(F32))
    o_ref[...] = _rms_f32(y, norm_ref[...]).astype(o_ref.dtype)


def _ssd(cx, rest, dc, dr, pc, pr, dskip, norm, h0, *, n_seq, seq_len, row0):
    L = seq_len
    blk0 = row0 // L
    ng = SSD_GROUPS
    zero_init = h0 is None
    x_off = (2 * GDN_QK_W + BRANCH_W) // GROUP_W
    b_off = (2 * GDN_QK_W + 2 * BRANCH_W) // SSD_N
    c_off = b_off + ng
    z_off = R_B_Z // GROUP_W
    in_specs = [pl.BlockSpec((L, GROUP_W), lambda s, g: (blk0 + s, x_off + g)),
                pl.BlockSpec((L, SSD_N), lambda s, g: (blk0 + s, b_off + g)),
                pl.BlockSpec((L, SSD_N), lambda s, g: (blk0 + s, c_off + g)),
                pl.BlockSpec((L, GROUP_W), lambda s, g: (blk0 + s, z_off + g)),
                pl.BlockSpec((1, L, 2 * SSD_HPG), lambda s, g: (g, blk0 + s, 0)),
                pl.BlockSpec((1, L // CHUNK, 2 * SSD_HPG, CHUNK), lambda s, g: (g, blk0 + s, 0, 0)),
                pl.BlockSpec((1, 8, LANE), lambda s, g: (g, 0, 0)),
                pl.BlockSpec((1, 2, 2 * SSD_HPG, CHUNK), lambda s, g: (g, 0, 0, 0)),
                pl.BlockSpec((1, 1, GROUP_W), lambda s, g: (g, 0, 0)),
                pl.BlockSpec((1, GROUP_W), lambda s, g: (0, g))]
    args = [cx, cx, cx, rest, dc, dr, pc, pr, dskip, norm]
    state_spec = pl.BlockSpec((1, 2, 1, SSD_N, GROUP_W), lambda s, g: (s, 0, g, 0, 0))
    if not zero_init:
        in_specs.append(state_spec)
        args.append(h0)
    return pl.pallas_call(
        functools.partial(_ssd_kernel, zero_init=zero_init),
        grid=(n_seq, ng),
        in_specs=in_specs,
        out_specs=[pl.BlockSpec((L, GROUP_W), lambda s, g: (s, g)), state_spec],
        out_shape=[jax.ShapeDtypeStruct((n_seq * L, BRANCH_W), BF16),
                   jax.ShapeDtypeStruct((n_seq, 2, ng, SSD_N, GROUP_W), F32)],
        scratch_shapes=[pltpu.VMEM((L, LANE), F32), pltpu.VMEM((L, LANE), F32), pltpu.VMEM((L, LANE), F32),
                        pltpu.VMEM((L // CHUNK, 2 * SSD_HPG, CHUNK), F32),
                        pltpu.VMEM((L // CHUNK, 2 * SSD_HPG, CHUNK), F32),
                        pltpu.VMEM((L // CHUNK, 2 * SSD_HPG, CHUNK), F32),
                        pltpu.VMEM((L, GROUP_W), F32), pltpu.VMEM((L, GROUP_W), F32),
                        pltpu.VMEM((SSD_N, GROUP_W), F32), pltpu.VMEM((SSD_N, GROUP_W), F32)],
        compiler_params=_cparams(("parallel", "arbitrary")),
        name="ssd_scan",
    )(*args)


def _ssd_side_inputs(small, ssd_a_log, ssd_dt_bias, ssd_d):
    t = small.shape[0]
    ng, nh = SSD_GROUPS, SSD_HPG
    off = 4 * GDN_HEADS
    dt = small[:, off:off + 2 * SSD_HEADS].reshape(t, 2, ng, nh)
    dc = dt.transpose(2, 0, 1, 3).reshape(ng, t, 2 * nh)
    dr = dt.transpose(2, 1, 3, 0).reshape(ng, 2 * nh, t // CHUNK, CHUNK).transpose(0, 2, 1, 3)
    a = ssd_a_log.reshape(2, ng, nh).transpose(1, 0, 2).reshape(ng, 2 * nh)
    bias = ssd_dt_bias.reshape(2, ng, nh).transpose(1, 0, 2).reshape(ng, 2 * nh)
    pc = jnp.zeros((ng, 8, LANE), F32).at[:, 0, 0:2 * nh].set(a).at[:, 1, 0:2 * nh].set(bias)
    pr = jnp.stack([jnp.broadcast_to(a[:, :, None], (ng, 2 * nh, CHUNK)),
                    jnp.broadcast_to(bias[:, :, None], (ng, 2 * nh, CHUNK))], axis=1)
    dskip = jnp.repeat(ssd_d.reshape(ng, nh), SSD_P, axis=1)[:, None, :]
    return dc, dr, pc, pr, dskip


def _ssd_state_to_kernel(s):
    n = s.shape[0]
    return s.reshape(n, 2, SSD_GROUPS, SSD_HPG, SSD_N, SSD_P).transpose(0, 1, 2, 4, 3, 5).reshape(
        n, 2, SSD_GROUPS, SSD_N, GROUP_W)


def _ssd_state_from_kernel(s):
    n = s.shape[0]
    return s.reshape(n, 2, SSD_GROUPS, SSD_N, SSD_HPG, SSD_P).transpose(0, 1, 2, 4, 3, 5).reshape(
        n, 2, SSD_HEADS, SSD_N, SSD_P)


SUB = 4
SUB_SHIFT = 2
N_SUB = CHUNK // SUB


def _hgrn_kernel(*refs, zero_init):
    if zero_init:
        (q_ref, zf_ref, zb_ref, v_ref, cg_ref, lb_ref, norm_ref,
         o_ref, sfin_ref, qs_ref, kf_ref, kb_ref, gf_ref, gb_ref, of_ref, ob_ref) = refs
        s0_ref = None
    else:
        (q_ref, zf_ref, zb_ref, v_ref, cg_ref, lb_ref, norm_ref, s0_ref,
         o_ref, sfin_ref, qs_ref, kf_ref, kb_ref, gf_ref, gb_ref, of_ref, ob_ref) = refs
    L = q_ref.shape[0]
    n_chunks = L // CHUNK

    qs_ref[...] = _silu(q_ref[...].astype(F32))
    blk = min(L, 256)
    lo_blk = _chunk_tri(blk, upper=False)
    up_blk = _chunk_tri(blk, upper=True)
    for d, (z_ref, k_ref, g_ref, tri) in enumerate(((zf_ref, kf_ref, gf_ref, lo_blk),
                                                    (zb_ref, kb_ref, gb_ref, up_blk))):
        lbd = lb_ref[d:d + 1, :]
        z = z_ref[...].astype(F32)
        k_ref[...] = (1.0 - lbd) * jax.nn.sigmoid(-z)
        log_f = jnp.log(lbd + (1.0 - lbd) * jax.nn.sigmoid(z))
        for b in range(L // blk):
            rows = slice(b * blk, (b + 1) * blk)
            g_ref[rows, :] = _dot3_left(tri, log_f[rows])

    ti = lax.broadcasted_iota(jnp.int32, (CHUNK, CHUNK), 0)
    si = lax.broadcasted_iota(jnp.int32, (CHUNK, CHUNK), 1)
    tcol = lax.broadcasted_iota(jnp.int32, (CHUNK, 1), 0)

    def padded(part, lo):
        hi = lo + part.shape[0]
        pieces = ([jnp.zeros((lo, HGRN_DK), F32)] if lo else []) + [part] + (
            [jnp.zeros((CHUNK - hi, HGRN_DK), F32)] if hi < CHUNK else [])
        return jnp.concatenate(pieces, axis=0).astype(BF16)

    def step(d, c, st_in, st_out):
        st = st_in()
        rows = pl.ds(pl.multiple_of(c * CHUNK, CHUNK), CHUNK)
        q = qs_ref[rows, :]
        k = (kb_ref if d else kf_ref)[rows, :]
        g = (gb_ref if d else gf_ref)[rows, :]
        v = v_ref[rows, :].astype(F32)
        if d:
            sel = jnp.where(si == ((ti >> SUB_SHIFT) << SUB_SHIFT) + SUB, 1.0, 0.0).astype(BF16)
        else:
            sel = jnp.where(si == ((ti >> SUB_SHIFT) << SUB_SHIFT) - 1, 1.0, 0.0).astype(BF16)
        g_ref_rows = _dot3_left(sel, g)
        g_last = g[0:1] if d else g[CHUNK - 1:CHUNK]
        st_out(st * jnp.exp(g_last) + _dot_tn(v, k * jnp.exp(g_last - g)))
        yield
        q_in = q * jnp.exp(g - g_ref_rows)
        k_parts, q_parts = [], []
        for j in range(1, N_SUB):
            if d:
                edge = CHUNK - SUB * j
                lo, hi = (edge // 8) * 8, CHUNK
                use = tcol[lo:hi] >= edge
                mine_blk = N_SUB - 1 - j
            else:
                edge = SUB * j
                lo, hi = 0, -(-edge // 8) * 8
                use = tcol[lo:hi] < edge
                mine_blk = j
            ref_row = g[edge:edge + 1, :] if d else g[edge - 1:edge, :]
            part = jnp.where(use, k[lo:hi] * jnp.exp(jnp.where(use, ref_row - g[lo:hi], 0.0)), 0.0)
            k_parts.append(padded(part, lo))
            qlo = (mine_blk * SUB // 8) * 8
            qpart = jnp.where((tcol[qlo:qlo + 8] >> SUB_SHIFT) == mine_blk, q_in[qlo:qlo + 8], 0.0)
            q_parts.append(padded(qpart, qlo))
        yield
        att = _dot_nt(jnp.concatenate(q_parts, axis=1), jnp.concatenate(k_parts, axis=1))
        o_state = _dot_nt(q * jnp.exp(g), st)
        yield
        o = _dot(att, v) + o_state
        pos = tcol & (SUB - 1)
        for lag in range(SUB):
            if lag == 0:
                a = jnp.sum(q * k, axis=-1, keepdims=True)
                o = o + a * v
                continue
            shift = (CHUNK - lag) if d else lag
            valid = (pos + lag <= SUB - 1) if d else (pos >= lag)
            k_l = pltpu.roll(k, shift, 0)
            g_l = pltpu.roll(g, shift, 0)
            v_l = pltpu.roll(v, shift, 0)
            dec = jnp.exp(jnp.where(valid, g - g_l, 0.0))
            a = jnp.where(valid, jnp.sum(q * k_l * dec, axis=-1, keepdims=True), 0.0)
            o = o + a * v_l
        yield
        (ob_ref if d else of_ref)[rows, :] = o

    def body(i, carry):
        box = {"f0": carry[0], "b0": carry[1]}
        put = lambda key: (lambda val: box.__setitem__(key, val))
        get = lambda key: (lambda: box[key])
        _interleave(step(0, 2 * i, get("f0"), put("f1")),
                    step(1, n_chunks - 1 - 2 * i, get("b0"), put("b1")),
                    step(0, 2 * i + 1, get("f1"), put("f2")),
                    step(1, n_chunks - 2 - 2 * i, get("b1"), put("b2")))
        return box["f2"], box["b2"]

    if zero_init:
        init = (jnp.zeros((HGRN_DV, HGRN_DK), F32), jnp.zeros((HGRN_DV, HGRN_DK), F32))
    else:
        init = (s0_ref[0, 0, 0], s0_ref[0, 1, 0])
    s_f, s_b = lax.fori_loop(0, n_chunks // 2, body, init)
    sfin_ref[0, 0, 0] = s_f
    sfin_ref[0, 1, 0] = s_b

    o = of_ref[...] + ob_ref[...]
    y = _rms_f32(o, norm_ref[...])
    o_ref[...] = (y * _silu(cg_ref[...].astype(F32))).astype(o_ref.dtype)


def _hgrn(rest, lb, norm, s0, *, n_seq, seq_len, row0):
    L = seq_len
    blk0 = row0 // L
    nh = HGRN_HEADS
    zero_init = s0 is None
    q_off, zf_off, v_off, g_off = R_C_Q // LANE, R_C_F // LANE, R_C_I // LANE, R_C_G // LANE
    zb_off = zf_off + nh
    in_specs = [pl.BlockSpec((L, LANE), lambda s, h: (blk0 + s, q_off + h)),
                pl.BlockSpec((L, LANE), lambda s, h: (blk0 + s, zf_off + h)),
                pl.BlockSpec((L, LANE), lambda s, h: (blk0 + s, zb_off + h)),
                pl.BlockSpec((L, LANE), lambda s, h: (blk0 + s, v_off + h)),
                pl.BlockSpec((L, LANE), lambda s, h: (blk0 + s, g_off + h)),
                pl.BlockSpec((2, HGRN_DK), lambda s, h: (0, h)),
                pl.BlockSpec((1, HGRN_DV), lambda s, h: (0, 0))]
    args = [rest, rest, rest, rest, rest, lb, norm]
    state_spec = pl.BlockSpec((1, 2, 1, HGRN_DV, HGRN_DK), lambda s, h: (s, 0, h, 0, 0))
    if not zero_init:
        in_specs.append(state_spec)
        args.append(s0)
    return pl.pallas_call(
        functools.partial(_hgrn_kernel, zero_init=zero_init),
        grid=(n_seq, nh),
        in_specs=in_specs,
        out_specs=[pl.BlockSpec((L, HGRN_DV), lambda s, h: (s, h)), state_spec],
        out_shape=[jax.ShapeDtypeStruct((n_seq * L, BRANCH_W), BF16),
                   jax.ShapeDtypeStruct((n_seq, 2, nh, HGRN_DV, HGRN_DK), F32)],
        scratch_shapes=[pltpu.VMEM((L, HGRN_DK), F32)] * 5 + [pltpu.VMEM((L, HGRN_DV), F32)] * 2,
        compiler_params=_cparams(("parallel", "arbitrary")),
        name="hgrn_scan",
    )(*args)


def _route_kernel(aff_ref, sel_ref, *, cap):
    bits = pltpu.bitcast(aff_ref[...], jnp.int32)
    n_rows = bits.shape[0]
    row = lax.broadcasted_iota(jnp.int32, bits.shape, 0)
    count = lambda m: jnp.sum(m.astype(jnp.int32), axis=0, keepdims=True)

    def thr_body(i, thr):
        cand = thr | (1 << (30 - i))
        return jnp.where(count(bits >= cand) >= cap, cand, thr)

    thr = lax.fori_loop(0, 31, thr_body, jnp.zeros((1, bits.shape[1]), jnp.int32))
    above = bits > thr
    tie = bits == thr
    need = cap - count(above)
    n_bits = max(1, (n_rows - 1).bit_length())

    def tie_body(i, lim):
        cand = lim | (1 << (n_bits - 1 - i))
        return jnp.where(count(tie & (row < cand)) < need, cand, lim)

    lim = lax.fori_loop(0, n_bits, tie_body, jnp.zeros_like(thr))
    sel = above | (tie & (row <= lim) & (need > 0))
    sel_ref[...] = jnp.where(sel, 1.0, 0.0)


def _route(aff, row0, n_rows):
    cap = CAPACITY_FACTOR * n_rows // N_EXPERTS
    return pl.pallas_call(
        functools.partial(_route_kernel, cap=cap),
        grid=(1,),
        in_specs=[pl.BlockSpec((n_rows, LANE), lambda i: (row0 // n_rows, 0))],
        out_specs=pl.BlockSpec((n_rows, LANE), lambda i: (0, 0)),
        out_shape=jax.ShapeDtypeStruct((n_rows, LANE), F32),
        compiler_params=_cparams(("arbitrary",)),
        name="route_topk",
    )(aff)


FF_TILE = 256


def _ffn_kernel(x_ref, wg_ref, wu_ref, wd_ref, gate_ref, o_ref, xb_ref):
    f = pl.program_id(1)

    @pl.when(f == 0)
    def _():
        words = x_ref[0]
        half = words.shape[1]
        xb_ref[:, :half] = pltpu.bitcast(words & jnp.int32(-65536), F32).astype(BF16)
        xb_ref[:, half:] = pltpu.bitcast(words << 16, F32).astype(BF16)

    x = xb_ref[...]
    g = jnp.dot(x, wg_ref[0].astype(BF16), preferred_element_type=F32)
    u = jnp.dot(x, wu_ref[0].astype(BF16), preferred_element_type=F32)
    y = jnp.dot((_silu(g) * u).astype(BF16), wd_ref[0].astype(BF16), preferred_element_type=F32)

    @pl.when(f == 0)
    def _():
        o_ref[0] = y

    @pl.when(f > 0)
    def _():
        o_ref[0] += y

    @pl.when(f == pl.num_programs(1) - 1)
    def _():
        o_ref[0] = o_ref[0] * gate_ref[0]


def _expert_ffn(xe, w_gu, w_down, gate):
    e, cap, _ = xe.shape
    ff, d = w_down.shape[1:]
    nf = ff // FF_TILE
    return pl.pallas_call(
        _ffn_kernel,
        grid=(e, nf),
        in_specs=[pl.BlockSpec((1, cap, d // 2), lambda g, f: (g, 0, 0)),
                  pl.BlockSpec((1, d, FF_TILE), lambda g, f: (g, 0, f)),
                  pl.BlockSpec((1, d, FF_TILE), lambda g, f: (g, 0, nf + f)),
                  pl.BlockSpec((1, FF_TILE, d), lambda g, f: (g, f, 0)),
                  pl.BlockSpec((1, cap, 1), lambda g, f: (g, 0, 0))],
        out_specs=pl.BlockSpec((1, cap, d), lambda g, f: (g, 0, 0)),
        out_shape=jax.ShapeDtypeStruct((e, cap, d), F32),
        scratch_shapes=[pltpu.VMEM((cap, d), BF16)],
        compiler_params=_cparams(("parallel", "arbitrary"), BIG_VMEM_LIMIT),
        name="expert_ffn",
    )(xe, w_gu, w_gu, w_down, gate)


def _ec_moe(h2w, aff, sel, w_gu, w_down):
    t = h2w.shape[0]
    d = w_down.shape[-1]
    cap = CAPACITY_FACTOR * t // N_EXPERTS
    tok = jnp.arange(t, dtype=jnp.int32)[:, None]
    keys = jnp.where(sel[:, :N_EXPERTS] > 0.5, tok, tok + t).T
    idx = jnp.sort(keys, axis=1)[:, :cap]
    gate = jnp.take_along_axis(aff[:, :N_EXPERTS].T, idx, axis=1)[..., None]
    ye = _expert_ffn(h2w[idx], w_gu, w_down, gate)
    return jnp.zeros((t, d), F32).at[idx.reshape(-1)].add(ye.reshape(-1, d))


def _resid_kernel(x_ref, m_ref, g_ref, mod_ref, gn_ref, modn_ref, xo_ref, h_ref):
    x = x_ref[...] + mod_ref[0, 5:6, :] * _rms_f32(m_ref[...], g_ref[...])
    xo_ref[...] = x
    h_ref[...] = (_rms_f32(x, gn_ref[...]) * (1.0 + modn_ref[0, 1:2, :]) + modn_ref[0, 0:1, :]).astype(BF16)


def _moe_residual(x1, moe, gain3, mod, gain_next, mod_next, n_ctx_rows, dec_seq, tm=512):
    t, d = x1.shape
    row_spec = pl.BlockSpec((tm, d), lambda i: (i, 0))
    vec_spec = pl.BlockSpec((1, d), lambda i: (0, 0))
    mod_spec = pl.BlockSpec((1, N_MOD, d), lambda i: (_mod_row(i, tm, n_ctx_rows, dec_seq), 0, 0))
    return pl.pallas_call(
        _resid_kernel,
        grid=(t // tm,),
        in_specs=[row_spec, row_spec, vec_spec, mod_spec, vec_spec, mod_spec],
        out_specs=[row_spec, row_spec],
        out_shape=[jax.ShapeDtypeStruct((t, d), F32), jax.ShapeDtypeStruct((t, d), BF16)],
        compiler_params=_cparams(("parallel",)),
        name="moe_residual",
    )(x1, moe, gain3, mod, gain_next, mod_next)


def kernel(x_prompt, x_sample, c, state_gdn, state_ssd, state_hgrn, c_ctx, w_mod, b_mod, norm_gain,
           w_in, conv_w, conv_b, gdn_a_log, gdn_dt_bias, gdn_norm, ssd_a_log, ssd_dt_bias, ssd_d,
           ssd_norm, hgrn_lb, hgrn_norm, w_branch, w_out, w_router, w_gu, w_down):
    nb_c, seq_c, d = x_prompt.shape
    nb_s, seq_s, _ = x_sample.shape
    n_ctx = nb_c * seq_c
    n_smp = nb_s * seq_s

    lb_w = jax.nn.softmax(hgrn_lb.astype(F32), axis=0)
    lb = jnp.cumsum(lb_w, axis=0) - lb_w[:1]

    cond = jnp.concatenate([c_ctx[None, :], c, jnp.zeros((8 - 1 - nb_s, d), F32)], axis=0)
    mod_all = _mod_all(cond, w_mod, b_mod[:, None, :]).reshape(DEPTH, 8, N_MOD, d)

    x = jnp.concatenate([x_prompt.reshape(n_ctx, d), x_sample.reshape(n_smp, d)], axis=0)
    new_gdn, new_ssd, new_hgrn = [], [], []

    for l in range(DEPTH):
        mod = mod_all[l]
        wl = w_in[l]
        w_rest = jnp.concatenate([wl[:, OFF_M_G:], wl[:, OFF_A_G:OFF_ALPHA], wl[:, OFF_B_Z:OFF_B_DT],
                                  wl[:, OFF_C_Q:OFF_M_G]], axis=1).astype(BF16)
        w_small = jnp.concatenate([wl[:, OFF_ALPHA:OFF_B_Z], wl[:, OFF_B_DT:OFF_C_Q],
                                   jnp.zeros((d, SMALL_W - 4 * GDN_HEADS - 2 * SSD_HEADS), F32)], axis=1)

        if l == 0:
            h = _hmod(x, norm_gain[l, 0:1], mod, n_ctx, seq_s, i_shift=0, i_scale=1)
        cx = _mm_conv(h, w_in, l, conv_w[l], conv_b[l][None, :], n_ctx, seq_c, GRID_W)
        rest = _mm_plain(h, w_rest, tn=1024)
        small = _mm_small(h, w_small)

        gdn_side = _gdn_side_inputs(small, gdn_a_log[l], gdn_dt_bias[l])
        oa_c, sg = _gdn(cx, rest, *gdn_side, gdn_norm[l][None, :], None, n_seq=nb_c, seq_len=seq_c, row0=0)
        oa_s, _ = _gdn(cx, rest, *gdn_side, gdn_norm[l][None, :], state_gdn[:, l], n_seq=nb_s, seq_len=seq_s,
                       row0=n_ctx)
        ssd_side = _ssd_side_inputs(small, ssd_a_log[l], ssd_dt_bias[l], ssd_d[l])
        ob_c, ss = _ssd(cx, rest, *ssd_side, ssd_norm[l][None, :], None, n_seq=nb_c, seq_len=seq_c, row0=0)
        ob_s, _ = _ssd(cx, rest, *ssd_side, ssd_norm[l][None, :], _ssd_state_to_kernel(state_ssd[:, l]),
                       n_seq=nb_s, seq_len=seq_s, row0=n_ctx)
        ss = _ssd_state_from_kernel(ss)
        oc_c, sh = _hgrn(rest, lb[l], hgrn_norm[l][None, :], None, n_seq=nb_c, seq_len=seq_c, row0=0)
        oc_s, _ = _hgrn(rest, lb[l], hgrn_norm[l][None, :], jnp.swapaxes(state_hgrn[:, l], -1, -2),
                        n_seq=nb_s, seq_len=seq_s, row0=n_ctx)
        sh = jnp.swapaxes(sh, -1, -2)
        new_gdn.append(sg)
        new_ssd.append(ss)
        new_hgrn.append(sh)
        branches = jnp.stack([jnp.concatenate([oa_c, oa_s]), jnp.concatenate([ob_c, ob_s]),
                              jnp.concatenate([oc_c, oc_s])])

        merged = _merge(branches, rest, w_branch[l].astype(BF16))
        w_router_p = jnp.concatenate([w_router[l], jnp.zeros((d, LANE - N_EXPERTS), F32)], axis=1)
        x1, h2, aff = _outproj(merged, w_out[l].astype(BF16), x, norm_gain[l], mod, w_router_p, n_ctx, seq_s)

        moe_c = _ec_moe(h2[:n_ctx], aff[:n_ctx], _route(aff, 0, n_ctx), w_gu[l], w_down[l])
        moe_s = _ec_moe(h2[n_ctx:], aff[n_ctx:], _route(aff, n_ctx, n_smp), w_gu[l], w_down[l])
        moe = jnp.concatenate([moe_c, moe_s])
        l_next = min(l + 1, DEPTH - 1)
        x, h = _moe_residual(x1, moe, norm_gain[l, 3:4], mod, norm_gain[l_next, 0:1], mod_all[l_next],
                             n_ctx, seq_s)

    y_prompt = x[:n_ctx].reshape(nb_c, seq_c, d)
    y_sample = x[n_ctx:].reshape(nb_s, seq_s, d)
    return (y_prompt, y_sample, jnp.stack(new_gdn, axis=1), jnp.stack(new_ssd, axis=1),
            jnp.stack(new_hgrn, axis=1))
```

```python
import functools

import numpy as np
import jax
import jax.numpy as jnp
from jax import lax
from jax.experimental import pallas as pl
from jax.experimental.pallas import tpu as pltpu

D_MODEL = 2048
DEPTH = 2
GRID_W = 64
N_BRANCH = 3
BRANCH_W = D_MODEL // 2
GDN_DK = 128
GDN_DV = 128
GDN_HEADS = BRANCH_W // GDN_DV
GDN_CHUNK = 64
SSD_P = 64
SSD_HEADS = BRANCH_W // SSD_P
SSD_GROUPS = 2
SSD_HPG = SSD_HEADS // SSD_GROUPS
SSD_N = 128
SSD_CHUNK = 64
HGRN_DK = 128
HGRN_DV = 128
HGRN_HEADS = BRANCH_W // HGRN_DV
HGRN_CHUNK = 16
CONV_K = 3
N_EXPERTS = 16
EXPERT_FF = D_MODEL // 2
CAPACITY_FACTOR = 2
N_MOD = 6
EPS = 1e-6
GDN_QK_W = GDN_HEADS * GDN_DK
HGRN_QK_W = HGRN_HEADS * HGRN_DK
CONV_SIZES = (GDN_QK_W, GDN_QK_W, BRANCH_W, BRANCH_W, SSD_GROUPS * SSD_N, SSD_GROUPS * SSD_N)
CONV_CH = GDN_QK_W + GDN_QK_W + BRANCH_W + BRANCH_W + 2 * SSD_GROUPS * SSD_N
OFF_A_G = CONV_CH
OFF_ALPHA = OFF_A_G + BRANCH_W
OFF_BETA = OFF_ALPHA + 2 * GDN_HEADS
OFF_B_Z = OFF_BETA + 2 * GDN_HEADS
OFF_B_DT = OFF_B_Z + BRANCH_W
OFF_C_Q = OFF_B_DT + 2 * SSD_HEADS
IN_COLS = OFF_C_Q + 3 * HGRN_QK_W + 2 * BRANCH_W + N_BRANCH * D_MODEL
OFF_M_G = IN_COLS - N_BRANCH * D_MODEL
REST_W = IN_COLS - CONV_CH - 4 * GDN_HEADS - 2 * SSD_HEADS
R_M_G, R_A_G, R_B_Z, R_C_Q, R_C_F, R_C_I, R_C_G = 0, 6144, 7168, 8192, 9216, 11264, 12288
SMALL_W = 128

LANE = 128
VMEM_LIMIT = 48 * 1024 * 1024
BIG_VMEM_LIMIT = 56 * 1024 * 1024
ROW_TILE = 1024

BF16 = jnp.bfloat16
F32 = jnp.float32


def _cparams(sem, vmem_limit=VMEM_LIMIT):
    return pltpu.CompilerParams(dimension_semantics=sem, vmem_limit_bytes=vmem_limit)


def _mod_row(i, tm, n_ctx_rows, dec_seq):
    return jnp.maximum((i * tm - n_ctx_rows) // dec_seq + 1, 0)


def _mod_kernel(c_ref, w_ref, b_ref, o_ref):
    c = c_ref[...]
    a = c * jax.nn.sigmoid(c)
    o_ref[0] = jnp.dot(a, w_ref[0], preferred_element_type=F32, precision=lax.Precision.HIGHEST) + b_ref[0]


def _mod_all(cond, w_mod, b_mod, tn=1024):
    nl, d, n = w_mod.shape
    return pl.pallas_call(
        _mod_kernel,
        grid=(nl, n // tn),
        in_specs=[pl.BlockSpec((8, d), lambda l, j: (0, 0)),
                  pl.BlockSpec((1, d, tn), lambda l, j: (l, 0, j)),
                  pl.BlockSpec((1, 1, tn), lambda l, j: (l, 0, j))],
        out_specs=pl.BlockSpec((1, 8, tn), lambda l, j: (l, 0, j)),
        out_shape=jax.ShapeDtypeStruct((nl, 8, n), F32),
        compiler_params=_cparams(("parallel", "arbitrary")),
        name="mod_vectors",
    )(cond, w_mod, b_mod)


def _rms_f32(x, gain):
    return x * lax.rsqrt(jnp.mean(x * x, axis=-1, keepdims=True) + EPS) * gain


def _hmod_kernel(x_ref, g_ref, mod_ref, h_ref, *, i_shift, i_scale):
    x = x_ref[...]
    y = _rms_f32(x, g_ref[...])
    h_ref[...] = (y * (1.0 + mod_ref[0, i_scale:i_scale + 1, :]) + mod_ref[0, i_shift:i_shift + 1, :]).astype(BF16)


def _hmod(x, gain, mod, n_ctx_rows, dec_seq, i_shift, i_scale, tm=512):
    t, d = x.shape
    return pl.pallas_call(
        functools.partial(_hmod_kernel, i_shift=i_shift, i_scale=i_scale),
        grid=(t // tm,),
        in_specs=[pl.BlockSpec((tm, d), lambda i: (i, 0)),
                  pl.BlockSpec((1, d), lambda i: (0, 0)),
                  pl.BlockSpec((1, N_MOD, d), lambda i: (_mod_row(i, tm, n_ctx_rows, dec_seq), 0, 0))],
        out_specs=pl.BlockSpec((tm, d), lambda i: (i, 0)),
        out_shape=jax.ShapeDtypeStruct((t, d), BF16),
        compiler_params=_cparams(("parallel",)),
        name="rms_modulate",
    )(x, gain, mod)


def _mm_plain_kernel(h_ref, w_ref, o_ref):
    o_ref[...] = jnp.dot(h_ref[...], w_ref[0], preferred_element_type=F32).astype(o_ref.dtype)


def _mm_plain(h, w, layer, tn, out_dtype=BF16, tm=ROW_TILE):
    t, k = h.shape
    n = w.shape[-1]
    return pl.pallas_call(
        _mm_plain_kernel,
        grid=(t // tm, n // tn),
        in_specs=[pl.BlockSpec((tm, k), lambda i, j: (i, 0)),
                  pl.BlockSpec((1, k, tn), lambda i, j: (layer, 0, j))],
        out_specs=pl.BlockSpec((tm, tn), lambda i, j: (i, j)),
        out_shape=jax.ShapeDtypeStruct((t, n), out_dtype),
        compiler_params=_cparams(("parallel", "arbitrary")),
        name="proj_rest",
    )(h, w)


def _mm_small_kernel(h_ref, w_ref, o_ref):
    o_ref[...] = jnp.dot(h_ref[...].astype(F32), w_ref[0], preferred_element_type=F32,
                         precision=lax.Precision.HIGHEST)


def _mm_small(h, w, layer, tm=256):
    t, k = h.shape
    n = w.shape[-1]
    return pl.pallas_call(
        _mm_small_kernel,
        grid=(t // tm,),
        in_specs=[pl.BlockSpec((tm, k), lambda i: (i, 0)),
                  pl.BlockSpec((1, k, n), lambda i: (layer, 0, 0))],
        out_specs=pl.BlockSpec((tm, n), lambda i: (i, 0)),
        out_shape=jax.ShapeDtypeStruct((t, n), F32),
        compiler_params=_cparams(("parallel",)),
        name="proj_small",
    )(h, w)


def _mm_conv_kernel(h_ref, w_ref, cw_ref, cb_ref, o_ref, *, n_ctx_tiles, ctx_period, grid_period):
    acc = jnp.dot(h_ref[...], w_ref[0], preferred_element_type=F32)
    tm = acc.shape[0]
    period = jnp.where(pl.program_id(0) < n_ctx_tiles, ctx_period, grid_period)
    pos = lax.broadcasted_iota(jnp.int32, (tm, 1), 0) & (period - 1)
    prev = jnp.where(pos == 0, 0.0, pltpu.roll(acc, 1, 0))
    nxt = jnp.where(pos == period - 1, 0.0, pltpu.roll(acc, tm - 1, 0))
    y = prev * cw_ref[0:1, :] + acc * cw_ref[1:2, :] + nxt * cw_ref[2:3, :] + cb_ref[...]
    o_ref[...] = (y * jax.nn.sigmoid(y)).astype(o_ref.dtype)


def _mm_conv(h, w, layer, conv_w, conv_b, n_ctx_rows, ctx_period, grid_period, tn=1152, tm=512):
    t, k = h.shape
    n = w.shape[-1]
    return pl.pallas_call(
        functools.partial(_mm_conv_kernel, n_ctx_tiles=n_ctx_rows // tm, ctx_period=ctx_period,
                          grid_period=grid_period),
        grid=(t // tm, n // tn),
        in_specs=[pl.BlockSpec((tm, k), lambda i, j: (i, 0)),
                  pl.BlockSpec((1, k, tn), lambda i, j: (layer, 0, j)),
                  pl.BlockSpec((CONV_K, tn), lambda i, j: (0, j)),
                  pl.BlockSpec((1, tn), lambda i, j: (0, j))],
        out_specs=pl.BlockSpec((tm, tn), lambda i, j: (i, j)),
        out_shape=jax.ShapeDtypeStruct((t, n), BF16),
        compiler_params=_cparams(("parallel", "arbitrary")),
        name="proj_conv",
    )(h, w, conv_w, conv_b)


def _merge_kernel(o_ref, g_ref, w_ref, out_ref, acc_ref):
    n = pl.program_id(1)
    p = jnp.dot(o_ref[0], w_ref[0], preferred_element_type=F32)
    contrib = jax.nn.sigmoid(g_ref[...].astype(F32)) * p

    @pl.when(n == 0)
    def _():
        acc_ref[...] = contrib

    @pl.when(n > 0)
    def _():
        acc_ref[...] += contrib

    @pl.when(n == N_BRANCH - 1)
    def _():
        out_ref[...] = acc_ref[...].astype(out_ref.dtype)


def _merge(branches, rest, w_branch, tm=ROW_TILE):
    _, t, kb = branches.shape
    d = w_branch.shape[-1]
    g_blk = R_M_G // d
    return pl.pallas_call(
        _merge_kernel,
        grid=(t // tm, N_BRANCH),
        in_specs=[pl.BlockSpec((1, tm, kb), lambda i, n: (n, i, 0)),
                  pl.BlockSpec((tm, d), lambda i, n: (i, g_blk + n)),
                  pl.BlockSpec((1, kb, d), lambda i, n: (n, 0, 0))],
        out_specs=pl.BlockSpec((tm, d), lambda i, n: (i, 0)),
        out_shape=jax.ShapeDtypeStruct((t, d), BF16),
        scratch_shapes=[pltpu.VMEM((tm, d), F32)],
        compiler_params=_cparams(("parallel", "arbitrary")),
        name="branch_merge",
    )(branches, rest, w_branch)


def _outproj_kernel(m_ref, w_ref, x_ref, g_ref, mod_ref, wr_ref, x1_ref, h2_ref, aff_ref):
    out = jnp.dot(m_ref[...], w_ref[...], preferred_element_type=F32)
    x1 = x_ref[...] + mod_ref[0, 2:3, :] * _rms_f32(out, g_ref[1:2, :])
    x1_ref[...] = x1
    h2 = _rms_f32(x1, g_ref[2:3, :]) * (1.0 + mod_ref[0, 4:5, :]) + mod_ref[0, 3:4, :]
    half = h2.shape[1] // 2
    hi = pltpu.bitcast(h2[:, :half].astype(BF16).astype(F32), jnp.int32)
    lo = pltpu.bitcast(h2[:, half:].astype(BF16).astype(F32), jnp.int32)
    h2_ref[...] = hi | lax.shift_right_logical(lo, 16)
    logits = jnp.dot(h2, wr_ref[...], preferred_element_type=F32, precision=lax.Precision.HIGHEST)
    lane = lax.broadcasted_iota(jnp.int32, logits.shape, 1)
    logits = jnp.where(lane < N_EXPERTS, logits, -jnp.inf)
    e = jnp.exp(logits - jnp.max(logits, axis=-1, keepdims=True))
    aff_ref[...] = e / jnp.sum(e, axis=-1, keepdims=True)


def _outproj(merged, w_out, x, gains, mod, w_router_p, n_ctx_rows, dec_seq, tm=512):
    t, d = x.shape
    return pl.pallas_call(
        _outproj_kernel,
        grid=(t // tm,),
        in_specs=[pl.BlockSpec((tm, d), lambda i: (i, 0)),
                  pl.BlockSpec((d, d), lambda i: (0, 0)),
                  pl.BlockSpec((tm, d), lambda i: (i, 0)),
                  pl.BlockSpec((4, d), lambda i: (0, 0)),
                  pl.BlockSpec((1, N_MOD, d), lambda i: (_mod_row(i, tm, n_ctx_rows, dec_seq), 0, 0)),
                  pl.BlockSpec((d, LANE), lambda i: (0, 0))],
        out_specs=[pl.BlockSpec((tm, d), lambda i: (i, 0)),
                   pl.BlockSpec((tm, d // 2), lambda i: (i, 0)),
                   pl.BlockSpec((tm, LANE), lambda i: (i, 0))],
        out_shape=[jax.ShapeDtypeStruct((t, d), F32),
                   jax.ShapeDtypeStruct((t, d // 2), jnp.int32),
                   jax.ShapeDtypeStruct((t, LANE), F32)],
        compiler_params=_cparams(("parallel",), BIG_VMEM_LIMIT),
        name="out_proj",
    )(merged, w_out, x, gains, mod, w_router_p)


CHUNK = 64
CHUNK_SHIFT = 6


def _dot(a, b):
    return jnp.dot(a.astype(BF16), b.astype(BF16), preferred_element_type=F32)


def _dot_nt(a, b):
    return lax.dot_general(a.astype(BF16), b.astype(BF16), (((1,), (1,)), ((), ())),
                           preferred_element_type=F32)


def _dot_tn(a, b):
    return lax.dot_general(a.astype(BF16), b.astype(BF16), (((0,), (0,)), ((), ())),
                           preferred_element_type=F32)


def _interleave(*gens):
    results = [None] * len(gens)
    live = list(range(len(gens)))
    while live:
        for i in list(live):
            try:
                next(gens[i])
            except StopIteration as stop:
                results[i] = stop.value
                live.remove(i)
    return results


def _split3(x):
    x1 = x.astype(BF16)
    r1 = x - x1.astype(F32)
    x2 = r1.astype(BF16)
    x3 = (r1 - x2.astype(F32)).astype(BF16)
    return x1, x2, x3


def _dot3_left(mask_bf16, x):
    x1, x2, x3 = _split3(x)
    f = lambda p: jnp.dot(mask_bf16, p, preferred_element_type=F32)
    return f(x3) + f(x2) + f(x1)


def _dot3_right(x, mask_bf16):
    x1, x2, x3 = _split3(x)
    f = lambda p: jnp.dot(p, mask_bf16, preferred_element_type=F32)
    return f(x3) + f(x2) + f(x1)


def _softplus(x):
    return jnp.maximum(x, 0.0) + jnp.log(1.0 + jnp.exp(-jnp.abs(x)))


def _silu(x):
    return x * jax.nn.sigmoid(x)


def _chunk_tri(n, upper):
    r = lax.broadcasted_iota(jnp.int32, (n, n), 0)
    c = lax.broadcasted_iota(jnp.int32, (n, n), 1)
    same = (r >> CHUNK_SHIFT) == (c >> CHUNK_SHIFT)
    tri = (c >= r) if upper else (c <= r)
    return jnp.where(same & tri, 1.0, 0.0).astype(BF16)


def _unit_tri_inverse(ms):
    n = ms[0].shape[0]
    r = lax.broadcasted_iota(jnp.int32, (n, n), 0)
    c = lax.broadcasted_iota(jnp.int32, (n, n), 1)
    eye = jnp.where(r == c, 1.0, 0.0)
    same = lambda s: (r >> s) == (c >> s)
    m8 = [jnp.where(same(3), m, 0.0) for m in ms]
    m2 = [_dot(a, a) for a in m8]
    p = [eye - a for a in m8]
    m4 = [_dot(a, a) for a in m2]
    p = [a + _dot(a, b) for a, b in zip(p, m2)]
    xs = [a + _dot(a, b) for a, b in zip(p, m4)]
    s = 3
    while (1 << s) < n:
        band = same(s + 1) & jnp.logical_not(same(s))
        t = [_dot(x, jnp.where(band, m, 0.0)) for x, m in zip(xs, ms)]
        xs = [x - _dot(a, x) for x, a in zip(xs, t)]
        s += 1
    return xs


def _gdn_kernel(*refs, zero_init):
    if zero_init:
        (q_ref, k_ref, v_ref, ag_ref, sc_ref, sr_ref, pc_ref, pr_ref, norm_ref,
         o_ref, sfin_ref, qn_ref, kn_ref, gcf_ref, gcb_ref, grf_ref, grb_ref, of_ref, ob_ref,
         uwf_ref, uwb_ref, qkf_ref, qkb_ref) = refs
        s0_ref = None
    else:
        (q_ref, k_ref, v_ref, ag_ref, sc_ref, sr_ref, pc_ref, pr_ref, norm_ref, s0_ref,
         o_ref, sfin_ref, qn_ref, kn_ref, gcf_ref, gcb_ref, grf_ref, grb_ref, of_ref, ob_ref,
         uwf_ref, uwb_ref, qkf_ref, qkb_ref) = refs
    L = q_ref.shape[0]
    n_chunks = L // CHUNK
    heads = range(q_ref.shape[1] // GDN_DK)
    hcols = lambda hb: slice(hb * GDN_DK, (hb + 1) * GDN_DK)

    blk = min(L, 256)
    lo_blk = _chunk_tri(blk, upper=False)
    up_blk = _chunk_tri(blk, upper=True)
    up_c = _chunk_tri(CHUNK, upper=True)
    lo_c = _chunk_tri(CHUNK, upper=False)
    for hb in heads:
        qf = q_ref[:, hcols(hb)].astype(F32)
        qn_ref[:, hcols(hb)] = qf * lax.rsqrt(jnp.sum(qf * qf, axis=-1, keepdims=True) + EPS) * (GDN_DK ** -0.5)
        kf = k_ref[:, hcols(hb)].astype(F32)
        kn_ref[:, hcols(hb)] = kf * lax.rsqrt(jnp.sum(kf * kf, axis=-1, keepdims=True) + EPS)
        gcf_ref[hb] = jnp.zeros(gcf_ref.shape[1:], F32)
        gcf_ref[hb, :, 0:4] = -jnp.exp(pc_ref[hb, 0:1, 0:4]) * _softplus(sc_ref[hb] + pc_ref[hb, 1:2, 0:4])
        for b in range(L // blk):
            rows = slice(b * blk, (b + 1) * blk)
            g_col = gcf_ref[hb, rows, :]
            gcb_ref[hb, rows, :] = _dot3_left(up_blk, g_col)
            gcf_ref[hb, rows, :] = _dot3_left(lo_blk, g_col)
        g_row = -jnp.exp(pr_ref[hb, 0]) * _softplus(sr_ref[hb] + pr_ref[hb, 1])
        g_row = g_row.reshape(n_chunks * 8, CHUNK)
        grf_ref[hb] = _dot3_right(g_row, up_c).reshape(n_chunks, 8, CHUNK)
        grb_ref[hb] = _dot3_right(g_row, lo_c).reshape(n_chunks, 8, CHUNK)

    ti = lax.broadcasted_iota(jnp.int32, (CHUNK, CHUNK), 0)
    si = lax.broadcasted_iota(jnp.int32, (CHUNK, CHUNK), 1)

    def chunk_rows(c):
        return pl.ds(c * CHUNK if isinstance(c, int) else pl.multiple_of(c * CHUNK, CHUNK), CHUNK)

    group = min(n_chunks, 4)

    def prep_group(gi):
        items = [(hb, d, gi * group + j) for j in range(group) for hb in heads for d in range(2)]
        ms, rhs = [], []
        for hb, d, c in items:
            rows = chunk_rows(c)
            q = qn_ref[rows, hcols(hb)]
            k = kn_ref[rows, hcols(hb)]
            v = v_ref[rows, hcols(hb)].astype(F32)
            gc = (gcb_ref if d else gcf_ref)[hb, rows, d:d + 1]
            beta = jax.nn.sigmoid(sc_ref[hb, rows, 2 + d:3 + d])
            gcr = (grb_ref if d else grf_ref)[hb, c][d:d + 1, :]
            incl = (si >= ti) if d else (si <= ti)
            strict = (si > ti) if d else (si < ti)
            decay = jnp.where(incl, jnp.exp(jnp.where(incl, gc - gcr, 0.0)), 0.0)
            kb = k * beta
            ms.append(jnp.where(strict, _dot_nt(kb, k) * decay, 0.0))
            (qkb_ref if d else qkf_ref)[hb, rows, :] = _dot_nt(q, k) * decay
            rhs.append(jnp.concatenate([v * beta, kb * jnp.exp(gc)], axis=1).astype(BF16))
        xs = _unit_tri_inverse(ms)
        for (hb, d, c), x, r in zip(items, xs, rhs):
            (uwb_ref if d else uwf_ref)[hb, chunk_rows(c), :] = _dot(x, r)

    if n_chunks == group:
        prep_group(0)
    else:
        def prep_body(gi, carry):
            prep_group(gi)
            return carry
        lax.fori_loop(0, n_chunks // group, prep_body, 0)

    chains = [(hb, d) for hb in heads for d in range(2)]

    def body(i, carry):
        rows = [chunk_rows(n_chunks - 1 - i if d else i) for _, d in chains]
        gcs = [(gcb_ref if d else gcf_ref)[hb, r, d:d + 1] for (hb, d), r in zip(chains, rows)]
        s_b = [s.astype(BF16) for s in carry]
        uw = [(uwb_ref if d else uwf_ref)[hb, r, :] for (hb, d), r in zip(chains, rows)]
        ws = [_dot(a[:, GDN_DV:], s) for a, s in zip(uw, s_b)]
        v_new = [a[:, :GDN_DV] - w for a, w in zip(uw, ws)]
        g_last = [g[0:1] if d else g[CHUNK - 1:CHUNK] for (_, d), g in zip(chains, gcs)]
        upd = [_dot_tn(kn_ref[r, hcols(hb)] * jnp.exp(gl - g), vn)
               for (hb, _), r, gl, g, vn in zip(chains, rows, g_last, gcs, v_new)]
        new = tuple(s * jnp.exp(gl) + u for s, gl, u in zip(carry, g_last, upd))
        for (hb, d), r, g, s, vn in zip(chains, rows, gcs, s_b, v_new):
            o = _dot(qn_ref[r, hcols(hb)] * jnp.exp(g), s) + _dot((qkb_ref if d else qkf_ref)[hb, r, :], vn)
            (ob_ref if d else of_ref)[r, hcols(hb)] = o
        return new

    if zero_init:
        init = tuple(jnp.zeros((GDN_DK, GDN_DV), F32) for _ in chains)
    else:
        init = tuple(s0_ref[0, d, hb] for hb, d in chains)
    final = lax.fori_loop(0, n_chunks, body, init)
    for (hb, d), s in zip(chains, final):
        sfin_ref[0, d, hb] = s

    for hb in heads:
        o = of_ref[:, hcols(hb)] + ob_ref[:, hcols(hb)]
        y = _rms_f32(o, norm_ref[...])
        o_ref[:, hcols(hb)] = (y * _silu(ag_ref[:, hcols(hb)].astype(F32))).astype(o_ref.dtype)


GDN_HB = 2


def _gdn(cx, rest, sc, sr, pc, pr, norm, s0, *, n_seq, seq_len, row0):
    L = seq_len
    blk0 = row0 // L
    nh = GDN_HEADS
    hb = GDN_HB
    w = hb * GDN_DK
    zero_init = s0 is None
    k_off = GDN_QK_W // w
    v_off = 2 * GDN_QK_W // w
    ag_off = R_A_G // w
    in_specs = [pl.BlockSpec((L, w), lambda s, h: (blk0 + s, h)),
                pl.BlockSpec((L, w), lambda s, h: (blk0 + s, k_off + h)),
                pl.BlockSpec((L, w), lambda s, h: (blk0 + s, v_off + h)),
                pl.BlockSpec((L, w), lambda s, h: (blk0 + s, ag_off + h)),
                pl.BlockSpec((hb, L, 4), lambda s, h: (h, blk0 + s, 0)),
                pl.BlockSpec((hb, L // CHUNK, 8, CHUNK), lambda s, h: (h, blk0 + s, 0, 0)),
                pl.BlockSpec((hb, 8, LANE), lambda s, h: (h, 0, 0)),
                pl.BlockSpec((hb, 2, 8, CHUNK), lambda s, h: (h, 0, 0, 0)),
                pl.BlockSpec((1, GDN_DV), lambda s, h: (0, 0))]
    args = [cx, cx, cx, rest, sc, sr, pc, pr, norm]
    state_spec = pl.BlockSpec((1, 2, hb, GDN_DK, GDN_DV), lambda s, h: (s, 0, h, 0, 0))
    if not zero_init:
        in_specs.append(state_spec)
        args.append(s0)
    return pl.pallas_call(
        functools.partial(_gdn_kernel, zero_init=zero_init),
        grid=(n_seq, nh // hb),
        in_specs=in_specs,
        out_specs=[pl.BlockSpec((L, w), lambda s, h: (s, h)), state_spec],
        out_shape=[jax.ShapeDtypeStruct((n_seq * L, BRANCH_W), BF16),
                   jax.ShapeDtypeStruct((n_seq, 2, nh, GDN_DK, GDN_DV), F32)],
        scratch_shapes=[pltpu.VMEM((L, w), F32), pltpu.VMEM((L, w), F32),
                        pltpu.VMEM((hb, L, LANE), F32), pltpu.VMEM((hb, L, LANE), F32),
                        pltpu.VMEM((hb, L // CHUNK, 8, CHUNK), F32), pltpu.VMEM((hb, L // CHUNK, 8, CHUNK), F32),
                        pltpu.VMEM((L, w), F32), pltpu.VMEM((L, w), F32),
                        pltpu.VMEM((hb, L, 2 * GDN_DV), F32), pltpu.VMEM((hb, L, 2 * GDN_DV), F32),
                        pltpu.VMEM((hb, L, CHUNK), F32), pltpu.VMEM((hb, L, CHUNK), F32)],
        compiler_params=_cparams(("parallel", "arbitrary")),
        name="gdn_scan",
    )(*args)


def _gdn_side_inputs(small, gdn_a_log, gdn_dt_bias):
    t = small.shape[0]
    nh = GDN_HEADS
    ab = small[:, :4 * nh].reshape(t, 2, 2, nh)
    sc = ab.transpose(3, 0, 1, 2).reshape(nh, t, 4)
    rows = ab.transpose(3, 1, 2, 0).reshape(nh, 4, t // CHUNK, CHUNK).transpose(0, 2, 1, 3)
    sr = jnp.concatenate([rows, jnp.zeros_like(rows)], axis=2)
    pc = jnp.zeros((nh, 8, LANE), F32)
    pc = pc.at[:, 0, 0:2].set(gdn_a_log.T).at[:, 1, 0:2].set(gdn_dt_bias.T)
    pr = jnp.zeros((nh, 2, 8, CHUNK), F32)
    pr = pr.at[:, 0, 0:2, :].set(jnp.broadcast_to(gdn_a_log.T[:, :, None], (nh, 2, CHUNK)))
    pr = pr.at[:, 1, 0:2, :].set(jnp.broadcast_to(gdn_dt_bias.T[:, :, None], (nh, 2, CHUNK)))
    return sc, sr, pc, pr


GROUP_W = SSD_HPG * SSD_P


def _ssd_kernel(*refs, zero_init):
    if zero_init:
        (x_ref, b_ref, c_ref, z_ref, dc_ref, dr_ref, pc_ref, pr_ref, dskip_ref, norm_ref,
         o_ref, hfin_ref, dtc_ref, acf_ref, acb_ref, dtr_ref, arf_ref, arb_ref, yf_ref, yb_ref,
         hf_ref, hb_ref) = refs
        h0_ref = None
    else:
        (x_ref, b_ref, c_ref, z_ref, dc_ref, dr_ref, pc_ref, pr_ref, dskip_ref, norm_ref, h0_ref,
         o_ref, hfin_ref, dtc_ref, acf_ref, acb_ref, dtr_ref, arf_ref, arb_ref, yf_ref, yb_ref,
         hf_ref, hb_ref) = refs
    L = x_ref.shape[0]
    n_chunks = L // CHUNK
    nh = SSD_HPG

    dtc_ref[...] = jnp.zeros_like(dtc_ref)
    dtc_ref[:, 0:2 * nh] = _softplus(dc_ref[0] + pc_ref[0, 1:2, 0:2 * nh])
    acf_ref[...] = jnp.zeros_like(acf_ref)
    acf_ref[:, 0:2 * nh] = -jnp.exp(pc_ref[0, 0:1, 0:2 * nh]) * dtc_ref[:, 0:2 * nh]
    blk = min(L, 256)
    lo_blk = _chunk_tri(blk, upper=False)
    up_blk = _chunk_tri(blk, upper=True)
    for b in range(L // blk):
        rows = slice(b * blk, (b + 1) * blk)
        da = acf_ref[rows, :]
        acb_ref[rows, :] = _dot3_left(up_blk, da)
        acf_ref[rows, :] = _dot3_left(lo_blk, da)
    dt_row = _softplus(dr_ref[0] + pr_ref[0, 1])
    dtr_ref[...] = dt_row
    da_row = (-jnp.exp(pr_ref[0, 0]) * dt_row).reshape(n_chunks * 2 * nh, CHUNK)
    arf_ref[...] = _dot3_right(da_row, _chunk_tri(CHUNK, upper=True)).reshape(n_chunks, 2 * nh, CHUNK)
    arb_ref[...] = _dot3_right(da_row, _chunk_tri(CHUNK, upper=False)).reshape(n_chunks, 2 * nh, CHUNK)

    if zero_init:
        hf_ref[...] = jnp.zeros_like(hf_ref)
        hb_ref[...] = jnp.zeros_like(hb_ref)
    else:
        hf_ref[...] = h0_ref[0, 0, 0]
        hb_ref[...] = h0_ref[0, 1, 0]

    ti = lax.broadcasted_iota(jnp.int32, (CHUNK, CHUNK), 0)
    si = lax.broadcasted_iota(jnp.int32, (CHUNK, CHUNK), 1)
    ej = lax.broadcasted_iota(jnp.int32, (LANE, GROUP_W), 0)
    ec = lax.broadcasted_iota(jnp.int32, (LANE, GROUP_W), 1)

    def step(d, c):
        rows = pl.ds(pl.multiple_of(c * CHUNK, CHUNK), CHUNK)
        h_ref = hb_ref if d else hf_ref
        x = x_ref[rows, :]
        bm = b_ref[rows, :]
        cm = c_ref[rows, :]
        acs = (acb_ref if d else acf_ref)[rows, :]
        acs_r = (arb_ref if d else arf_ref)[c][d * nh:(d + 1) * nh, :]
        dt_r = dtr_ref[c][d * nh:(d + 1) * nh, :]
        incl = (si >= ti) if d else (si <= ti)
        cb = _dot_nt(cm, bm)
        spread = jnp.where(ej == d * nh + (ec >> 6), 1.0, 0.0).astype(BF16)
        last = acs[0:1] if d else acs[CHUNK - 1:CHUNK]
        p_full = _dot3_right(dtc_ref[rows, :] * jnp.exp(last - acs), spread)
        e_full = _dot3_right(jnp.exp(acs), spread)
        cd_full = _dot3_right(jnp.broadcast_to(jnp.exp(last), (8, LANE)), spread)[0:1]
        h_in = h_ref[...]
        y_off = _dot(cm, h_in)
        h_ref[...] = h_in * cd_full + _dot_tn(bm, x.astype(F32) * p_full)
        yield
        ys = []
        for h in range(nh):
            diff = acs[:, d * nh + h:d * nh + h + 1] - acs_r[h:h + 1, :]
            seg = jnp.where(incl, jnp.exp(jnp.where(incl, diff, 0.0)), 0.0)
            ys.append(_dot(cb * seg * dt_r[h:h + 1, :], x[:, h * SSD_P:(h + 1) * SSD_P]))
        yield
        (yb_ref if d else yf_ref)[rows, :] = jnp.concatenate(ys, axis=1) + y_off * e_full

    def body(i, carry):
        _interleave(step(0, 2 * i), step(1, n_chunks - 1 - 2 * i),
                    step(0, 2 * i + 1), step(1, n_chunks - 2 - 2 * i))
        return carry

    lax.fori_loop(0, n_chunks // 2, body, 0)
    hfin_ref[0, 0, 0] = hf_ref[...]
    hfin_ref[0, 1, 0] = hb_ref[...]

    y = yf_ref[...] + yb_ref[...] + dskip_ref[0] * x_ref[...].astype(F32)
    y = y * _silu(z_ref[...].astype(F32))
    o_ref[...] = _rms_f32(y, norm_ref[...]).astype(o_ref.dtype)


def _ssd(cx, rest, dc, dr, pc, pr, dskip, norm, h0, *, n_seq, seq_len, row0):
    L = seq_len
    blk0 = row0 // L
    ng = SSD_GROUPS
    zero_init = h0 is None
    x_off = (2 * GDN_QK_W + BRANCH_W) // GROUP_W
    b_off = (2 * GDN_QK_W + 2 * BRANCH_W) // SSD_N
    c_off = b_off + ng
    z_off = R_B_Z // GROUP_W
    in_specs = [pl.BlockSpec((L, GROUP_W), lambda s, g: (blk0 + s, x_off + g)),
                pl.BlockSpec((L, SSD_N), lambda s, g: (blk0 + s, b_off + g)),
                pl.BlockSpec((L, SSD_N), lambda s, g: (blk0 + s, c_off + g)),
                pl.BlockSpec((L, GROUP_W), lambda s, g: (blk0 + s, z_off + g)),
                pl.BlockSpec((1, L, 2 * SSD_HPG), lambda s, g: (g, blk0 + s, 0)),
                pl.BlockSpec((1, L // CHUNK, 2 * SSD_HPG, CHUNK), lambda s, g: (g, blk0 + s, 0, 0)),
                pl.BlockSpec((1, 8, LANE), lambda s, g: (g, 0, 0)),
                pl.BlockSpec((1, 2, 2 * SSD_HPG, CHUNK), lambda s, g: (g, 0, 0, 0)),
                pl.BlockSpec((1, 1, GROUP_W), lambda s, g: (g, 0, 0)),
                pl.BlockSpec((1, GROUP_W), lambda s, g: (0, g))]
    args = [cx, cx, cx, rest, dc, dr, pc, pr, dskip, norm]
    state_spec = pl.BlockSpec((1, 2, 1, SSD_N, GROUP_W), lambda s, g: (s, 0, g, 0, 0))
    if not zero_init:
        in_specs.append(state_spec)
        args.append(h0)
    return pl.pallas_call(
        functools.partial(_ssd_kernel, zero_init=zero_init),
        grid=(n_seq, ng),
        in_specs=in_specs,
        out_specs=[pl.BlockSpec((L, GROUP_W), lambda s, g: (s, g)), state_spec],
        out_shape=[jax.ShapeDtypeStruct((n_seq * L, BRANCH_W), BF16),
                   jax.ShapeDtypeStruct((n_seq, 2, ng, SSD_N, GROUP_W), F32)],
        scratch_shapes=[pltpu.VMEM((L, LANE), F32), pltpu.VMEM((L, LANE), F32), pltpu.VMEM((L, LANE), F32),
                        pltpu.VMEM((L // CHUNK, 2 * SSD_HPG, CHUNK), F32),
                        pltpu.VMEM((L // CHUNK, 2 * SSD_HPG, CHUNK), F32),
                        pltpu.VMEM((L // CHUNK, 2 * SSD_HPG, CHUNK), F32),
                        pltpu.VMEM((L, GROUP_W), F32), pltpu.VMEM((L, GROUP_W), F32),
                        pltpu.VMEM((SSD_N, GROUP_W), F32), pltpu.VMEM((SSD_N, GROUP_W), F32)],
        compiler_params=_cparams(("parallel", "arbitrary")),
        name="ssd_scan",
    )(*args)


def _ssd_side_inputs(small, ssd_a_log, ssd_dt_bias, ssd_d):
    t = small.shape[0]
    ng, nh = SSD_GROUPS, SSD_HPG
    off = 4 * GDN_HEADS
    dt = small[:, off:off + 2 * SSD_HEADS].reshape(t, 2, ng, nh)
    dc = dt.transpose(2, 0, 1, 3).reshape(ng, t, 2 * nh)
    dr = dt.transpose(2, 1, 3, 0).reshape(ng, 2 * nh, t // CHUNK, CHUNK).transpose(0, 2, 1, 3)
    a = ssd_a_log.reshape(2, ng, nh).transpose(1, 0, 2).reshape(ng, 2 * nh)
    bias = ssd_dt_bias.reshape(2, ng, nh).transpose(1, 0, 2).reshape(ng, 2 * nh)
    pc = jnp.zeros((ng, 8, LANE), F32).at[:, 0, 0:2 * nh].set(a).at[:, 1, 0:2 * nh].set(bias)
    pr = jnp.stack([jnp.broadcast_to(a[:, :, None], (ng, 2 * nh, CHUNK)),
                    jnp.broadcast_to(bias[:, :, None], (ng, 2 * nh, CHUNK))], axis=1)
    dskip = jnp.repeat(ssd_d.reshape(ng, nh), SSD_P, axis=1)[:, None, :]
    return dc, dr, pc, pr, dskip


def _ssd_state_to_kernel(s):
    n = s.shape[0]
    return s.reshape(n, 2, SSD_GROUPS, SSD_HPG, SSD_N, SSD_P).transpose(0, 1, 2, 4, 3, 5).reshape(
        n, 2, SSD_GROUPS, SSD_N, GROUP_W)


def _ssd_state_from_kernel(s):
    n = s.shape[0]
    return s.reshape(n, 2, SSD_GROUPS, SSD_N, SSD_HPG, SSD_P).transpose(0, 1, 2, 4, 3, 5).reshape(
        n, 2, SSD_HEADS, SSD_N, SSD_P)


SUB = 4
SUB_SHIFT = 2
N_SUB = CHUNK // SUB


def _hgrn_kernel(*refs, zero_init):
    if zero_init:
        (q_ref, zf_ref, zb_ref, v_ref, cg_ref, lb_ref, norm_ref,
         o_ref, sfin_ref, qs_ref, kf_ref, kb_ref, gf_ref, gb_ref, of_ref, ob_ref) = refs
        s0_ref = None
    else:
        (q_ref, zf_ref, zb_ref, v_ref, cg_ref, lb_ref, norm_ref, s0_ref,
         o_ref, sfin_ref, qs_ref, kf_ref, kb_ref, gf_ref, gb_ref, of_ref, ob_ref) = refs
    L = q_ref.shape[0]
    n_chunks = L // CHUNK

    qs_ref[...] = _silu(q_ref[...].astype(F32))
    blk = min(L, 256)
    lo_blk = _chunk_tri(blk, upper=False)
    up_blk = _chunk_tri(blk, upper=True)
    for d, (z_ref, k_ref, g_ref, tri) in enumerate(((zf_ref, kf_ref, gf_ref, lo_blk),
                                                    (zb_ref, kb_ref, gb_ref, up_blk))):
        lbd = lb_ref[d:d + 1, :]
        z = z_ref[...].astype(F32)
        k_ref[...] = (1.0 - lbd) * jax.nn.sigmoid(-z)
        log_f = jnp.log(lbd + (1.0 - lbd) * jax.nn.sigmoid(z))
        for b in range(L // blk):
            rows = slice(b * blk, (b + 1) * blk)
            g_ref[rows, :] = _dot3_left(tri, log_f[rows])

    ti = lax.broadcasted_iota(jnp.int32, (CHUNK, CHUNK), 0)
    si = lax.broadcasted_iota(jnp.int32, (CHUNK, CHUNK), 1)
    tcol = lax.broadcasted_iota(jnp.int32, (CHUNK, 1), 0)

    def padded(part, lo):
        hi = lo + part.shape[0]
        pieces = ([jnp.zeros((lo, HGRN_DK), F32)] if lo else []) + [part] + (
            [jnp.zeros((CHUNK - hi, HGRN_DK), F32)] if hi < CHUNK else [])
        return jnp.concatenate(pieces, axis=0).astype(BF16)

    def step(d, c, st_in, st_out):
        st = st_in()
        rows = pl.ds(pl.multiple_of(c * CHUNK, CHUNK), CHUNK)
        q = qs_ref[rows, :]
        k = (kb_ref if d else kf_ref)[rows, :]
        g = (gb_ref if d else gf_ref)[rows, :]
        v = v_ref[rows, :].astype(F32)
        if d:
            sel = jnp.where(si == ((ti >> SUB_SHIFT) << SUB_SHIFT) + SUB, 1.0, 0.0).astype(BF16)
        else:
            sel = jnp.where(si == ((ti >> SUB_SHIFT) << SUB_SHIFT) - 1, 1.0, 0.0).astype(BF16)
        g_ref_rows = _dot3_left(sel, g)
        g_last = g[0:1] if d else g[CHUNK - 1:CHUNK]
        st_out(st * jnp.exp(g_last) + _dot_tn(v, k * jnp.exp(g_last - g)))
        yield
        q_in = q * jnp.exp(g - g_ref_rows)
        k_parts, q_parts = [], []
        for j in range(1, N_SUB):
            if d:
                edge = CHUNK - SUB * j
                lo, hi = (edge // 8) * 8, CHUNK
                use = tcol[lo:hi] >= edge
                mine_blk = N_SUB - 1 - j
            else:
                edge = SUB * j
                lo, hi = 0, -(-edge // 8) * 8
                use = tcol[lo:hi] < edge
                mine_blk = j
            ref_row = g[edge:edge + 1, :] if d else g[edge - 1:edge, :]
            part = jnp.where(use, k[lo:hi] * jnp.exp(jnp.where(use, ref_row - g[lo:hi], 0.0)), 0.0)
            k_parts.append(padded(part, lo))
            qlo = (mine_blk * SUB // 8) * 8
            qpart = jnp.where((tcol[qlo:qlo + 8] >> SUB_SHIFT) == mine_blk, q_in[qlo:qlo + 8], 0.0)
            q_parts.append(padded(qpart, qlo))
        yield
        att = _dot_nt(jnp.concatenate(q_parts, axis=1), jnp.concatenate(k_parts, axis=1))
        o_state = _dot_nt(q * jnp.exp(g), st)
        yield
        o = _dot(att, v) + o_state
        pos = tcol & (SUB - 1)
        for lag in range(SUB):
            if lag == 0:
                a = jnp.sum(q * k, axis=-1, keepdims=True)
                o = o + a * v
                continue
            shift = (CHUNK - lag) if d else lag
            valid = (pos + lag <= SUB - 1) if d else (pos >= lag)
            k_l = pltpu.roll(k, shift, 0)
            g_l = pltpu.roll(g, shift, 0)
            v_l = pltpu.roll(v, shift, 0)
            dec = jnp.exp(jnp.where(valid, g - g_l, 0.0))
            a = jnp.where(valid, jnp.sum(q * k_l * dec, axis=-1, keepdims=True), 0.0)
            o = o + a * v_l
        yield
        (ob_ref if d else of_ref)[rows, :] = o

    def body(i, carry):
        box = {"f0": carry[0], "b0": carry[1]}
        put = lambda key: (lambda val: box.__setitem__(key, val))
        get = lambda key: (lambda: box[key])
        _interleave(step(0, 2 * i, get("f0"), put("f1")),
                    step(1, n_chunks - 1 - 2 * i, get("b0"), put("b1")),
                    step(0, 2 * i + 1, get("f1"), put("f2")),
                    step(1, n_chunks - 2 - 2 * i, get("b1"), put("b2")))
        return box["f2"], box["b2"]

    if zero_init:
        init = (jnp.zeros((HGRN_DV, HGRN_DK), F32), jnp.zeros((HGRN_DV, HGRN_DK), F32))
    else:
        init = (s0_ref[0, 0, 0], s0_ref[0, 1, 0])
    s_f, s_b = lax.fori_loop(0, n_chunks // 2, body, init)
    sfin_ref[0, 0, 0] = s_f
    sfin_ref[0, 1, 0] = s_b

    o = of_ref[...] + ob_ref[...]
    y = _rms_f32(o, norm_ref[...])
    o_ref[...] = (y * _silu(cg_ref[...].astype(F32))).astype(o_ref.dtype)


def _hgrn(rest, lb, norm, s0, *, n_seq, seq_len, row0):
    L = seq_len
    blk0 = row0 // L
    nh = HGRN_HEADS
    zero_init = s0 is None
    q_off, zf_off, v_off, g_off = R_C_Q // LANE, R_C_F // LANE, R_C_I // LANE, R_C_G // LANE
    zb_off = zf_off + nh
    in_specs = [pl.BlockSpec((L, LANE), lambda s, h: (blk0 + s, q_off + h)),
                pl.BlockSpec((L, LANE), lambda s, h: (blk0 + s, zf_off + h)),
                pl.BlockSpec((L, LANE), lambda s, h: (blk0 + s, zb_off + h)),
                pl.BlockSpec((L, LANE), lambda s, h: (blk0 + s, v_off + h)),
                pl.BlockSpec((L, LANE), lambda s, h: (blk0 + s, g_off + h)),
                pl.BlockSpec((2, HGRN_DK), lambda s, h: (0, h)),
                pl.BlockSpec((1, HGRN_DV), lambda s, h: (0, 0))]
    args = [rest, rest, rest, rest, rest, lb, norm]
    state_spec = pl.BlockSpec((1, 2, 1, HGRN_DV, HGRN_DK), lambda s, h: (s, 0, h, 0, 0))
    if not zero_init:
        in_specs.append(state_spec)
        args.append(s0)
    return pl.pallas_call(
        functools.partial(_hgrn_kernel, zero_init=zero_init),
        grid=(n_seq, nh),
        in_specs=in_specs,
        out_specs=[pl.BlockSpec((L, HGRN_DV), lambda s, h: (s, h)), state_spec],
        out_shape=[jax.ShapeDtypeStruct((n_seq * L, BRANCH_W), BF16),
                   jax.ShapeDtypeStruct((n_seq, 2, nh, HGRN_DV, HGRN_DK), F32)],
        scratch_shapes=[pltpu.VMEM((L, HGRN_DK), F32)] * 5 + [pltpu.VMEM((L, HGRN_DV), F32)] * 2,
        compiler_params=_cparams(("parallel", "arbitrary")),
        name="hgrn_scan",
    )(*args)


def _route_kernel(aff_ref, sel_ref, *, cap):
    bits = pltpu.bitcast(aff_ref[...], jnp.int32)
    n_rows = bits.shape[0]
    row = lax.broadcasted_iota(jnp.int32, bits.shape, 0)
    count = lambda m: jnp.sum(m.astype(jnp.int32), axis=0, keepdims=True)

    def thr_body(i, thr):
        cand = thr | (1 << (30 - i))
        return jnp.where(count(bits >= cand) >= cap, cand, thr)

    thr = lax.fori_loop(0, 31, thr_body, jnp.zeros((1, bits.shape[1]), jnp.int32))
    above = bits > thr
    tie = bits == thr
    need = cap - count(above)
    n_bits = max(1, (n_rows - 1).bit_length())

    def tie_body(i, lim):
        cand = lim | (1 << (n_bits - 1 - i))
        return jnp.where(count(tie & (row < cand)) < need, cand, lim)

    lim = lax.fori_loop(0, n_bits, tie_body, jnp.zeros_like(thr))
    sel = above | (tie & (row <= lim) & (need > 0))
    sel_ref[...] = jnp.where(sel, 1.0, 0.0)


def _route(aff, row0, n_rows):
    cap = CAPACITY_FACTOR * n_rows // N_EXPERTS
    return pl.pallas_call(
        functools.partial(_route_kernel, cap=cap),
        grid=(1,),
        in_specs=[pl.BlockSpec((n_rows, LANE), lambda i: (row0 // n_rows, 0))],
        out_specs=pl.BlockSpec((n_rows, LANE), lambda i: (0, 0)),
        out_shape=jax.ShapeDtypeStruct((n_rows, LANE), F32),
        compiler_params=_cparams(("arbitrary",)),
        name="route_topk",
    )(aff)


FF_TILE = 256


def _ffn_kernel(x_ref, wg_ref, wu_ref, wd_ref, gate_ref, o_ref, xb_ref):
    f = pl.program_id(1)

    @pl.when(f == 0)
    def _():
        words = x_ref[0]
        half = words.shape[1]
        xb_ref[:, :half] = pltpu.bitcast(words & jnp.int32(-65536), F32).astype(BF16)
        xb_ref[:, half:] = pltpu.bitcast(words << 16, F32).astype(BF16)

    x = xb_ref[...]
    g = jnp.dot(x, wg_ref[0].astype(BF16), preferred_element_type=F32)
    u = jnp.dot(x, wu_ref[0].astype(BF16), preferred_element_type=F32)
    y = jnp.dot((_silu(g) * u).astype(BF16), wd_ref[0].astype(BF16), preferred_element_type=F32)

    @pl.when(f == 0)
    def _():
        o_ref[0] = y

    @pl.when(f > 0)
    def _():
        o_ref[0] += y

    @pl.when(f == pl.num_programs(1) - 1)
    def _():
        o_ref[0] = o_ref[0] * gate_ref[0]


def _expert_ffn(xe, w_gu, w_down, layer, gate):
    e, cap, _ = xe.shape
    ff, d = w_down.shape[1:]
    nf = ff // FF_TILE
    e0 = layer * e
    return pl.pallas_call(
        _ffn_kernel,
        grid=(e, nf),
        in_specs=[pl.BlockSpec((1, cap, d // 2), lambda g, f: (g, 0, 0)),
                  pl.BlockSpec((1, d, FF_TILE), lambda g, f: (e0 + g, 0, f)),
                  pl.BlockSpec((1, d, FF_TILE), lambda g, f: (e0 + g, 0, nf + f)),
                  pl.BlockSpec((1, FF_TILE, d), lambda g, f: (e0 + g, f, 0)),
                  pl.BlockSpec((1, cap, 1), lambda g, f: (g, 0, 0))],
        out_specs=pl.BlockSpec((1, cap, d), lambda g, f: (g, 0, 0)),
        out_shape=jax.ShapeDtypeStruct((e, cap, d), F32),
        scratch_shapes=[pltpu.VMEM((cap, d), BF16)],
        compiler_params=_cparams(("parallel", "arbitrary"), BIG_VMEM_LIMIT),
        name="expert_ffn",
    )(xe, w_gu, w_gu, w_down, gate)


def _ec_moe(h2w, aff, sel, w_gu, w_down, layer):
    t = h2w.shape[0]
    d = w_down.shape[-1]
    cap = CAPACITY_FACTOR * t // N_EXPERTS
    tok = jnp.arange(t, dtype=jnp.int32)[:, None]
    keys = jnp.where(sel[:, :N_EXPERTS] > 0.5, tok, tok + t).T
    idx = jnp.sort(keys, axis=1)[:, :cap]
    gate = jnp.take_along_axis(aff[:, :N_EXPERTS].T, idx, axis=1)[..., None]
    ye = _expert_ffn(h2w[idx], w_gu, w_down, layer, gate)
    return jnp.zeros((t, d), F32).at[idx.reshape(-1)].add(ye.reshape(-1, d))


def _resid_kernel(x_ref, m_ref, g_ref, mod_ref, gn_ref, modn_ref, xo_ref, h_ref):
    x = x_ref[...] + mod_ref[0, 5:6, :] * _rms_f32(m_ref[...], g_ref[...])
    xo_ref[...] = x
    h_ref[...] = (_rms_f32(x, gn_ref[...]) * (1.0 + modn_ref[0, 1:2, :]) + modn_ref[0, 0:1, :]).astype(BF16)


def _moe_residual(x1, moe, gain3, mod, gain_next, mod_next, n_ctx_rows, dec_seq, tm=512):
    t, d = x1.shape
    row_spec = pl.BlockSpec((tm, d), lambda i: (i, 0))
    vec_spec = pl.BlockSpec((1, d), lambda i: (0, 0))
    mod_spec = pl.BlockSpec((1, N_MOD, d), lambda i: (_mod_row(i, tm, n_ctx_rows, dec_seq), 0, 0))
    return pl.pallas_call(
        _resid_kernel,
        grid=(t // tm,),
        in_specs=[row_spec, row_spec, vec_spec, mod_spec, vec_spec, mod_spec],
        out_specs=[row_spec, row_spec],
        out_shape=[jax.ShapeDtypeStruct((t, d), F32), jax.ShapeDtypeStruct((t, d), BF16)],
        compiler_params=_cparams(("parallel",)),
        name="moe_residual",
    )(x1, moe, gain3, mod, gain_next, mod_next)


def kernel(x_prompt, x_sample, c, state_gdn, state_ssd, state_hgrn, c_ctx, w_mod, b_mod, norm_gain,
           w_in, conv_w, conv_b, gdn_a_log, gdn_dt_bias, gdn_norm, ssd_a_log, ssd_dt_bias, ssd_d,
           ssd_norm, hgrn_lb, hgrn_norm, w_branch, w_out, w_router, w_gu, w_down):
    nb_c, seq_c, d = x_prompt.shape
    nb_s, seq_s, _ = x_sample.shape
    n_ctx = nb_c * seq_c
    n_smp = nb_s * seq_s

    lb_w = jax.nn.softmax(hgrn_lb.astype(F32), axis=0)
    lb = jnp.cumsum(lb_w, axis=0) - lb_w[:1]

    cond = jnp.concatenate([c_ctx[None, :], c, jnp.zeros((8 - 1 - nb_s, d), F32)], axis=0)
    mod_all = _mod_all(cond, w_mod, b_mod[:, None, :]).reshape(DEPTH, 8, N_MOD, d)

    x = jnp.concatenate([x_prompt.reshape(n_ctx, d), x_sample.reshape(n_smp, d)], axis=0)
    new_gdn, new_ssd, new_hgrn = [], [], []

    w_conv = w_in[:, :, :CONV_CH].astype(BF16)
    w_rest = jnp.concatenate([w_in[:, :, OFF_M_G:], w_in[:, :, OFF_A_G:OFF_ALPHA], w_in[:, :, OFF_B_Z:OFF_B_DT],
                              w_in[:, :, OFF_C_Q:OFF_M_G]], axis=2).astype(BF16)
    w_small = jnp.concatenate([w_in[:, :, OFF_ALPHA:OFF_B_Z], w_in[:, :, OFF_B_DT:OFF_C_Q],
                               jnp.zeros((DEPTH, d, SMALL_W - 4 * GDN_HEADS - 2 * SSD_HEADS), F32)], axis=2)
    w_gu_all = w_gu.reshape(DEPTH * N_EXPERTS, d, 2 * EXPERT_FF)
    w_down_all = w_down.reshape(DEPTH * N_EXPERTS, EXPERT_FF, d)

    for l in range(DEPTH):
        mod = mod_all[l]
        if l == 0:
            h = _hmod(x, norm_gain[l, 0:1], mod, n_ctx, seq_s, i_shift=0, i_scale=1)
        cx = _mm_conv(h, w_conv, l, conv_w[l], conv_b[l][None, :], n_ctx, seq_c, GRID_W)
        rest = _mm_plain(h, w_rest, l, tn=1024)
        small = _mm_small(h, w_small, l)

        gdn_side = _gdn_side_inputs(small, gdn_a_log[l], gdn_dt_bias[l])
        oa_c, sg = _gdn(cx, rest, *gdn_side, gdn_norm[l][None, :], None, n_seq=nb_c, seq_len=seq_c, row0=0)
        oa_s, _ = _gdn(cx, rest, *gdn_side, gdn_norm[l][None, :], state_gdn[:, l], n_seq=nb_s, seq_len=seq_s,
                       row0=n_ctx)
        ssd_side = _ssd_side_inputs(small, ssd_a_log[l], ssd_dt_bias[l], ssd_d[l])
        ob_c, ss = _ssd(cx, rest, *ssd_side, ssd_norm[l][None, :], None, n_seq=nb_c, seq_len=seq_c, row0=0)
        ob_s, _ = _ssd(cx, rest, *ssd_side, ssd_norm[l][None, :], _ssd_state_to_kernel(state_ssd[:, l]),
                       n_seq=nb_s, seq_len=seq_s, row0=n_ctx)
        ss = _ssd_state_from_kernel(ss)
        oc_c, sh = _hgrn(rest, lb[l], hgrn_norm[l][None, :], None, n_seq=nb_c, seq_len=seq_c, row0=0)
        oc_s, _ = _hgrn(rest, lb[l], hgrn_norm[l][None, :], jnp.swapaxes(state_hgrn[:, l], -1, -2),
                        n_seq=nb_s, seq_len=seq_s, row0=n_ctx)
        sh = jnp.swapaxes(sh, -1, -2)
        new_gdn.append(sg)
        new_ssd.append(ss)
        new_hgrn.append(sh)
        branches = jnp.stack([jnp.concatenate([oa_c, oa_s]), jnp.concatenate([ob_c, ob_s]),
                              jnp.concatenate([oc_c, oc_s])])

        merged = _merge(branches, rest, w_branch[l].astype(BF16))
        w_router_p = jnp.concatenate([w_router[l], jnp.zeros((d, LANE - N_EXPERTS), F32)], axis=1)
        x1, h2, aff = _outproj(merged, w_out[l].astype(BF16), x, norm_gain[l], mod, w_router_p, n_ctx, seq_s)

        moe_c = _ec_moe(h2[:n_ctx], aff[:n_ctx], _route(aff, 0, n_ctx), w_gu_all, w_down_all, l)
        moe_s = _ec_moe(h2[n_ctx:], aff[n_ctx:], _route(aff, n_ctx, n_smp), w_gu_all, w_down_all, l)
        moe = jnp.concatenate([moe_c, moe_s])
        l_next = min(l + 1, DEPTH - 1)
        x, h = _moe_residual(x1, moe, norm_gain[l, 3:4], mod, norm_gain[l_next, 0:1], mod_all[l_next],
                             n_ctx, seq_s)

    y_prompt = x[:n_ctx].reshape(nb_c, seq_c, d)
    y_sample = x[n_ctx:].reshape(nb_s, seq_s, d)
    return (y_prompt, y_sample, jnp.stack(new_gdn, axis=1), jnp.stack(new_ssd, axis=1),
            jnp.stack(new_hgrn, axis=1))
```

```python
import functools

import numpy as np
import jax
import jax.numpy as jnp
from jax import lax
from jax.experimental import pallas as pl
from jax.experimental.pallas import tpu as pltpu

D_MODEL = 2048
DEPTH = 2
GRID_W = 64
N_BRANCH = 3
BRANCH_W = D_MODEL // 2
GDN_DK = 128
GDN_DV = 128
GDN_HEADS = BRANCH_W // GDN_DV
GDN_CHUNK = 64
SSD_P = 64
SSD_HEADS = BRANCH_W // SSD_P
SSD_GROUPS = 2
SSD_HPG = SSD_HEADS // SSD_GROUPS
SSD_N = 128
SSD_CHUNK = 64
HGRN_DK = 128
HGRN_DV = 128
HGRN_HEADS = BRANCH_W // HGRN_DV
HGRN_CHUNK = 16
CONV_K = 3
N_EXPERTS = 16
EXPERT_FF = D_MODEL // 2
CAPACITY_FACTOR = 2
N_MOD = 6
EPS = 1e-6
GDN_QK_W = GDN_HEADS * GDN_DK
HGRN_QK_W = HGRN_HEADS * HGRN_DK
CONV_SIZES = (GDN_QK_W, GDN_QK_W, BRANCH_W, BRANCH_W, SSD_GROUPS * SSD_N, SSD_GROUPS * SSD_N)
CONV_CH = GDN_QK_W + GDN_QK_W + BRANCH_W + BRANCH_W + 2 * SSD_GROUPS * SSD_N
OFF_A_G = CONV_CH
OFF_ALPHA = OFF_A_G + BRANCH_W
OFF_BETA = OFF_ALPHA + 2 * GDN_HEADS
OFF_B_Z = OFF_BETA + 2 * GDN_HEADS
OFF_B_DT = OFF_B_Z + BRANCH_W
OFF_C_Q = OFF_B_DT + 2 * SSD_HEADS
IN_COLS = OFF_C_Q + 3 * HGRN_QK_W + 2 * BRANCH_W + N_BRANCH * D_MODEL
OFF_M_G = IN_COLS - N_BRANCH * D_MODEL
REST_W = IN_COLS - CONV_CH - 4 * GDN_HEADS - 2 * SSD_HEADS
R_M_G, R_A_G, R_B_Z, R_C_Q, R_C_F, R_C_I, R_C_G = 0, 6144, 7168, 8192, 9216, 11264, 12288
SMALL_W = 128

LANE = 128
VMEM_LIMIT = 48 * 1024 * 1024
BIG_VMEM_LIMIT = 56 * 1024 * 1024
ROW_TILE = 1024

BF16 = jnp.bfloat16
F32 = jnp.float32


def _cparams(sem, vmem_limit=VMEM_LIMIT):
    return pltpu.CompilerParams(dimension_semantics=sem, vmem_limit_bytes=vmem_limit)


def _mod_row(i, tm, n_ctx_rows, dec_seq):
    return jnp.maximum((i * tm - n_ctx_rows) // dec_seq + 1, 0)


def _mod_kernel(c_ref, w_ref, b_ref, o_ref):
    c = c_ref[...]
    a = c * jax.nn.sigmoid(c)
    o_ref[0] = jnp.dot(a, w_ref[0], preferred_element_type=F32, precision=lax.Precision.HIGHEST) + b_ref[0]


def _mod_all(cond, w_mod, b_mod, tn=1024):
    nl, d, n = w_mod.shape
    return pl.pallas_call(
        _mod_kernel,
        grid=(nl, n // tn),
        in_specs=[pl.BlockSpec((8, d), lambda l, j: (0, 0)),
                  pl.BlockSpec((1, d, tn), lambda l, j: (l, 0, j)),
                  pl.BlockSpec((1, 1, tn), lambda l, j: (l, 0, j))],
        out_specs=pl.BlockSpec((1, 8, tn), lambda l, j: (l, 0, j)),
        out_shape=jax.ShapeDtypeStruct((nl, 8, n), F32),
        compiler_params=_cparams(("parallel", "arbitrary")),
        name="mod_vectors",
    )(cond, w_mod, b_mod)


def _rms_f32(x, gain):
    return x * lax.rsqrt(jnp.mean(x * x, axis=-1, keepdims=True) + EPS) * gain


def _hmod_kernel(x_ref, g_ref, mod_ref, h_ref, *, i_shift, i_scale):
    x = x_ref[...]
    y = _rms_f32(x, g_ref[...])
    h_ref[...] = (y * (1.0 + mod_ref[0, i_scale:i_scale + 1, :]) + mod_ref[0, i_shift:i_shift + 1, :]).astype(BF16)


def _hmod(x, gain, mod, n_ctx_rows, dec_seq, i_shift, i_scale, tm=512):
    t, d = x.shape
    return pl.pallas_call(
        functools.partial(_hmod_kernel, i_shift=i_shift, i_scale=i_scale),
        grid=(t // tm,),
        in_specs=[pl.BlockSpec((tm, d), lambda i: (i, 0)),
                  pl.BlockSpec((1, d), lambda i: (0, 0)),
                  pl.BlockSpec((1, N_MOD, d), lambda i: (_mod_row(i, tm, n_ctx_rows, dec_seq), 0, 0))],
        out_specs=pl.BlockSpec((tm, d), lambda i: (i, 0)),
        out_shape=jax.ShapeDtypeStruct((t, d), BF16),
        compiler_params=_cparams(("parallel",)),
        name="rms_modulate",
    )(x, gain, mod)


def _mm_plain_kernel(h_ref, w_ref, o_ref):
    o_ref[...] = jnp.dot(h_ref[...], w_ref[0], preferred_element_type=F32).astype(o_ref.dtype)


def _mm_plain(h, w, layer, tn, out_dtype=BF16, tm=ROW_TILE):
    t, k = h.shape
    n = w.shape[-1]
    return pl.pallas_call(
        _mm_plain_kernel,
        grid=(t // tm, n // tn),
        in_specs=[pl.BlockSpec((tm, k), lambda i, j: (i, 0)),
                  pl.BlockSpec((1, k, tn), lambda i, j: (layer, 0, j))],
        out_specs=pl.BlockSpec((tm, tn), lambda i, j: (i, j)),
        out_shape=jax.ShapeDtypeStruct((t, n), out_dtype),
        compiler_params=_cparams(("parallel", "arbitrary")),
        name="proj_rest",
    )(h, w)


def _mm_small_kernel(h_ref, w_ref, o_ref):
    o_ref[...] = _dot3_right_exact(h_ref[...], w_ref[0])


def _mm_small(h, w, layer, tm=256):
    t, k = h.shape
    n = w.shape[-1]
    return pl.pallas_call(
        _mm_small_kernel,
        grid=(t // tm,),
        in_specs=[pl.BlockSpec((tm, k), lambda i: (i, 0)),
                  pl.BlockSpec((1, k, n), lambda i: (layer, 0, 0))],
        out_specs=pl.BlockSpec((tm, n), lambda i: (i, 0)),
        out_shape=jax.ShapeDtypeStruct((t, n), F32),
        compiler_params=_cparams(("parallel",)),
        name="proj_small",
    )(h, w)


def _mm_conv_kernel(h_ref, w_ref, cw_ref, cb_ref, o_ref, *, n_ctx_tiles, ctx_period, grid_period):
    acc = jnp.dot(h_ref[...], w_ref[0], preferred_element_type=F32)
    tm = acc.shape[0]
    period = jnp.where(pl.program_id(0) < n_ctx_tiles, ctx_period, grid_period)
    pos = lax.broadcasted_iota(jnp.int32, (tm, 1), 0) & (period - 1)
    prev = jnp.where(pos == 0, 0.0, pltpu.roll(acc, 1, 0))
    nxt = jnp.where(pos == period - 1, 0.0, pltpu.roll(acc, tm - 1, 0))
    y = prev * cw_ref[0:1, :] + acc * cw_ref[1:2, :] + nxt * cw_ref[2:3, :] + cb_ref[...]
    o_ref[...] = (y * jax.nn.sigmoid(y)).astype(o_ref.dtype)


def _mm_conv(h, w, layer, conv_w, conv_b, n_ctx_rows, ctx_period, grid_period, tn=1152, tm=512):
    t, k = h.shape
    n = w.shape[-1]
    return pl.pallas_call(
        functools.partial(_mm_conv_kernel, n_ctx_tiles=n_ctx_rows // tm, ctx_period=ctx_period,
                          grid_period=grid_period),
        grid=(t // tm, n // tn),
        in_specs=[pl.BlockSpec((tm, k), lambda i, j: (i, 0)),
                  pl.BlockSpec((1, k, tn), lambda i, j: (layer, 0, j)),
                  pl.BlockSpec((CONV_K, tn), lambda i, j: (0, j)),
                  pl.BlockSpec((1, tn), lambda i, j: (0, j))],
        out_specs=pl.BlockSpec((tm, tn), lambda i, j: (i, j)),
        out_shape=jax.ShapeDtypeStruct((t, n), BF16),
        compiler_params=_cparams(("parallel", "arbitrary")),
        name="proj_conv",
    )(h, w, conv_w, conv_b)


def _merge_kernel(o_ref, g_ref, w_ref, out_ref, acc_ref):
    n = pl.program_id(1)
    p = jnp.dot(o_ref[0], w_ref[0], preferred_element_type=F32)
    contrib = jax.nn.sigmoid(g_ref[...].astype(F32)) * p

    @pl.when(n == 0)
    def _():
        acc_ref[...] = contrib

    @pl.when(n > 0)
    def _():
        acc_ref[...] += contrib

    @pl.when(n == N_BRANCH - 1)
    def _():
        out_ref[...] = acc_ref[...].astype(out_ref.dtype)


def _merge(branches, rest, w_branch, tm=ROW_TILE):
    _, t, kb = branches.shape
    d = w_branch.shape[-1]
    g_blk = R_M_G // d
    return pl.pallas_call(
        _merge_kernel,
        grid=(t // tm, N_BRANCH),
        in_specs=[pl.BlockSpec((1, tm, kb), lambda i, n: (n, i, 0)),
                  pl.BlockSpec((tm, d), lambda i, n: (i, g_blk + n)),
                  pl.BlockSpec((1, kb, d), lambda i, n: (n, 0, 0))],
        out_specs=pl.BlockSpec((tm, d), lambda i, n: (i, 0)),
        out_shape=jax.ShapeDtypeStruct((t, d), BF16),
        scratch_shapes=[pltpu.VMEM((tm, d), F32)],
        compiler_params=_cparams(("parallel", "arbitrary")),
        name="branch_merge",
    )(branches, rest, w_branch)


def _outproj_kernel(m_ref, w_ref, x_ref, g_ref, mod_ref, wr_ref, x1_ref, h2_ref, aff_ref):
    out = jnp.dot(m_ref[...], w_ref[...], preferred_element_type=F32)
    x1 = x_ref[...] + mod_ref[0, 2:3, :] * _rms_f32(out, g_ref[1:2, :])
    x1_ref[...] = x1
    h2 = _rms_f32(x1, g_ref[2:3, :]) * (1.0 + mod_ref[0, 4:5, :]) + mod_ref[0, 3:4, :]
    half = h2.shape[1] // 2
    hi = pltpu.bitcast(h2[:, :half].astype(BF16).astype(F32), jnp.int32)
    lo = pltpu.bitcast(h2[:, half:].astype(BF16).astype(F32), jnp.int32)
    h2_ref[...] = hi | lax.shift_right_logical(lo, 16)
    logits = jnp.dot(h2, wr_ref[...], preferred_element_type=F32, precision=lax.Precision.HIGHEST)
    lane = lax.broadcasted_iota(jnp.int32, logits.shape, 1)
    logits = jnp.where(lane < N_EXPERTS, logits, -jnp.inf)
    e = jnp.exp(logits - jnp.max(logits, axis=-1, keepdims=True))
    aff_ref[...] = e / jnp.sum(e, axis=-1, keepdims=True)


def _outproj(merged, w_out, x, gains, mod, w_router_p, n_ctx_rows, dec_seq, tm=512):
    t, d = x.shape
    return pl.pallas_call(
        _outproj_kernel,
        grid=(t // tm,),
        in_specs=[pl.BlockSpec((tm, d), lambda i: (i, 0)),
                  pl.BlockSpec((d, d), lambda i: (0, 0)),
                  pl.BlockSpec((tm, d), lambda i: (i, 0)),
                  pl.BlockSpec((4, d), lambda i: (0, 0)),
                  pl.BlockSpec((1, N_MOD, d), lambda i: (_mod_row(i, tm, n_ctx_rows, dec_seq), 0, 0)),
                  pl.BlockSpec((d, LANE), lambda i: (0, 0))],
        out_specs=[pl.BlockSpec((tm, d), lambda i: (i, 0)),
                   pl.BlockSpec((tm, d // 2), lambda i: (i, 0)),
                   pl.BlockSpec((tm, LANE), lambda i: (i, 0))],
        out_shape=[jax.ShapeDtypeStruct((t, d), F32),
                   jax.ShapeDtypeStruct((t, d // 2), jnp.int32),
                   jax.ShapeDtypeStruct((t, LANE), F32)],
        compiler_params=_cparams(("parallel",), BIG_VMEM_LIMIT),
        name="out_proj",
    )(merged, w_out, x, gains, mod, w_router_p)


CHUNK = 64
CHUNK_SHIFT = 6


def _dot(a, b):
    return jnp.dot(a.astype(BF16), b.astype(BF16), preferred_element_type=F32)


def _dot_nt(a, b):
    return lax.dot_general(a.astype(BF16), b.astype(BF16), (((1,), (1,)), ((), ())),
                           preferred_element_type=F32)


def _dot_tn(a, b):
    return lax.dot_general(a.astype(BF16), b.astype(BF16), (((0,), (0,)), ((), ())),
                           preferred_element_type=F32)


def _interleave(*gens):
    results = [None] * len(gens)
    live = list(range(len(gens)))
    while live:
        for i in list(live):
            try:
                next(gens[i])
            except StopIteration as stop:
                results[i] = stop.value
                live.remove(i)
    return results


def _split3(x):
    x1 = x.astype(BF16)
    r1 = x - x1.astype(F32)
    x2 = r1.astype(BF16)
    x3 = (r1 - x2.astype(F32)).astype(BF16)
    return x1, x2, x3


def _dot3_left(mask_bf16, x):
    x1, x2, x3 = _split3(x)
    f = lambda p: jnp.dot(mask_bf16, p, preferred_element_type=F32)
    return f(x3) + f(x2) + f(x1)


def _dot3_right_exact(a_bf16, x):
    x1, x2, x3 = _split3(x)
    f = lambda p: jnp.dot(a_bf16, p, preferred_element_type=F32)
    return f(x3) + f(x2) + f(x1)


def _dot3_right(x, mask_bf16):
    x1, x2, x3 = _split3(x)
    f = lambda p: jnp.dot(p, mask_bf16, preferred_element_type=F32)
    return f(x3) + f(x2) + f(x1)


def _softplus(x):
    return jnp.maximum(x, 0.0) + jnp.log(1.0 + jnp.exp(-jnp.abs(x)))


def _silu(x):
    return x * jax.nn.sigmoid(x)


def _chunk_tri(n, upper):
    r = lax.broadcasted_iota(jnp.int32, (n, n), 0)
    c = lax.broadcasted_iota(jnp.int32, (n, n), 1)
    same = (r >> CHUNK_SHIFT) == (c >> CHUNK_SHIFT)
    tri = (c >= r) if upper else (c <= r)
    return jnp.where(same & tri, 1.0, 0.0).astype(BF16)


def _unit_tri_inverse(ms):
    n = ms[0].shape[0]
    r = lax.broadcasted_iota(jnp.int32, (n, n), 0)
    c = lax.broadcasted_iota(jnp.int32, (n, n), 1)
    eye = jnp.where(r == c, 1.0, 0.0)
    same = lambda s: (r >> s) == (c >> s)
    m8 = [jnp.where(same(3), m, 0.0) for m in ms]
    m2 = [_dot(a, a) for a in m8]
    p = [eye - a for a in m8]
    m4 = [_dot(a, a) for a in m2]
    p = [a + _dot(a, b) for a, b in zip(p, m2)]
    xs = [a + _dot(a, b) for a, b in zip(p, m4)]
    s = 3
    while (1 << s) < n:
        band = same(s + 1) & jnp.logical_not(same(s))
        t = [_dot(x, jnp.where(band, m, 0.0)) for x, m in zip(xs, ms)]
        xs = [x - _dot(a, x) for x, a in zip(xs, t)]
        s += 1
    return xs


def _gdn_kernel(*refs, zero_init):
    if zero_init:
        (q_ref, k_ref, v_ref, ag_ref, sc_ref, sr_ref, pc_ref, pr_ref, norm_ref,
         o_ref, sfin_ref, qn_ref, kn_ref, gcf_ref, gcb_ref, grf_ref, grb_ref, of_ref, ob_ref,
         uwf_ref, uwb_ref, qkf_ref, qkb_ref) = refs
        s0_ref = None
    else:
        (q_ref, k_ref, v_ref, ag_ref, sc_ref, sr_ref, pc_ref, pr_ref, norm_ref, s0_ref,
         o_ref, sfin_ref, qn_ref, kn_ref, gcf_ref, gcb_ref, grf_ref, grb_ref, of_ref, ob_ref,
         uwf_ref, uwb_ref, qkf_ref, qkb_ref) = refs
    L = q_ref.shape[0]
    n_chunks = L // CHUNK
    heads = range(q_ref.shape[1] // GDN_DK)
    hcols = lambda hb: slice(hb * GDN_DK, (hb + 1) * GDN_DK)

    blk = min(L, 256)
    lo_blk = _chunk_tri(blk, upper=False)
    up_blk = _chunk_tri(blk, upper=True)
    up_c = _chunk_tri(CHUNK, upper=True)
    lo_c = _chunk_tri(CHUNK, upper=False)
    for hb in heads:
        qf = q_ref[:, hcols(hb)].astype(F32)
        qn_ref[:, hcols(hb)] = qf * lax.rsqrt(jnp.sum(qf * qf, axis=-1, keepdims=True) + EPS) * (GDN_DK ** -0.5)
        kf = k_ref[:, hcols(hb)].astype(F32)
        kn_ref[:, hcols(hb)] = kf * lax.rsqrt(jnp.sum(kf * kf, axis=-1, keepdims=True) + EPS)
        gcf_ref[hb] = jnp.zeros(gcf_ref.shape[1:], F32)
        gcf_ref[hb, :, 0:4] = -jnp.exp(pc_ref[hb, 0:1, 0:4]) * _softplus(sc_ref[hb] + pc_ref[hb, 1:2, 0:4])
        for b in range(L // blk):
            rows = slice(b * blk, (b + 1) * blk)
            g_col = gcf_ref[hb, rows, :]
            gcb_ref[hb, rows, :] = _dot3_left(up_blk, g_col)
            gcf_ref[hb, rows, :] = _dot3_left(lo_blk, g_col)
        g_row = -jnp.exp(pr_ref[hb, 0]) * _softplus(sr_ref[hb] + pr_ref[hb, 1])
        g_row = g_row.reshape(n_chunks * 8, CHUNK)
        grf_ref[hb] = _dot3_right(g_row, up_c).reshape(n_chunks, 8, CHUNK)
        grb_ref[hb] = _dot3_right(g_row, lo_c).reshape(n_chunks, 8, CHUNK)

    ti = lax.broadcasted_iota(jnp.int32, (CHUNK, CHUNK), 0)
    si = lax.broadcasted_iota(jnp.int32, (CHUNK, CHUNK), 1)

    def chunk_rows(c):
        return pl.ds(c * CHUNK if isinstance(c, int) else pl.multiple_of(c * CHUNK, CHUNK), CHUNK)

    group = min(n_chunks, 4)

    def prep_group(gi):
        items = [(hb, d, gi * group + j) for j in range(group) for hb in heads for d in range(2)]
        ms, rhs = [], []
        for hb, d, c in items:
            rows = chunk_rows(c)
            q = qn_ref[rows, hcols(hb)]
            k = kn_ref[rows, hcols(hb)]
            v = v_ref[rows, hcols(hb)].astype(F32)
            gc = (gcb_ref if d else gcf_ref)[hb, rows, d:d + 1]
            beta = jax.nn.sigmoid(sc_ref[hb, rows, 2 + d:3 + d])
            gcr = (grb_ref if d else grf_ref)[hb, c][d:d + 1, :]
            incl = (si >= ti) if d else (si <= ti)
            strict = (si > ti) if d else (si < ti)
            decay = jnp.where(incl, jnp.exp(jnp.where(incl, gc - gcr, 0.0)), 0.0)
            kb = k * beta
            ms.append(jnp.where(strict, _dot_nt(kb, k) * decay, 0.0))
            (qkb_ref if d else qkf_ref)[hb, rows, :] = _dot_nt(q, k) * decay
            rhs.append(jnp.concatenate([v * beta, kb * jnp.exp(gc)], axis=1).astype(BF16))
        xs = _unit_tri_inverse(ms)
        for (hb, d, c), x, r in zip(items, xs, rhs):
            (uwb_ref if d else uwf_ref)[hb, chunk_rows(c), :] = _dot(x, r)

    if n_chunks == group:
        prep_group(0)
    else:
        def prep_body(gi, carry):
            prep_group(gi)
            return carry
        lax.fori_loop(0, n_chunks // group, prep_body, 0)

    chains = [(hb, d) for hb in heads for d in range(2)]

    def body(i, carry):
        rows = [chunk_rows(n_chunks - 1 - i if d else i) for _, d in chains]
        gcs = [(gcb_ref if d else gcf_ref)[hb, r, d:d + 1] for (hb, d), r in zip(chains, rows)]
        s_b = [s.astype(BF16) for s in carry]
        uw = [(uwb_ref if d else uwf_ref)[hb, r, :] for (hb, d), r in zip(chains, rows)]
        ws = [_dot(a[:, GDN_DV:], s) for a, s in zip(uw, s_b)]
        v_new = [a[:, :GDN_DV] - w for a, w in zip(uw, ws)]
        g_last = [g[0:1] if d else g[CHUNK - 1:CHUNK] for (_, d), g in zip(chains, gcs)]
        upd = [_dot_tn(kn_ref[r, hcols(hb)] * jnp.exp(gl - g), vn)
               for (hb, _), r, gl, g, vn in zip(chains, rows, g_last, gcs, v_new)]
        new = tuple(s * jnp.exp(gl) + u for s, gl, u in zip(carry, g_last, upd))
        for (hb, d), r, g, s, vn in zip(chains, rows, gcs, s_b, v_new):
            o = _dot(qn_ref[r, hcols(hb)] * jnp.exp(g), s) + _dot((qkb_ref if d else qkf_ref)[hb, r, :], vn)
            (ob_ref if d else of_ref)[r, hcols(hb)] = o
        return new

    if zero_init:
        init = tuple(jnp.zeros((GDN_DK, GDN_DV), F32) for _ in chains)
    else:
        init = tuple(s0_ref[0, d, hb] for hb, d in chains)
    final = lax.fori_loop(0, n_chunks, body, init)
    for (hb, d), s in zip(chains, final):
        sfin_ref[0, d, hb] = s

    for hb in heads:
        o = of_ref[:, hcols(hb)] + ob_ref[:, hcols(hb)]
        y = _rms_f32(o, norm_ref[...])
        o_ref[:, hcols(hb)] = (y * _silu(ag_ref[:, hcols(hb)].astype(F32))).astype(o_ref.dtype)


GDN_HB = 4


def _gdn(cx, rest, sc, sr, pc, pr, norm, s0, *, n_seq, seq_len, row0):
    L = seq_len
    blk0 = row0 // L
    nh = GDN_HEADS
    hb = GDN_HB
    w = hb * GDN_DK
    zero_init = s0 is None
    k_off = GDN_QK_W // w
    v_off = 2 * GDN_QK_W // w
    ag_off = R_A_G // w
    in_specs = [pl.BlockSpec((L, w), lambda s, h: (blk0 + s, h)),
                pl.BlockSpec((L, w), lambda s, h: (blk0 + s, k_off + h)),
                pl.BlockSpec((L, w), lambda s, h: (blk0 + s, v_off + h)),
                pl.BlockSpec((L, w), lambda s, h: (blk0 + s, ag_off + h)),
                pl.BlockSpec((hb, L, 4), lambda s, h: (h, blk0 + s, 0)),
                pl.BlockSpec((hb, L // CHUNK, 8, CHUNK), lambda s, h: (h, blk0 + s, 0, 0)),
                pl.BlockSpec((hb, 8, LANE), lambda s, h: (h, 0, 0)),
                pl.BlockSpec((hb, 2, 8, CHUNK), lambda s, h: (h, 0, 0, 0)),
                pl.BlockSpec((1, GDN_DV), lambda s, h: (0, 0))]
    args = [cx, cx, cx, rest, sc, sr, pc, pr, norm]
    state_spec = pl.BlockSpec((1, 2, hb, GDN_DK, GDN_DV), lambda s, h: (s, 0, h, 0, 0))
    if not zero_init:
        in_specs.append(state_spec)
        args.append(s0)
    return pl.pallas_call(
        functools.partial(_gdn_kernel, zero_init=zero_init),
        grid=(n_seq, nh // hb),
        in_specs=in_specs,
        out_specs=[pl.BlockSpec((L, w), lambda s, h: (s, h)), state_spec],
        out_shape=[jax.ShapeDtypeStruct((n_seq * L, BRANCH_W), BF16),
                   jax.ShapeDtypeStruct((n_seq, 2, nh, GDN_DK, GDN_DV), F32)],
        scratch_shapes=[pltpu.VMEM((L, w), F32), pltpu.VMEM((L, w), F32),
                        pltpu.VMEM((hb, L, LANE), F32), pltpu.VMEM((hb, L, LANE), F32),
                        pltpu.VMEM((hb, L // CHUNK, 8, CHUNK), F32), pltpu.VMEM((hb, L // CHUNK, 8, CHUNK), F32),
                        pltpu.VMEM((L, w), F32), pltpu.VMEM((L, w), F32),
                        pltpu.VMEM((hb, L, 2 * GDN_DV), F32), pltpu.VMEM((hb, L, 2 * GDN_DV), F32),
                        pltpu.VMEM((hb, L, CHUNK), F32), pltpu.VMEM((hb, L, CHUNK), F32)],
        compiler_params=_cparams(("parallel", "arbitrary")),
        name="gdn_scan",
    )(*args)


def _gdn_side_inputs(small, gdn_a_log, gdn_dt_bias):
    t = small.shape[0]
    nh = GDN_HEADS
    ab = small[:, :4 * nh].reshape(t, 2, 2, nh)
    sc = ab.transpose(3, 0, 1, 2).reshape(nh, t, 4)
    rows = ab.transpose(3, 1, 2, 0).reshape(nh, 4, t // CHUNK, CHUNK).transpose(0, 2, 1, 3)
    sr = jnp.concatenate([rows, jnp.zeros_like(rows)], axis=2)
    pc = jnp.zeros((nh, 8, LANE), F32)
    pc = pc.at[:, 0, 0:2].set(gdn_a_log.T).at[:, 1, 0:2].set(gdn_dt_bias.T)
    pr = jnp.zeros((nh, 2, 8, CHUNK), F32)
    pr = pr.at[:, 0, 0:2, :].set(jnp.broadcast_to(gdn_a_log.T[:, :, None], (nh, 2, CHUNK)))
    pr = pr.at[:, 1, 0:2, :].set(jnp.broadcast_to(gdn_dt_bias.T[:, :, None], (nh, 2, CHUNK)))
    return sc, sr, pc, pr


GROUP_W = SSD_HPG * SSD_P


def _ssd_kernel(*refs, zero_init):
    if zero_init:
        (x_ref, b_ref, c_ref, z_ref, dc_ref, dr_ref, pc_ref, pr_ref, dskip_ref, norm_ref,
         o_ref, hfin_ref, dtc_ref, acf_ref, acb_ref, dtr_ref, arf_ref, arb_ref, yf_ref, yb_ref,
         hf_ref, hb_ref) = refs
        h0_ref = None
    else:
        (x_ref, b_ref, c_ref, z_ref, dc_ref, dr_ref, pc_ref, pr_ref, dskip_ref, norm_ref, h0_ref,
         o_ref, hfin_ref, dtc_ref, acf_ref, acb_ref, dtr_ref, arf_ref, arb_ref, yf_ref, yb_ref,
         hf_ref, hb_ref) = refs
    L = x_ref.shape[0]
    n_chunks = L // CHUNK
    nh = SSD_HPG

    dtc_ref[...] = jnp.zeros_like(dtc_ref)
    dtc_ref[:, 0:2 * nh] = _softplus(dc_ref[0] + pc_ref[0, 1:2, 0:2 * nh])
    acf_ref[...] = jnp.zeros_like(acf_ref)
    acf_ref[:, 0:2 * nh] = -jnp.exp(pc_ref[0, 0:1, 0:2 * nh]) * dtc_ref[:, 0:2 * nh]
    blk = min(L, 256)
    lo_blk = _chunk_tri(blk, upper=False)
    up_blk = _chunk_tri(blk, upper=True)
    for b in range(L // blk):
        rows = slice(b * blk, (b + 1) * blk)
        da = acf_ref[rows, :]
        acb_ref[rows, :] = _dot3_left(up_blk, da)
        acf_ref[rows, :] = _dot3_left(lo_blk, da)
    dt_row = _softplus(dr_ref[0] + pr_ref[0, 1])
    dtr_ref[...] = dt_row
    da_row = (-jnp.exp(pr_ref[0, 0]) * dt_row).reshape(n_chunks * 2 * nh, CHUNK)
    arf_ref[...] = _dot3_right(da_row, _chunk_tri(CHUNK, upper=True)).reshape(n_chunks, 2 * nh, CHUNK)
    arb_ref[...] = _dot3_right(da_row, _chunk_tri(CHUNK, upper=False)).reshape(n_chunks, 2 * nh, CHUNK)

    if zero_init:
        hf_ref[...] = jnp.zeros_like(hf_ref)
        hb_ref[...] = jnp.zeros_like(hb_ref)
    else:
        hf_ref[...] = h0_ref[0, 0, 0]
        hb_ref[...] = h0_ref[0, 1, 0]

    ti = lax.broadcasted_iota(jnp.int32, (CHUNK, CHUNK), 0)
    si = lax.broadcasted_iota(jnp.int32, (CHUNK, CHUNK), 1)
    ej = lax.broadcasted_iota(jnp.int32, (LANE, GROUP_W), 0)
    ec = lax.broadcasted_iota(jnp.int32, (LANE, GROUP_W), 1)

    def step(d, c):
        rows = pl.ds(pl.multiple_of(c * CHUNK, CHUNK), CHUNK)
        h_ref = hb_ref if d else hf_ref
        x = x_ref[rows, :]
        bm = b_ref[rows, :]
        cm = c_ref[rows, :]
        acs = (acb_ref if d else acf_ref)[rows, :]
        acs_r = (arb_ref if d else arf_ref)[c][d * nh:(d + 1) * nh, :]
        dt_r = dtr_ref[c][d * nh:(d + 1) * nh, :]
        incl = (si >= ti) if d else (si <= ti)
        cb = _dot_nt(cm, bm)
        spread = jnp.where(ej == d * nh + (ec >> 6), 1.0, 0.0).astype(BF16)
        last = acs[0:1] if d else acs[CHUNK - 1:CHUNK]
        p_full = _dot3_right(dtc_ref[rows, :] * jnp.exp(last - acs), spread)
        e_full = _dot3_right(jnp.exp(acs), spread)
        cd_full = _dot3_right(jnp.broadcast_to(jnp.exp(last), (8, LANE)), spread)[0:1]
        h_in = h_ref[...]
        y_off = _dot(cm, h_in)
        h_ref[...] = h_in * cd_full + _dot_tn(bm, x.astype(F32) * p_full)
        yield
        ys = []
        for h in range(nh):
            diff = acs[:, d * nh + h:d * nh + h + 1] - acs_r[h:h + 1, :]
            seg = jnp.where(incl, jnp.exp(jnp.where(incl, diff, 0.0)), 0.0)
            ys.append(_dot(cb * seg * dt_r[h:h + 1, :], x[:, h * SSD_P:(h + 1) * SSD_P]))
        yield
        (yb_ref if d else yf_ref)[rows, :] = jnp.concatenate(ys, axis=1) + y_off * e_full

    def body(i, carry):
        _interleave(step(0, 2 * i), step(1, n_chunks - 1 - 2 * i),
                    step(0, 2 * i + 1), step(1, n_chunks - 2 - 2 * i))
        return carry

    lax.fori_loop(0, n_chunks // 2, body, 0)
    hfin_ref[0, 0, 0] = hf_ref[...]
    hfin_ref[0, 1, 0] = hb_ref[...]

    y = yf_ref[...] + yb_ref[...] + dskip_ref[0] * x_ref[...].astype(F32)
    y = y * _silu(z_ref[...].astype(F32))
    o_ref[...] = _rms_f32(y, norm_ref[...]).astype(o_ref.dtype)


def _ssd(cx, rest, dc, dr, pc, pr, dskip, norm, h0, *, n_seq, seq_len, row0):
    L = seq_len
    blk0 = row0 // L
    ng = SSD_GROUPS
    zero_init = h0 is None
    x_off = (2 * GDN_QK_W + BRANCH_W) // GROUP_W
    b_off = (2 * GDN_QK_W + 2 * BRANCH_W) // SSD_N
    c_off = b_off + ng
    z_off = R_B_Z // GROUP_W
    in_specs = [pl.BlockSpec((L, GROUP_W), lambda s, g: (blk0 + s, x_off + g)),
                pl.BlockSpec((L, SSD_N), lambda s, g: (blk0 + s, b_off + g)),
                pl.BlockSpec((L, SSD_N), lambda s, g: (blk0 + s, c_off + g)),
                pl.BlockSpec((L, GROUP_W), lambda s, g: (blk0 + s, z_off + g)),
                pl.BlockSpec((1, L, 2 * SSD_HPG), lambda s, g: (g, blk0 + s, 0)),
                pl.BlockSpec((1, L // CHUNK, 2 * SSD_HPG, CHUNK), lambda s, g: (g, blk0 + s, 0, 0)),
                pl.BlockSpec((1, 8, LANE), lambda s, g: (g, 0, 0)),
                pl.BlockSpec((1, 2, 2 * SSD_HPG, CHUNK), lambda s, g: (g, 0, 0, 0)),
                pl.BlockSpec((1, 1, GROUP_W), lambda s, g: (g, 0, 0)),
                pl.BlockSpec((1, GROUP_W), lambda s, g: (0, g))]
    args = [cx, cx, cx, rest, dc, dr, pc, pr, dskip, norm]
    state_spec = pl.BlockSpec((1, 2, 1, SSD_N, GROUP_W), lambda s, g: (s, 0, g, 0, 0))
    if not zero_init:
        in_specs.append(state_spec)
        args.append(h0)
    return pl.pallas_call(
        functools.partial(_ssd_kernel, zero_init=zero_init),
        grid=(n_seq, ng),
        in_specs=in_specs,
        out_specs=[pl.BlockSpec((L, GROUP_W), lambda s, g: (s, g)), state_spec],
        out_shape=[jax.ShapeDtypeStruct((n_seq * L, BRANCH_W), BF16),
                   jax.ShapeDtypeStruct((n_seq, 2, ng, SSD_N, GROUP_W), F32)],
        scratch_shapes=[pltpu.VMEM((L, LANE), F32), pltpu.VMEM((L, LANE), F32), pltpu.VMEM((L, LANE), F32),
                        pltpu.VMEM((L // CHUNK, 2 * SSD_HPG, CHUNK), F32),
                        pltpu.VMEM((L // CHUNK, 2 * SSD_HPG, CHUNK), F32),
                        pltpu.VMEM((L // CHUNK, 2 * SSD_HPG, CHUNK), F32),
                        pltpu.VMEM((L, GROUP_W), F32), pltpu.VMEM((L, GROUP_W), F32),
                        pltpu.VMEM((SSD_N, GROUP_W), F32), pltpu.VMEM((SSD_N, GROUP_W), F32)],
        compiler_params=_cparams(("parallel", "arbitrary")),
        name="ssd_scan",
    )(*args)


def _ssd_side_inputs(small, ssd_a_log, ssd_dt_bias, ssd_d):
    t = small.shape[0]
    ng, nh = SSD_GROUPS, SSD_HPG
    off = 4 * GDN_HEADS
    dt = small[:, off:off + 2 * SSD_HEADS].reshape(t, 2, ng, nh)
    dc = dt.transpose(2, 0, 1, 3).reshape(ng, t, 2 * nh)
    dr = dt.transpose(2, 1, 3, 0).reshape(ng, 2 * nh, t // CHUNK, CHUNK).transpose(0, 2, 1, 3)
    a = ssd_a_log.reshape(2, ng, nh).transpose(1, 0, 2).reshape(ng, 2 * nh)
    bias = ssd_dt_bias.reshape(2, ng, nh).transpose(1, 0, 2).reshape(ng, 2 * nh)
    pc = jnp.zeros((ng, 8, LANE), F32).at[:, 0, 0:2 * nh].set(a).at[:, 1, 0:2 * nh].set(bias)
    pr = jnp.stack([jnp.broadcast_to(a[:, :, None], (ng, 2 * nh, CHUNK)),
                    jnp.broadcast_to(bias[:, :, None], (ng, 2 * nh, CHUNK))], axis=1)
    dskip = jnp.repeat(ssd_d.reshape(ng, nh), SSD_P, axis=1)[:, None, :]
    return dc, dr, pc, pr, dskip


def _ssd_state_to_kernel(s):
    n = s.shape[0]
    return s.reshape(n, 2, SSD_GROUPS, SSD_HPG, SSD_N, SSD_P).transpose(0, 1, 2, 4, 3, 5).reshape(
        n, 2, SSD_GROUPS, SSD_N, GROUP_W)


def _ssd_state_from_kernel(s):
    n = s.shape[0]
    return s.reshape(n, 2, SSD_GROUPS, SSD_N, SSD_HPG, SSD_P).transpose(0, 1, 2, 4, 3, 5).reshape(
        n, 2, SSD_HEADS, SSD_N, SSD_P)


SUB = 4
SUB_SHIFT = 2
N_SUB = CHUNK // SUB


def _hgrn_kernel(*refs, zero_init):
    if zero_init:
        (q_ref, zf_ref, zb_ref, v_ref, cg_ref, lb_ref, norm_ref,
         o_ref, sfin_ref, qs_ref, kf_ref, kb_ref, gf_ref, gb_ref, of_ref, ob_ref) = refs
        s0_ref = None
    else:
        (q_ref, zf_ref, zb_ref, v_ref, cg_ref, lb_ref, norm_ref, s0_ref,
         o_ref, sfin_ref, qs_ref, kf_ref, kb_ref, gf_ref, gb_ref, of_ref, ob_ref) = refs
    L = q_ref.shape[0]
    n_chunks = L // CHUNK

    qs_ref[...] = _silu(q_ref[...].astype(F32))
    blk = min(L, 256)
    lo_blk = _chunk_tri(blk, upper=False)
    up_blk = _chunk_tri(blk, upper=True)
    for d, (z_ref, k_ref, g_ref, tri) in enumerate(((zf_ref, kf_ref, gf_ref, lo_blk),
                                                    (zb_ref, kb_ref, gb_ref, up_blk))):
        lbd = lb_ref[d:d + 1, :]
        z = z_ref[...].astype(F32)
        sig = jax.nn.sigmoid(z)
        k_ref[...] = (1.0 - lbd) * (1.0 - sig)
        log_f = jnp.log(lbd + (1.0 - lbd) * sig)
        for b in range(L // blk):
            rows = slice(b * blk, (b + 1) * blk)
            g_ref[rows, :] = _dot3_left(tri, log_f[rows])

    ti = lax.broadcasted_iota(jnp.int32, (CHUNK, CHUNK), 0)
    si = lax.broadcasted_iota(jnp.int32, (CHUNK, CHUNK), 1)
    tcol = lax.broadcasted_iota(jnp.int32, (CHUNK, 1), 0)

    def padded(part, lo):
        hi = lo + part.shape[0]
        pieces = ([jnp.zeros((lo, HGRN_DK), F32)] if lo else []) + [part] + (
            [jnp.zeros((CHUNK - hi, HGRN_DK), F32)] if hi < CHUNK else [])
        return jnp.concatenate(pieces, axis=0).astype(BF16)

    def step(d, c, st_in, st_out):
        st = st_in()
        rows = pl.ds(pl.multiple_of(c * CHUNK, CHUNK), CHUNK)
        q = qs_ref[rows, :]
        k = (kb_ref if d else kf_ref)[rows, :]
        g = (gb_ref if d else gf_ref)[rows, :]
        v = v_ref[rows, :].astype(F32)
        if d:
            sel = jnp.where(si == ((ti >> SUB_SHIFT) << SUB_SHIFT) + SUB, 1.0, 0.0).astype(BF16)
        else:
            sel = jnp.where(si == ((ti >> SUB_SHIFT) << SUB_SHIFT) - 1, 1.0, 0.0).astype(BF16)
        g_ref_rows = _dot3_left(sel, g)
        g_last = g[0:1] if d else g[CHUNK - 1:CHUNK]
        st_out(st * jnp.exp(g_last) + _dot_tn(v, k * jnp.exp(g_last - g)))
        yield
        q_in = q * jnp.exp(g - g_ref_rows)
        k_parts, q_parts = [], []
        for j in range(1, N_SUB):
            if d:
                edge = CHUNK - SUB * j
                lo, hi = (edge // 8) * 8, CHUNK
                use = tcol[lo:hi] >= edge
                mine_blk = N_SUB - 1 - j
            else:
                edge = SUB * j
                lo, hi = 0, -(-edge // 8) * 8
                use = tcol[lo:hi] < edge
                mine_blk = j
            ref_row = g[edge:edge + 1, :] if d else g[edge - 1:edge, :]
            part = jnp.where(use, k[lo:hi] * jnp.exp(jnp.where(use, ref_row - g[lo:hi], 0.0)), 0.0)
            k_parts.append(padded(part, lo))
            qlo = (mine_blk * SUB // 8) * 8
            qpart = jnp.where((tcol[qlo:qlo + 8] >> SUB_SHIFT) == mine_blk, q_in[qlo:qlo + 8], 0.0)
            q_parts.append(padded(qpart, qlo))
        yield
        att = _dot_nt(jnp.concatenate(q_parts, axis=1), jnp.concatenate(k_parts, axis=1))
        o_state = _dot_nt(q * jnp.exp(g), st)
        yield
        o = _dot(att, v) + o_state
        pos = tcol & (SUB - 1)
        for lag in range(SUB):
            if lag == 0:
                a = jnp.sum(q * k, axis=-1, keepdims=True)
                o = o + a * v
                continue
            shift = (CHUNK - lag) if d else lag
            valid = (pos + lag <= SUB - 1) if d else (pos >= lag)
            k_l = pltpu.roll(k, shift, 0)
            g_l = pltpu.roll(g, shift, 0)
            v_l = pltpu.roll(v, shift, 0)
            dec = jnp.exp(jnp.where(valid, g - g_l, 0.0))
            a = jnp.where(valid, jnp.sum(q * k_l * dec, axis=-1, keepdims=True), 0.0)
            o = o + a * v_l
        yield
        (ob_ref if d else of_ref)[rows, :] = o

    def body(i, carry):
        box = {"f0": carry[0], "b0": carry[1]}
        put = lambda key: (lambda val: box.__setitem__(key, val))
        get = lambda key: (lambda: box[key])
        _interleave(step(0, 2 * i, get("f0"), put("f1")),
                    step(1, n_chunks - 1 - 2 * i, get("b0"), put("b1")),
                    step(0, 2 * i + 1, get("f1"), put("f2")),
                    step(1, n_chunks - 2 - 2 * i, get("b1"), put("b2")))
        return box["f2"], box["b2"]

    if zero_init:
        init = (jnp.zeros((HGRN_DV, HGRN_DK), F32), jnp.zeros((HGRN_DV, HGRN_DK), F32))
    else:
        init = (s0_ref[0, 0, 0], s0_ref[0, 1, 0])
    s_f, s_b = lax.fori_loop(0, n_chunks // 2, body, init)
    sfin_ref[0, 0, 0] = s_f
    sfin_ref[0, 1, 0] = s_b

    o = of_ref[...] + ob_ref[...]
    y = _rms_f32(o, norm_ref[...])
    o_ref[...] = (y * _silu(cg_ref[...].astype(F32))).astype(o_ref.dtype)


def _hgrn(rest, lb, norm, s0, *, n_seq, seq_len, row0):
    L = seq_len
    blk0 = row0 // L
    nh = HGRN_HEADS
    zero_init = s0 is None
    q_off, zf_off, v_off, g_off = R_C_Q // LANE, R_C_F // LANE, R_C_I // LANE, R_C_G // LANE
    zb_off = zf_off + nh
    in_specs = [pl.BlockSpec((L, LANE), lambda s, h: (blk0 + s, q_off + h)),
                pl.BlockSpec((L, LANE), lambda s, h: (blk0 + s, zf_off + h)),
                pl.BlockSpec((L, LANE), lambda s, h: (blk0 + s, zb_off + h)),
                pl.BlockSpec((L, LANE), lambda s, h: (blk0 + s, v_off + h)),
                pl.BlockSpec((L, LANE), lambda s, h: (blk0 + s, g_off + h)),
                pl.BlockSpec((2, HGRN_DK), lambda s, h: (0, h)),
                pl.BlockSpec((1, HGRN_DV), lambda s, h: (0, 0))]
    args = [rest, rest, rest, rest, rest, lb, norm]
    state_spec = pl.BlockSpec((1, 2, 1, HGRN_DV, HGRN_DK), lambda s, h: (s, 0, h, 0, 0))
    if not zero_init:
        in_specs.append(state_spec)
        args.append(s0)
    return pl.pallas_call(
        functools.partial(_hgrn_kernel, zero_init=zero_init),
        grid=(n_seq, nh),
        in_specs=in_specs,
        out_specs=[pl.BlockSpec((L, HGRN_DV), lambda s, h: (s, h)), state_spec],
        out_shape=[jax.ShapeDtypeStruct((n_seq * L, BRANCH_W), BF16),
                   jax.ShapeDtypeStruct((n_seq, 2, nh, HGRN_DV, HGRN_DK), F32)],
        scratch_shapes=[pltpu.VMEM((L, HGRN_DK), F32)] * 5 + [pltpu.VMEM((L, HGRN_DV), F32)] * 2,
        compiler_params=_cparams(("parallel", "arbitrary")),
        name="hgrn_scan",
    )(*args)


def _route_kernel(aff_ref, sel_ref, *, cap):
    bits = pltpu.bitcast(aff_ref[...], jnp.int32)
    n_rows = bits.shape[0]
    row = lax.broadcasted_iota(jnp.int32, bits.shape, 0)
    count = lambda m: jnp.sum(m.astype(jnp.int32), axis=0, keepdims=True)

    def thr_body(i, thr):
        cand = thr | (1 << (30 - i))
        return jnp.where(count(bits >= cand) >= cap, cand, thr)

    thr = lax.fori_loop(0, 31, thr_body, jnp.zeros((1, bits.shape[1]), jnp.int32))
    above = bits > thr
    tie = bits == thr
    need = cap - count(above)
    n_bits = max(1, (n_rows - 1).bit_length())

    def tie_body(i, lim):
        cand = lim | (1 << (n_bits - 1 - i))
        return jnp.where(count(tie & (row < cand)) < need, cand, lim)

    lim = lax.fori_loop(0, n_bits, tie_body, jnp.zeros_like(thr))
    sel = above | (tie & (row <= lim) & (need > 0))
    sel_ref[...] = jnp.where(sel, 1.0, 0.0)


def _route(aff, row0, n_rows):
    cap = CAPACITY_FACTOR * n_rows // N_EXPERTS
    return pl.pallas_call(
        functools.partial(_route_kernel, cap=cap),
        grid=(1,),
        in_specs=[pl.BlockSpec((n_rows, LANE), lambda i: (row0 // n_rows, 0))],
        out_specs=pl.BlockSpec((n_rows, LANE), lambda i: (0, 0)),
        out_shape=jax.ShapeDtypeStruct((n_rows, LANE), F32),
        compiler_params=_cparams(("arbitrary",)),
        name="route_topk",
    )(aff)


FF_TILE = 256


def _ffn_kernel(x_ref, wg_ref, wu_ref, wd_ref, gate_ref, o_ref, xb_ref):
    f = pl.program_id(1)

    @pl.when(f == 0)
    def _():
        words = x_ref[0]
        half = words.shape[1]
        xb_ref[:, :half] = pltpu.bitcast(words & jnp.int32(-65536), F32).astype(BF16)
        xb_ref[:, half:] = pltpu.bitcast(words << 16, F32).astype(BF16)

    x = xb_ref[...]
    g = jnp.dot(x, wg_ref[0].astype(BF16), preferred_element_type=F32)
    u = jnp.dot(x, wu_ref[0].astype(BF16), preferred_element_type=F32)
    y = jnp.dot((_silu(g) * u).astype(BF16), wd_ref[0].astype(BF16), preferred_element_type=F32)

    @pl.when(f == 0)
    def _():
        o_ref[0] = y

    @pl.when(f > 0)
    def _():
        o_ref[0] += y

    @pl.when(f == pl.num_programs(1) - 1)
    def _():
        o_ref[0] = o_ref[0] * gate_ref[0]


def _expert_ffn(xe, w_gu, w_down, layer, gate):
    e, cap, _ = xe.shape
    ff, d = w_down.shape[1:]
    nf = ff // FF_TILE
    e0 = layer * e
    return pl.pallas_call(
        _ffn_kernel,
        grid=(e, nf),
        in_specs=[pl.BlockSpec((1, cap, d // 2), lambda g, f: (g, 0, 0)),
                  pl.BlockSpec((1, d, FF_TILE), lambda g, f: (e0 + g, 0, f)),
                  pl.BlockSpec((1, d, FF_TILE), lambda g, f: (e0 + g, 0, nf + f)),
                  pl.BlockSpec((1, FF_TILE, d), lambda g, f: (e0 + g, f, 0)),
                  pl.BlockSpec((1, cap, 1), lambda g, f: (g, 0, 0))],
        out_specs=pl.BlockSpec((1, cap, d), lambda g, f: (g, 0, 0)),
        out_shape=jax.ShapeDtypeStruct((e, cap, d), F32),
        scratch_shapes=[pltpu.VMEM((cap, d), BF16)],
        compiler_params=_cparams(("parallel", "arbitrary"), BIG_VMEM_LIMIT),
        name="expert_ffn",
    )(xe, w_gu, w_gu, w_down, gate)


def _ec_moe(out, row0, h2w, aff, sel, w_gu, w_down, layer):
    t = h2w.shape[0]
    d = w_down.shape[-1]
    cap = CAPACITY_FACTOR * t // N_EXPERTS
    tok = jnp.arange(t, dtype=jnp.int32)[:, None]
    keys = jnp.where(sel[:, :N_EXPERTS] > 0.5, tok, tok + t).T
    idx = jnp.sort(keys, axis=1)[:, :cap]
    gate = jnp.take_along_axis(aff[:, :N_EXPERTS].T, idx, axis=1)[..., None]
    ye = _expert_ffn(h2w[idx], w_gu, w_down, layer, gate)
    return out.at[idx.reshape(-1) + row0].add(ye.reshape(-1, d))


def _resid_kernel(x_ref, m_ref, g_ref, mod_ref, gn_ref, modn_ref, xo_ref, h_ref):
    x = x_ref[...] + mod_ref[0, 5:6, :] * _rms_f32(m_ref[...], g_ref[...])
    xo_ref[...] = x
    h_ref[...] = (_rms_f32(x, gn_ref[...]) * (1.0 + modn_ref[0, 1:2, :]) + modn_ref[0, 0:1, :]).astype(BF16)


def _moe_residual(x1, moe, gain3, mod, gain_next, mod_next, n_ctx_rows, dec_seq, tm=512):
    t, d = x1.shape
    row_spec = pl.BlockSpec((tm, d), lambda i: (i, 0))
    vec_spec = pl.BlockSpec((1, d), lambda i: (0, 0))
    mod_spec = pl.BlockSpec((1, N_MOD, d), lambda i: (_mod_row(i, tm, n_ctx_rows, dec_seq), 0, 0))
    return pl.pallas_call(
        _resid_kernel,
        grid=(t // tm,),
        in_specs=[row_spec, row_spec, vec_spec, mod_spec, vec_spec, mod_spec],
        out_specs=[row_spec, row_spec],
        out_shape=[jax.ShapeDtypeStruct((t, d), F32), jax.ShapeDtypeStruct((t, d), BF16)],
        compiler_params=_cparams(("parallel",)),
        name="moe_residual",
    )(x1, moe, gain3, mod, gain_next, mod_next)


def kernel(x_prompt, x_sample, c, state_gdn, state_ssd, state_hgrn, c_ctx, w_mod, b_mod, norm_gain,
           w_in, conv_w, conv_b, gdn_a_log, gdn_dt_bias, gdn_norm, ssd_a_log, ssd_dt_bias, ssd_d,
           ssd_norm, hgrn_lb, hgrn_norm, w_branch, w_out, w_router, w_gu, w_down):
    nb_c, seq_c, d = x_prompt.shape
    nb_s, seq_s, _ = x_sample.shape
    n_ctx = nb_c * seq_c
    n_smp = nb_s * seq_s

    lb_w = jax.nn.softmax(hgrn_lb.astype(F32), axis=0)
    lb = jnp.cumsum(lb_w, axis=0) - lb_w[:1]

    cond = jnp.concatenate([c_ctx[None, :], c, jnp.zeros((8 - 1 - nb_s, d), F32)], axis=0)
    mod_all = _mod_all(cond, w_mod, b_mod[:, None, :]).reshape(DEPTH, 8, N_MOD, d)

    x = jnp.concatenate([x_prompt.reshape(n_ctx, d), x_sample.reshape(n_smp, d)], axis=0)
    new_gdn, new_ssd, new_hgrn = [], [], []

    w_conv = w_in[:, :, :CONV_CH].astype(BF16)
    w_rest = jnp.concatenate([w_in[:, :, OFF_M_G:], w_in[:, :, OFF_A_G:OFF_ALPHA], w_in[:, :, OFF_B_Z:OFF_B_DT],
                              w_in[:, :, OFF_C_Q:OFF_M_G]], axis=2).astype(BF16)
    w_small = jnp.concatenate([w_in[:, :, OFF_ALPHA:OFF_B_Z], w_in[:, :, OFF_B_DT:OFF_C_Q],
                               jnp.zeros((DEPTH, d, SMALL_W - 4 * GDN_HEADS - 2 * SSD_HEADS), F32)], axis=2)
    w_gu_all = w_gu.reshape(DEPTH * N_EXPERTS, d, 2 * EXPERT_FF)
    w_down_all = w_down.reshape(DEPTH * N_EXPERTS, EXPERT_FF, d)

    for l in range(DEPTH):
        mod = mod_all[l]
        if l == 0:
            h = _hmod(x, norm_gain[l, 0:1], mod, n_ctx, seq_s, i_shift=0, i_scale=1)
        cx = _mm_conv(h, w_conv, l, conv_w[l], conv_b[l][None, :], n_ctx, seq_c, GRID_W)
        rest = _mm_plain(h, w_rest, l, tn=1024)
        small = _mm_small(h, w_small, l)

        gdn_side = _gdn_side_inputs(small, gdn_a_log[l], gdn_dt_bias[l])
        oa_c, sg = _gdn(cx, rest, *gdn_side, gdn_norm[l][None, :], None, n_seq=nb_c, seq_len=seq_c, row0=0)
        oa_s, _ = _gdn(cx, rest, *gdn_side, gdn_norm[l][None, :], state_gdn[:, l], n_seq=nb_s, seq_len=seq_s,
                       row0=n_ctx)
        ssd_side = _ssd_side_inputs(small, ssd_a_log[l], ssd_dt_bias[l], ssd_d[l])
        ob_c, ss = _ssd(cx, rest, *ssd_side, ssd_norm[l][None, :], None, n_seq=nb_c, seq_len=seq_c, row0=0)
        ob_s, _ = _ssd(cx, rest, *ssd_side, ssd_norm[l][None, :], _ssd_state_to_kernel(state_ssd[:, l]),
                       n_seq=nb_s, seq_len=seq_s, row0=n_ctx)
        ss = _ssd_state_from_kernel(ss)
        oc_c, sh = _hgrn(rest, lb[l], hgrn_norm[l][None, :], None, n_seq=nb_c, seq_len=seq_c, row0=0)
        oc_s, _ = _hgrn(rest, lb[l], hgrn_norm[l][None, :], jnp.swapaxes(state_hgrn[:, l], -1, -2),
                        n_seq=nb_s, seq_len=seq_s, row0=n_ctx)
        sh = jnp.swapaxes(sh, -1, -2)
        new_gdn.append(sg)
        new_ssd.append(ss)
        new_hgrn.append(sh)
        branches = jnp.stack([jnp.concatenate([oa_c, oa_s]), jnp.concatenate([ob_c, ob_s]),
                              jnp.concatenate([oc_c, oc_s])])

        merged = _merge(branches, rest, w_branch[l].astype(BF16))
        w_router_p = jnp.concatenate([w_router[l], jnp.zeros((d, LANE - N_EXPERTS), F32)], axis=1)
        x1, h2, aff = _outproj(merged, w_out[l].astype(BF16), x, norm_gain[l], mod, w_router_p, n_ctx, seq_s)

        moe = jnp.zeros((n_ctx + n_smp, d), F32)
        moe = _ec_moe(moe, 0, h2[:n_ctx], aff[:n_ctx], _route(aff, 0, n_ctx), w_gu_all, w_down_all, l)
        moe = _ec_moe(moe, n_ctx, h2[n_ctx:], aff[n_ctx:], _route(aff, n_ctx, n_smp), w_gu_all, w_down_all, l)
        l_next = min(l + 1, DEPTH - 1)
        x, h = _moe_residual(x1, moe, norm_gain[l, 3:4], mod, norm_gain[l_next, 0:1], mod_all[l_next],
                             n_ctx, seq_s)

    y_prompt = x[:n_ctx].reshape(nb_c, seq_c, d)
    y_sample = x[n_ctx:].reshape(nb_s, seq_s, d)
    return (y_prompt, y_sample, jnp.stack(new_gdn, axis=1), jnp.stack(new_ssd, axis=1),
            jnp.stack(new_hgrn, axis=1))
```

```python
import functools

import jax
import jax.numpy as jnp
from jax import lax
from jax.experimental import pallas as pl
from jax.experimental.pallas import tpu as pltpu

D_MODEL = 2048
DEPTH = 2
GRID_W = 64
N_BRANCH = 3
BRANCH_W = D_MODEL // 2
GDN_DK = 128
GDN_DV = 128
GDN_HEADS = BRANCH_W // GDN_DV
GDN_CHUNK = 64
SSD_P = 64
SSD_HEADS = BRANCH_W // SSD_P
SSD_GROUPS = 2
SSD_HPG = SSD_HEADS // SSD_GROUPS
SSD_N = 128
SSD_CHUNK = 64
HGRN_DK = 128
HGRN_DV = 128
HGRN_HEADS = BRANCH_W // HGRN_DV
HGRN_CHUNK = 16
CONV_K = 3
N_EXPERTS = 16
EXPERT_FF = D_MODEL // 2
CAPACITY_FACTOR = 2
N_MOD = 6
EPS = 1e-6
GDN_QK_W = GDN_HEADS * GDN_DK
HGRN_QK_W = HGRN_HEADS * HGRN_DK
CONV_SIZES = (GDN_QK_W, GDN_QK_W, BRANCH_W, BRANCH_W, SSD_GROUPS * SSD_N, SSD_GROUPS * SSD_N)
CONV_CH = GDN_QK_W + GDN_QK_W + BRANCH_W + BRANCH_W + 2 * SSD_GROUPS * SSD_N
OFF_A_G = CONV_CH
OFF_ALPHA = OFF_A_G + BRANCH_W
OFF_BETA = OFF_ALPHA + 2 * GDN_HEADS
OFF_B_Z = OFF_BETA + 2 * GDN_HEADS
OFF_B_DT = OFF_B_Z + BRANCH_W
OFF_C_Q = OFF_B_DT + 2 * SSD_HEADS
IN_COLS = OFF_C_Q + 3 * HGRN_QK_W + 2 * BRANCH_W + N_BRANCH * D_MODEL
OFF_M_G = IN_COLS - N_BRANCH * D_MODEL
REST_W = IN_COLS - CONV_CH - 4 * GDN_HEADS - 2 * SSD_HEADS
R_M_G, R_A_G, R_B_Z, R_C_Q, R_C_F, R_C_I, R_C_G = 0, 6144, 7168, 8192, 9216, 11264, 12288
SMALL_W = 128

LANE = 128
VMEM_LIMIT = 48 * 1024 * 1024
BIG_VMEM_LIMIT = 56 * 1024 * 1024
ROW_TILE = 1024

BF16 = jnp.bfloat16
F32 = jnp.float32


def _cparams(sem, vmem_limit=VMEM_LIMIT):
    return pltpu.CompilerParams(dimension_semantics=sem, vmem_limit_bytes=vmem_limit)


def _mod_row(i, tm, n_ctx_rows, dec_seq):
    return jnp.maximum((i * tm - n_ctx_rows) // dec_seq + 1, 0)


def _mod_kernel(c_ref, w_ref, b_ref, o_ref):
    c = c_ref[...]
    a = c * jax.nn.sigmoid(c)
    o_ref[0] = jnp.dot(a, w_ref[0], preferred_element_type=F32, precision=lax.Precision.HIGHEST) + b_ref[0]


def _mod_all(cond, w_mod, b_mod, tn=1024):
    nl, d, n = w_mod.shape
    return pl.pallas_call(
        _mod_kernel,
        grid=(nl, n // tn),
        in_specs=[pl.BlockSpec((8, d), lambda l, j: (0, 0)),
                  pl.BlockSpec((1, d, tn), lambda l, j: (l, 0, j)),
                  pl.BlockSpec((1, 1, tn), lambda l, j: (l, 0, j))],
        out_specs=pl.BlockSpec((1, 8, tn), lambda l, j: (l, 0, j)),
        out_shape=jax.ShapeDtypeStruct((nl, 8, n), F32),
        compiler_params=_cparams(("parallel", "arbitrary")),
        name="mod_vectors",
    )(cond, w_mod, b_mod)


def _rms_f32(x, gain):
    return x * lax.rsqrt(jnp.mean(x * x, axis=-1, keepdims=True) + EPS) * gain


def _hmod_kernel(x_ref, g_ref, mod_ref, h_ref, *, i_shift, i_scale):
    x = x_ref[...]
    y = _rms_f32(x, g_ref[...])
    h_ref[...] = (y * (1.0 + mod_ref[0, i_scale:i_scale + 1, :]) + mod_ref[0, i_shift:i_shift + 1, :]).astype(BF16)


def _hmod(x, gain, mod, n_ctx_rows, dec_seq, i_shift, i_scale, tm=512):
    t, d = x.shape
    return pl.pallas_call(
        functools.partial(_hmod_kernel, i_shift=i_shift, i_scale=i_scale),
        grid=(t // tm,),
        in_specs=[pl.BlockSpec((tm, d), lambda i: (i, 0)),
                  pl.BlockSpec((1, d), lambda i: (0, 0)),
                  pl.BlockSpec((1, N_MOD, d), lambda i: (_mod_row(i, tm, n_ctx_rows, dec_seq), 0, 0))],
        out_specs=pl.BlockSpec((tm, d), lambda i: (i, 0)),
        out_shape=jax.ShapeDtypeStruct((t, d), BF16),
        compiler_params=_cparams(("parallel",)),
        name="rms_modulate",
    )(x, gain, mod)


def _mm_plain_kernel(h_ref, w_ref, o_ref):
    o_ref[...] = jnp.dot(h_ref[...], w_ref[0], preferred_element_type=F32).astype(o_ref.dtype)


def _mm_plain(h, w, layer, tn, out_dtype=BF16, tm=ROW_TILE):
    t, k = h.shape
    n = w.shape[-1]
    return pl.pallas_call(
        _mm_plain_kernel,
        grid=(t // tm, n // tn),
        in_specs=[pl.BlockSpec((tm, k), lambda i, j: (i, 0)),
                  pl.BlockSpec((1, k, tn), lambda i, j: (layer, 0, j))],
        out_specs=pl.BlockSpec((tm, tn), lambda i, j: (i, j)),
        out_shape=jax.ShapeDtypeStruct((t, n), out_dtype),
        compiler_params=_cparams(("parallel", "arbitrary")),
        name="proj_rest",
    )(h, w)


def _mm_small_kernel(h_ref, w_ref, o_ref):
    o_ref[...] = _dot3_right_exact(h_ref[...], w_ref[0])


def _mm_small(h, w, layer, tm=256):
    t, k = h.shape
    n = w.shape[-1]
    return pl.pallas_call(
        _mm_small_kernel,
        grid=(t // tm,),
        in_specs=[pl.BlockSpec((tm, k), lambda i: (i, 0)),
                  pl.BlockSpec((1, k, n), lambda i: (layer, 0, 0))],
        out_specs=pl.BlockSpec((tm, n), lambda i: (i, 0)),
        out_shape=jax.ShapeDtypeStruct((t, n), F32),
        compiler_params=_cparams(("parallel",)),
        name="proj_small",
    )(h, w)


def _mm_conv_kernel(h_ref, w_ref, cw_ref, cb_ref, o_ref, *, n_ctx_tiles, ctx_period, grid_period):
    acc = jnp.dot(h_ref[...], w_ref[0], preferred_element_type=F32)
    tm = acc.shape[0]
    period = jnp.where(pl.program_id(0) < n_ctx_tiles, ctx_period, grid_period)
    pos = lax.broadcasted_iota(jnp.int32, (tm, 1), 0) & (period - 1)
    prev = jnp.where(pos == 0, 0.0, pltpu.roll(acc, 1, 0))
    nxt = jnp.where(pos == period - 1, 0.0, pltpu.roll(acc, tm - 1, 0))
    y = prev * cw_ref[0:1, :] + acc * cw_ref[1:2, :] + nxt * cw_ref[2:3, :] + cb_ref[...]
    o_ref[...] = (y * jax.nn.sigmoid(y)).astype(o_ref.dtype)


def _mm_conv(h, w, layer, conv_w, conv_b, n_ctx_rows, ctx_period, grid_period, tn=1152, tm=512):
    t, k = h.shape
    n = w.shape[-1]
    return pl.pallas_call(
        functools.partial(_mm_conv_kernel, n_ctx_tiles=n_ctx_rows // tm, ctx_period=ctx_period,
                          grid_period=grid_period),
        grid=(t // tm, n // tn),
        in_specs=[pl.BlockSpec((tm, k), lambda i, j: (i, 0)),
                  pl.BlockSpec((1, k, tn), lambda i, j: (layer, 0, j)),
                  pl.BlockSpec((CONV_K, tn), lambda i, j: (0, j)),
                  pl.BlockSpec((1, tn), lambda i, j: (0, j))],
        out_specs=pl.BlockSpec((tm, tn), lambda i, j: (i, j)),
        out_shape=jax.ShapeDtypeStruct((t, n), BF16),
        compiler_params=_cparams(("parallel", "arbitrary")),
        name="proj_conv",
    )(h, w, conv_w, conv_b)


def _merge_kernel(o_ref, g_ref, w_ref, out_ref, acc_ref):
    n = pl.program_id(1)
    p = jnp.dot(o_ref[0], w_ref[0], preferred_element_type=F32)
    contrib = jax.nn.sigmoid(g_ref[...].astype(F32)) * p

    @pl.when(n == 0)
    def _():
        acc_ref[...] = contrib

    @pl.when(n > 0)
    def _():
        acc_ref[...] += contrib

    @pl.when(n == N_BRANCH - 1)
    def _():
        out_ref[...] = acc_ref[...].astype(out_ref.dtype)


def _merge(branches, rest, w_branch, tm=ROW_TILE):
    _, t, kb = branches.shape
    d = w_branch.shape[-1]
    g_blk = R_M_G // d
    return pl.pallas_call(
        _merge_kernel,
        grid=(t // tm, N_BRANCH),
        in_specs=[pl.BlockSpec((1, tm, kb), lambda i, n: (n, i, 0)),
                  pl.BlockSpec((tm, d), lambda i, n: (i, g_blk + n)),
                  pl.BlockSpec((1, kb, d), lambda i, n: (n, 0, 0))],
        out_specs=pl.BlockSpec((tm, d), lambda i, n: (i, 0)),
        out_shape=jax.ShapeDtypeStruct((t, d), BF16),
        scratch_shapes=[pltpu.VMEM((tm, d), F32)],
        compiler_params=_cparams(("parallel", "arbitrary")),
        name="branch_merge",
    )(branches, rest, w_branch)


def _outproj_kernel(m_ref, w_ref, x_ref, g_ref, mod_ref, wr_ref, x1_ref, h2_ref, aff_ref):
    out = jnp.dot(m_ref[...], w_ref[...], preferred_element_type=F32)
    x1 = x_ref[...] + mod_ref[0, 2:3, :] * _rms_f32(out, g_ref[1:2, :])
    x1_ref[...] = x1
    h2 = _rms_f32(x1, g_ref[2:3, :]) * (1.0 + mod_ref[0, 4:5, :]) + mod_ref[0, 3:4, :]
    half = h2.shape[1] // 2
    hi = pltpu.bitcast(h2[:, :half].astype(BF16).astype(F32), jnp.int32)
    lo = pltpu.bitcast(h2[:, half:].astype(BF16).astype(F32), jnp.int32)
    h2_ref[...] = hi | lax.shift_right_logical(lo, 16)
    logits = jnp.dot(h2, wr_ref[...], preferred_element_type=F32, precision=lax.Precision.HIGHEST)
    lane = lax.broadcasted_iota(jnp.int32, logits.shape, 1)
    logits = jnp.where(lane < N_EXPERTS, logits, -jnp.inf)
    e = jnp.exp(logits - jnp.max(logits, axis=-1, keepdims=True))
    aff_ref[...] = e / jnp.sum(e, axis=-1, keepdims=True)


def _outproj(merged, w_out, x, gains, mod, w_router_p, n_ctx_rows, dec_seq, tm=512):
    t, d = x.shape
    return pl.pallas_call(
        _outproj_kernel,
        grid=(t // tm,),
        in_specs=[pl.BlockSpec((tm, d), lambda i: (i, 0)),
                  pl.BlockSpec((d, d), lambda i: (0, 0)),
                  pl.BlockSpec((tm, d), lambda i: (i, 0)),
                  pl.BlockSpec((4, d), lambda i: (0, 0)),
                  pl.BlockSpec((1, N_MOD, d), lambda i: (_mod_row(i, tm, n_ctx_rows, dec_seq), 0, 0)),
                  pl.BlockSpec((d, LANE), lambda i: (0, 0))],
        out_specs=[pl.BlockSpec((tm, d), lambda i: (i, 0)),
                   pl.BlockSpec((tm, d // 2), lambda i: (i, 0)),
                   pl.BlockSpec((tm, LANE), lambda i: (i, 0))],
        out_shape=[jax.ShapeDtypeStruct((t, d), F32),
                   jax.ShapeDtypeStruct((t, d // 2), jnp.int32),
                   jax.ShapeDtypeStruct((t, LANE), F32)],
        compiler_params=_cparams(("parallel",), BIG_VMEM_LIMIT),
        name="out_proj",
    )(merged, w_out, x, gains, mod, w_router_p)


CHUNK = 64
CHUNK_SHIFT = 6


def _dot(a, b):
    return jnp.dot(a.astype(BF16), b.astype(BF16), preferred_element_type=F32)


def _dot_nt(a, b):
    return lax.dot_general(a.astype(BF16), b.astype(BF16), (((1,), (1,)), ((), ())),
                           preferred_element_type=F32)


def _dot_tn(a, b):
    return lax.dot_general(a.astype(BF16), b.astype(BF16), (((0,), (0,)), ((), ())),
                           preferred_element_type=F32)


def _interleave(*gens):
    results = [None] * len(gens)
    live = list(range(len(gens)))
    while live:
        for i in list(live):
            try:
                next(gens[i])
            except StopIteration as stop:
                results[i] = stop.value
                live.remove(i)
    return results


def _split3(x):
    x1 = x.astype(BF16)
    r1 = x - x1.astype(F32)
    x2 = r1.astype(BF16)
    x3 = (r1 - x2.astype(F32)).astype(BF16)
    return x1, x2, x3


def _dot3_left(mask_bf16, x):
    x1, x2, x3 = _split3(x)
    f = lambda p: jnp.dot(mask_bf16, p, preferred_element_type=F32)
    return f(x3) + f(x2) + f(x1)


def _dot3_right_exact(a_bf16, x):
    x1, x2, x3 = _split3(x)
    f = lambda p: jnp.dot(a_bf16, p, preferred_element_type=F32)
    return f(x3) + f(x2) + f(x1)


def _dot3_right(x, mask_bf16):
    x1, x2, x3 = _split3(x)
    f = lambda p: jnp.dot(p, mask_bf16, preferred_element_type=F32)
    return f(x3) + f(x2) + f(x1)


def _softplus(x):
    return jnp.maximum(x, 0.0) + jnp.log(1.0 + jnp.exp(-jnp.abs(x)))


def _silu(x):
    return x * jax.nn.sigmoid(x)


def _chunk_tri(n, upper):
    r = lax.broadcasted_iota(jnp.int32, (n, n), 0)
    c = lax.broadcasted_iota(jnp.int32, (n, n), 1)
    same = (r >> CHUNK_SHIFT) == (c >> CHUNK_SHIFT)
    tri = (c >= r) if upper else (c <= r)
    return jnp.where(same & tri, 1.0, 0.0).astype(BF16)


def _unit_tri_inverse(ms):
    n = ms[0].shape[0]
    r = lax.broadcasted_iota(jnp.int32, (n, n), 0)
    c = lax.broadcasted_iota(jnp.int32, (n, n), 1)
    eye = jnp.where(r == c, 1.0, 0.0)
    same = lambda s: (r >> s) == (c >> s)
    m8 = [jnp.where(same(3), m, 0.0) for m in ms]
    m2 = [_dot(a, a) for a in m8]
    p = [eye - a for a in m8]
    m4 = [_dot(a, a) for a in m2]
    p = [a + _dot(a, b) for a, b in zip(p, m2)]
    xs = [a + _dot(a, b) for a, b in zip(p, m4)]
    s = 3
    while (1 << s) < n:
        band = same(s + 1) & jnp.logical_not(same(s))
        t = [_dot(x, jnp.where(band, m, 0.0)) for x, m in zip(xs, ms)]
        xs = [x - _dot(a, x) for x, a in zip(xs, t)]
        s += 1
    return xs


def _gdn_kernel(*refs, zero_init):
    if zero_init:
        (q_ref, k_ref, v_ref, ag_ref, sc_ref, sr_ref, pc_ref, pr_ref, norm_ref,
         o_ref, sfin_ref, qn_ref, kn_ref, gcf_ref, gcb_ref, grf_ref, grb_ref, of_ref, ob_ref,
         uwf_ref, uwb_ref, qkf_ref, qkb_ref) = refs
        s0_ref = None
    else:
        (q_ref, k_ref, v_ref, ag_ref, sc_ref, sr_ref, pc_ref, pr_ref, norm_ref, s0_ref,
         o_ref, sfin_ref, qn_ref, kn_ref, gcf_ref, gcb_ref, grf_ref, grb_ref, of_ref, ob_ref,
         uwf_ref, uwb_ref, qkf_ref, qkb_ref) = refs
    L = q_ref.shape[0]
    n_chunks = L // CHUNK
    heads = range(q_ref.shape[1] // GDN_DK)
    hcols = lambda hb: slice(hb * GDN_DK, (hb + 1) * GDN_DK)

    blk = min(L, 256)
    lo_blk = _chunk_tri(blk, upper=False)
    up_blk = _chunk_tri(blk, upper=True)
    up_c = _chunk_tri(CHUNK, upper=True)
    lo_c = _chunk_tri(CHUNK, upper=False)
    for hb in heads:
        qf = q_ref[:, hcols(hb)].astype(F32)
        qn_ref[:, hcols(hb)] = qf * lax.rsqrt(jnp.sum(qf * qf, axis=-1, keepdims=True) + EPS) * (GDN_DK ** -0.5)
        kf = k_ref[:, hcols(hb)].astype(F32)
        kn_ref[:, hcols(hb)] = kf * lax.rsqrt(jnp.sum(kf * kf, axis=-1, keepdims=True) + EPS)
        gcf_ref[hb] = jnp.zeros(gcf_ref.shape[1:], F32)
        gcf_ref[hb, :, 0:4] = -jnp.exp(pc_ref[hb, 0:1, 0:4]) * _softplus(sc_ref[hb] + pc_ref[hb, 1:2, 0:4])
        for b in range(L // blk):
            rows = slice(b * blk, (b + 1) * blk)
            g_col = gcf_ref[hb, rows, :]
            gcb_ref[hb, rows, :] = _dot3_left(up_blk, g_col)
            gcf_ref[hb, rows, :] = _dot3_left(lo_blk, g_col)
        g_row = -jnp.exp(pr_ref[hb, 0]) * _softplus(sr_ref[hb] + pr_ref[hb, 1])
        g_row = g_row.reshape(n_chunks * 8, CHUNK)
        grf_ref[hb] = _dot3_right(g_row, up_c).reshape(n_chunks, 8, CHUNK)
        grb_ref[hb] = _dot3_right(g_row, lo_c).reshape(n_chunks, 8, CHUNK)

    ti = lax.broadcasted_iota(jnp.int32, (CHUNK, CHUNK), 0)
    si = lax.broadcasted_iota(jnp.int32, (CHUNK, CHUNK), 1)

    def chunk_rows(c):
        return pl.ds(c * CHUNK if isinstance(c, int) else pl.multiple_of(c * CHUNK, CHUNK), CHUNK)

    group = min(n_chunks, 4)

    def prep_group(gi):
        items = [(hb, d, gi * group + j) for j in range(group) for hb in heads for d in range(2)]
        ms, rhs = [], []
        for hb, d, c in items:
            rows = chunk_rows(c)
            q = qn_ref[rows, hcols(hb)]
            k = kn_ref[rows, hcols(hb)]
            v = v_ref[rows, hcols(hb)].astype(F32)
            gc = (gcb_ref if d else gcf_ref)[hb, rows, d:d + 1]
            beta = jax.nn.sigmoid(sc_ref[hb, rows, 2 + d:3 + d])
            gcr = (grb_ref if d else grf_ref)[hb, c][d:d + 1, :]
            incl = (si >= ti) if d else (si <= ti)
            strict = (si > ti) if d else (si < ti)
            decay = jnp.where(incl, jnp.exp(jnp.where(incl, gc - gcr, 0.0)), 0.0)
            kb = k * beta
            ms.append(jnp.where(strict, _dot_nt(kb, k) * decay, 0.0))
            (qkb_ref if d else qkf_ref)[hb, rows, :] = _dot_nt(q, k) * decay
            rhs.append(jnp.concatenate([v * beta, kb * jnp.exp(gc)], axis=1).astype(BF16))
        xs = _unit_tri_inverse(ms)
        for (hb, d, c), x, r in zip(items, xs, rhs):
            (uwb_ref if d else uwf_ref)[hb, chunk_rows(c), :] = _dot(x, r)

    if n_chunks == group:
        prep_group(0)
    else:
        def prep_body(gi, carry):
            prep_group(gi)
            return carry
        lax.fori_loop(0, n_chunks // group, prep_body, 0)

    chains = [(hb, d) for hb in heads for d in range(2)]

    def body(i, carry):
        rows = [chunk_rows(n_chunks - 1 - i if d else i) for _, d in chains]
        gcs = [(gcb_ref if d else gcf_ref)[hb, r, d:d + 1] for (hb, d), r in zip(chains, rows)]
        s_b = [s.astype(BF16) for s in carry]
        uw = [(uwb_ref if d else uwf_ref)[hb, r, :] for (hb, d), r in zip(chains, rows)]
        ws = [_dot(a[:, GDN_DV:], s) for a, s in zip(uw, s_b)]
        v_new = [a[:, :GDN_DV] - w for a, w in zip(uw, ws)]
        g_last = [g[0:1] if d else g[CHUNK - 1:CHUNK] for (_, d), g in zip(chains, gcs)]
        upd = [_dot_tn(kn_ref[r, hcols(hb)] * jnp.exp(gl - g), vn)
               for (hb, _), r, gl, g, vn in zip(chains, rows, g_last, gcs, v_new)]
        new = tuple(s * jnp.exp(gl) + u for s, gl, u in zip(carry, g_last, upd))
        for (hb, d), r, g, s, vn in zip(chains, rows, gcs, s_b, v_new):
            o = _dot(qn_ref[r, hcols(hb)] * jnp.exp(g), s) + _dot((qkb_ref if d else qkf_ref)[hb, r, :], vn)
            (ob_ref if d else of_ref)[r, hcols(hb)] = o
        return new

    if zero_init:
        init = tuple(jnp.zeros((GDN_DK, GDN_DV), F32) for _ in chains)
    else:
        init = tuple(s0_ref[0, d, hb] for hb, d in chains)
    final = lax.fori_loop(0, n_chunks, body, init)
    for (hb, d), s in zip(chains, final):
        sfin_ref[0, d, hb] = s

    for hb in heads:
        o = of_ref[:, hcols(hb)] + ob_ref[:, hcols(hb)]
        y = _rms_f32(o, norm_ref[...])
        o_ref[:, hcols(hb)] = (y * _silu(ag_ref[:, hcols(hb)].astype(F32))).astype(o_ref.dtype)


GDN_HB = 4


def _gdn(cx, rest, sc, sr, pc, pr, norm, s0, *, n_seq, seq_len, row0):
    L = seq_len
    blk0 = row0 // L
    nh = GDN_HEADS
    hb = GDN_HB
    w = hb * GDN_DK
    zero_init = s0 is None
    k_off = GDN_QK_W // w
    v_off = 2 * GDN_QK_W // w
    ag_off = R_A_G // w
    in_specs = [pl.BlockSpec((L, w), lambda s, h: (blk0 + s, h)),
                pl.BlockSpec((L, w), lambda s, h: (blk0 + s, k_off + h)),
                pl.BlockSpec((L, w), lambda s, h: (blk0 + s, v_off + h)),
                pl.BlockSpec((L, w), lambda s, h: (blk0 + s, ag_off + h)),
                pl.BlockSpec((hb, L, 4), lambda s, h: (h, blk0 + s, 0)),
                pl.BlockSpec((hb, L // CHUNK, 8, CHUNK), lambda s, h: (h, blk0 + s, 0, 0)),
                pl.BlockSpec((hb, 8, LANE), lambda s, h: (h, 0, 0)),
                pl.BlockSpec((hb, 2, 8, CHUNK), lambda s, h: (h, 0, 0, 0)),
                pl.BlockSpec((1, GDN_DV), lambda s, h: (0, 0))]
    args = [cx, cx, cx, rest, sc, sr, pc, pr, norm]
    state_spec = pl.BlockSpec((1, 2, hb, GDN_DK, GDN_DV), lambda s, h: (s, 0, h, 0, 0))
    if not zero_init:
        in_specs.append(state_spec)
        args.append(s0)
    return pl.pallas_call(
        functools.partial(_gdn_kernel, zero_init=zero_init),
        grid=(n_seq, nh // hb),
        in_specs=in_specs,
        out_specs=[pl.BlockSpec((L, w), lambda s, h: (s, h)), state_spec],
        out_shape=[jax.ShapeDtypeStruct((n_seq * L, BRANCH_W), BF16),
                   jax.ShapeDtypeStruct((n_seq, 2, nh, GDN_DK, GDN_DV), F32)],
        scratch_shapes=[pltpu.VMEM((L, w), F32), pltpu.VMEM((L, w), F32),
                        pltpu.VMEM((hb, L, LANE), F32), pltpu.VMEM((hb, L, LANE), F32),
                        pltpu.VMEM((hb, L // CHUNK, 8, CHUNK), F32), pltpu.VMEM((hb, L // CHUNK, 8, CHUNK), F32),
                        pltpu.VMEM((L, w), F32), pltpu.VMEM((L, w), F32),
                        pltpu.VMEM((hb, L, 2 * GDN_DV), F32), pltpu.VMEM((hb, L, 2 * GDN_DV), F32),
                        pltpu.VMEM((hb, L, CHUNK), F32), pltpu.VMEM((hb, L, CHUNK), F32)],
        compiler_params=_cparams(("parallel", "arbitrary")),
        name="gdn_scan",
    )(*args)


def _gdn_side_inputs(small, gdn_a_log, gdn_dt_bias):
    t = small.shape[0]
    nh = GDN_HEADS
    ab = small[:, :4 * nh].reshape(t, 2, 2, nh)
    sc = ab.transpose(3, 0, 1, 2).reshape(nh, t, 4)
    rows = ab.transpose(3, 1, 2, 0).reshape(nh, 4, t // CHUNK, CHUNK).transpose(0, 2, 1, 3)
    sr = jnp.concatenate([rows, jnp.zeros_like(rows)], axis=2)
    pc = jnp.zeros((nh, 8, LANE), F32)
    pc = pc.at[:, 0, 0:2].set(gdn_a_log.T).at[:, 1, 0:2].set(gdn_dt_bias.T)
    pr = jnp.zeros((nh, 2, 8, CHUNK), F32)
    pr = pr.at[:, 0, 0:2, :].set(jnp.broadcast_to(gdn_a_log.T[:, :, None], (nh, 2, CHUNK)))
    pr = pr.at[:, 1, 0:2, :].set(jnp.broadcast_to(gdn_dt_bias.T[:, :, None], (nh, 2, CHUNK)))
    return sc, sr, pc, pr


GROUP_W = SSD_HPG * SSD_P


def _ssd_kernel(*refs, zero_init):
    if zero_init:
        (x_ref, b_ref, c_ref, z_ref, dc_ref, dr_ref, pc_ref, pr_ref, dskip_ref, norm_ref,
         o_ref, hfin_ref, dtc_ref, acf_ref, acb_ref, dtr_ref, arf_ref, arb_ref, yf_ref, yb_ref,
         hf_ref, hb_ref) = refs
        h0_ref = None
    else:
        (x_ref, b_ref, c_ref, z_ref, dc_ref, dr_ref, pc_ref, pr_ref, dskip_ref, norm_ref, h0_ref,
         o_ref, hfin_ref, dtc_ref, acf_ref, acb_ref, dtr_ref, arf_ref, arb_ref, yf_ref, yb_ref,
         hf_ref, hb_ref) = refs
    L = x_ref.shape[0]
    n_chunks = L // CHUNK
    nh = SSD_HPG

    dtc_ref[...] = jnp.zeros_like(dtc_ref)
    dtc_ref[:, 0:2 * nh] = _softplus(dc_ref[0] + pc_ref[0, 1:2, 0:2 * nh])
    acf_ref[...] = jnp.zeros_like(acf_ref)
    acf_ref[:, 0:2 * nh] = -jnp.exp(pc_ref[0, 0:1, 0:2 * nh]) * dtc_ref[:, 0:2 * nh]
    blk = min(L, 256)
    lo_blk = _chunk_tri(blk, upper=False)
    up_blk = _chunk_tri(blk, upper=True)
    for b in range(L // blk):
        rows = slice(b * blk, (b + 1) * blk)
        da = acf_ref[rows, :]
        acb_ref[rows, :] = _dot3_left(up_blk, da)
        acf_ref[rows, :] = _dot3_left(lo_blk, da)
    dt_row = _softplus(dr_ref[0] + pr_ref[0, 1])
    dtr_ref[...] = dt_row
    da_row = (-jnp.exp(pr_ref[0, 0]) * dt_row).reshape(n_chunks * 2 * nh, CHUNK)
    arf_ref[...] = _dot3_right(da_row, _chunk_tri(CHUNK, upper=True)).reshape(n_chunks, 2 * nh, CHUNK)
    arb_ref[...] = _dot3_right(da_row, _chunk_tri(CHUNK, upper=False)).reshape(n_chunks, 2 * nh, CHUNK)

    if zero_init:
        hf_ref[...] = jnp.zeros_like(hf_ref)
        hb_ref[...] = jnp.zeros_like(hb_ref)
    else:
        hf_ref[...] = h0_ref[0, 0, 0]
        hb_ref[...] = h0_ref[0, 1, 0]

    ti = lax.broadcasted_iota(jnp.int32, (CHUNK, CHUNK), 0)
    si = lax.broadcasted_iota(jnp.int32, (CHUNK, CHUNK), 1)
    ej = lax.broadcasted_iota(jnp.int32, (LANE, GROUP_W), 0)
    ec = lax.broadcasted_iota(jnp.int32, (LANE, GROUP_W), 1)

    def step(d, c):
        rows = pl.ds(pl.multiple_of(c * CHUNK, CHUNK), CHUNK)
        h_ref = hb_ref if d else hf_ref
        x = x_ref[rows, :]
        bm = b_ref[rows, :]
        cm = c_ref[rows, :]
        acs = (acb_ref if d else acf_ref)[rows, :]
        acs_r = (arb_ref if d else arf_ref)[c][d * nh:(d + 1) * nh, :]
        dt_r = dtr_ref[c][d * nh:(d + 1) * nh, :]
        incl = (si >= ti) if d else (si <= ti)
        cb = _dot_nt(cm, bm)
        spread = jnp.where(ej == d * nh + (ec >> 6), 1.0, 0.0).astype(BF16)
        last = acs[0:1] if d else acs[CHUNK - 1:CHUNK]
        p_full = _dot3_right(dtc_ref[rows, :] * jnp.exp(last - acs), spread)
        e_full = _dot3_right(jnp.exp(acs), spread)
        cd_full = _dot3_right(jnp.broadcast_to(jnp.exp(last), (8, LANE)), spread)[0:1]
        h_in = h_ref[...]
        y_off = _dot(cm, h_in)
        h_ref[...] = h_in * cd_full + _dot_tn(bm, x.astype(F32) * p_full)
        yield
        ys = []
        for h in range(nh):
            diff = acs[:, d * nh + h:d * nh + h + 1] - acs_r[h:h + 1, :]
            seg = jnp.where(incl, jnp.exp(jnp.where(incl, diff, 0.0)), 0.0)
            ys.append(_dot(cb * seg * dt_r[h:h + 1, :], x[:, h * SSD_P:(h + 1) * SSD_P]))
        yield
        (yb_ref if d else yf_ref)[rows, :] = jnp.concatenate(ys, axis=1) + y_off * e_full

    def body(i, carry):
        _interleave(step(0, 2 * i), step(1, n_chunks - 1 - 2 * i),
                    step(0, 2 * i + 1), step(1, n_chunks - 2 - 2 * i))
        return carry

    lax.fori_loop(0, n_chunks // 2, body, 0)
    hfin_ref[0, 0, 0] = hf_ref[...]
    hfin_ref[0, 1, 0] = hb_ref[...]

    y = yf_ref[...] + yb_ref[...] + dskip_ref[0] * x_ref[...].astype(F32)
    y = y * _silu(z_ref[...].astype(F32))
    o_ref[...] = _rms_f32(y, norm_ref[...]).astype(o_ref.dtype)


def _ssd(cx, rest, dc, dr, pc, pr, dskip, norm, h0, *, n_seq, seq_len, row0):
    L = seq_len
    blk0 = row0 // L
    ng = SSD_GROUPS
    zero_init = h0 is None
    x_off = (2 * GDN_QK_W + BRANCH_W) // GROUP_W
    b_off = (2 * GDN_QK_W + 2 * BRANCH_W) // SSD_N
    c_off = b_off + ng
    z_off = R_B_Z // GROUP_W
    in_specs = [pl.BlockSpec((L, GROUP_W), lambda s, g: (blk0 + s, x_off + g)),
                pl.BlockSpec((L, SSD_N), lambda s, g: (blk0 + s, b_off + g)),
                pl.BlockSpec((L, SSD_N), lambda s, g: (blk0 + s, c_off + g)),
                pl.BlockSpec((L, GROUP_W), lambda s, g: (blk0 + s, z_off + g)),
                pl.BlockSpec((1, L, 2 * SSD_HPG), lambda s, g: (g, blk0 + s, 0)),
                pl.BlockSpec((1, L // CHUNK, 2 * SSD_HPG, CHUNK), lambda s, g: (g, blk0 + s, 0, 0)),
                pl.BlockSpec((1, 8, LANE), lambda s, g: (g, 0, 0)),
                pl.BlockSpec((1, 2, 2 * SSD_HPG, CHUNK), lambda s, g: (g, 0, 0, 0)),
                pl.BlockSpec((1, 1, GROUP_W), lambda s, g: (g, 0, 0)),
                pl.BlockSpec((1, GROUP_W), lambda s, g: (0, g))]
    args = [cx, cx, cx, rest, dc, dr, pc, pr, dskip, norm]
    state_spec = pl.BlockSpec((1, 2, 1, SSD_N, GROUP_W), lambda s, g: (s, 0, g, 0, 0))
    if not zero_init:
        in_specs.append(state_spec)
        args.append(h0)
    return pl.pallas_call(
        functools.partial(_ssd_kernel, zero_init=zero_init),
        grid=(n_seq, ng),
        in_specs=in_specs,
        out_specs=[pl.BlockSpec((L, GROUP_W), lambda s, g: (s, g)), state_spec],
        out_shape=[jax.ShapeDtypeStruct((n_seq * L, BRANCH_W), BF16),
                   jax.ShapeDtypeStruct((n_seq, 2, ng, SSD_N, GROUP_W), F32)],
        scratch_shapes=[pltpu.VMEM((L, LANE), F32), pltpu.VMEM((L, LANE), F32), pltpu.VMEM((L, LANE), F32),
                        pltpu.VMEM((L // CHUNK, 2 * SSD_HPG, CHUNK), F32),
                        pltpu.VMEM((L // CHUNK, 2 * SSD_HPG, CHUNK), F32),
                        pltpu.VMEM((L // CHUNK, 2 * SSD_HPG, CHUNK), F32),
                        pltpu.VMEM((L, GROUP_W), F32), pltpu.VMEM((L, GROUP_W), F32),
                        pltpu.VMEM((SSD_N, GROUP_W), F32), pltpu.VMEM((SSD_N, GROUP_W), F32)],
        compiler_params=_cparams(("parallel", "arbitrary")),
        name="ssd_scan",
    )(*args)


def _ssd_side_inputs(small, ssd_a_log, ssd_dt_bias, ssd_d):
    t = small.shape[0]
    ng, nh = SSD_GROUPS, SSD_HPG
    off = 4 * GDN_HEADS
    dt = small[:, off:off + 2 * SSD_HEADS].reshape(t, 2, ng, nh)
    dc = dt.transpose(2, 0, 1, 3).reshape(ng, t, 2 * nh)
    dr = dt.transpose(2, 1, 3, 0).reshape(ng, 2 * nh, t // CHUNK, CHUNK).transpose(0, 2, 1, 3)
    a = ssd_a_log.reshape(2, ng, nh).transpose(1, 0, 2).reshape(ng, 2 * nh)
    bias = ssd_dt_bias.reshape(2, ng, nh).transpose(1, 0, 2).reshape(ng, 2 * nh)
    pc = jnp.zeros((ng, 8, LANE), F32).at[:, 0, 0:2 * nh].set(a).at[:, 1, 0:2 * nh].set(bias)
    pr = jnp.stack([jnp.broadcast_to(a[:, :, None], (ng, 2 * nh, CHUNK)),
                    jnp.broadcast_to(bias[:, :, None], (ng, 2 * nh, CHUNK))], axis=1)
    dskip = jnp.repeat(ssd_d.reshape(ng, nh), SSD_P, axis=1)[:, None, :]
    return dc, dr, pc, pr, dskip


def _ssd_state_to_kernel(s):
    n = s.shape[0]
    return s.reshape(n, 2, SSD_GROUPS, SSD_HPG, SSD_N, SSD_P).transpose(0, 1, 2, 4, 3, 5).reshape(
        n, 2, SSD_GROUPS, SSD_N, GROUP_W)


def _ssd_state_from_kernel(s):
    n = s.shape[0]
    return s.reshape(n, 2, SSD_GROUPS, SSD_N, SSD_HPG, SSD_P).transpose(0, 1, 2, 4, 3, 5).reshape(
        n, 2, SSD_HEADS, SSD_N, SSD_P)


SUB = 4
SUB_SHIFT = 2
N_SUB = CHUNK // SUB


def _hgrn_kernel(*refs, zero_init):
    if zero_init:
        (q_ref, zf_ref, zb_ref, v_ref, cg_ref, lb_ref, norm_ref,
         o_ref, sfin_ref, qs_ref, kf_ref, kb_ref, gf_ref, gb_ref, of_ref, ob_ref) = refs
        s0_ref = None
    else:
        (q_ref, zf_ref, zb_ref, v_ref, cg_ref, lb_ref, norm_ref, s0_ref,
         o_ref, sfin_ref, qs_ref, kf_ref, kb_ref, gf_ref, gb_ref, of_ref, ob_ref) = refs
    L = q_ref.shape[0]
    n_chunks = L // CHUNK

    qs_ref[...] = _silu(q_ref[...].astype(F32))
    blk = min(L, 256)
    lo_blk = _chunk_tri(blk, upper=False)
    up_blk = _chunk_tri(blk, upper=True)
    for d, (z_ref, k_ref, g_ref, tri) in enumerate(((zf_ref, kf_ref, gf_ref, lo_blk),
                                                    (zb_ref, kb_ref, gb_ref, up_blk))):
        lbd = lb_ref[d:d + 1, :]
        z = z_ref[...].astype(F32)
        sig = jax.nn.sigmoid(z)
        k_ref[...] = (1.0 - lbd) * (1.0 - sig)
        log_f = jnp.log(lbd + (1.0 - lbd) * sig)
        for b in range(L // blk):
            rows = slice(b * blk, (b + 1) * blk)
            g_ref[rows, :] = _dot3_left(tri, log_f[rows])

    ti = lax.broadcasted_iota(jnp.int32, (CHUNK, CHUNK), 0)
    si = lax.broadcasted_iota(jnp.int32, (CHUNK, CHUNK), 1)
    tcol = lax.broadcasted_iota(jnp.int32, (CHUNK, 1), 0)

    def padded(part, lo):
        hi = lo + part.shape[0]
        pieces = ([jnp.zeros((lo, HGRN_DK), F32)] if lo else []) + [part] + (
            [jnp.zeros((CHUNK - hi, HGRN_DK), F32)] if hi < CHUNK else [])
        return jnp.concatenate(pieces, axis=0).astype(BF16)

    def step(d, c, st_in, st_out):
        st = st_in()
        rows = pl.ds(pl.multiple_of(c * CHUNK, CHUNK), CHUNK)
        q = qs_ref[rows, :]
        k = (kb_ref if d else kf_ref)[rows, :]
        g = (gb_ref if d else gf_ref)[rows, :]
        v = v_ref[rows, :].astype(F32)
        if d:
            sel = jnp.where(si == ((ti >> SUB_SHIFT) << SUB_SHIFT) + SUB, 1.0, 0.0).astype(BF16)
        else:
            sel = jnp.where(si == ((ti >> SUB_SHIFT) << SUB_SHIFT) - 1, 1.0, 0.0).astype(BF16)
        g_ref_rows = _dot3_left(sel, g)
        g_last = g[0:1] if d else g[CHUNK - 1:CHUNK]
        st_out(st * jnp.exp(g_last) + _dot_tn(v, k * jnp.exp(g_last - g)))
        yield
        q_in = q * jnp.exp(g - g_ref_rows)
        k_parts, q_parts = [], []
        for j in range(1, N_SUB):
            if d:
                edge = CHUNK - SUB * j
                lo, hi = (edge // 8) * 8, CHUNK
                use = tcol[lo:hi] >= edge
                mine_blk = N_SUB - 1 - j
            else:
                edge = SUB * j
                lo, hi = 0, -(-edge // 8) * 8
                use = tcol[lo:hi] < edge
                mine_blk = j
            ref_row = g[edge:edge + 1, :] if d else g[edge - 1:edge, :]
            part = jnp.where(use, k[lo:hi] * jnp.exp(jnp.where(use, ref_row - g[lo:hi], 0.0)), 0.0)
            k_parts.append(padded(part, lo))
            qlo = (mine_blk * SUB // 8) * 8
            qpart = jnp.where((tcol[qlo:qlo + 8] >> SUB_SHIFT) == mine_blk, q_in[qlo:qlo + 8], 0.0)
            q_parts.append(padded(qpart, qlo))
        yield
        att = _dot_nt(jnp.concatenate(q_parts, axis=1), jnp.concatenate(k_parts, axis=1))
        o_state = _dot_nt(q * jnp.exp(g), st)
        yield
        o = _dot(att, v) + o_state
        pos = tcol & (SUB - 1)
        for lag in range(SUB):
            if lag == 0:
                a = jnp.sum(q * k, axis=-1, keepdims=True)
                o = o + a * v
                continue
            shift = (CHUNK - lag) if d else lag
            valid = (pos + lag <= SUB - 1) if d else (pos >= lag)
            k_l = pltpu.roll(k, shift, 0)
            g_l = pltpu.roll(g, shift, 0)
            v_l = pltpu.roll(v, shift, 0)
            dec = jnp.exp(jnp.where(valid, g - g_l, 0.0))
            a = jnp.where(valid, jnp.sum(q * k_l * dec, axis=-1, keepdims=True), 0.0)
            o = o + a * v_l
        yield
        (ob_ref if d else of_ref)[rows, :] = o

    def body(i, carry):
        box = {"f0": carry[0], "b0": carry[1]}
        put = lambda key: (lambda val: box.__setitem__(key, val))
        get = lambda key: (lambda: box[key])
        _interleave(step(0, 2 * i, get("f0"), put("f1")),
                    step(1, n_chunks - 1 - 2 * i, get("b0"), put("b1")),
                    step(0, 2 * i + 1, get("f1"), put("f2")),
                    step(1, n_chunks - 2 - 2 * i, get("b1"), put("b2")))
        return box["f2"], box["b2"]

    if zero_init:
        init = (jnp.zeros((HGRN_DV, HGRN_DK), F32), jnp.zeros((HGRN_DV, HGRN_DK), F32))
    else:
        init = (s0_ref[0, 0, 0], s0_ref[0, 1, 0])
    s_f, s_b = lax.fori_loop(0, n_chunks // 2, body, init)
    sfin_ref[0, 0, 0] = s_f
    sfin_ref[0, 1, 0] = s_b

    o = of_ref[...] + ob_ref[...]
    y = _rms_f32(o, norm_ref[...])
    o_ref[...] = (y * _silu(cg_ref[...].astype(F32))).astype(o_ref.dtype)


def _hgrn(rest, lb, norm, s0, *, n_seq, seq_len, row0):
    L = seq_len
    blk0 = row0 // L
    nh = HGRN_HEADS
    zero_init = s0 is None
    q_off, zf_off, v_off, g_off = R_C_Q // LANE, R_C_F // LANE, R_C_I // LANE, R_C_G // LANE
    zb_off = zf_off + nh
    in_specs = [pl.BlockSpec((L, LANE), lambda s, h: (blk0 + s, q_off + h)),
                pl.BlockSpec((L, LANE), lambda s, h: (blk0 + s, zf_off + h)),
                pl.BlockSpec((L, LANE), lambda s, h: (blk0 + s, zb_off + h)),
                pl.BlockSpec((L, LANE), lambda s, h: (blk0 + s, v_off + h)),
                pl.BlockSpec((L, LANE), lambda s, h: (blk0 + s, g_off + h)),
                pl.BlockSpec((2, HGRN_DK), lambda s, h: (0, h)),
                pl.BlockSpec((1, HGRN_DV), lambda s, h: (0, 0))]
    args = [rest, rest, rest, rest, rest, lb, norm]
    state_spec = pl.BlockSpec((1, 2, 1, HGRN_DV, HGRN_DK), lambda s, h: (s, 0, h, 0, 0))
    if not zero_init:
        in_specs.append(state_spec)
        args.append(s0)
    return pl.pallas_call(
        functools.partial(_hgrn_kernel, zero_init=zero_init),
        grid=(n_seq, nh),
        in_specs=in_specs,
        out_specs=[pl.BlockSpec((L, HGRN_DV), lambda s, h: (s, h)), state_spec],
        out_shape=[jax.ShapeDtypeStruct((n_seq * L, BRANCH_W), BF16),
                   jax.ShapeDtypeStruct((n_seq, 2, nh, HGRN_DV, HGRN_DK), F32)],
        scratch_shapes=[pltpu.VMEM((L, HGRN_DK), F32)] * 5 + [pltpu.VMEM((L, HGRN_DV), F32)] * 2,
        compiler_params=_cparams(("parallel", "arbitrary")),
        name="hgrn_scan",
    )(*args)


def _route_kernel(aff_ref, sel_ref, *, cap):
    bits = pltpu.bitcast(aff_ref[...], jnp.int32)
    n_rows = bits.shape[0]
    row = lax.broadcasted_iota(jnp.int32, bits.shape, 0)
    count = lambda m: jnp.sum(m.astype(jnp.int32), axis=0, keepdims=True)

    def thr_body(i, thr):
        cand = thr | (1 << (30 - i))
        return jnp.where(count(bits >= cand) >= cap, cand, thr)

    thr = lax.fori_loop(0, 31, thr_body, jnp.zeros((1, bits.shape[1]), jnp.int32))
    above = bits > thr
    tie = bits == thr
    need = cap - count(above)
    n_bits = max(1, (n_rows - 1).bit_length())

    def tie_body(i, lim):
        cand = lim | (1 << (n_bits - 1 - i))
        return jnp.where(count(tie & (row < cand)) < need, cand, lim)

    lim = lax.fori_loop(0, n_bits, tie_body, jnp.zeros_like(thr))
    sel = above | (tie & (row <= lim) & (need > 0))
    sel_ref[...] = jnp.where(sel, 1.0, 0.0)


def _route(aff, row0, n_rows):
    cap = CAPACITY_FACTOR * n_rows // N_EXPERTS
    return pl.pallas_call(
        functools.partial(_route_kernel, cap=cap),
        grid=(1,),
        in_specs=[pl.BlockSpec((n_rows, LANE), lambda i: (row0 // n_rows, 0))],
        out_specs=pl.BlockSpec((n_rows, LANE), lambda i: (0, 0)),
        out_shape=jax.ShapeDtypeStruct((n_rows, LANE), F32),
        compiler_params=_cparams(("arbitrary",)),
        name="route_topk",
    )(aff)


FF_TILE = 256


def _ffn_kernel(x_ref, wg_ref, wu_ref, wd_ref, gate_ref, o_ref, xb_ref):
    f = pl.program_id(1)

    @pl.when(f == 0)
    def _():
        words = x_ref[0]
        half = words.shape[1]
        xb_ref[:, :half] = pltpu.bitcast(words & jnp.int32(-65536), F32).astype(BF16)
        xb_ref[:, half:] = pltpu.bitcast(words << 16, F32).astype(BF16)

    x = xb_ref[...]
    g = jnp.dot(x, wg_ref[0].astype(BF16), preferred_element_type=F32)
    u = jnp.dot(x, wu_ref[0].astype(BF16), preferred_element_type=F32)
    y = jnp.dot((_silu(g) * u).astype(BF16), wd_ref[0].astype(BF16), preferred_element_type=F32)

    @pl.when(f == 0)
    def _():
        o_ref[0] = y

    @pl.when(f > 0)
    def _():
        o_ref[0] += y

    @pl.when(f == pl.num_programs(1) - 1)
    def _():
        o_ref[0] = o_ref[0] * gate_ref[0]


def _expert_ffn(xe, w_gu, w_down, layer, gate):
    e, cap, _ = xe.shape
    ff, d = w_down.shape[1:]
    nf = ff // FF_TILE
    e0 = layer * e
    return pl.pallas_call(
        _ffn_kernel,
        grid=(e, nf),
        in_specs=[pl.BlockSpec((1, cap, d // 2), lambda g, f: (g, 0, 0)),
                  pl.BlockSpec((1, d, FF_TILE), lambda g, f: (e0 + g, 0, f)),
                  pl.BlockSpec((1, d, FF_TILE), lambda g, f: (e0 + g, 0, nf + f)),
                  pl.BlockSpec((1, FF_TILE, d), lambda g, f: (e0 + g, f, 0)),
                  pl.BlockSpec((1, cap, 1), lambda g, f: (g, 0, 0))],
        out_specs=pl.BlockSpec((1, cap, d), lambda g, f: (g, 0, 0)),
        out_shape=jax.ShapeDtypeStruct((e, cap, d), F32),
        scratch_shapes=[pltpu.VMEM((cap, d), BF16)],
        compiler_params=_cparams(("parallel", "arbitrary"), BIG_VMEM_LIMIT),
        name="expert_ffn",
    )(xe, w_gu, w_gu, w_down, gate)


def _ec_moe(h2w, aff, sel, w_gu, w_down, layer):
    t = h2w.shape[0]
    d = w_down.shape[-1]
    cap = CAPACITY_FACTOR * t // N_EXPERTS
    tok = jnp.arange(t, dtype=jnp.int32)[:, None]
    keys = jnp.where(sel[:, :N_EXPERTS] > 0.5, tok, tok + t).T
    idx = jnp.sort(keys, axis=1)[:, :cap]
    gate = jnp.take_along_axis(aff[:, :N_EXPERTS].T, idx, axis=1)[..., None]
    ye = _expert_ffn(h2w[idx], w_gu, w_down, layer, gate)
    return jnp.zeros((t, d), F32).at[idx.reshape(-1)].add(ye.reshape(-1, d))


def _resid_kernel(x_ref, mc_ref, ms_ref, g_ref, mod_ref, gn_ref, modn_ref, xo_ref, h_ref, *, n_ctx_tiles):
    moe = jnp.where(pl.program_id(0) < n_ctx_tiles, mc_ref[...], ms_ref[...])
    x = x_ref[...] + mod_ref[0, 5:6, :] * _rms_f32(moe, g_ref[...])
    xo_ref[...] = x
    h_ref[...] = (_rms_f32(x, gn_ref[...]) * (1.0 + modn_ref[0, 1:2, :]) + modn_ref[0, 0:1, :]).astype(BF16)


def _moe_residual(x1, moe_c, moe_s, gain3, mod, gain_next, mod_next, n_ctx_rows, dec_seq, tm=512):
    t, d = x1.shape
    nc = n_ctx_rows // tm
    row_spec = pl.BlockSpec((tm, d), lambda i: (i, 0))
    ctx_spec = pl.BlockSpec((tm, d), lambda i: (jnp.minimum(i, nc - 1), 0))
    smp_spec = pl.BlockSpec((tm, d), lambda i: (jnp.maximum(i - nc, 0), 0))
    vec_spec = pl.BlockSpec((1, d), lambda i: (0, 0))
    mod_spec = pl.BlockSpec((1, N_MOD, d), lambda i: (_mod_row(i, tm, n_ctx_rows, dec_seq), 0, 0))
    return pl.pallas_call(
        functools.partial(_resid_kernel, n_ctx_tiles=nc),
        grid=(t // tm,),
        in_specs=[row_spec, ctx_spec, smp_spec, vec_spec, mod_spec, vec_spec, mod_spec],
        out_specs=[row_spec, row_spec],
        out_shape=[jax.ShapeDtypeStruct((t, d), F32), jax.ShapeDtypeStruct((t, d), BF16)],
        compiler_params=_cparams(("parallel",)),
        name="moe_residual",
    )(x1, moe_c, moe_s, gain3, mod, gain_next, mod_next)


def kernel(x_prompt, x_sample, c, state_gdn, state_ssd, state_hgrn, c_ctx, w_mod, b_mod, norm_gain,
           w_in, conv_w, conv_b, gdn_a_log, gdn_dt_bias, gdn_norm, ssd_a_log, ssd_dt_bias, ssd_d,
           ssd_norm, hgrn_lb, hgrn_norm, w_branch, w_out, w_router, w_gu, w_down):
    nb_c, seq_c, d = x_prompt.shape
    nb_s, seq_s, _ = x_sample.shape
    n_ctx = nb_c * seq_c
    n_smp = nb_s * seq_s

    lb_w = jax.nn.softmax(hgrn_lb.astype(F32), axis=0)
    lb = jnp.cumsum(lb_w, axis=0) - lb_w[:1]

    cond = jnp.concatenate([c_ctx[None, :], c, jnp.zeros((8 - 1 - nb_s, d), F32)], axis=0)
    mod_all = _mod_all(cond, w_mod, b_mod[:, None, :]).reshape(DEPTH, 8, N_MOD, d)

    x = jnp.concatenate([x_prompt.reshape(n_ctx, d), x_sample.reshape(n_smp, d)], axis=0)
    new_gdn, new_ssd, new_hgrn = [], [], []

    w_conv = w_in[:, :, :CONV_CH].astype(BF16)
    w_rest = jnp.concatenate([w_in[:, :, OFF_M_G:], w_in[:, :, OFF_A_G:OFF_ALPHA], w_in[:, :, OFF_B_Z:OFF_B_DT],
                              w_in[:, :, OFF_C_Q:OFF_M_G]], axis=2).astype(BF16)
    w_small = jnp.concatenate([w_in[:, :, OFF_ALPHA:OFF_B_Z], w_in[:, :, OFF_B_DT:OFF_C_Q],
                               jnp.zeros((DEPTH, d, SMALL_W - 4 * GDN_HEADS - 2 * SSD_HEADS), F32)], axis=2)
    w_gu_all = w_gu.reshape(DEPTH * N_EXPERTS, d, 2 * EXPERT_FF)
    w_down_all = w_down.reshape(DEPTH * N_EXPERTS, EXPERT_FF, d)

    for l in range(DEPTH):
        mod = mod_all[l]
        if l == 0:
            h = _hmod(x, norm_gain[l, 0:1], mod, n_ctx, seq_s, i_shift=0, i_scale=1)
        cx = _mm_conv(h, w_conv, l, conv_w[l], conv_b[l][None, :], n_ctx, seq_c, GRID_W)
        rest = _mm_plain(h, w_rest, l, tn=1024)
        small = _mm_small(h, w_small, l)

        gdn_side = _gdn_side_inputs(small, gdn_a_log[l], gdn_dt_bias[l])
        oa_c, sg = _gdn(cx, rest, *gdn_side, gdn_norm[l][None, :], None, n_seq=nb_c, seq_len=seq_c, row0=0)
        oa_s, _ = _gdn(cx, rest, *gdn_side, gdn_norm[l][None, :], state_gdn[:, l], n_seq=nb_s, seq_len=seq_s,
                       row0=n_ctx)
        ssd_side = _ssd_side_inputs(small, ssd_a_log[l], ssd_dt_bias[l], ssd_d[l])
        ob_c, ss = _ssd(cx, rest, *ssd_side, ssd_norm[l][None, :], None, n_seq=nb_c, seq_len=seq_c, row0=0)
        ob_s, _ = _ssd(cx, rest, *ssd_side, ssd_norm[l][None, :], _ssd_state_to_kernel(state_ssd[:, l]),
                       n_seq=nb_s, seq_len=seq_s, row0=n_ctx)
        ss = _ssd_state_from_kernel(ss)
        oc_c, sh = _hgrn(rest, lb[l], hgrn_norm[l][None, :], None, n_seq=nb_c, seq_len=seq_c, row0=0)
        oc_s, _ = _hgrn(rest, lb[l], hgrn_norm[l][None, :], jnp.swapaxes(state_hgrn[:, l], -1, -2),
                        n_seq=nb_s, seq_len=seq_s, row0=n_ctx)
        sh = jnp.swapaxes(sh, -1, -2)
        new_gdn.append(sg)
        new_ssd.append(ss)
        new_hgrn.append(sh)
        branches = jnp.stack([jnp.concatenate([oa_c, oa_s]), jnp.concatenate([ob_c, ob_s]),
                              jnp.concatenate([oc_c, oc_s])])

        merged = _merge(branches, rest, w_branch[l].astype(BF16))
        w_router_p = jnp.concatenate([w_router[l], jnp.zeros((d, LANE - N_EXPERTS), F32)], axis=1)
        x1, h2, aff = _outproj(merged, w_out[l].astype(BF16), x, norm_gain[l], mod, w_router_p, n_ctx, seq_s)

        moe_c = _ec_moe(h2[:n_ctx], aff[:n_ctx], _route(aff, 0, n_ctx), w_gu_all, w_down_all, l)
        moe_s = _ec_moe(h2[n_ctx:], aff[n_ctx:], _route(aff, n_ctx, n_smp), w_gu_all, w_down_all, l)
        l_next = min(l + 1, DEPTH - 1)
        x, h = _moe_residual(x1, moe_c, moe_s, norm_gain[l, 3:4], mod, norm_gain[l_next, 0:1], mod_all[l_next],
                             n_ctx, seq_s)

    y_prompt = x[:n_ctx].reshape(nb_c, seq_c, d)
    y_sample = x[n_ctx:].reshape(nb_s, seq_s, d)
    return (y_prompt, y_sample, jnp.stack(new_gdn, axis=1), jnp.stack(new_ssd, axis=1),
            jnp.stack(new_hgrn, axis=1))
```

```python
import functools

import jax
import jax.numpy as jnp
from jax import lax
from jax.experimental import pallas as pl
from jax.experimental.pallas import tpu as pltpu

D_MODEL = 2048
DEPTH = 2
GRID_W = 64
N_BRANCH = 3
BRANCH_W = D_MODEL // 2
GDN_DK = 128
GDN_DV = 128
GDN_HEADS = BRANCH_W // GDN_DV
GDN_CHUNK = 64
SSD_P = 64
SSD_HEADS = BRANCH_W // SSD_P
SSD_GROUPS = 2
SSD_HPG = SSD_HEADS // SSD_GROUPS
SSD_N = 128
SSD_CHUNK = 64
HGRN_DK = 128
HGRN_DV = 128
HGRN_HEADS = BRANCH_W // HGRN_DV
HGRN_CHUNK = 16
CONV_K = 3
N_EXPERTS = 16
EXPERT_FF = D_MODEL // 2
CAPACITY_FACTOR = 2
N_MOD = 6
EPS = 1e-6
GDN_QK_W = GDN_HEADS * GDN_DK
HGRN_QK_W = HGRN_HEADS * HGRN_DK
CONV_SIZES = (GDN_QK_W, GDN_QK_W, BRANCH_W, BRANCH_W, SSD_GROUPS * SSD_N, SSD_GROUPS * SSD_N)
CONV_CH = GDN_QK_W + GDN_QK_W + BRANCH_W + BRANCH_W + 2 * SSD_GROUPS * SSD_N
OFF_A_G = CONV_CH
OFF_ALPHA = OFF_A_G + BRANCH_W
OFF_BETA = OFF_ALPHA + 2 * GDN_HEADS
OFF_B_Z = OFF_BETA + 2 * GDN_HEADS
OFF_B_DT = OFF_B_Z + BRANCH_W
OFF_C_Q = OFF_B_DT + 2 * SSD_HEADS
IN_COLS = OFF_C_Q + 3 * HGRN_QK_W + 2 * BRANCH_W + N_BRANCH * D_MODEL
OFF_M_G = IN_COLS - N_BRANCH * D_MODEL
REST_W = IN_COLS - CONV_CH - 4 * GDN_HEADS - 2 * SSD_HEADS
R_M_G, R_A_G, R_B_Z, R_C_Q, R_C_F, R_C_I, R_C_G = 0, 6144, 7168, 8192, 9216, 11264, 12288
SMALL_W = 128

LANE = 128
VMEM_LIMIT = 48 * 1024 * 1024
BIG_VMEM_LIMIT = 56 * 1024 * 1024
ROW_TILE = 1024

BF16 = jnp.bfloat16
F32 = jnp.float32


def _cparams(sem, vmem_limit=VMEM_LIMIT):
    return pltpu.CompilerParams(dimension_semantics=sem, vmem_limit_bytes=vmem_limit)


def _mod_row(i, tm, n_ctx_rows, dec_seq):
    return jnp.maximum((i * tm - n_ctx_rows) // dec_seq + 1, 0)


def _mod_kernel(c_ref, w_ref, b_ref, o_ref):
    c = c_ref[...]
    a = c * jax.nn.sigmoid(c)
    o_ref[0] = jnp.dot(a, w_ref[0], preferred_element_type=F32, precision=lax.Precision.HIGHEST) + b_ref[0]


def _mod_all(cond, w_mod, b_mod, tn=1024):
    nl, d, n = w_mod.shape
    return pl.pallas_call(
        _mod_kernel,
        grid=(nl, n // tn),
        in_specs=[pl.BlockSpec((8, d), lambda l, j: (0, 0)),
                  pl.BlockSpec((1, d, tn), lambda l, j: (l, 0, j)),
                  pl.BlockSpec((1, 1, tn), lambda l, j: (l, 0, j))],
        out_specs=pl.BlockSpec((1, 8, tn), lambda l, j: (l, 0, j)),
        out_shape=jax.ShapeDtypeStruct((nl, 8, n), F32),
        compiler_params=_cparams(("parallel", "arbitrary")),
        name="mod_vectors",
    )(cond, w_mod, b_mod)


def _rms_f32(x, gain):
    return x * lax.rsqrt(jnp.mean(x * x, axis=-1, keepdims=True) + EPS) * gain


def _hmod_kernel(x_ref, g_ref, mod_ref, h_ref, *, i_shift, i_scale):
    x = x_ref[...]
    y = _rms_f32(x, g_ref[...])
    h_ref[...] = (y * (1.0 + mod_ref[0, i_scale:i_scale + 1, :]) + mod_ref[0, i_shift:i_shift + 1, :]).astype(BF16)


def _hmod(x, gain, mod, n_ctx_rows, dec_seq, i_shift, i_scale, tm=512):
    t, d = x.shape
    return pl.pallas_call(
        functools.partial(_hmod_kernel, i_shift=i_shift, i_scale=i_scale),
        grid=(t // tm,),
        in_specs=[pl.BlockSpec((tm, d), lambda i: (i, 0)),
                  pl.BlockSpec((1, d), lambda i: (0, 0)),
                  pl.BlockSpec((1, N_MOD, d), lambda i: (_mod_row(i, tm, n_ctx_rows, dec_seq), 0, 0))],
        out_specs=pl.BlockSpec((tm, d), lambda i: (i, 0)),
        out_shape=jax.ShapeDtypeStruct((t, d), BF16),
        compiler_params=_cparams(("parallel",)),
        name="rms_modulate",
    )(x, gain, mod)


def _mm_plain_kernel(h_ref, w_ref, o_ref):
    o_ref[...] = jnp.dot(h_ref[...], w_ref[0], preferred_element_type=F32).astype(o_ref.dtype)


def _mm_plain(h, w, layer, tn, out_dtype=BF16, tm=ROW_TILE):
    t, k = h.shape
    n = w.shape[-1]
    return pl.pallas_call(
        _mm_plain_kernel,
        grid=(t // tm, n // tn),
        in_specs=[pl.BlockSpec((tm, k), lambda i, j: (i, 0)),
                  pl.BlockSpec((1, k, tn), lambda i, j: (layer, 0, j))],
        out_specs=pl.BlockSpec((tm, tn), lambda i, j: (i, j)),
        out_shape=jax.ShapeDtypeStruct((t, n), out_dtype),
        compiler_params=_cparams(("parallel", "arbitrary")),
        name="proj_rest",
    )(h, w)


def _mm_small_kernel(h_ref, w_ref, o_ref):
    o_ref[...] = _dot3_right_exact(h_ref[...], w_ref[0])


def _mm_small(h, w, layer, tm=256):
    t, k = h.shape
    n = w.shape[-1]
    return pl.pallas_call(
        _mm_small_kernel,
        grid=(t // tm,),
        in_specs=[pl.BlockSpec((tm, k), lambda i: (i, 0)),
                  pl.BlockSpec((1, k, n), lambda i: (layer, 0, 0))],
        out_specs=pl.BlockSpec((tm, n), lambda i: (i, 0)),
        out_shape=jax.ShapeDtypeStruct((t, n), F32),
        compiler_params=_cparams(("parallel",)),
        name="proj_small",
    )(h, w)


def _mm_conv_kernel(h_ref, w_ref, cw_ref, cb_ref, o_ref, *, n_ctx_tiles, ctx_period, grid_period):
    acc = jnp.dot(h_ref[...], w_ref[0], preferred_element_type=F32)
    tm = acc.shape[0]
    period = jnp.where(pl.program_id(0) < n_ctx_tiles, ctx_period, grid_period)
    pos = lax.broadcasted_iota(jnp.int32, (tm, 1), 0) & (period - 1)
    prev = jnp.where(pos == 0, 0.0, pltpu.roll(acc, 1, 0))
    nxt = jnp.where(pos == period - 1, 0.0, pltpu.roll(acc, tm - 1, 0))
    y = prev * cw_ref[0:1, :] + acc * cw_ref[1:2, :] + nxt * cw_ref[2:3, :] + cb_ref[...]
    o_ref[...] = (y * jax.nn.sigmoid(y)).astype(o_ref.dtype)


def _mm_conv(h, w, layer, conv_w, conv_b, n_ctx_rows, ctx_period, grid_period, tn=1152, tm=512):
    t, k = h.shape
    n = w.shape[-1]
    return pl.pallas_call(
        functools.partial(_mm_conv_kernel, n_ctx_tiles=n_ctx_rows // tm, ctx_period=ctx_period,
                          grid_period=grid_period),
        grid=(t // tm, n // tn),
        in_specs=[pl.BlockSpec((tm, k), lambda i, j: (i, 0)),
                  pl.BlockSpec((1, k, tn), lambda i, j: (layer, 0, j)),
                  pl.BlockSpec((CONV_K, tn), lambda i, j: (0, j)),
                  pl.BlockSpec((1, tn), lambda i, j: (0, j))],
        out_specs=pl.BlockSpec((tm, tn), lambda i, j: (i, j)),
        out_shape=jax.ShapeDtypeStruct((t, n), BF16),
        compiler_params=_cparams(("parallel", "arbitrary")),
        name="proj_conv",
    )(h, w, conv_w, conv_b)


def _merge_kernel(o_ref, g_ref, w_ref, out_ref, acc_ref):
    n = pl.program_id(1)
    p = jnp.dot(o_ref[0], w_ref[0], preferred_element_type=F32)
    contrib = jax.nn.sigmoid(g_ref[...].astype(F32)) * p

    @pl.when(n == 0)
    def _():
        acc_ref[...] = contrib

    @pl.when(n > 0)
    def _():
        acc_ref[...] += contrib

    @pl.when(n == N_BRANCH - 1)
    def _():
        out_ref[...] = acc_ref[...].astype(out_ref.dtype)


def _merge(branches, rest, w_branch, tm=ROW_TILE):
    _, t, kb = branches.shape
    d = w_branch.shape[-1]
    g_blk = R_M_G // d
    return pl.pallas_call(
        _merge_kernel,
        grid=(t // tm, N_BRANCH),
        in_specs=[pl.BlockSpec((1, tm, kb), lambda i, n: (n, i, 0)),
                  pl.BlockSpec((tm, d), lambda i, n: (i, g_blk + n)),
                  pl.BlockSpec((1, kb, d), lambda i, n: (n, 0, 0))],
        out_specs=pl.BlockSpec((tm, d), lambda i, n: (i, 0)),
        out_shape=jax.ShapeDtypeStruct((t, d), BF16),
        scratch_shapes=[pltpu.VMEM((tm, d), F32)],
        compiler_params=_cparams(("parallel", "arbitrary")),
        name="branch_merge",
    )(branches, rest, w_branch)


def _outproj_kernel(m_ref, w_ref, x_ref, g_ref, mod_ref, wr_ref, x1_ref, h2_ref, aff_ref):
    out = jnp.dot(m_ref[...], w_ref[...], preferred_element_type=F32)
    x1 = x_ref[...] + mod_ref[0, 2:3, :] * _rms_f32(out, g_ref[1:2, :])
    x1_ref[...] = x1
    h2 = _rms_f32(x1, g_ref[2:3, :]) * (1.0 + mod_ref[0, 4:5, :]) + mod_ref[0, 3:4, :]
    half = h2.shape[1] // 2
    hi = pltpu.bitcast(h2[:, :half].astype(BF16).astype(F32), jnp.int32)
    lo = pltpu.bitcast(h2[:, half:].astype(BF16).astype(F32), jnp.int32)
    h2_ref[...] = hi | lax.shift_right_logical(lo, 16)
    a1 = h2.astype(BF16)
    a2 = (h2 - a1.astype(F32)).astype(BF16)
    wr = wr_ref[...]
    w1 = wr.astype(BF16)
    w2 = (wr - w1.astype(F32)).astype(BF16)
    mm = lambda a, b: jnp.dot(a, b, preferred_element_type=F32)
    logits = mm(a2, w1) + mm(a1, w2) + mm(a1, w1)
    lane = lax.broadcasted_iota(jnp.int32, logits.shape, 1)
    logits = jnp.where(lane < N_EXPERTS, logits, -jnp.inf)
    e = jnp.exp(logits - jnp.max(logits, axis=-1, keepdims=True))
    aff_ref[...] = e / jnp.sum(e, axis=-1, keepdims=True)


def _outproj(merged, w_out, x, gains, mod, w_router_p, n_ctx_rows, dec_seq, tm=512):
    t, d = x.shape
    return pl.pallas_call(
        _outproj_kernel,
        grid=(t // tm,),
        in_specs=[pl.BlockSpec((tm, d), lambda i: (i, 0)),
                  pl.BlockSpec((d, d), lambda i: (0, 0)),
                  pl.BlockSpec((tm, d), lambda i: (i, 0)),
                  pl.BlockSpec((4, d), lambda i: (0, 0)),
                  pl.BlockSpec((1, N_MOD, d), lambda i: (_mod_row(i, tm, n_ctx_rows, dec_seq), 0, 0)),
                  pl.BlockSpec((d, LANE), lambda i: (0, 0))],
        out_specs=[pl.BlockSpec((tm, d), lambda i: (i, 0)),
                   pl.BlockSpec((tm, d // 2), lambda i: (i, 0)),
                   pl.BlockSpec((tm, LANE), lambda i: (i, 0))],
        out_shape=[jax.ShapeDtypeStruct((t, d), F32),
                   jax.ShapeDtypeStruct((t, d // 2), jnp.int32),
                   jax.ShapeDtypeStruct((t, LANE), F32)],
        compiler_params=_cparams(("parallel",), BIG_VMEM_LIMIT),
        name="out_proj",
    )(merged, w_out, x, gains, mod, w_router_p)


CHUNK = 64
CHUNK_SHIFT = 6


def _dot(a, b):
    return jnp.dot(a.astype(BF16), b.astype(BF16), preferred_element_type=F32)


def _dot_nt(a, b):
    return lax.dot_general(a.astype(BF16), b.astype(BF16), (((1,), (1,)), ((), ())),
                           preferred_element_type=F32)


def _dot_tn(a, b):
    return lax.dot_general(a.astype(BF16), b.astype(BF16), (((0,), (0,)), ((), ())),
                           preferred_element_type=F32)


def _interleave(*gens):
    results = [None] * len(gens)
    live = list(range(len(gens)))
    while live:
        for i in list(live):
            try:
                next(gens[i])
            except StopIteration as stop:
                results[i] = stop.value
                live.remove(i)
    return results


def _split3(x):
    x1 = x.astype(BF16)
    r1 = x - x1.astype(F32)
    x2 = r1.astype(BF16)
    x3 = (r1 - x2.astype(F32)).astype(BF16)
    return x1, x2, x3


def _dot3_left(mask_bf16, x):
    x1, x2, x3 = _split3(x)
    f = lambda p: jnp.dot(mask_bf16, p, preferred_element_type=F32)
    return f(x3) + f(x2) + f(x1)


def _dot3_right_exact(a_bf16, x):
    x1, x2, x3 = _split3(x)
    f = lambda p: jnp.dot(a_bf16, p, preferred_element_type=F32)
    return f(x3) + f(x2) + f(x1)


def _dot3_right(x, mask_bf16):
    x1, x2, x3 = _split3(x)
    f = lambda p: jnp.dot(p, mask_bf16, preferred_element_type=F32)
    return f(x3) + f(x2) + f(x1)


def _softplus(x):
    return jnp.maximum(x, 0.0) + jnp.log(1.0 + jnp.exp(-jnp.abs(x)))


def _silu(x):
    return x * jax.nn.sigmoid(x)


def _chunk_tri(n, upper):
    r = lax.broadcasted_iota(jnp.int32, (n, n), 0)
    c = lax.broadcasted_iota(jnp.int32, (n, n), 1)
    same = (r >> CHUNK_SHIFT) == (c >> CHUNK_SHIFT)
    tri = (c >= r) if upper else (c <= r)
    return jnp.where(same & tri, 1.0, 0.0).astype(BF16)


def _unit_tri_inverse(ms):
    n = ms[0].shape[0]
    r = lax.broadcasted_iota(jnp.int32, (n, n), 0)
    c = lax.broadcasted_iota(jnp.int32, (n, n), 1)
    eye = jnp.where(r == c, 1.0, 0.0)
    same = lambda s: (r >> s) == (c >> s)
    m8 = [jnp.where(same(3), m, 0.0) for m in ms]
    m2 = [_dot(a, a) for a in m8]
    p = [eye - a for a in m8]
    m4 = [_dot(a, a) for a in m2]
    p = [a + _dot(a, b) for a, b in zip(p, m2)]
    xs = [a + _dot(a, b) for a, b in zip(p, m4)]
    s = 3
    while (1 << s) < n:
        band = same(s + 1) & jnp.logical_not(same(s))
        t = [_dot(x, jnp.where(band, m, 0.0)) for x, m in zip(xs, ms)]
        xs = [x - _dot(a, x) for x, a in zip(xs, t)]
        s += 1
    return xs


def _gdn_kernel(*refs, zero_init):
    if zero_init:
        (q_ref, k_ref, v_ref, ag_ref, sc_ref, sr_ref, pc_ref, pr_ref, norm_ref,
         o_ref, sfin_ref, qn_ref, kn_ref, gcf_ref, gcb_ref, grf_ref, grb_ref, of_ref, ob_ref,
         uwf_ref, uwb_ref, qkf_ref, qkb_ref) = refs
        s0_ref = None
    else:
        (q_ref, k_ref, v_ref, ag_ref, sc_ref, sr_ref, pc_ref, pr_ref, norm_ref, s0_ref,
         o_ref, sfin_ref, qn_ref, kn_ref, gcf_ref, gcb_ref, grf_ref, grb_ref, of_ref, ob_ref,
         uwf_ref, uwb_ref, qkf_ref, qkb_ref) = refs
    L = q_ref.shape[0]
    n_chunks = L // CHUNK
    heads = range(q_ref.shape[1] // GDN_DK)
    hcols = lambda hb: slice(hb * GDN_DK, (hb + 1) * GDN_DK)

    blk = min(L, 256)
    lo_blk = _chunk_tri(blk, upper=False)
    up_blk = _chunk_tri(blk, upper=True)
    up_c = _chunk_tri(CHUNK, upper=True)
    lo_c = _chunk_tri(CHUNK, upper=False)
    for hb in heads:
        qf = q_ref[:, hcols(hb)].astype(F32)
        qn_ref[:, hcols(hb)] = qf * lax.rsqrt(jnp.sum(qf * qf, axis=-1, keepdims=True) + EPS) * (GDN_DK ** -0.5)
        kf = k_ref[:, hcols(hb)].astype(F32)
        kn_ref[:, hcols(hb)] = kf * lax.rsqrt(jnp.sum(kf * kf, axis=-1, keepdims=True) + EPS)
        gcf_ref[hb] = jnp.zeros(gcf_ref.shape[1:], F32)
        gcf_ref[hb, :, 0:4] = -jnp.exp(pc_ref[hb, 0:1, 0:4]) * _softplus(sc_ref[hb] + pc_ref[hb, 1:2, 0:4])
        for b in range(L // blk):
            rows = slice(b * blk, (b + 1) * blk)
            g_col = gcf_ref[hb, rows, :]
            gcb_ref[hb, rows, :] = _dot3_left(up_blk, g_col)
            gcf_ref[hb, rows, :] = _dot3_left(lo_blk, g_col)
        g_row = -jnp.exp(pr_ref[hb, 0]) * _softplus(sr_ref[hb] + pr_ref[hb, 1])
        g_row = g_row.reshape(n_chunks * 8, CHUNK)
        grf_ref[hb] = _dot3_right(g_row, up_c).reshape(n_chunks, 8, CHUNK)
        grb_ref[hb] = _dot3_right(g_row, lo_c).reshape(n_chunks, 8, CHUNK)

    ti = lax.broadcasted_iota(jnp.int32, (CHUNK, CHUNK), 0)
    si = lax.broadcasted_iota(jnp.int32, (CHUNK, CHUNK), 1)

    def chunk_rows(c):
        return pl.ds(c * CHUNK if isinstance(c, int) else pl.multiple_of(c * CHUNK, CHUNK), CHUNK)

    group = min(n_chunks, 4)

    def prep_group(gi):
        items = [(hb, d, gi * group + j) for j in range(group) for hb in heads for d in range(2)]
        ms, rhs = [], []
        for hb, d, c in items:
            rows = chunk_rows(c)
            q = qn_ref[rows, hcols(hb)]
            k = kn_ref[rows, hcols(hb)]
            v = v_ref[rows, hcols(hb)].astype(F32)
            gc = (gcb_ref if d else gcf_ref)[hb, rows, d:d + 1]
            beta = jax.nn.sigmoid(sc_ref[hb, rows, 2 + d:3 + d])
            gcr = (grb_ref if d else grf_ref)[hb, c][d:d + 1, :]
            incl = (si >= ti) if d else (si <= ti)
            strict = (si > ti) if d else (si < ti)
            decay = jnp.where(incl, jnp.exp(jnp.where(incl, gc - gcr, 0.0)), 0.0)
            kb = k * beta
            ms.append(jnp.where(strict, _dot_nt(kb, k) * decay, 0.0))
            (qkb_ref if d else qkf_ref)[hb, rows, :] = _dot_nt(q, k) * decay
            rhs.append(jnp.concatenate([v * beta, kb * jnp.exp(gc)], axis=1).astype(BF16))
        xs = _unit_tri_inverse(ms)
        for (hb, d, c), x, r in zip(items, xs, rhs):
            (uwb_ref if d else uwf_ref)[hb, chunk_rows(c), :] = _dot(x, r)

    if n_chunks == group:
        prep_group(0)
    else:
        def prep_body(gi, carry):
            prep_group(gi)
            return carry
        lax.fori_loop(0, n_chunks // group, prep_body, 0)

    chains = [(hb, d) for hb in heads for d in range(2)]

    def body(i, carry):
        rows = [chunk_rows(n_chunks - 1 - i if d else i) for _, d in chains]
        gcs = [(gcb_ref if d else gcf_ref)[hb, r, d:d + 1] for (hb, d), r in zip(chains, rows)]
        s_b = [s.astype(BF16) for s in carry]
        uw = [(uwb_ref if d else uwf_ref)[hb, r, :] for (hb, d), r in zip(chains, rows)]
        ws = [_dot(a[:, GDN_DV:], s) for a, s in zip(uw, s_b)]
        v_new = [a[:, :GDN_DV] - w for a, w in zip(uw, ws)]
        g_last = [g[0:1] if d else g[CHUNK - 1:CHUNK] for (_, d), g in zip(chains, gcs)]
        upd = [_dot_tn(kn_ref[r, hcols(hb)] * jnp.exp(gl - g), vn)
               for (hb, _), r, gl, g, vn in zip(chains, rows, g_last, gcs, v_new)]
        new = tuple(s * jnp.exp(gl) + u for s, gl, u in zip(carry, g_last, upd))
        for (hb, d), r, g, s, vn in zip(chains, rows, gcs, s_b, v_new):
            o = _dot(qn_ref[r, hcols(hb)] * jnp.exp(g), s) + _dot((qkb_ref if d else qkf_ref)[hb, r, :], vn)
            (ob_ref if d else of_ref)[r, hcols(hb)] = o
        return new

    if zero_init:
        init = tuple(jnp.zeros((GDN_DK, GDN_DV), F32) for _ in chains)
    else:
        init = tuple(s0_ref[0, d, hb] for hb, d in chains)
    final = lax.fori_loop(0, n_chunks, body, init)
    for (hb, d), s in zip(chains, final):
        sfin_ref[0, d, hb] = s

    for hb in heads:
        o = of_ref[:, hcols(hb)] + ob_ref[:, hcols(hb)]
        y = _rms_f32(o, norm_ref[...])
        o_ref[:, hcols(hb)] = (y * _silu(ag_ref[:, hcols(hb)].astype(F32))).astype(o_ref.dtype)


GDN_HB = 4


def _gdn(cx, rest, sc, sr, pc, pr, norm, s0, *, n_seq, seq_len, row0):
    L = seq_len
    blk0 = row0 // L
    nh = GDN_HEADS
    hb = GDN_HB
    w = hb * GDN_DK
    zero_init = s0 is None
    k_off = GDN_QK_W // w
    v_off = 2 * GDN_QK_W // w
    ag_off = R_A_G // w
    in_specs = [pl.BlockSpec((L, w), lambda s, h: (blk0 + s, h)),
                pl.BlockSpec((L, w), lambda s, h: (blk0 + s, k_off + h)),
                pl.BlockSpec((L, w), lambda s, h: (blk0 + s, v_off + h)),
                pl.BlockSpec((L, w), lambda s, h: (blk0 + s, ag_off + h)),
                pl.BlockSpec((hb, L, 4), lambda s, h: (h, blk0 + s, 0)),
                pl.BlockSpec((hb, L // CHUNK, 8, CHUNK), lambda s, h: (h, blk0 + s, 0, 0)),
                pl.BlockSpec((hb, 8, LANE), lambda s, h: (h, 0, 0)),
                pl.BlockSpec((hb, 2, 8, CHUNK), lambda s, h: (h, 0, 0, 0)),
                pl.BlockSpec((1, GDN_DV), lambda s, h: (0, 0))]
    args = [cx, cx, cx, rest, sc, sr, pc, pr, norm]
    state_spec = pl.BlockSpec((1, 2, hb, GDN_DK, GDN_DV), lambda s, h: (s, 0, h, 0, 0))
    if not zero_init:
        in_specs.append(state_spec)
        args.append(s0)
    return pl.pallas_call(
        functools.partial(_gdn_kernel, zero_init=zero_init),
        grid=(n_seq, nh // hb),
        in_specs=in_specs,
        out_specs=[pl.BlockSpec((L, w), lambda s, h: (s, h)), state_spec],
        out_shape=[jax.ShapeDtypeStruct((n_seq * L, BRANCH_W), BF16),
                   jax.ShapeDtypeStruct((n_seq, 2, nh, GDN_DK, GDN_DV), F32)],
        scratch_shapes=[pltpu.VMEM((L, w), F32), pltpu.VMEM((L, w), F32),
                        pltpu.VMEM((hb, L, LANE), F32), pltpu.VMEM((hb, L, LANE), F32),
                        pltpu.VMEM((hb, L // CHUNK, 8, CHUNK), F32), pltpu.VMEM((hb, L // CHUNK, 8, CHUNK), F32),
                        pltpu.VMEM((L, w), F32), pltpu.VMEM((L, w), F32),
                        pltpu.VMEM((hb, L, 2 * GDN_DV), F32), pltpu.VMEM((hb, L, 2 * GDN_DV), F32),
                        pltpu.VMEM((hb, L, CHUNK), F32), pltpu.VMEM((hb, L, CHUNK), F32)],
        compiler_params=_cparams(("parallel", "arbitrary")),
        name="gdn_scan",
    )(*args)


def _gdn_side_inputs(small, gdn_a_log, gdn_dt_bias):
    t = small.shape[0]
    nh = GDN_HEADS
    ab = small[:, :4 * nh].reshape(t, 2, 2, nh)
    sc = ab.transpose(3, 0, 1, 2).reshape(nh, t, 4)
    rows = ab.transpose(3, 1, 2, 0).reshape(nh, 4, t // CHUNK, CHUNK).transpose(0, 2, 1, 3)
    sr = jnp.concatenate([rows, jnp.zeros_like(rows)], axis=2)
    pc = jnp.zeros((nh, 8, LANE), F32)
    pc = pc.at[:, 0, 0:2].set(gdn_a_log.T).at[:, 1, 0:2].set(gdn_dt_bias.T)
    pr = jnp.zeros((nh, 2, 8, CHUNK), F32)
    pr = pr.at[:, 0, 0:2, :].set(jnp.broadcast_to(gdn_a_log.T[:, :, None], (nh, 2, CHUNK)))
    pr = pr.at[:, 1, 0:2, :].set(jnp.broadcast_to(gdn_dt_bias.T[:, :, None], (nh, 2, CHUNK)))
    return sc, sr, pc, pr


GROUP_W = SSD_HPG * SSD_P


def _ssd_kernel(*refs, zero_init):
    if zero_init:
        (x_ref, b_ref, c_ref, z_ref, dc_ref, dr_ref, pc_ref, pr_ref, dskip_ref, norm_ref,
         o_ref, hfin_ref, dtc_ref, acf_ref, acb_ref, dtr_ref, arf_ref, arb_ref, yf_ref, yb_ref,
         hf_ref, hb_ref) = refs
        h0_ref = None
    else:
        (x_ref, b_ref, c_ref, z_ref, dc_ref, dr_ref, pc_ref, pr_ref, dskip_ref, norm_ref, h0_ref,
         o_ref, hfin_ref, dtc_ref, acf_ref, acb_ref, dtr_ref, arf_ref, arb_ref, yf_ref, yb_ref,
         hf_ref, hb_ref) = refs
    L = x_ref.shape[0]
    n_chunks = L // CHUNK
    nh = SSD_HPG

    dtc_ref[...] = jnp.zeros_like(dtc_ref)
    dtc_ref[:, 0:2 * nh] = _softplus(dc_ref[0] + pc_ref[0, 1:2, 0:2 * nh])
    acf_ref[...] = jnp.zeros_like(acf_ref)
    acf_ref[:, 0:2 * nh] = -jnp.exp(pc_ref[0, 0:1, 0:2 * nh]) * dtc_ref[:, 0:2 * nh]
    blk = min(L, 256)
    lo_blk = _chunk_tri(blk, upper=False)
    up_blk = _chunk_tri(blk, upper=True)
    for b in range(L // blk):
        rows = slice(b * blk, (b + 1) * blk)
        da = acf_ref[rows, :]
        acb_ref[rows, :] = _dot3_left(up_blk, da)
        acf_ref[rows, :] = _dot3_left(lo_blk, da)
    dt_row = _softplus(dr_ref[0] + pr_ref[0, 1])
    dtr_ref[...] = dt_row
    da_row = (-jnp.exp(pr_ref[0, 0]) * dt_row).reshape(n_chunks * 2 * nh, CHUNK)
    arf_ref[...] = _dot3_right(da_row, _chunk_tri(CHUNK, upper=True)).reshape(n_chunks, 2 * nh, CHUNK)
    arb_ref[...] = _dot3_right(da_row, _chunk_tri(CHUNK, upper=False)).reshape(n_chunks, 2 * nh, CHUNK)

    if zero_init:
        hf_ref[...] = jnp.zeros_like(hf_ref)
        hb_ref[...] = jnp.zeros_like(hb_ref)
    else:
        hf_ref[...] = h0_ref[0, 0, 0]
        hb_ref[...] = h0_ref[0, 1, 0]

    ti = lax.broadcasted_iota(jnp.int32, (CHUNK, CHUNK), 0)
    si = lax.broadcasted_iota(jnp.int32, (CHUNK, CHUNK), 1)
    ej = lax.broadcasted_iota(jnp.int32, (LANE, GROUP_W), 0)
    ec = lax.broadcasted_iota(jnp.int32, (LANE, GROUP_W), 1)

    def step(d, c):
        rows = pl.ds(pl.multiple_of(c * CHUNK, CHUNK), CHUNK)
        h_ref = hb_ref if d else hf_ref
        x = x_ref[rows, :]
        bm = b_ref[rows, :]
        cm = c_ref[rows, :]
        acs = (acb_ref if d else acf_ref)[rows, :]
        acs_r = (arb_ref if d else arf_ref)[c][d * nh:(d + 1) * nh, :]
        dt_r = dtr_ref[c][d * nh:(d + 1) * nh, :]
        incl = (si >= ti) if d else (si <= ti)
        cb = _dot_nt(cm, bm)
        spread = jnp.where(ej == d * nh + (ec >> 6), 1.0, 0.0).astype(BF16)
        last = acs[0:1] if d else acs[CHUNK - 1:CHUNK]
        p_full = _dot3_right(dtc_ref[rows, :] * jnp.exp(last - acs), spread)
        e_full = _dot3_right(jnp.exp(acs), spread)
        cd_full = _dot3_right(jnp.broadcast_to(jnp.exp(last), (8, LANE)), spread)[0:1]
        h_in = h_ref[...]
        y_off = _dot(cm, h_in)
        h_ref[...] = h_in * cd_full + _dot_tn(bm, x.astype(F32) * p_full)
        yield
        ys = []
        for h in range(nh):
            diff = acs[:, d * nh + h:d * nh + h + 1] - acs_r[h:h + 1, :]
            seg = jnp.where(incl, jnp.exp(jnp.where(incl, diff, 0.0)), 0.0)
            ys.append(_dot(cb * seg * dt_r[h:h + 1, :], x[:, h * SSD_P:(h + 1) * SSD_P]))
        yield
        (yb_ref if d else yf_ref)[rows, :] = jnp.concatenate(ys, axis=1) + y_off * e_full

    def body(i, carry):
        _interleave(step(0, 2 * i), step(1, n_chunks - 1 - 2 * i),
                    step(0, 2 * i + 1), step(1, n_chunks - 2 - 2 * i))
        return carry

    lax.fori_loop(0, n_chunks // 2, body, 0)
    hfin_ref[0, 0, 0] = hf_ref[...]
    hfin_ref[0, 1, 0] = hb_ref[...]

    y = yf_ref[...] + yb_ref[...] + dskip_ref[0] * x_ref[...].astype(F32)
    y = y * _silu(z_ref[...].astype(F32))
    o_ref[...] = _rms_f32(y, norm_ref[...]).astype(o_ref.dtype)


def _ssd(cx, rest, dc, dr, pc, pr, dskip, norm, h0, *, n_seq, seq_len, row0):
    L = seq_len
    blk0 = row0 // L
    ng = SSD_GROUPS
    zero_init = h0 is None
    x_off = (2 * GDN_QK_W + BRANCH_W) // GROUP_W
    b_off = (2 * GDN_QK_W + 2 * BRANCH_W) // SSD_N
    c_off = b_off + ng
    z_off = R_B_Z // GROUP_W
    in_specs = [pl.BlockSpec((L, GROUP_W), lambda s, g: (blk0 + s, x_off + g)),
                pl.BlockSpec((L, SSD_N), lambda s, g: (blk0 + s, b_off + g)),
                pl.BlockSpec((L, SSD_N), lambda s, g: (blk0 + s, c_off + g)),
                pl.BlockSpec((L, GROUP_W), lambda s, g: (blk0 + s, z_off + g)),
                pl.BlockSpec((1, L, 2 * SSD_HPG), lambda s, g: (g, blk0 + s, 0)),
                pl.BlockSpec((1, L // CHUNK, 2 * SSD_HPG, CHUNK), lambda s, g: (g, blk0 + s, 0, 0)),
                pl.BlockSpec((1, 8, LANE), lambda s, g: (g, 0, 0)),
                pl.BlockSpec((1, 2, 2 * SSD_HPG, CHUNK), lambda s, g: (g, 0, 0, 0)),
                pl.BlockSpec((1, 1, GROUP_W), lambda s, g: (g, 0, 0)),
                pl.BlockSpec((1, GROUP_W), lambda s, g: (0, g))]
    args = [cx, cx, cx, rest, dc, dr, pc, pr, dskip, norm]
    state_spec = pl.BlockSpec((1, 2, 1, SSD_N, GROUP_W), lambda s, g: (s, 0, g, 0, 0))
    if not zero_init:
        in_specs.append(state_spec)
        args.append(h0)
    return pl.pallas_call(
        functools.partial(_ssd_kernel, zero_init=zero_init),
        grid=(n_seq, ng),
        in_specs=in_specs,
        out_specs=[pl.BlockSpec((L, GROUP_W), lambda s, g: (s, g)), state_spec],
        out_shape=[jax.ShapeDtypeStruct((n_seq * L, BRANCH_W), BF16),
                   jax.ShapeDtypeStruct((n_seq, 2, ng, SSD_N, GROUP_W), F32)],
        scratch_shapes=[pltpu.VMEM((L, LANE), F32), pltpu.VMEM((L, LANE), F32), pltpu.VMEM((L, LANE), F32),
                        pltpu.VMEM((L // CHUNK, 2 * SSD_HPG, CHUNK), F32),
                        pltpu.VMEM((L // CHUNK, 2 * SSD_HPG, CHUNK), F32),
                        pltpu.VMEM((L // CHUNK, 2 * SSD_HPG, CHUNK), F32),
                        pltpu.VMEM((L, GROUP_W), F32), pltpu.VMEM((L, GROUP_W), F32),
                        pltpu.VMEM((SSD_N, GROUP_W), F32), pltpu.VMEM((SSD_N, GROUP_W), F32)],
        compiler_params=_cparams(("parallel", "arbitrary")),
        name="ssd_scan",
    )(*args)


def _ssd_side_inputs(small, ssd_a_log, ssd_dt_bias, ssd_d):
    t = small.shape[0]
    ng, nh = SSD_GROUPS, SSD_HPG
    off = 4 * GDN_HEADS
    dt = small[:, off:off + 2 * SSD_HEADS].reshape(t, 2, ng, nh)
    dc = dt.transpose(2, 0, 1, 3).reshape(ng, t, 2 * nh)
    dr = dt.transpose(2, 1, 3, 0).reshape(ng, 2 * nh, t // CHUNK, CHUNK).transpose(0, 2, 1, 3)
    a = ssd_a_log.reshape(2, ng, nh).transpose(1, 0, 2).reshape(ng, 2 * nh)
    bias = ssd_dt_bias.reshape(2, ng, nh).transpose(1, 0, 2).reshape(ng, 2 * nh)
    pc = jnp.zeros((ng, 8, LANE), F32).at[:, 0, 0:2 * nh].set(a).at[:, 1, 0:2 * nh].set(bias)
    pr = jnp.stack([jnp.broadcast_to(a[:, :, None], (ng, 2 * nh, CHUNK)),
                    jnp.broadcast_to(bias[:, :, None], (ng, 2 * nh, CHUNK))], axis=1)
    dskip = jnp.repeat(ssd_d.reshape(ng, nh), SSD_P, axis=1)[:, None, :]
    return dc, dr, pc, pr, dskip


def _ssd_state_to_kernel(s):
    n = s.shape[0]
    return s.reshape(n, 2, SSD_GROUPS, SSD_HPG, SSD_N, SSD_P).transpose(0, 1, 2, 4, 3, 5).reshape(
        n, 2, SSD_GROUPS, SSD_N, GROUP_W)


def _ssd_state_from_kernel(s):
    n = s.shape[0]
    return s.reshape(n, 2, SSD_GROUPS, SSD_N, SSD_HPG, SSD_P).transpose(0, 1, 2, 4, 3, 5).reshape(
        n, 2, SSD_HEADS, SSD_N, SSD_P)


SUB = 4
SUB_SHIFT = 2
N_SUB = CHUNK // SUB


def _hgrn_kernel(*refs, zero_init):
    if zero_init:
        (q_ref, zf_ref, zb_ref, v_ref, cg_ref, lb_ref, norm_ref,
         o_ref, sfin_ref, qs_ref, kf_ref, kb_ref, gf_ref, gb_ref, of_ref, ob_ref) = refs
        s0_ref = None
    else:
        (q_ref, zf_ref, zb_ref, v_ref, cg_ref, lb_ref, norm_ref, s0_ref,
         o_ref, sfin_ref, qs_ref, kf_ref, kb_ref, gf_ref, gb_ref, of_ref, ob_ref) = refs
    L = q_ref.shape[0]
    n_chunks = L // CHUNK

    qs_ref[...] = _silu(q_ref[...].astype(F32))
    blk = min(L, 256)
    lo_blk = _chunk_tri(blk, upper=False)
    up_blk = _chunk_tri(blk, upper=True)
    for d, (z_ref, k_ref, g_ref, tri) in enumerate(((zf_ref, kf_ref, gf_ref, lo_blk),
                                                    (zb_ref, kb_ref, gb_ref, up_blk))):
        lbd = lb_ref[d:d + 1, :]
        z = z_ref[...].astype(F32)
        sig = jax.nn.sigmoid(z)
        k_ref[...] = (1.0 - lbd) * (1.0 - sig)
        log_f = jnp.log(lbd + (1.0 - lbd) * sig)
        for b in range(L // blk):
            rows = slice(b * blk, (b + 1) * blk)
            g_ref[rows, :] = _dot3_left(tri, log_f[rows])

    ti = lax.broadcasted_iota(jnp.int32, (CHUNK, CHUNK), 0)
    si = lax.broadcasted_iota(jnp.int32, (CHUNK, CHUNK), 1)
    tcol = lax.broadcasted_iota(jnp.int32, (CHUNK, 1), 0)

    def padded(part, lo):
        hi = lo + part.shape[0]
        pieces = ([jnp.zeros((lo, HGRN_DK), F32)] if lo else []) + [part] + (
            [jnp.zeros((CHUNK - hi, HGRN_DK), F32)] if hi < CHUNK else [])
        return jnp.concatenate(pieces, axis=0).astype(BF16)

    def step(d, c, st_in, st_out):
        st = st_in()
        rows = pl.ds(pl.multiple_of(c * CHUNK, CHUNK), CHUNK)
        q = qs_ref[rows, :]
        k = (kb_ref if d else kf_ref)[rows, :]
        g = (gb_ref if d else gf_ref)[rows, :]
        v = v_ref[rows, :].astype(F32)
        if d:
            sel = jnp.where(si == ((ti >> SUB_SHIFT) << SUB_SHIFT) + SUB, 1.0, 0.0).astype(BF16)
        else:
            sel = jnp.where(si == ((ti >> SUB_SHIFT) << SUB_SHIFT) - 1, 1.0, 0.0).astype(BF16)
        g_ref_rows = _dot3_left(sel, g)
        g_last = g[0:1] if d else g[CHUNK - 1:CHUNK]
        st_out(st * jnp.exp(g_last) + _dot_tn(v, k * jnp.exp(g_last - g)))
        yield
        q_in = q * jnp.exp(g - g_ref_rows)
        k_parts, q_parts = [], []
        for j in range(1, N_SUB):
            if d:
                edge = CHUNK - SUB * j
                lo, hi = (edge // 8) * 8, CHUNK
                use = tcol[lo:hi] >= edge
                mine_blk = N_SUB - 1 - j
            else:
                edge = SUB * j
                lo, hi = 0, -(-edge // 8) * 8
                use = tcol[lo:hi] < edge
                mine_blk = j
            ref_row = g[edge:edge + 1, :] if d else g[edge - 1:edge, :]
            part = jnp.where(use, k[lo:hi] * jnp.exp(jnp.where(use, ref_row - g[lo:hi], 0.0)), 0.0)
            k_parts.append(padded(part, lo))
            qlo = (mine_blk * SUB // 8) * 8
            qpart = jnp.where((tcol[qlo:qlo + 8] >> SUB_SHIFT) == mine_blk, q_in[qlo:qlo + 8], 0.0)
            q_parts.append(padded(qpart, qlo))
        yield
        att = _dot_nt(jnp.concatenate(q_parts, axis=1), jnp.concatenate(k_parts, axis=1))
        o_state = _dot_nt(q * jnp.exp(g), st)
        yield
        o = _dot(att, v) + o_state
        pos = tcol & (SUB - 1)
        for lag in range(SUB):
            if lag == 0:
                a = jnp.sum(q * k, axis=-1, keepdims=True)
                o = o + a * v
                continue
            shift = (CHUNK - lag) if d else lag
            valid = (pos + lag <= SUB - 1) if d else (pos >= lag)
            k_l = pltpu.roll(k, shift, 0)
            g_l = pltpu.roll(g, shift, 0)
            v_l = pltpu.roll(v, shift, 0)
            dec = jnp.exp(jnp.where(valid, g - g_l, 0.0))
            a = jnp.where(valid, jnp.sum(q * k_l * dec, axis=-1, keepdims=True), 0.0)
            o = o + a * v_l
        yield
        (ob_ref if d else of_ref)[rows, :] = o

    def body(i, carry):
        box = {"f0": carry[0], "b0": carry[1]}
        put = lambda key: (lambda val: box.__setitem__(key, val))
        get = lambda key: (lambda: box[key])
        _interleave(step(0, 2 * i, get("f0"), put("f1")),
                    step(1, n_chunks - 1 - 2 * i, get("b0"), put("b1")),
                    step(0, 2 * i + 1, get("f1"), put("f2")),
                    step(1, n_chunks - 2 - 2 * i, get("b1"), put("b2")))
        return box["f2"], box["b2"]

    if zero_init:
        init = (jnp.zeros((HGRN_DV, HGRN_DK), F32), jnp.zeros((HGRN_DV, HGRN_DK), F32))
    else:
        init = (s0_ref[0, 0, 0], s0_ref[0, 1, 0])
    s_f, s_b = lax.fori_loop(0, n_chunks // 2, body, init)
    sfin_ref[0, 0, 0] = s_f
    sfin_ref[0, 1, 0] = s_b

    o = of_ref[...] + ob_ref[...]
    y = _rms_f32(o, norm_ref[...])
    o_ref[...] = (y * _silu(cg_ref[...].astype(F32))).astype(o_ref.dtype)


def _hgrn(rest, lb, norm, s0, *, n_seq, seq_len, row0):
    L = seq_len
    blk0 = row0 // L
    nh = HGRN_HEADS
    zero_init = s0 is None
    q_off, zf_off, v_off, g_off = R_C_Q // LANE, R_C_F // LANE, R_C_I // LANE, R_C_G // LANE
    zb_off = zf_off + nh
    in_specs = [pl.BlockSpec((L, LANE), lambda s, h: (blk0 + s, q_off + h)),
                pl.BlockSpec((L, LANE), lambda s, h: (blk0 + s, zf_off + h)),
                pl.BlockSpec((L, LANE), lambda s, h: (blk0 + s, zb_off + h)),
                pl.BlockSpec((L, LANE), lambda s, h: (blk0 + s, v_off + h)),
                pl.BlockSpec((L, LANE), lambda s, h: (blk0 + s, g_off + h)),
                pl.BlockSpec((2, HGRN_DK), lambda s, h: (0, h)),
                pl.BlockSpec((1, HGRN_DV), lambda s, h: (0, 0))]
    args = [rest, rest, rest, rest, rest, lb, norm]
    state_spec = pl.BlockSpec((1, 2, 1, HGRN_DV, HGRN_DK), lambda s, h: (s, 0, h, 0, 0))
    if not zero_init:
        in_specs.append(state_spec)
        args.append(s0)
    return pl.pallas_call(
        functools.partial(_hgrn_kernel, zero_init=zero_init),
        grid=(n_seq, nh),
        in_specs=in_specs,
        out_specs=[pl.BlockSpec((L, HGRN_DV), lambda s, h: (s, h)), state_spec],
        out_shape=[jax.ShapeDtypeStruct((n_seq * L, BRANCH_W), BF16),
                   jax.ShapeDtypeStruct((n_seq, 2, nh, HGRN_DV, HGRN_DK), F32)],
        scratch_shapes=[pltpu.VMEM((L, HGRN_DK), F32)] * 5 + [pltpu.VMEM((L, HGRN_DV), F32)] * 2,
        compiler_params=_cparams(("parallel", "arbitrary")),
        name="hgrn_scan",
    )(*args)


def _route_kernel(aff_ref, sel_ref, *, cap):
    bits = pltpu.bitcast(aff_ref[...], jnp.int32)
    n_rows = bits.shape[0]
    row = lax.broadcasted_iota(jnp.int32, bits.shape, 0)
    count = lambda m: jnp.sum(m.astype(jnp.int32), axis=0, keepdims=True)

    def thr_body(i, thr):
        cand = thr | (1 << (30 - i))
        return jnp.where(count(bits >= cand) >= cap, cand, thr)

    thr = lax.fori_loop(0, 31, thr_body, jnp.zeros((1, bits.shape[1]), jnp.int32))
    above = bits > thr
    tie = bits == thr
    need = cap - count(above)
    n_bits = max(1, (n_rows - 1).bit_length())

    def tie_body(i, lim):
        cand = lim | (1 << (n_bits - 1 - i))
        return jnp.where(count(tie & (row < cand)) < need, cand, lim)

    lim = lax.fori_loop(0, n_bits, tie_body, jnp.zeros_like(thr))
    sel = above | (tie & (row <= lim) & (need > 0))
    sel_ref[...] = jnp.where(sel, 1.0, 0.0)


def _route(aff, row0, n_rows):
    cap = CAPACITY_FACTOR * n_rows // N_EXPERTS
    return pl.pallas_call(
        functools.partial(_route_kernel, cap=cap),
        grid=(1,),
        in_specs=[pl.BlockSpec((n_rows, LANE), lambda i: (row0 // n_rows, 0))],
        out_specs=pl.BlockSpec((n_rows, LANE), lambda i: (0, 0)),
        out_shape=jax.ShapeDtypeStruct((n_rows, LANE), F32),
        compiler_params=_cparams(("arbitrary",)),
        name="route_topk",
    )(aff)


FF_TILE = 256


def _ffn_kernel(x_ref, wg_ref, wu_ref, wd_ref, gate_ref, o_ref, xb_ref):
    f = pl.program_id(1)

    @pl.when(f == 0)
    def _():
        words = x_ref[0]
        half = words.shape[1]
        xb_ref[:, :half] = pltpu.bitcast(words & jnp.int32(-65536), F32).astype(BF16)
        xb_ref[:, half:] = pltpu.bitcast(words << 16, F32).astype(BF16)

    x = xb_ref[...]
    g = jnp.dot(x, wg_ref[0].astype(BF16), preferred_element_type=F32)
    u = jnp.dot(x, wu_ref[0].astype(BF16), preferred_element_type=F32)
    y = jnp.dot((_silu(g) * u).astype(BF16), wd_ref[0].astype(BF16), preferred_element_type=F32)

    @pl.when(f == 0)
    def _():
        o_ref[0] = y

    @pl.when(f > 0)
    def _():
        o_ref[0] += y

    @pl.when(f == pl.num_programs(1) - 1)
    def _():
        o_ref[0] = o_ref[0] * gate_ref[0]


def _expert_ffn(xe, w_gu, w_down, layer, gate):
    e, cap, _ = xe.shape
    ff, d = w_down.shape[1:]
    nf = ff // FF_TILE
    e0 = layer * e
    return pl.pallas_call(
        _ffn_kernel,
        grid=(e, nf),
        in_specs=[pl.BlockSpec((1, cap, d // 2), lambda g, f: (g, 0, 0)),
                  pl.BlockSpec((1, d, FF_TILE), lambda g, f: (e0 + g, 0, f)),
                  pl.BlockSpec((1, d, FF_TILE), lambda g, f: (e0 + g, 0, nf + f)),
                  pl.BlockSpec((1, FF_TILE, d), lambda g, f: (e0 + g, f, 0)),
                  pl.BlockSpec((1, cap, 1), lambda g, f: (g, 0, 0))],
        out_specs=pl.BlockSpec((1, cap, d), lambda g, f: (g, 0, 0)),
        out_shape=jax.ShapeDtypeStruct((e, cap, d), F32),
        scratch_shapes=[pltpu.VMEM((cap, d), BF16)],
        compiler_params=_cparams(("parallel", "arbitrary"), BIG_VMEM_LIMIT),
        name="expert_ffn",
    )(xe, w_gu, w_gu, w_down, gate)


def _ec_moe(h2w, aff, sel, w_gu, w_down, layer):
    t = h2w.shape[0]
    d = w_down.shape[-1]
    cap = CAPACITY_FACTOR * t // N_EXPERTS
    tok = jnp.arange(t, dtype=jnp.int32)[:, None]
    keys = jnp.where(sel[:, :N_EXPERTS] > 0.5, tok, tok + t).T
    idx = jnp.sort(keys, axis=1)[:, :cap]
    gate = jnp.take_along_axis(aff[:, :N_EXPERTS].T, idx, axis=1)[..., None]
    ye = _expert_ffn(h2w[idx], w_gu, w_down, layer, gate)
    return jnp.zeros((t, d), F32).at[idx.reshape(-1)].add(ye.reshape(-1, d))


def _resid_kernel(x_ref, mc_ref, ms_ref, g_ref, mod_ref, gn_ref, modn_ref, xo_ref, h_ref, *, n_ctx_tiles):
    moe = jnp.where(pl.program_id(0) < n_ctx_tiles, mc_ref[...], ms_ref[...])
    x = x_ref[...] + mod_ref[0, 5:6, :] * _rms_f32(moe, g_ref[...])
    xo_ref[...] = x
    h_ref[...] = (_rms_f32(x, gn_ref[...]) * (1.0 + modn_ref[0, 1:2, :]) + modn_ref[0, 0:1, :]).astype(BF16)


def _moe_residual(x1, moe_c, moe_s, gain3, mod, gain_next, mod_next, n_ctx_rows, dec_seq, tm=512):
    t, d = x1.shape
    nc = n_ctx_rows // tm
    row_spec = pl.BlockSpec((tm, d), lambda i: (i, 0))
    ctx_spec = pl.BlockSpec((tm, d), lambda i: (jnp.minimum(i, nc - 1), 0))
    smp_spec = pl.BlockSpec((tm, d), lambda i: (jnp.maximum(i - nc, 0), 0))
    vec_spec = pl.BlockSpec((1, d), lambda i: (0, 0))
    mod_spec = pl.BlockSpec((1, N_MOD, d), lambda i: (_mod_row(i, tm, n_ctx_rows, dec_seq), 0, 0))
    return pl.pallas_call(
        functools.partial(_resid_kernel, n_ctx_tiles=nc),
        grid=(t // tm,),
        in_specs=[row_spec, ctx_spec, smp_spec, vec_spec, mod_spec, vec_spec, mod_spec],
        out_specs=[row_spec, row_spec],
        out_shape=[jax.ShapeDtypeStruct((t, d), F32), jax.ShapeDtypeStruct((t, d), BF16)],
        compiler_params=_cparams(("parallel",)),
        name="moe_residual",
    )(x1, moe_c, moe_s, gain3, mod, gain_next, mod_next)


def kernel(x_prompt, x_sample, c, state_gdn, state_ssd, state_hgrn, c_ctx, w_mod, b_mod, norm_gain,
           w_in, conv_w, conv_b, gdn_a_log, gdn_dt_bias, gdn_norm, ssd_a_log, ssd_dt_bias, ssd_d,
           ssd_norm, hgrn_lb, hgrn_norm, w_branch, w_out, w_router, w_gu, w_down):
    nb_c, seq_c, d = x_prompt.shape
    nb_s, seq_s, _ = x_sample.shape
    n_ctx = nb_c * seq_c
    n_smp = nb_s * seq_s

    lb_w = jax.nn.softmax(hgrn_lb.astype(F32), axis=0)
    lb = jnp.cumsum(lb_w, axis=0) - lb_w[:1]

    cond = jnp.concatenate([c_ctx[None, :], c, jnp.zeros((8 - 1 - nb_s, d), F32)], axis=0)
    mod_all = _mod_all(cond, w_mod, b_mod[:, None, :]).reshape(DEPTH, 8, N_MOD, d)

    x = jnp.concatenate([x_prompt.reshape(n_ctx, d), x_sample.reshape(n_smp, d)], axis=0)
    new_gdn, new_ssd, new_hgrn = [], [], []

    w_conv = w_in[:, :, :CONV_CH].astype(BF16)
    w_rest = jnp.concatenate([w_in[:, :, OFF_M_G:], w_in[:, :, OFF_A_G:OFF_ALPHA], w_in[:, :, OFF_B_Z:OFF_B_DT],
                              w_in[:, :, OFF_C_Q:OFF_M_G]], axis=2).astype(BF16)
    w_small = jnp.concatenate([w_in[:, :, OFF_ALPHA:OFF_B_Z], w_in[:, :, OFF_B_DT:OFF_C_Q],
                               jnp.zeros((DEPTH, d, SMALL_W - 4 * GDN_HEADS - 2 * SSD_HEADS), F32)], axis=2)
    w_gu_all = w_gu.reshape(DEPTH * N_EXPERTS, d, 2 * EXPERT_FF)
    w_down_all = w_down.reshape(DEPTH * N_EXPERTS, EXPERT_FF, d)

    for l in range(DEPTH):
        mod = mod_all[l]
        if l == 0:
            h = _hmod(x, norm_gain[l, 0:1], mod, n_ctx, seq_s, i_shift=0, i_scale=1)
        cx = _mm_conv(h, w_conv, l, conv_w[l], conv_b[l][None, :], n_ctx, seq_c, GRID_W)
        rest = _mm_plain(h, w_rest, l, tn=1024)
        small = _mm_small(h, w_small, l)

        gdn_side = _gdn_side_inputs(small, gdn_a_log[l], gdn_dt_bias[l])
        oa_c, sg = _gdn(cx, rest, *gdn_side, gdn_norm[l][None, :], None, n_seq=nb_c, seq_len=seq_c, row0=0)
        oa_s, _ = _gdn(cx, rest, *gdn_side, gdn_norm[l][None, :], state_gdn[:, l], n_seq=nb_s, seq_len=seq_s,
                       row0=n_ctx)
        ssd_side = _ssd_side_inputs(small, ssd_a_log[l], ssd_dt_bias[l], ssd_d[l])
        ob_c, ss = _ssd(cx, rest, *ssd_side, ssd_norm[l][None, :], None, n_seq=nb_c, seq_len=seq_c, row0=0)
        ob_s, _ = _ssd(cx, rest, *ssd_side, ssd_norm[l][None, :], _ssd_state_to_kernel(state_ssd[:, l]),
                       n_seq=nb_s, seq_len=seq_s, row0=n_ctx)
        ss = _ssd_state_from_kernel(ss)
        oc_c, sh = _hgrn(rest, lb[l], hgrn_norm[l][None, :], None, n_seq=nb_c, seq_len=seq_c, row0=0)
        oc_s, _ = _hgrn(rest, lb[l], hgrn_norm[l][None, :], jnp.swapaxes(state_hgrn[:, l], -1, -2),
                        n_seq=nb_s, seq_len=seq_s, row0=n_ctx)
        sh = jnp.swapaxes(sh, -1, -2)
        new_gdn.append(sg)
        new_ssd.append(ss)
        new_hgrn.append(sh)
        branches = jnp.stack([jnp.concatenate([oa_c, oa_s]), jnp.concatenate([ob_c, ob_s]),
                              jnp.concatenate([oc_c, oc_s])])

        merged = _merge(branches, rest, w_branch[l].astype(BF16))
        w_router_p = jnp.concatenate([w_router[l], jnp.zeros((d, LANE - N_EXPERTS), F32)], axis=1)
        x1, h2, aff = _outproj(merged, w_out[l].astype(BF16), x, norm_gain[l], mod, w_router_p, n_ctx, seq_s)

        moe_c = _ec_moe(h2[:n_ctx], aff[:n_ctx], _route(aff, 0, n_ctx), w_gu_all, w_down_all, l)
        moe_s = _ec_moe(h2[n_ctx:], aff[n_ctx:], _route(aff, n_ctx, n_smp), w_gu_all, w_down_all, l)
        l_next = min(l + 1, DEPTH - 1)
        x, h = _moe_residual(x1, moe_c, moe_s, norm_gain[l, 3:4], mod, norm_gain[l_next, 0:1], mod_all[l_next],
                             n_ctx, seq_s)

    y_prompt = x[:n_ctx].reshape(nb_c, seq_c, d)
    y_sample = x[n_ctx:].reshape(nb_s, seq_s, d)
    return (y_prompt, y_sample, jnp.stack(new_gdn, axis=1), jnp.stack(new_ssd, axis=1),
            jnp.stack(new_hgrn, axis=1))
```
